```python
import jax, jax.numpy as jnp
from jax import lax
import numpy as np

D_MODEL = 1024
BATCH = 1
SEQ = 16384
DEPTH = 2

GRID_W = 64
CTX_LEN = 256
RET_HEADS = 4
RET_DK = 128
RET_DV = 128
RET_CHUNK = 128
WIN_HEADS = 8
WIN_KV_HEADS = 2
WIN_HEAD_DIM = 64
WINDOW = 128
WIN_BLOCK = 128
NA_HEADS = 8
NA_HEAD_DIM = 64
NA_ROWS_MAX = 8
NA_COLS = 16
N_EXPERTS = 16
EXPERT_HIDDEN = 1024
CAPACITY_FACTOR = 2
N_BRANCHES = 3
N_MOD = 6
ROPE_BASE = 10000.0
EPS = 1e-6
NEG_INF = -1e30
RET_QK = RET_HEADS * RET_DK
RET_V = RET_HEADS * RET_DV
WIN_Q = WIN_HEADS * WIN_HEAD_DIM
WIN_KV = WIN_KV_HEADS * WIN_HEAD_DIM
NA_W = NA_HEADS * NA_HEAD_DIM
IN_SPLITS = (('q_r', RET_QK), ('k_r', RET_QK), ('v_r', RET_V), ('g_r', RET_V),
             ('q_w', WIN_Q), ('k_w', WIN_KV), ('v_w', WIN_KV),
             ('q_n', NA_W), ('k_n', NA_W), ('v_n', NA_W),
             ('gates', N_BRANCHES * D_MODEL))
IN_COLS = 2 * RET_QK + 2 * RET_V + WIN_Q + 2 * WIN_KV + 3 * NA_W + N_BRANCHES * D_MODEL
KV_NAMES = ('k_r', 'v_r', 'k_w', 'v_w', 'k_n', 'v_n')

kernel_name = 'hybrid_retention_window_natten_ecmoe'


def rms_norm(x, g):
    xf = x.astype(jnp.float32)
    y = xf * lax.rsqrt(jnp.mean(xf * xf, axis=-1, keepdims=True) + EPS)
    return (y * g.astype(jnp.float32)).astype(x.dtype)


def modulate(h, shift, scale):
    return h * (1 + scale[:, None, :]) + shift[:, None, :]


def column_layout():
    layout, off = {}, 0
    for name, size in IN_SPLITS:
        layout[name] = (off, size)
        off += size
    return layout


def split_columns(p):
    return {n: p[..., o:o + s] for n, (o, s) in column_layout().items()}


def heads(t, n_heads):
    return t.reshape(t.shape[0], t.shape[1], n_heads, -1)


def axial_rope(x):
    n, d = x.shape[1], x.shape[-1]
    nf = d // 4
    t = jnp.arange(n)
    row = (t // GRID_W).astype(jnp.float32)
    col = (t % GRID_W).astype(jnp.float32)
    inv = ROPE_BASE ** (-jnp.arange(nf, dtype=jnp.float32) / nf)
    ang = jnp.concatenate([row[:, None] * inv, col[:, None] * inv], axis=-1)
    cos = jnp.cos(ang)[None, :, None, :]
    sin = jnp.sin(ang)[None, :, None, :]
    xf = x.astype(jnp.float32)
    x1, x2 = xf[..., :d // 2], xf[..., d // 2:]
    return jnp.concatenate([x1 * cos - x2 * sin, x1 * sin + x2 * cos], axis=-1).astype(x.dtype)


def retention_final_state(k, v, log_gamma):
    n = k.shape[2]
    w = jnp.exp((n - 1 - jnp.arange(n, dtype=jnp.float32))[None, :] * log_gamma[:, None])
    return jnp.einsum('bhnd,bhne->bhde', k * w[None, :, :, None], v)


def retention_chunkwise(q, k, v, log_gamma, state0, include_diag):
    b, h, n, dk = q.shape
    dv = v.shape[-1]
    ch = RET_CHUNK
    nc = n // ch
    qc = q.reshape(b, h, nc, ch, dk)
    kc = k.reshape(b, h, nc, ch, dk)
    vc = v.reshape(b, h, nc, ch, dv)
    i = jnp.arange(ch, dtype=jnp.float32)
    diff = i[:, None] - i[None, :]
    lg = log_gamma[:, None, None]
    dmat = jnp.where(diff >= (0.0 if include_diag else 1.0), jnp.exp(jnp.maximum(diff, 0.0) * lg), 0.0)
    inner = jnp.einsum('bhnid,bhnjd->bhnij', qc, kc) * dmat[None, :, None]
    inner = jnp.einsum('bhnij,bhnje->bhnie', inner, vc)
    zeta = jnp.exp((ch - 1 - i)[None, :] * log_gamma[:, None])
    s = jnp.einsum('bhnjd,bhnje->nbhde', kc * zeta[None, :, None, :, None], vc)
    chunk_decay = jnp.exp(ch * log_gamma)[None, :, None, None]

    def step(r, s_n):
        return chunk_decay * r + s_n, r

    _, r_prev = lax.scan(step, state0, s)
    xi = jnp.exp((i + 1)[None, :] * log_gamma[:, None])
    cross = jnp.einsum('bhnid,nbhde->bhnie', qc, r_prev) * xi[None, :, None, :, None]
    return (inner + cross).reshape(b, h, n, dv)


def bidirectional_retention(q, k, v, lg_f, lg_b, state_f, state_b):
    flip = lambda t: jnp.flip(t, axis=2)
    fwd = retention_chunkwise(q, k, v, lg_f, state_f, True)
    bwd = retention_chunkwise(flip(q), flip(k), flip(v), lg_b, state_b, False)
    return fwd + flip(bwd)


def retention_readout(y, g, gn_gain):
    mu = jnp.mean(y, axis=-1, keepdims=True)
    var = jnp.mean(jnp.square(y - mu), axis=-1, keepdims=True)
    y = (y - mu) * lax.rsqrt(var + EPS)
    y = jnp.swapaxes(y, 1, 2).reshape(y.shape[0], y.shape[2], -1) * gn_gain.astype(jnp.float32)
    return (y * jax.nn.silu(g.astype(jnp.float32))).astype(g.dtype)


def retention_mixer(px, pc, decay_logit, gn_gain, need_ctx):
    lg_f = jax.nn.log_sigmoid(decay_logit[0].astype(jnp.float32))
    lg_b = jax.nn.log_sigmoid(decay_logit[1].astype(jnp.float32))
    to_bhnd = lambda t: jnp.swapaxes(t, 1, 2).astype(jnp.float32)
    k_scale = RET_DK ** -0.5
    qx = to_bhnd(axial_rope(heads(px['q_r'], RET_HEADS)))
    kx = to_bhnd(axial_rope(heads(px['k_r'], RET_HEADS))) * k_scale
    vx = to_bhnd(heads(px['v_r'], RET_HEADS))
    kc = to_bhnd(heads(pc['k_r'], RET_HEADS)) * k_scale
    vc = to_bhnd(heads(pc['v_r'], RET_HEADS))
    flip = lambda t: jnp.flip(t, axis=2)
    state_f = retention_final_state(kc, vc, lg_f)
    state_b = retention_final_state(flip(kc), flip(vc), lg_b)
    yx = retention_readout(bidirectional_retention(qx, kx, vx, lg_f, lg_b, state_f, state_b), px['g_r'], gn_gain)
    yc = None
    if need_ctx:
        qc = to_bhnd(heads(pc['q_r'], RET_HEADS))
        zero = jnp.zeros(state_f.shape, jnp.float32)
        yc = retention_readout(bidirectional_retention(qc, kc, vc, lg_f, lg_b, zero, zero), pc['g_r'], gn_gain)
    return yx, yc


def context_self_attention(q, k, v, sink):
    b, l, hq, d = q.shape
    hkv = k.shape[2]
    g = hq // hkv
    qg = q.reshape(b, l, hkv, g, d)
    s = jnp.einsum('bqhgd,bkhd->bhgqk', qg, k).astype(jnp.float32) * d ** -0.5
    if sink is not None:
        s_sink = jnp.broadcast_to(sink.astype(jnp.float32).reshape(1, hkv, g, 1, 1), s.shape[:-1] + (1,))
        s = jnp.concatenate([s, s_sink], axis=-1)
    p = jax.nn.softmax(s, axis=-1)[..., :l].astype(v.dtype)
    return jnp.einsum('bhgqk,bkhd->bqhgd', p, v).reshape(b, l, hq * d)


def banded_window_attention(q, k, v, kc, vc, sink):
    b, n, hq, d = q.shape
    hkv = k.shape[2]
    g = hq // hkv
    l = kc.shape[1]
    nb = n // WIN_BLOCK
    pad = ((0, 0), (WIN_BLOCK, WIN_BLOCK), (0, 0), (0, 0))

    def band(t):
        tb = jnp.pad(t, pad).reshape(b, nb + 2, WIN_BLOCK, hkv, d)
        return jnp.concatenate([tb[:, :-2], tb[:, 1:-1], tb[:, 2:]], axis=2)

    kb, vb = band(k), band(v)
    qb = q.reshape(b, nb, WIN_BLOCK, hkv, g, d)
    scale = d ** -0.5
    s_loc = jnp.einsum('bnqhgd,bnkhd->bnhgqk', qb, kb).astype(jnp.float32) * scale
    s_ctx = jnp.einsum('bnqhgd,blhd->bnhgql', qb, kc).astype(jnp.float32) * scale
    qpos = jnp.arange(WIN_BLOCK)[:, None]
    kpos = jnp.arange(3 * WIN_BLOCK)[None, :] - WIN_BLOCK
    kabs = jnp.arange(nb)[:, None, None] * WIN_BLOCK + kpos[None]
    valid = (jnp.abs(kpos - qpos) <= WINDOW)[None] & (kabs >= 0) & (kabs < n)
    s_loc = jnp.where(valid[None, :, None, None], s_loc, NEG_INF)
    s_sink = jnp.broadcast_to(sink.astype(jnp.float32).reshape(1, 1, hkv, g, 1, 1), s_ctx.shape[:-1] + (1,))
    p = jax.nn.softmax(jnp.concatenate([s_loc, s_ctx, s_sink], axis=-1), axis=-1)
    kl = 3 * WIN_BLOCK
    p_loc = p[..., :kl].astype(v.dtype)
    p_ctx = p[..., kl:kl + l].astype(v.dtype)
    out = jnp.einsum('bnhgqk,bnkhd->bnqhgd', p_loc, vb) + jnp.einsum('bnhgql,blhd->bnqhgd', p_ctx, vc)
    return out.reshape(b, n, hq * d)


def window_mixer(px, pc, sink, need_ctx):
    q = axial_rope(heads(px['q_w'], WIN_HEADS))
    k = axial_rope(heads(px['k_w'], WIN_KV_HEADS))
    v = heads(px['v_w'], WIN_KV_HEADS)
    kc = heads(pc['k_w'], WIN_KV_HEADS)
    vc = heads(pc['v_w'], WIN_KV_HEADS)
    yx = banded_window_attention(q, k, v, kc, vc, sink)
    yc = context_self_attention(heads(pc['q_w'], WIN_HEADS), kc, vc, sink) if need_ctx else None
    return yx, yc


def neighbourhood_attention(q, k, v, kc, vc, rpb):
    b, n, h, d = q.shape
    rows = n // GRID_W
    kr = min(NA_ROWS_MAX, rows)
    r = jnp.arange(rows)
    row_idx = jnp.clip(r - kr // 2, 0, rows - kr)[:, None] + jnp.arange(kr)[None, :]
    cq = jnp.arange(GRID_W)
    col_start = jnp.clip(cq - NA_COLS // 2, 0, GRID_W - NA_COLS)
    col_valid = (cq[None, :] >= col_start[:, None]) & (cq[None, :] < col_start[:, None] + NA_COLS)
    ri = row_idx - r[:, None] + NA_ROWS_MAX - 1
    ci = jnp.clip(cq[None, :] - cq[:, None], -(NA_COLS - 1), NA_COLS - 1) + NA_COLS - 1
    bias = rpb.astype(jnp.float32)[:, ri[:, None, :, None], ci[None, :, None, :]]
    qg = q.reshape(b, rows, GRID_W, h, d)
    k_rows = k.reshape(b, rows, GRID_W, h, d)[:, row_idx]
    v_rows = v.reshape(b, rows, GRID_W, h, d)[:, row_idx]
    scale = d ** -0.5
    s_loc = jnp.einsum('brchd,brkwhd->bhrckw', qg, k_rows).astype(jnp.float32) * scale + bias[None]
    s_loc = jnp.where(col_valid[:, None, :], s_loc, NEG_INF).reshape(b, h, rows, GRID_W, kr * GRID_W)
    s_ctx = jnp.einsum('brchd,blhd->bhrcl', qg, kc).astype(jnp.float32) * scale
    p = jax.nn.softmax(jnp.concatenate([s_loc, s_ctx], axis=-1), axis=-1)
    nl = kr * GRID_W
    p_loc = p[..., :nl].reshape(b, h, rows, GRID_W, kr, GRID_W).astype(v.dtype)
    p_ctx = p[..., nl:].astype(v.dtype)
    out = jnp.einsum('bhrckw,brkwhd->brchd', p_loc, v_rows) + jnp.einsum('bhrcl,blhd->brchd', p_ctx, vc)
    return out.reshape(b, n, h * d)


def na_mixer(px, pc, rpb, need_ctx):
    q = heads(px['q_n'], NA_HEADS)
    k = heads(px['k_n'], NA_HEADS)
    v = heads(px['v_n'], NA_HEADS)
    kc = heads(pc['k_n'], NA_HEADS)
    vc = heads(pc['v_n'], NA_HEADS)
    yx = neighbourhood_attention(q, k, v, kc, vc, rpb)
    yc = context_self_attention(heads(pc['q_n'], NA_HEADS), kc, vc, None) if need_ctx else None
    return yx, yc


def gated_merge(gates, ya, yb, yc, w_ret, w_win, w_na, w_out):
    ga, gb, gc = jnp.split(jax.nn.sigmoid(gates), N_BRANCHES, axis=-1)
    return (ga * (ya @ w_ret) + gb * (yb @ w_win) + gc * (yc @ w_na)) @ w_out


def hybrid_mixer(hx, hc, w_in, decay_logit, gn_gain, w_ret, sink, w_win, rpb, w_na, w_out, need_ctx):
    px = split_columns(hx @ w_in)
    if need_ctx:
        pc = split_columns(hc @ w_in)
    else:
        lay = column_layout()
        pc = {nm: hc @ w_in[:, lay[nm][0]:lay[nm][0] + lay[nm][1]] for nm in KV_NAMES}
    ya, yca = retention_mixer(px, pc, decay_logit, gn_gain, need_ctx)
    yb, ycb = window_mixer(px, pc, sink, need_ctx)
    yc, ycc = na_mixer(px, pc, rpb, need_ctx)
    mx = gated_merge(px['gates'], ya, yb, yc, w_ret, w_win, w_na, w_out)
    mc = gated_merge(pc['gates'], yca, ycb, ycc, w_ret, w_win, w_na, w_out) if need_ctx else None
    return mx, mc


def expert_choice_ffn(h, w_router, w_gate, w_up, w_down):
    b, n, d = h.shape
    cap = CAPACITY_FACTOR * n // N_EXPERTS
    aff = jax.nn.softmax((h @ w_router).astype(jnp.float32), axis=-1)
    gate, idx = lax.top_k(jnp.swapaxes(aff, 1, 2), cap)
    bidx = jnp.arange(b)[:, None, None]
    xe = h[bidx, idx]
    hid = jax.nn.silu(jnp.einsum('becd,edf->becf', xe, w_gate)) * jnp.einsum('becd,edf->becf', xe, w_up)
    ye = jnp.einsum('becf,efd->becd', hid, w_down) * gate[..., None].astype(h.dtype)
    return jnp.zeros_like(h).at[bidx, idx].add(ye)


def setup_inputs(seed: int = 0) -> dict:
    key = jax.random.key(seed)
    ks = jax.random.split(key, 24)
    f32 = jnp.float32
    d = D_MODEL
    nrm = lambda k, shape, s: jax.random.normal(k, shape, f32) * s
    gamma0 = 1.0 - 2.0 ** (-5.0 - np.arange(RET_HEADS))
    logit0 = jnp.asarray(np.log(gamma0 / (1.0 - gamma0)).astype(np.float32))
    return {
        'x': nrm(ks[0], (BATCH, SEQ, d), 1.0),
        'c': nrm(ks[1], (BATCH, d), 1.0),
        'ctx': nrm(ks[2], (BATCH, CTX_LEN, d), 1.0),
        'c_ctx': nrm(ks[3], (d,), 1.0),
        'w_mod': nrm(ks[4], (DEPTH, d, N_MOD * d), 0.5 * d ** -0.5),
        'b_mod': nrm(ks[5], (DEPTH, N_MOD * d), 0.02),
        'g_mix': 1.0 + nrm(ks[6], (DEPTH, d), 0.02),
        'g_ffn': 1.0 + nrm(ks[7], (DEPTH, d), 0.02),
        'w_in': nrm(ks[8], (DEPTH, d, IN_COLS), d ** -0.5),
        'ret_decay_logit': logit0[None, None, :] + nrm(ks[9], (DEPTH, 2, RET_HEADS), 0.1),
        'ret_gn': 1.0 + nrm(ks[10], (DEPTH, RET_V), 0.02),
        'w_ret': nrm(ks[11], (DEPTH, RET_V, d), RET_V ** -0.5),
        'win_sink': nrm(ks[12], (DEPTH, WIN_HEADS), 1.0),
        'w_win': nrm(ks[13], (DEPTH, WIN_Q, d), WIN_Q ** -0.5),
        'na_rpb': nrm(ks[14], (DEPTH, NA_HEADS, 2 * NA_ROWS_MAX - 1, 2 * NA_COLS - 1), 0.1),
        'w_na': nrm(ks[15], (DEPTH, NA_W, d), NA_W ** -0.5),
        'w_out': nrm(ks[16], (DEPTH, d, d), d ** -0.5),
        'w_router': nrm(ks[17], (DEPTH, d, N_EXPERTS), d ** -0.5),
        'w_exp_gate': nrm(ks[18], (DEPTH, N_EXPERTS, d, EXPERT_HIDDEN), d ** -0.5),
        'w_exp_up': nrm(ks[19], (DEPTH, N_EXPERTS, d, EXPERT_HIDDEN), d ** -0.5),
        'w_exp_down': nrm(ks[20], (DEPTH, N_EXPERTS, EXPERT_HIDDEN, d), EXPERT_HIDDEN ** -0.5),
        'g_final': 1.0 + nrm(ks[21], (d,), 0.02),
    }


def reference(x, c, ctx, c_ctx, w_mod, b_mod, g_mix, g_ffn, w_in, ret_decay_logit, ret_gn, w_ret,
              win_sink, w_win, na_rpb, w_na, w_out, w_router, w_exp_gate, w_exp_up, w_exp_down, g_final):
    d = x.shape[-1]
    sc = jax.nn.silu(c)
    scc = jax.nn.silu(c_ctx)[None]
    for layer in range(DEPTH):
        need_ctx = layer < DEPTH - 1
        mod_x = (sc @ w_mod[layer] + b_mod[layer]).reshape(-1, N_MOD, d)
        n_mod_c = N_MOD if need_ctx else 2
        mod_c = (scc @ w_mod[layer][:, :n_mod_c * d] + b_mod[layer][:n_mod_c * d]).reshape(1, n_mod_c, d)
        hx = modulate(rms_norm(x, g_mix[layer]), mod_x[:, 0], mod_x[:, 1])
        hc = modulate(rms_norm(ctx, g_mix[layer]), mod_c[:, 0], mod_c[:, 1])
        mx, mc = hybrid_mixer(hx, hc, w_in[layer], ret_decay_logit[layer], ret_gn[layer], w_ret[layer],
                              win_sink[layer], w_win[layer], na_rpb[layer], w_na[layer], w_out[layer], need_ctx)
        x = x + mod_x[:, 2, None] * mx
        hx = modulate(rms_norm(x, g_ffn[layer]), mod_x[:, 3], mod_x[:, 4])
        x = x + mod_x[:, 5, None] * expert_choice_ffn(hx, w_router[layer], w_exp_gate[layer],
                                                      w_exp_up[layer], w_exp_down[layer])
        if need_ctx:
            ctx = ctx + mod_c[:, 2, None] * mc
            hc = modulate(rms_norm(ctx, g_ffn[layer]), mod_c[:, 3], mod_c[:, 4])
            ctx = ctx + mod_c[:, 5, None] * expert_choice_ffn(hc, w_router[layer], w_exp_gate[layer],
                                                              w_exp_up[layer], w_exp_down[layer])
    return rms_norm(x, g_final)
```

```python
import functools

import jax
import jax.numpy as jnp
from jax import lax
from jax.experimental import pallas as pl
from jax.experimental.pallas import tpu as pltpu

F32, BF16, I32 = jnp.float32, jnp.bfloat16, jnp.int32
HIGHEST = lax.Precision.HIGHEST

GRID_W = 64
RET_HEADS, RET_D, RET_CHUNK = 4, 128, 128
WIN_HEADS, WIN_KV_HEADS, HEAD_D, WINDOW, WIN_BLOCK = 8, 2, 64, 128, 128
NA_HEADS, NA_ROWS, NA_COLS = 8, 8, 16
N_EXPERTS, CAPACITY_FACTOR = 16, 2
N_MOD = 6
ROPE_BASE = 10000.0
EPS = 1e-6
NEG_INF = -1e30

LANES = 128
SUBLANES = 8
BF16_ROWS = 16
VMEM_LIMIT = 56 * 1024 * 1024

SEG = 512
T_QR, T_KR, T_VR, T_GR, T_QW, T_QN, T_KN, T_VN, T_GATES, T_KVW = 0, 1, 2, 3, 4, 5, 6, 7, 8, 14
P_TILES = 15
P_COLS = P_TILES * SEG
C_KW = T_KVW * SEG // LANES
C_VW = C_KW + 1
NA_SLAB_ROWS = 10
COMBINE_WINDOW = 144


def _cparams(sem):
    return pltpu.CompilerParams(dimension_semantics=sem, vmem_limit_bytes=VMEM_LIMIT)


def _dot(a, b):
    return jnp.dot(a, b, preferred_element_type=F32)


def _dot_nt(a, b, precision=None):
    return lax.dot_general(a, b, (((1,), (1,)), ((), ())), precision=precision, preferred_element_type=F32)


def _dot_tn(a, b, precision=None):
    return lax.dot_general(a, b, (((0,), (0,)), ((), ())), precision=precision, preferred_element_type=F32)


def _iota(shape, dim, dtype=I32):
    return lax.broadcasted_iota(I32, shape, dim).astype(dtype)


def _mod_kernel(s_ref, w_ref, b_ref, o_ref):
    s = s_ref[...]
    s = s * jax.nn.sigmoid(s)
    o_ref[...] = jnp.dot(s, w_ref[...], precision=HIGHEST, preferred_element_type=F32) + b_ref[...]


def _modulation(cc, w_mod, b_mod):
    depth, d, md = w_mod.shape
    tn = 1536
    return pl.pallas_call(
        _mod_kernel,
        grid=(depth, md // tn),
        in_specs=[
            pl.BlockSpec((SUBLANES, d), lambda l, j: (0, 0)),
            pl.BlockSpec((None, d, tn), lambda l, j: (l, 0, j)),
            pl.BlockSpec((None, 1, tn), lambda l, j: (l, 0, j)),
        ],
        out_specs=pl.BlockSpec((None, SUBLANES, tn), lambda l, j: (l, 0, j)),
        out_shape=jax.ShapeDtypeStruct((depth, SUBLANES, md), F32),
        compiler_params=_cparams(("arbitrary", "arbitrary")),
        name="modulation",
    )(cc, w_mod, b_mod.reshape(depth, 1, md))


def _norm_mod(x, g, shift, scale):
    ms = jnp.mean(x * x, axis=-1, keepdims=True)
    y = x * lax.rsqrt(ms + EPS) * g
    return y * (1.0 + scale) + shift


def _inproj_kernel(x_ref, g_ref, sh_ref, sc_ref, c128_ref, s128_ref, c64_ref, sa_ref, sb_ref, w_ref, o_ref, hx_ref, *, rope):
    j = pl.program_id(1)

    @pl.when(j == 0)
    def _():
        hx_ref[...] = _norm_mod(x_ref[...], g_ref[...], sh_ref[...], sc_ref[...]).astype(BF16)

    acc = _dot(hx_ref[...], w_ref[...])
    groups = [acc[:, k * LANES:(k + 1) * LANES] for k in range(SEG // LANES)]

    def rope128(a):
        return a * c128_ref[...] + pltpu.roll(a, 64, 1) * s128_ref[...]

    def rope64(a):
        return a * c64_ref[...] + pltpu.roll(a, 96, 1) * sa_ref[...] + pltpu.roll(a, 32, 1) * sb_ref[...]

    def put(gs):
        o_ref[...] = jnp.concatenate(gs, axis=1).astype(BF16)

    k_scale = RET_D ** -0.5
    a_scale = HEAD_D ** -0.5
    ident = lambda a: a
    f_qr = rope128 if rope else ident
    f_kr = (lambda a: rope128(a) * k_scale) if rope else (lambda a: a * k_scale)
    f_qw = (lambda a: rope64(a) * a_scale) if rope else (lambda a: a * a_scale)
    f_kw = rope64 if rope else ident

    @pl.when(j == T_QR)
    def _():
        put([f_qr(a) for a in groups])

    @pl.when(j == T_KR)
    def _():
        put([f_kr(a) for a in groups])

    @pl.when(j == T_QW)
    def _():
        put([f_qw(a) for a in groups])

    @pl.when(j == T_QN)
    def _():
        put([a * a_scale for a in groups])

    @pl.when(j == T_KVW)
    def _():
        put([f_kw(groups[0])] + groups[1:])

    plain = (j != T_QR) & (j != T_KR) & (j != T_QW) & (j != T_QN) & (j != T_KVW)

    @pl.when(plain)
    def _():
        put(groups)


def _inproj(x, g, shift, scale, tabs, w, *, rope, tm):
    n, d = x.shape
    vec = lambda: pl.BlockSpec((1, d), lambda i, j: (0, 0))
    tab = lambda: pl.BlockSpec((tm, LANES), lambda i, j: (i, 0))
    return pl.pallas_call(
        functools.partial(_inproj_kernel, rope=rope),
        grid=(n // tm, P_TILES),
        in_specs=[pl.BlockSpec((tm, d), lambda i, j: (i, 0)), vec(), vec(), vec(),
                  tab(), tab(), tab(), tab(), tab(),
                  pl.BlockSpec((d, SEG), lambda i, j: (0, j))],
        out_specs=pl.BlockSpec((tm, SEG), lambda i, j: (i, j)),
        out_shape=jax.ShapeDtypeStruct((n, P_COLS), BF16),
        scratch_shapes=[pltpu.VMEM((tm, d), BF16)],
        compiler_params=_cparams(("arbitrary", "arbitrary")),
        name="inproj_rope" if rope else "inproj_ctx",
    )(x, g, shift, scale, *tabs, w)


def _ret_kernel(lg_ref, qf_ref, kf_ref, vf_ref, qb_ref, kb_ref, vb_ref, kc_ref, vc_ref, yf_ref, yb_ref,
                rf_ref, rb_ref, tab_ref, *, zero_init):
    c = pl.program_id(0)
    ch = RET_CHUNK
    dh = RET_D
    states = (rf_ref, rb_ref)

    @pl.when(c == 0)
    def _():
        lg = jax.nn.log_sigmoid(lg_ref[...])
        ii = _iota((ch, ch), 0, F32)
        jj = _iota((ch, ch), 1, F32)
        n_ctx = kc_ref.shape[0]
        mm = _iota((n_ctx, dh), 0, F32)
        for d in range(2):
            for h in range(RET_HEADS):
                row = lg[RET_HEADS * d + h:RET_HEADS * d + h + 1, :]
                l = jnp.broadcast_to(row, (ch, ch))
                if d == 0:
                    diff = ii - jj
                    dmat = jnp.where(diff >= 0.0, jnp.exp(jnp.maximum(diff, 0.0) * l), 0.0)
                    zeta = jnp.exp((ch - 1.0 - ii) * l)
                    xi = jnp.exp((ii + 1.0) * l)
                else:
                    diff = jj - ii
                    dmat = jnp.where(diff >= 1.0, jnp.exp(jnp.maximum(diff, 0.0) * l), 0.0)
                    zeta = jnp.exp(ii * l)
                    xi = jnp.exp((ch - ii) * l)
                tab_ref[d, h, 0] = dmat
                tab_ref[d, h, 1] = zeta
                tab_ref[d, h, 2] = xi
                tab_ref[d, h, 3] = jnp.exp(ch * l)
                if zero_init:
                    states[d][h] = jnp.zeros((dh, dh), F32)
                else:
                    lc = jnp.broadcast_to(row, (n_ctx, dh))
                    w = jnp.exp((n_ctx - 1.0 - mm) * lc) if d == 0 else jnp.exp(mm * lc)
                    sl = slice(h * dh, (h + 1) * dh)
                    kw = (kc_ref[:, sl].astype(F32) * w).astype(BF16)
                    states[d][h] = _dot_tn(kw, vc_ref[:, sl])

    for d, (q_ref, k_ref, v_ref, y_ref) in enumerate(((qf_ref, kf_ref, vf_ref, yf_ref), (qb_ref, kb_ref, vb_ref, yb_ref))):
        for h in range(RET_HEADS):
            sl = slice(h * dh, (h + 1) * dh)
            q, k, v = q_ref[:, sl], k_ref[:, sl], v_ref[:, sl]
            r = states[d][h]
            a = (_dot_nt(q, k) * tab_ref[d, h, 0]).astype(BF16)
            inner = _dot(a, v)
            cross = _dot(q, r.astype(BF16)) * tab_ref[d, h, 2]
            y_ref[:, sl] = inner + cross
            kz = (k.astype(F32) * tab_ref[d, h, 1]).astype(BF16)
            states[d][h] = tab_ref[d, h, 3] * r + _dot_tn(kz, v)


def _retention(p, p_ctx, lg_rows, *, zero_init):
    n = p.shape[0]
    nc = n // RET_CHUNK
    w = RET_HEADS * RET_D
    fwd = lambda t: pl.BlockSpec((RET_CHUNK, w), lambda c: (c, t))
    bwd = lambda t: pl.BlockSpec((RET_CHUNK, w), lambda c: (nc - 1 - c, t))
    ctx = lambda t: pl.BlockSpec((p_ctx.shape[0], w), lambda c: (0, t))
    return pl.pallas_call(
        functools.partial(_ret_kernel, zero_init=zero_init),
        grid=(nc,),
        in_specs=[pl.BlockSpec((SUBLANES, LANES), lambda c: (0, 0)),
                  fwd(T_QR), fwd(T_KR), fwd(T_VR), bwd(T_QR), bwd(T_KR), bwd(T_VR), ctx(T_KR), ctx(T_VR)],
        out_specs=[pl.BlockSpec((RET_CHUNK, w), lambda c: (c, 0)), pl.BlockSpec((RET_CHUNK, w), lambda c: (nc - 1 - c, 0))],
        out_shape=[jax.ShapeDtypeStruct((n, w), F32), jax.ShapeDtypeStruct((n, w), F32)],
        scratch_shapes=[pltpu.VMEM((RET_HEADS, RET_D, RET_D), F32), pltpu.VMEM((RET_HEADS, RET_D, RET_D), F32),
                        pltpu.VMEM((2, RET_HEADS, 4, RET_CHUNK, RET_CHUNK), F32)],
        compiler_params=_cparams(("arbitrary",)),
        name="retention_ctx" if zero_init else "retention",
    )(lg_rows, p, p, p, p, p, p, p_ctx, p_ctx)


def _half_mask(x, half):
    lane = _iota(x.shape, 1)
    keep = (lane < HEAD_D) if half == 0 else (lane >= HEAD_D)
    return jnp.where(keep, x, jnp.zeros_like(x))


def _softmax_pv(s, v, extra=None):
    m = jnp.max(s, axis=1, keepdims=True)
    if extra is not None:
        m = jnp.maximum(m, extra)
    p = jnp.exp(s - m)
    den = jnp.sum(p, axis=1, keepdims=True)
    if extra is not None:
        den = den + jnp.exp(extra - m)
    return _dot(p.astype(BF16), v) / den


def _win_kernel(sink_ref, q_ref, kp_ref, kc_ref, kn_ref, vp_ref, vc_ref, vn_ref, kx_ref, vx_ref, o_ref, *, n_tokens):
    nb = pl.program_id(0)
    wb = WIN_BLOCK
    k_all = jnp.concatenate([kp_ref[...], kc_ref[...], kn_ref[...], kx_ref[...]], axis=0)
    v_all = jnp.concatenate([vp_ref[...], vc_ref[...], vn_ref[...], vx_ref[...]], axis=0)
    n_cols = SEG // LANES
    qs = jnp.concatenate([q_ref[:, c * LANES:(c + 1) * LANES] for c in range(n_cols)], axis=0)
    m_rows = n_cols * wb
    qpos = _iota((m_rows, 3 * wb), 0) & (wb - 1)
    kpos = _iota((m_rows, 3 * wb), 1) - wb
    kabs = nb * wb + kpos
    valid = (jnp.abs(kpos - qpos) <= WINDOW) & (kabs >= 0) & (kabs < n_tokens)
    outs = []
    for g in range(WIN_KV_HEADS):
        s = _dot_nt(qs, _half_mask(k_all, g))
        s = jnp.concatenate([jnp.where(valid, s[:, :3 * wb], NEG_INF), s[:, 3 * wb:]], axis=1)
        sink = jnp.concatenate([jnp.full((wb, 1), sink_ref[n_cols * g + c], F32) for c in range(n_cols)], axis=0)
        outs.append(_softmax_pv(s, _half_mask(v_all, g), sink))
    o = outs[0] + outs[1]
    o_ref[...] = jnp.concatenate([o[c * wb:(c + 1) * wb, :] for c in range(n_cols)], axis=1).astype(BF16)


def _window_attention(p, p_ctx, sink):
    n = p.shape[0]
    nb = n // WIN_BLOCK
    l = p_ctx.shape[0]
    blk = lambda col, f: pl.BlockSpec((WIN_BLOCK, LANES), lambda i: (f(i), col))
    prev = lambda i: jnp.maximum(i - 1, 0)
    cur = lambda i: i
    nxt = lambda i: jnp.minimum(i + 1, nb - 1)
    return pl.pallas_call(
        functools.partial(_win_kernel, n_tokens=n),
        grid=(nb,),
        in_specs=[pl.BlockSpec(memory_space=pltpu.SMEM),
                  pl.BlockSpec((WIN_BLOCK, SEG), lambda i: (i, T_QW)),
                  blk(C_KW, prev), blk(C_KW, cur), blk(C_KW, nxt),
                  blk(C_VW, prev), blk(C_VW, cur), blk(C_VW, nxt),
                  pl.BlockSpec((l, LANES), lambda i: (0, C_KW)), pl.BlockSpec((l, LANES), lambda i: (0, C_VW))],
        out_specs=pl.BlockSpec((WIN_BLOCK, SEG), lambda i: (i, 0)),
        out_shape=jax.ShapeDtypeStruct((n, SEG), BF16),
        compiler_params=_cparams(("arbitrary",)),
        name="window_attention",
    )(sink, p, p, p, p, p, p, p, p_ctx, p_ctx)


def _na_slab_start(s, half_rows):
    return jnp.clip(s - 2, 0, half_rows - NA_SLAB_ROWS // 2)


def _na_kernel(q_ref, k0, k1, k2, k3, k4, v0, v1, v2, v3, v4, kx_ref, vx_ref, bias_ref, o_ref, *, rows):
    s = pl.program_id(0)
    w = GRID_W
    nq = 2 * w
    nk = NA_SLAB_ROWS * w
    r0 = 2 * s
    sb = 2 * _na_slab_start(s, rows // 2)
    delta = sb - r0
    k_refs = (k0, k1, k2, k3, k4)
    v_refs = (v0, v1, v2, v3, v4)
    sub = _iota((nq, nk), 0)
    lane = _iota((nq, nk), 1)
    shift = w.bit_length() - 1
    r_q = r0 + (sub >> shift)
    c_q = sub & (w - 1)
    r_k = sb + (lane >> shift)
    c_k = lane & (w - 1)
    r_start = jnp.clip(r_q - NA_ROWS // 2, 0, rows - NA_ROWS)
    c_start = jnp.clip(c_q - NA_COLS // 2, 0, w - NA_COLS)
    valid = (r_k >= r_start) & (r_k < r_start + NA_ROWS) & (c_k >= c_start) & (c_k < c_start + NA_COLS)
    for pair in range(NA_HEADS // 2):
        sl = slice(pair * LANES, (pair + 1) * LANES)
        q = q_ref[:, sl]
        k_all = jnp.concatenate([r[:, sl] for r in k_refs] + [kx_ref[:, sl]], axis=0)
        v_all = jnp.concatenate([r[:, sl] for r in v_refs] + [vx_ref[:, sl]], axis=0)
        out = None
        for u in range(2):
            h = 2 * pair + u
            sc = _dot_nt(q, _half_mask(k_all, u))
            bias = jnp.concatenate(
                [jnp.concatenate([bias_ref[h, delta + 2 * i - a + NA_ROWS + 1] for i in range(NA_SLAB_ROWS // 2)], axis=1)
                 for a in range(2)], axis=0)
            s_loc = jnp.where(valid, sc[:, :nk] + bias, NEG_INF)
            o = _softmax_pv(jnp.concatenate([s_loc, sc[:, nk:]], axis=1), _half_mask(v_all, u))
            out = o if out is None else out + o
        o_ref[:, sl] = out.astype(BF16)


def _na_attention(p, p_ctx, bias_tab):
    n = p.shape[0]
    rows = n // GRID_W
    steps = rows // 2
    l = p_ctx.shape[0]
    slab = lambda t, i: pl.BlockSpec((2 * GRID_W, SEG), lambda s: (_na_slab_start(s, steps) + i, t))
    nslab = NA_SLAB_ROWS // 2
    return pl.pallas_call(
        functools.partial(_na_kernel, rows=rows),
        grid=(steps,),
        in_specs=[pl.BlockSpec((2 * GRID_W, SEG), lambda s: (s, T_QN))]
        + [slab(T_KN, i) for i in range(nslab)] + [slab(T_VN, i) for i in range(nslab)]
        + [pl.BlockSpec((l, SEG), lambda s: (0, T_KN)), pl.BlockSpec((l, SEG), lambda s: (0, T_VN)),
           pl.BlockSpec(bias_tab.shape, lambda s: (0, 0, 0, 0))],
        out_specs=pl.BlockSpec((2 * GRID_W, SEG), lambda s: (s, 0)),
        out_shape=jax.ShapeDtypeStruct((n, SEG), BF16),
        compiler_params=_cparams(("arbitrary",)),
        name="neighbourhood_attention",
    )(p, *([p] * (2 * nslab)), p_ctx, p_ctx, bias_tab)


def _na_bias_table(rpb):
    d = jnp.arange(2 * NA_ROWS + 2)[:, None, None] - (NA_ROWS + 1)
    lane = jnp.arange(2 * GRID_W)[None, None, :]
    cq = jnp.arange(GRID_W)[None, :, None]
    ri = jnp.clip(d + lane // GRID_W + NA_ROWS - 1, 0, 2 * NA_ROWS - 2)
    ci = jnp.clip(lane % GRID_W - cq, -(NA_COLS - 1), NA_COLS - 1) + NA_COLS - 1
    ri, ci = jnp.broadcast_arrays(ri, ci)
    return rpb.astype(F32)[:, ri, ci]


def _ctx_attn_kernel(sink_ref, p_ref, ow_ref, on_ref):
    l = p_ref.shape[0]
    n_cols = SEG // LANES
    k_all = p_ref[:, T_KVW * SEG:T_KVW * SEG + LANES]
    v_all = p_ref[:, T_KVW * SEG + LANES:T_KVW * SEG + 2 * LANES]
    qs = jnp.concatenate([p_ref[:, T_QW * SEG + c * LANES:T_QW * SEG + (c + 1) * LANES] for c in range(n_cols)], axis=0)
    outs = []
    for g in range(WIN_KV_HEADS):
        s = _dot_nt(qs, _half_mask(k_all, g))
        sink = jnp.concatenate([jnp.full((l, 1), sink_ref[n_cols * g + c], F32) for c in range(n_cols)], axis=0)
        outs.append(_softmax_pv(s, _half_mask(v_all, g), sink))
    o = outs[0] + outs[1]
    ow_ref[...] = jnp.concatenate([o[c * l:(c + 1) * l, :] for c in range(n_cols)], axis=1).astype(BF16)
    for pair in range(NA_HEADS // 2):
        sl = lambda t: slice(t * SEG + pair * LANES, t * SEG + (pair + 1) * LANES)
        q, k, v = p_ref[:, sl(T_QN)], p_ref[:, sl(T_KN)], p_ref[:, sl(T_VN)]
        out = None
        for u in range(2):
            o = _softmax_pv(_dot_nt(q, _half_mask(k, u)), _half_mask(v, u))
            out = o if out is None else out + o
        on_ref[:, pair * LANES:(pair + 1) * LANES] = out.astype(BF16)


def _ctx_attention(p_ctx, sink):
    l = p_ctx.shape[0]
    return pl.pallas_call(
        _ctx_attn_kernel,
        in_specs=[pl.BlockSpec(memory_space=pltpu.SMEM), pl.BlockSpec(p_ctx.shape, lambda: (0, 0))],
        out_specs=[pl.BlockSpec((l, SEG), lambda: (0, 0)), pl.BlockSpec((l, SEG), lambda: (0, 0))],
        out_shape=[jax.ShapeDtypeStruct((l, SEG), BF16), jax.ShapeDtypeStruct((l, SEG), BF16)],
        compiler_params=pltpu.CompilerParams(vmem_limit_bytes=VMEM_LIMIT),
        name="context_attention",
    )(sink, p_ctx)


def _merge_kernel(yf_ref, yb_ref, gr_ref, gn_ref, yw_ref, yn_ref, ga_ref, gb_ref, gc_ref,
                  wr_ref, ww_ref, wn_ref, wo_ref, x_ref, m2_ref, gf_ref, m3_ref, m4_ref, wrt_ref,
                  xo_ref, hx_ref, lt_ref):
    y = yf_ref[...] + yb_ref[...]
    parts = []
    for h in range(RET_HEADS):
        yh = y[:, h * RET_D:(h + 1) * RET_D]
        mu = jnp.mean(yh, axis=-1, keepdims=True)
        var = jnp.mean(jnp.square(yh - mu), axis=-1, keepdims=True)
        parts.append((yh - mu) * lax.rsqrt(var + EPS))
    g = gr_ref[...].astype(F32)
    ya = jnp.concatenate(parts, axis=1) * gn_ref[...] * (g * jax.nn.sigmoid(g))
    za = _dot(ya.astype(BF16), wr_ref[...])
    zb = _dot(yw_ref[...], ww_ref[...])
    zc = _dot(yn_ref[...], wn_ref[...])
    sig = lambda r: jax.nn.sigmoid(r[...].astype(F32))
    mix = sig(ga_ref) * za + sig(gb_ref) * zb + sig(gc_ref) * zc
    x_new = x_ref[...] + m2_ref[...] * _dot(mix.astype(BF16), wo_ref[...])
    xo_ref[...] = x_new
    h2 = _norm_mod(x_new, gf_ref[...], m3_ref[...], m4_ref[...])
    for s in range(h2.shape[1] // LANES):
        hx_ref[:, s, :] = h2[:, s * LANES:(s + 1) * LANES]
    lt_ref[...] = _dot_nt(wrt_ref[...], h2, precision=HIGHEST)


def _merge(yf, yb, p, gn, yw, yn, wr, ww, wn, wo, x, m2, gf, m3, m4, wrt, *, tm):
    n, d = x.shape
    row = lambda wdt, t: pl.BlockSpec((tm, wdt), lambda i: (i, t))
    full = lambda a: pl.BlockSpec(a.shape, lambda i: (0,) * a.ndim)
    gate0 = T_GATES * SEG // d
    return pl.pallas_call(
        _merge_kernel,
        grid=(n // tm,),
        in_specs=[row(SEG, 0), row(SEG, 0), row(SEG, T_GR), full(gn), row(SEG, 0), row(SEG, 0),
                  row(d, gate0), row(d, gate0 + 1), row(d, gate0 + 2),
                  full(wr), full(ww), full(wn), full(wo), row(d, 0), full(m2), full(gf), full(m3), full(m4), full(wrt)],
        out_specs=[pl.BlockSpec((tm, d), lambda i: (i, 0)),
                   pl.BlockSpec((tm, d // LANES, LANES), lambda i: (i, 0, 0)),
                   pl.BlockSpec((N_EXPERTS, tm), lambda i: (0, i))],
        out_shape=[jax.ShapeDtypeStruct((n, d), F32), jax.ShapeDtypeStruct((n, d // LANES, LANES), F32),
                   jax.ShapeDtypeStruct((N_EXPERTS, n), F32)],
        compiler_params=_cparams(("arbitrary",)),
        name="merge",
    )(yf, yb, p, gn, yw, yn, p, p, p, wr, ww, wn, wo, x, m2, gf, m3, m4, wrt)


def _route_kernel(lt_ref, idx_ref, gate_ref, pos_ref, goff_ref, aff_ref, thr_ref, *, cap, n_groups):
    e = pl.program_id(0)
    n_exp, ag, _ = lt_ref.shape
    capp = idx_ref.shape[-1]

    @pl.when(e == 0)
    def _():
        lt = lt_ref[...]
        ex = jnp.exp(lt - jnp.max(lt, axis=0, keepdims=True))
        aff = ex / jnp.sum(ex, axis=0, keepdims=True)
        real = _iota(aff.shape, 1) < n_groups
        aff = jnp.where(real, aff, 0.0)
        aff_ref[...] = aff
        bits = pltpu.bitcast(aff, I32)

        def body(k, t):
            cand = t | (jnp.int32(1) << (30 - k))
            cnt = jnp.sum(jnp.sum((bits >= cand).astype(F32), axis=2, keepdims=True), axis=1, keepdims=True)
            return jnp.where(cnt >= float(cap), cand, t)

        t = lax.fori_loop(0, 31, body, jnp.zeros((n_exp, 1, 1), I32))
        thr_ref[...] = jnp.broadcast_to(t, thr_ref.shape)

    a = aff_ref[e]
    bits = pltpu.bitcast(a, I32)
    t = thr_ref[e][0:1, :]
    gt = bits > t
    eq = bits == t
    tri_lane_strict = (_iota((LANES, LANES), 0) < _iota((LANES, LANES), 1)).astype(BF16)
    tri_lane_incl = (_iota((LANES, LANES), 0) <= _iota((LANES, LANES), 1)).astype(BF16)
    tri_grp_strict = (_iota((ag, ag), 1) < _iota((ag, ag), 0)).astype(BF16)
    tri_grp_incl = (_iota((ag, ag), 1) <= _iota((ag, ag), 0)).astype(BF16)

    def total(mask_f):
        return jnp.sum(jnp.sum(mask_f, axis=1, keepdims=True), axis=0, keepdims=True)

    def group_sum(mask_f):
        return jnp.broadcast_to(jnp.sum(mask_f, axis=1, keepdims=True), (ag, LANES)).astype(BF16)

    eq_f = eq.astype(F32)
    need = float(cap) - total(gt.astype(F32))
    rank_eq = _dot(tri_grp_strict, group_sum(eq_f)) + _dot(eq_f.astype(BF16), tri_lane_strict)
    sel = gt | (eq & (rank_eq < need))
    sel_f = sel.astype(F32)
    cl = _dot(sel_f.astype(BF16), tri_lane_incl)
    cg = _dot(tri_grp_incl, group_sum(sel_f))
    goff = cg - jnp.broadcast_to(jnp.sum(sel_f, axis=1, keepdims=True), (ag, LANES))
    pos_ref[...] = jnp.where(sel, goff + cl - 1.0, -1.0)
    diag = _iota((ag, ag), 0) == _iota((ag, ag), 1)
    goff_sq = goff if ag == LANES else goff[:, :ag]
    goff_ref[...] = jnp.sum(jnp.where(diag, goff_sq, 0.0), axis=0, keepdims=True).astype(I32)

    pp = _iota((ag, capp), 1, F32)
    cg_b = jnp.broadcast_to(cg[:, 0:1], (ag, capp))
    below = cg_b <= pp
    grp = jnp.sum(below.astype(F32), axis=0, keepdims=True)
    off = jnp.max(jnp.where(below, cg_b, 0.0), axis=0, keepdims=True)
    onehot = _iota((ag, capp), 0, F32) == grp
    in_grp = _dot_tn(cl.astype(BF16), onehot.astype(BF16))
    local = pp[0:1, :] - off
    lane_of = jnp.sum((in_grp <= local).astype(F32), axis=0, keepdims=True)
    live = pp[0:1, :] < float(cap)
    idx = jnp.where(live, grp * float(LANES) + lane_of, 0.0)
    idx_ref[...] = idx.astype(I32)
    aff_grp = _dot_tn(a, onehot.astype(F32), precision=HIGHEST)
    pick = _iota((LANES, capp), 0, F32) == lane_of
    gate = jnp.sum(jnp.where(pick, aff_grp, 0.0), axis=0, keepdims=True)
    gate_ref[...] = jnp.where(live, gate, 0.0)


def _route(logits_t, *, cap, n_groups):
    n_exp, ag, _ = logits_t.shape
    capp = -(-cap // LANES) * LANES
    return pl.pallas_call(
        functools.partial(_route_kernel, cap=cap, n_groups=n_groups),
        grid=(n_exp,),
        in_specs=[pl.BlockSpec(logits_t.shape, lambda e: (0, 0, 0))],
        out_specs=[pl.BlockSpec((None, 1, capp), lambda e: (e, 0, 0)),
                   pl.BlockSpec((None, 1, capp), lambda e: (e, 0, 0)),
                   pl.BlockSpec((None, ag, LANES), lambda e: (e, 0, 0)),
                   pl.BlockSpec((None, 1, ag), lambda e: (e, 0, 0))],
        out_shape=[jax.ShapeDtypeStruct((n_exp, 1, capp), I32), jax.ShapeDtypeStruct((n_exp, 1, capp), F32),
                   jax.ShapeDtypeStruct((n_exp, ag, LANES), F32), jax.ShapeDtypeStruct((n_exp, 1, ag), I32)],
        scratch_shapes=[pltpu.VMEM((n_exp, ag, LANES), F32), pltpu.VMEM((n_exp, SUBLANES, LANES), I32)],
        compiler_params=_cparams(("arbitrary",)),
        name="route",
    )(logits_t)


def _ffn_kernel(idx_ref, idxn_ref, gate_ref, hx_ref, wg_ref, wu_ref, wd_ref, o_ref, xbuf, sem, wgb, wub, wdb, *, tm):
    e = pl.program_id(0)
    j = pl.program_id(1)
    nt = pl.num_programs(1)
    step = e * nt + j
    last = pl.num_programs(0) * nt - 1

    def gather(rows_ref, slot):
        def body(r, carry):
            pltpu.make_async_copy(hx_ref.at[rows_ref[0, 0, r]], xbuf.at[slot, r], sem.at[slot]).start()
            return carry
        lax.fori_loop(0, tm, body, 0, unroll=8)

    @pl.when(step == 0)
    def _():
        gather(idx_ref, 0)

    @pl.when(step < last)
    def _():
        gather(idxn_ref, (step + 1) % 2)

    @pl.when(j == 0)
    def _():
        wgb[...] = wg_ref[...].astype(BF16)
        wub[...] = wu_ref[...].astype(BF16)
        wdb[...] = wd_ref[...].astype(BF16)

    slot = step % 2
    pltpu.make_async_copy(hx_ref.at[pl.ds(0, tm)], xbuf.at[slot], sem.at[slot]).wait()
    xb = xbuf.at[slot]
    x = jnp.concatenate([xb[:, s, :] for s in range(xbuf.shape[2])], axis=1).astype(BF16)
    g = _dot(x, wgb[...])
    u = _dot(x, wub[...])
    hid = (g * jax.nn.sigmoid(g)) * u
    y = _dot(hid.astype(BF16), wdb[...])
    eye = _iota((tm, tm), 0) == _iota((tm, tm), 1)
    gcol = jnp.sum(jnp.where(eye, jnp.broadcast_to(gate_ref[0], (tm, tm)), 0.0), axis=1, keepdims=True)
    o_ref[...] = (y * gcol).astype(BF16)


def _expert_ffn(idx, gate, hx3, w_gate, w_up, w_down, *, cap, tm):
    n_exp = idx.shape[0]
    d = w_gate.shape[1]
    f = w_gate.shape[2]
    nt = cap // tm
    idx_t = idx[:, :, :cap].reshape(n_exp * nt, 1, tm)
    gate_t = gate[:, :, :cap].reshape(n_exp * nt, 1, tm)
    n_steps = n_exp * nt
    wspec = lambda a, b: pl.BlockSpec((None, a, b), lambda e, j: (e, 0, 0))
    return pl.pallas_call(
        functools.partial(_ffn_kernel, tm=tm),
        grid=(n_exp, nt),
        in_specs=[pl.BlockSpec((1, 1, tm), lambda e, j: (e * nt + j, 0, 0), memory_space=pltpu.SMEM),
                  pl.BlockSpec((1, 1, tm), lambda e, j: (jnp.minimum(e * nt + j + 1, n_steps - 1), 0, 0), memory_space=pltpu.SMEM),
                  pl.BlockSpec((1, 1, tm), lambda e, j: (e * nt + j, 0, 0)),
                  pl.BlockSpec(memory_space=pl.ANY),
                  wspec(d, f), wspec(d, f), wspec(f, d)],
        out_specs=pl.BlockSpec((tm, d), lambda e, j: (e * nt + j, 0)),
        out_shape=jax.ShapeDtypeStruct((n_exp * cap, d), BF16),
        scratch_shapes=[pltpu.VMEM((2, tm, d // LANES, LANES), F32), pltpu.SemaphoreType.DMA((2,)),
                        pltpu.VMEM((d, f), BF16), pltpu.VMEM((d, f), BF16), pltpu.VMEM((f, d), BF16)],
        compiler_params=_cparams(("arbitrary", "arbitrary")),
        name="expert_ffn",
    )(idx_t, idx_t, gate_t, hx3, w_gate, w_up, w_down)


def _combine_kernel(ws_ref, *refs, n_exp, cap, win, final_norm):
    y_refs = refs[:n_exp]
    pos_ref, x_ref, m5_ref, gf_ref, o_ref = refs[n_exp:]
    a = pl.program_id(0)
    ng = pl.num_programs(0)
    lp = _iota((win, LANES), 0, F32)
    acc = None
    for e in range(n_exp):
        start = (ws_ref[e * ng + a] - e * cap).astype(F32)
        onehot = (pos_ref[e, pl.ds(a, 1), :] - start) == lp
        part = _dot_tn(onehot.astype(BF16), y_refs[e][...])
        acc = part if acc is None else acc + part
    x_new = x_ref[...] + m5_ref[...] * acc
    if final_norm:
        ms = jnp.mean(x_new * x_new, axis=-1, keepdims=True)
        x_new = x_new * lax.rsqrt(ms + EPS) * gf_ref[...]
    o_ref[...] = x_new


def _combine(ye, posmap, goff, x, m5, g_final, *, cap, final_norm):
    n, d = x.shape
    n_exp = posmap.shape[0]
    ng = n // LANES
    win = min(COMBINE_WINDOW, cap)
    goff = goff[:, 0, :ng]
    start = jnp.minimum((goff // BF16_ROWS) * BF16_ROWS, cap - win)
    wstart = (start + jnp.arange(n_exp, dtype=I32)[:, None] * cap).reshape(-1)
    yspec = lambda e: pl.BlockSpec((pl.Element(win), pl.Element(d)),
                                   lambda a, ws: (pl.multiple_of(ws[e * ng + a], BF16_ROWS), 0))
    gs = pltpu.PrefetchScalarGridSpec(
        num_scalar_prefetch=1,
        grid=(ng,),
        in_specs=[yspec(e) for e in range(n_exp)]
        + [pl.BlockSpec(posmap.shape, lambda a, ws: (0, 0, 0)),
           pl.BlockSpec((LANES, d), lambda a, ws: (a, 0)),
           pl.BlockSpec((1, d), lambda a, ws: (0, 0)), pl.BlockSpec((1, d), lambda a, ws: (0, 0))],
        out_specs=pl.BlockSpec((LANES, d), lambda a, ws: (a, 0)),
    )
    return pl.pallas_call(
        functools.partial(_combine_kernel, n_exp=n_exp, cap=cap, win=win, final_norm=final_norm),
        grid_spec=gs,
        out_shape=jax.ShapeDtypeStruct((n, d), F32),
        compiler_params=_cparams(("arbitrary",)),
        name="combine_final" if final_norm else "combine",
    )(wstart, *([ye] * n_exp), posmap, x, m5, g_final)


def _rope_tables(n):
    t = jnp.arange(n)
    row = (t // GRID_W).astype(F32)
    col = (t % GRID_W).astype(F32)

    def cs(d):
        nf = d // 4
        inv = ROPE_BASE ** (-jnp.arange(nf, dtype=F32) / nf)
        ang = jnp.concatenate([row[:, None] * inv, col[:, None] * inv], axis=-1)
        return jnp.cos(ang), jnp.sin(ang)

    c, s = cs(RET_D)
    c128 = jnp.concatenate([c, c], axis=1)
    s128 = jnp.concatenate([-s, s], axis=1)
    c, s = cs(HEAD_D)
    z = jnp.zeros_like(s)
    c64 = jnp.concatenate([c, c, c, c], axis=1)
    sa = jnp.concatenate([-s, z, -s, z], axis=1)
    sb = jnp.concatenate([z, s, z, s], axis=1)
    return c128, s128, c64, sa, sb


def _in_columns(d):
    sizes = (("q_r", 512), ("k_r", 512), ("v_r", 512), ("g_r", 512), ("q_w", 512), ("k_w", 128), ("v_w", 128),
             ("q_n", 512), ("k_n", 512), ("v_n", 512), ("gates", 3 * d))
    off, lay = 0, {}
    for name, size in sizes:
        lay[name] = jnp.arange(off, off + size)
        off += size
    perm = jnp.array([0, 4, 1, 5, 2, 6, 3, 7])
    q_w = lay["q_w"].reshape(WIN_HEADS, HEAD_D)[perm].reshape(-1)
    cols = jnp.concatenate([lay["q_r"], lay["k_r"], lay["v_r"], lay["g_r"], q_w, lay["q_n"], lay["k_n"], lay["v_n"],
                            lay["gates"], lay["k_w"], lay["v_w"]])
    return cols, perm


def kernel(x, c, ctx, c_ctx, w_mod, b_mod, g_mix, g_ffn, w_in, ret_decay_logit, ret_gn, w_ret, win_sink, w_win, na_rpb,
           w_na, w_out, w_router, w_exp_gate, w_exp_up, w_exp_down, g_final):
    _, n, d = x.shape
    l = ctx.shape[1]
    depth = w_in.shape[0]
    xs, cs = x[0], ctx[0]
    cc = jnp.zeros((SUBLANES, d), F32).at[0].set(c[0]).at[1].set(c_ctx)
    mods = _modulation(cc, w_mod, b_mod)
    tabs_x = _rope_tables(n)
    tabs_c = tuple(jnp.zeros((l, LANES), F32) for _ in range(5))
    cols, perm = _in_columns(d)
    cap_x = CAPACITY_FACTOR * n // N_EXPERTS
    cap_c = CAPACITY_FACTOR * l // N_EXPERTS
    grp_c = -(-(l // LANES) // BF16_ROWS) * BF16_ROWS
    vec = lambda v: v.reshape(1, -1)

    for layer in range(depth):
        need_ctx = layer < depth - 1
        last = layer == depth - 1
        mx = [vec(mods[layer, 0, k * d:(k + 1) * d]) for k in range(N_MOD)]
        mc = [vec(mods[layer, 1, k * d:(k + 1) * d]) for k in range(N_MOD)]
        w_p = jnp.pad(jnp.take(w_in[layer], cols, axis=1), ((0, 0), (0, P_COLS - cols.shape[0]))).astype(BF16)
        wr = w_ret[layer].astype(BF16)
        ww = w_win[layer].reshape(WIN_HEADS, HEAD_D, d)[perm].reshape(WIN_HEADS * HEAD_D, d).astype(BF16)
        wn = w_na[layer].astype(BF16)
        wo = w_out[layer].astype(BF16)
        wrt = w_router[layer].T
        sink = win_sink[layer].astype(F32)
        lg_rows = jnp.broadcast_to(ret_decay_logit[layer].astype(F32).reshape(2 * RET_HEADS, 1), (2 * RET_HEADS, LANES))
        gn = vec(ret_gn[layer])
        bias_tab = _na_bias_table(na_rpb[layer])
        gmix, gffn = vec(g_mix[layer]), vec(g_ffn[layer])

        p_x = _inproj(xs, gmix, mx[0], mx[1], tabs_x, w_p, rope=True, tm=1024)
        p_c = _inproj(cs, gmix, mc[0], mc[1], tabs_c, w_p, rope=False, tm=l)

        yf, yb = _retention(p_x, p_c, lg_rows, zero_init=False)
        yw = _window_attention(p_x, p_c, sink)
        yn = _na_attention(p_x, p_c, bias_tab)
        xs, hx3, lt = _merge(yf, yb, p_x, gn, yw, yn, wr, ww, wn, wo, xs, mx[2], gffn, mx[3], mx[4], wrt, tm=256)

        idx, gate, posmap, goff = _route(lt.reshape(N_EXPERTS, n // LANES, LANES), cap=cap_x, n_groups=n // LANES)
        ye = _expert_ffn(idx, gate, hx3, w_exp_gate[layer], w_exp_up[layer], w_exp_down[layer], cap=cap_x, tm=256)
        xs = _combine(ye, posmap, goff, xs, mx[5], vec(g_final), cap=cap_x, final_norm=last)

        if need_ctx:
            yfc, ybc = _retention(p_c, p_c, lg_rows, zero_init=True)
            ywc, ync = _ctx_attention(p_c, sink)
            cs, hc3, ltc = _merge(yfc, ybc, p_c, gn, ywc, ync, wr, ww, wn, wo, cs, mc[2], gffn, mc[3], mc[4], wrt, tm=l)
            ltc = jnp.pad(ltc.reshape(N_EXPERTS, l // LANES, LANES), ((0, 0), (0, grp_c - l // LANES), (0, 0)))
            idc, gtc, posc, goffc = _route(ltc, cap=cap_c, n_groups=l // LANES)
            yec = _expert_ffn(idc, gtc, hc3, w_exp_gate[layer], w_exp_up[layer], w_exp_down[layer], cap=cap_c, tm=cap_c)
            cs = _combine(yec, posc, goffc, cs, mc[5], vec(g_final), cap=cap_c, final_norm=False)

    return xs[None]
```

```python
import functools

import jax
import jax.numpy as jnp
from jax import lax
from jax.experimental import pallas as pl
from jax.experimental.pallas import tpu as pltpu

F32, BF16, I32 = jnp.float32, jnp.bfloat16, jnp.int32
HIGHEST = lax.Precision.HIGHEST

GRID_W = 64
RET_HEADS, RET_D, RET_CHUNK = 4, 128, 128
WIN_HEADS, WIN_KV_HEADS, HEAD_D, WINDOW, WIN_BLOCK = 8, 2, 64, 128, 128
NA_HEADS, NA_ROWS, NA_COLS = 8, 8, 16
N_EXPERTS, CAPACITY_FACTOR = 16, 2
N_MOD = 6
ROPE_BASE = 10000.0
EPS = 1e-6
NEG_INF = -1e30

LANES = 128
SUBLANES = 8
BF16_ROWS = 16
VMEM_LIMIT = 56 * 1024 * 1024

SEG = 512
T_QR, T_KR, T_VR, T_GR, T_QW, T_QN, T_KN, T_VN, T_GATES, T_KVW = 0, 1, 2, 3, 4, 5, 6, 7, 8, 14
P_TILES = 15
P_COLS = P_TILES * SEG
C_KW = T_KVW * SEG // LANES
C_VW = C_KW + 1
NA_SLAB_ROWS = 10
COMBINE_WINDOW = 144


def _cparams(sem):
    return pltpu.CompilerParams(dimension_semantics=sem, vmem_limit_bytes=VMEM_LIMIT)


def _dot(a, b):
    return jnp.dot(a, b, preferred_element_type=F32)


def _dot_nt(a, b, precision=None):
    return lax.dot_general(a, b, (((1,), (1,)), ((), ())), precision=precision, preferred_element_type=F32)


def _dot_tn(a, b, precision=None):
    return lax.dot_general(a, b, (((0,), (0,)), ((), ())), precision=precision, preferred_element_type=F32)


def _iota(shape, dim, dtype=I32):
    return lax.broadcasted_iota(I32, shape, dim).astype(dtype)


def _mod_kernel(s_ref, w_ref, b_ref, o_ref):
    s = s_ref[...]
    s = s * jax.nn.sigmoid(s)
    o_ref[...] = jnp.dot(s, w_ref[...], precision=HIGHEST, preferred_element_type=F32) + b_ref[...]


def _modulation(cc, w_mod, b_mod):
    depth, d, md = w_mod.shape
    tn = 1536
    return pl.pallas_call(
        _mod_kernel,
        grid=(depth, md // tn),
        in_specs=[
            pl.BlockSpec((SUBLANES, d), lambda l, j: (0, 0)),
            pl.BlockSpec((None, d, tn), lambda l, j: (l, 0, j)),
            pl.BlockSpec((None, 1, tn), lambda l, j: (l, 0, j)),
        ],
        out_specs=pl.BlockSpec((None, SUBLANES, tn), lambda l, j: (l, 0, j)),
        out_shape=jax.ShapeDtypeStruct((depth, SUBLANES, md), F32),
        compiler_params=_cparams(("arbitrary", "arbitrary")),
        name="modulation",
    )(cc, w_mod, b_mod.reshape(depth, 1, md))


def _norm_mod(x, g, shift, scale):
    ms = jnp.mean(x * x, axis=-1, keepdims=True)
    y = x * lax.rsqrt(ms + EPS) * g
    return y * (1.0 + scale) + shift


def _inproj_kernel(x_ref, g_ref, sh_ref, sc_ref, c128_ref, s128_ref, c64_ref, sa_ref, sb_ref, w_ref, o_ref, hx_ref, *, rope):
    j = pl.program_id(1)

    @pl.when(j == 0)
    def _():
        hx_ref[...] = _norm_mod(x_ref[...], g_ref[...], sh_ref[...], sc_ref[...]).astype(BF16)

    acc = _dot(hx_ref[...], w_ref[...])
    groups = [acc[:, k * LANES:(k + 1) * LANES] for k in range(SEG // LANES)]

    def rope128(a):
        return a * c128_ref[...] + pltpu.roll(a, 64, 1) * s128_ref[...]

    def rope64(a):
        return a * c64_ref[...] + pltpu.roll(a, 96, 1) * sa_ref[...] + pltpu.roll(a, 32, 1) * sb_ref[...]

    def put(gs):
        o_ref[...] = jnp.concatenate(gs, axis=1).astype(BF16)

    k_scale = RET_D ** -0.5
    a_scale = HEAD_D ** -0.5
    ident = lambda a: a
    f_qr = rope128 if rope else ident
    f_kr = (lambda a: rope128(a) * k_scale) if rope else (lambda a: a * k_scale)
    f_qw = (lambda a: rope64(a) * a_scale) if rope else (lambda a: a * a_scale)
    f_kw = rope64 if rope else ident

    @pl.when(j == T_QR)
    def _():
        put([f_qr(a) for a in groups])

    @pl.when(j == T_KR)
    def _():
        put([f_kr(a) for a in groups])

    @pl.when(j == T_QW)
    def _():
        put([f_qw(a) for a in groups])

    @pl.when(j == T_QN)
    def _():
        put([a * a_scale for a in groups])

    @pl.when(j == T_KVW)
    def _():
        put([f_kw(groups[0])] + groups[1:])

    plain = (j != T_QR) & (j != T_KR) & (j != T_QW) & (j != T_QN) & (j != T_KVW)

    @pl.when(plain)
    def _():
        put(groups)


def _inproj(x, g, shift, scale, tabs, w, *, rope, tm):
    n, d = x.shape
    vec = lambda: pl.BlockSpec((1, d), lambda i, j: (0, 0))
    tab = lambda: pl.BlockSpec((tm, LANES), lambda i, j: (i, 0))
    return pl.pallas_call(
        functools.partial(_inproj_kernel, rope=rope),
        grid=(n // tm, P_TILES),
        in_specs=[pl.BlockSpec((tm, d), lambda i, j: (i, 0)), vec(), vec(), vec(),
                  tab(), tab(), tab(), tab(), tab(),
                  pl.BlockSpec((d, SEG), lambda i, j: (0, j))],
        out_specs=pl.BlockSpec((tm, SEG), lambda i, j: (i, j)),
        out_shape=jax.ShapeDtypeStruct((n, P_COLS), BF16),
        scratch_shapes=[pltpu.VMEM((tm, d), BF16)],
        compiler_params=_cparams(("arbitrary", "arbitrary")),
        name="inproj_rope" if rope else "inproj_ctx",
    )(x, g, shift, scale, *tabs, w)


def _ret_kernel(lg_ref, qf_ref, kf_ref, vf_ref, qb_ref, kb_ref, vb_ref, kc_ref, vc_ref, yf_ref, yb_ref,
                rf_ref, rb_ref, tab_ref, *, zero_init):
    c = pl.program_id(0)
    ch = RET_CHUNK
    dh = RET_D
    states = (rf_ref, rb_ref)

    @pl.when(c == 0)
    def _():
        lg = jax.nn.log_sigmoid(lg_ref[...])
        ii = _iota((ch, ch), 0, F32)
        jj = _iota((ch, ch), 1, F32)
        n_ctx = kc_ref.shape[0]
        mm = _iota((n_ctx, dh), 0, F32)
        for d in range(2):
            for h in range(RET_HEADS):
                row = lg[RET_HEADS * d + h:RET_HEADS * d + h + 1, :]
                l = jnp.broadcast_to(row, (ch, ch))
                if d == 0:
                    diff = ii - jj
                    dmat = jnp.where(diff >= 0.0, jnp.exp(jnp.maximum(diff, 0.0) * l), 0.0)
                    zeta = jnp.exp((ch - 1.0 - ii) * l)
                    xi = jnp.exp((ii + 1.0) * l)
                else:
                    diff = jj - ii
                    dmat = jnp.where(diff >= 1.0, jnp.exp(jnp.maximum(diff, 0.0) * l), 0.0)
                    zeta = jnp.exp(ii * l)
                    xi = jnp.exp((ch - ii) * l)
                tab_ref[d, h, 0] = dmat
                tab_ref[d, h, 1] = zeta
                tab_ref[d, h, 2] = xi
                tab_ref[d, h, 3] = jnp.exp(ch * l)
                if zero_init:
                    states[d][h] = jnp.zeros((dh, dh), F32)
                else:
                    lc = jnp.broadcast_to(row, (n_ctx, dh))
                    w = jnp.exp((n_ctx - 1.0 - mm) * lc) if d == 0 else jnp.exp(mm * lc)
                    sl = slice(h * dh, (h + 1) * dh)
                    kw = (kc_ref[:, sl].astype(F32) * w).astype(BF16)
                    states[d][h] = _dot_tn(kw, vc_ref[:, sl])

    for d, (q_ref, k_ref, v_ref, y_ref) in enumerate(((qf_ref, kf_ref, vf_ref, yf_ref), (qb_ref, kb_ref, vb_ref, yb_ref))):
        for h in range(RET_HEADS):
            sl = slice(h * dh, (h + 1) * dh)
            q, k, v = q_ref[:, sl], k_ref[:, sl], v_ref[:, sl]
            r = states[d][h]
            a = (_dot_nt(q, k) * tab_ref[d, h, 0]).astype(BF16)
            inner = _dot(a, v)
            cross = _dot(q, r.astype(BF16)) * tab_ref[d, h, 2]
            y_ref[:, sl] = inner + cross
            kz = (k.astype(F32) * tab_ref[d, h, 1]).astype(BF16)
            states[d][h] = tab_ref[d, h, 3] * r + _dot_tn(kz, v)


def _retention(p, p_ctx, lg_rows, *, zero_init):
    n = p.shape[0]
    nc = n // RET_CHUNK
    w = RET_HEADS * RET_D
    fwd = lambda t: pl.BlockSpec((RET_CHUNK, w), lambda c: (c, t))
    bwd = lambda t: pl.BlockSpec((RET_CHUNK, w), lambda c: (nc - 1 - c, t))
    ctx = lambda t: pl.BlockSpec((p_ctx.shape[0], w), lambda c: (0, t))
    return pl.pallas_call(
        functools.partial(_ret_kernel, zero_init=zero_init),
        grid=(nc,),
        in_specs=[pl.BlockSpec((SUBLANES, LANES), lambda c: (0, 0)),
                  fwd(T_QR), fwd(T_KR), fwd(T_VR), bwd(T_QR), bwd(T_KR), bwd(T_VR), ctx(T_KR), ctx(T_VR)],
        out_specs=[pl.BlockSpec((RET_CHUNK, w), lambda c: (c, 0)), pl.BlockSpec((RET_CHUNK, w), lambda c: (nc - 1 - c, 0))],
        out_shape=[jax.ShapeDtypeStruct((n, w), F32), jax.ShapeDtypeStruct((n, w), F32)],
        scratch_shapes=[pltpu.VMEM((RET_HEADS, RET_D, RET_D), F32), pltpu.VMEM((RET_HEADS, RET_D, RET_D), F32),
                        pltpu.VMEM((2, RET_HEADS, 4, RET_CHUNK, RET_CHUNK), F32)],
        compiler_params=_cparams(("arbitrary",)),
        name="retention_ctx" if zero_init else "retention",
    )(lg_rows, p, p, p, p, p, p, p_ctx, p_ctx)


def _half_mask(x, half):
    lane = _iota(x.shape, 1)
    keep = (lane < HEAD_D) if half == 0 else (lane >= HEAD_D)
    return jnp.where(keep, x, jnp.zeros_like(x))


def _softmax_pv(s, v, extra=None):
    m = jnp.max(s, axis=1, keepdims=True)
    if extra is not None:
        m = jnp.maximum(m, extra)
    p = jnp.exp(s - m)
    den = jnp.sum(p, axis=1, keepdims=True)
    if extra is not None:
        den = den + jnp.exp(extra - m)
    return _dot(p.astype(BF16), v) / den


def _win_kernel(sink_ref, q_ref, kp_ref, kc_ref, kn_ref, vp_ref, vc_ref, vn_ref, kx_ref, vx_ref, o_ref, *, n_tokens):
    nb = pl.program_id(0)
    wb = WIN_BLOCK
    k_all = jnp.concatenate([kp_ref[...], kc_ref[...], kn_ref[...], kx_ref[...]], axis=0)
    v_all = jnp.concatenate([vp_ref[...], vc_ref[...], vn_ref[...], vx_ref[...]], axis=0)
    n_cols = SEG // LANES
    qs = jnp.concatenate([q_ref[:, c * LANES:(c + 1) * LANES] for c in range(n_cols)], axis=0)
    m_rows = n_cols * wb
    qpos = _iota((m_rows, 3 * wb), 0) & (wb - 1)
    kpos = _iota((m_rows, 3 * wb), 1) - wb
    kabs = nb * wb + kpos
    valid = (jnp.abs(kpos - qpos) <= WINDOW) & (kabs >= 0) & (kabs < n_tokens)
    outs = []
    for g in range(WIN_KV_HEADS):
        s = _dot_nt(qs, _half_mask(k_all, g))
        s = jnp.concatenate([jnp.where(valid, s[:, :3 * wb], NEG_INF), s[:, 3 * wb:]], axis=1)
        sink = jnp.concatenate([jnp.full((wb, 1), sink_ref[n_cols * g + c], F32) for c in range(n_cols)], axis=0)
        outs.append(_softmax_pv(s, _half_mask(v_all, g), sink))
    o = outs[0] + outs[1]
    o_ref[...] = jnp.concatenate([o[c * wb:(c + 1) * wb, :] for c in range(n_cols)], axis=1).astype(BF16)


def _window_attention(p, p_ctx, sink):
    n = p.shape[0]
    nb = n // WIN_BLOCK
    l = p_ctx.shape[0]
    blk = lambda col, f: pl.BlockSpec((WIN_BLOCK, LANES), lambda i: (f(i), col))
    prev = lambda i: jnp.maximum(i - 1, 0)
    cur = lambda i: i
    nxt = lambda i: jnp.minimum(i + 1, nb - 1)
    return pl.pallas_call(
        functools.partial(_win_kernel, n_tokens=n),
        grid=(nb,),
        in_specs=[pl.BlockSpec(memory_space=pltpu.SMEM),
                  pl.BlockSpec((WIN_BLOCK, SEG), lambda i: (i, T_QW)),
                  blk(C_KW, prev), blk(C_KW, cur), blk(C_KW, nxt),
                  blk(C_VW, prev), blk(C_VW, cur), blk(C_VW, nxt),
                  pl.BlockSpec((l, LANES), lambda i: (0, C_KW)), pl.BlockSpec((l, LANES), lambda i: (0, C_VW))],
        out_specs=pl.BlockSpec((WIN_BLOCK, SEG), lambda i: (i, 0)),
        out_shape=jax.ShapeDtypeStruct((n, SEG), BF16),
        compiler_params=_cparams(("arbitrary",)),
        name="window_attention",
    )(sink, p, p, p, p, p, p, p, p_ctx, p_ctx)


def _na_slab_start(s, half_rows):
    return jnp.clip(s - 2, 0, half_rows - NA_SLAB_ROWS // 2)


def _na_kernel(q_ref, k0, k1, k2, k3, k4, v0, v1, v2, v3, v4, kx_ref, vx_ref, bias_ref, o_ref, *, rows):
    s = pl.program_id(0)
    w = GRID_W
    nq = 2 * w
    nk = NA_SLAB_ROWS * w
    r0 = 2 * s
    sb = 2 * _na_slab_start(s, rows // 2)
    delta = sb - r0
    k_refs = (k0, k1, k2, k3, k4)
    v_refs = (v0, v1, v2, v3, v4)
    sub = _iota((nq, nk), 0)
    lane = _iota((nq, nk), 1)
    shift = w.bit_length() - 1
    r_q = r0 + (sub >> shift)
    c_q = sub & (w - 1)
    r_k = sb + (lane >> shift)
    c_k = lane & (w - 1)
    r_start = jnp.clip(r_q - NA_ROWS // 2, 0, rows - NA_ROWS)
    c_start = jnp.clip(c_q - NA_COLS // 2, 0, w - NA_COLS)
    valid = (r_k >= r_start) & (r_k < r_start + NA_ROWS) & (c_k >= c_start) & (c_k < c_start + NA_COLS)
    for pair in range(NA_HEADS // 2):
        sl = slice(pair * LANES, (pair + 1) * LANES)
        q = q_ref[:, sl]
        k_all = jnp.concatenate([r[:, sl] for r in k_refs] + [kx_ref[:, sl]], axis=0)
        v_all = jnp.concatenate([r[:, sl] for r in v_refs] + [vx_ref[:, sl]], axis=0)
        out = None
        for u in range(2):
            h = 2 * pair + u
            sc = _dot_nt(q, _half_mask(k_all, u))
            bias = jnp.concatenate(
                [jnp.concatenate([bias_ref[h, delta + 2 * i - a + NA_ROWS + 1] for i in range(NA_SLAB_ROWS // 2)], axis=1)
                 for a in range(2)], axis=0)
            s_loc = jnp.where(valid, sc[:, :nk] + bias, NEG_INF)
            o = _softmax_pv(jnp.concatenate([s_loc, sc[:, nk:]], axis=1), _half_mask(v_all, u))
            out = o if out is None else out + o
        o_ref[:, sl] = out.astype(BF16)


def _na_attention(p, p_ctx, bias_tab):
    n = p.shape[0]
    rows = n // GRID_W
    steps = rows // 2
    l = p_ctx.shape[0]
    slab = lambda t, i: pl.BlockSpec((2 * GRID_W, SEG), lambda s: (_na_slab_start(s, steps) + i, t))
    nslab = NA_SLAB_ROWS // 2
    return pl.pallas_call(
        functools.partial(_na_kernel, rows=rows),
        grid=(steps,),
        in_specs=[pl.BlockSpec((2 * GRID_W, SEG), lambda s: (s, T_QN))]
        + [slab(T_KN, i) for i in range(nslab)] + [slab(T_VN, i) for i in range(nslab)]
        + [pl.BlockSpec((l, SEG), lambda s: (0, T_KN)), pl.BlockSpec((l, SEG), lambda s: (0, T_VN)),
           pl.BlockSpec(bias_tab.shape, lambda s: (0, 0, 0, 0))],
        out_specs=pl.BlockSpec((2 * GRID_W, SEG), lambda s: (s, 0)),
        out_shape=jax.ShapeDtypeStruct((n, SEG), BF16),
        compiler_params=_cparams(("arbitrary",)),
        name="neighbourhood_attention",
    )(p, *([p] * (2 * nslab)), p_ctx, p_ctx, bias_tab)


def _na_bias_table(rpb):
    d = jnp.arange(2 * NA_ROWS + 2)[:, None, None] - (NA_ROWS + 1)
    lane = jnp.arange(2 * GRID_W)[None, None, :]
    cq = jnp.arange(GRID_W)[None, :, None]
    ri = jnp.clip(d + lane // GRID_W + NA_ROWS - 1, 0, 2 * NA_ROWS - 2)
    ci = jnp.clip(lane % GRID_W - cq, -(NA_COLS - 1), NA_COLS - 1) + NA_COLS - 1
    ri, ci = jnp.broadcast_arrays(ri, ci)
    return rpb.astype(F32)[:, ri, ci]


def _ctx_attn_kernel(sink_ref, p_ref, ow_ref, on_ref):
    l = p_ref.shape[0]
    n_cols = SEG // LANES
    k_all = p_ref[:, T_KVW * SEG:T_KVW * SEG + LANES]
    v_all = p_ref[:, T_KVW * SEG + LANES:T_KVW * SEG + 2 * LANES]
    qs = jnp.concatenate([p_ref[:, T_QW * SEG + c * LANES:T_QW * SEG + (c + 1) * LANES] for c in range(n_cols)], axis=0)
    outs = []
    for g in range(WIN_KV_HEADS):
        s = _dot_nt(qs, _half_mask(k_all, g))
        sink = jnp.concatenate([jnp.full((l, 1), sink_ref[n_cols * g + c], F32) for c in range(n_cols)], axis=0)
        outs.append(_softmax_pv(s, _half_mask(v_all, g), sink))
    o = outs[0] + outs[1]
    ow_ref[...] = jnp.concatenate([o[c * l:(c + 1) * l, :] for c in range(n_cols)], axis=1).astype(BF16)
    for pair in range(NA_HEADS // 2):
        sl = lambda t: slice(t * SEG + pair * LANES, t * SEG + (pair + 1) * LANES)
        q, k, v = p_ref[:, sl(T_QN)], p_ref[:, sl(T_KN)], p_ref[:, sl(T_VN)]
        out = None
        for u in range(2):
            o = _softmax_pv(_dot_nt(q, _half_mask(k, u)), _half_mask(v, u))
            out = o if out is None else out + o
        on_ref[:, pair * LANES:(pair + 1) * LANES] = out.astype(BF16)


def _ctx_attention(p_ctx, sink):
    l = p_ctx.shape[0]
    return pl.pallas_call(
        _ctx_attn_kernel,
        in_specs=[pl.BlockSpec(memory_space=pltpu.SMEM), pl.BlockSpec(p_ctx.shape, lambda: (0, 0))],
        out_specs=[pl.BlockSpec((l, SEG), lambda: (0, 0)), pl.BlockSpec((l, SEG), lambda: (0, 0))],
        out_shape=[jax.ShapeDtypeStruct((l, SEG), BF16), jax.ShapeDtypeStruct((l, SEG), BF16)],
        compiler_params=pltpu.CompilerParams(vmem_limit_bytes=VMEM_LIMIT),
        name="context_attention",
    )(sink, p_ctx)


def _merge_kernel(yf_ref, yb_ref, gr_ref, gn_ref, yw_ref, yn_ref, ga_ref, gb_ref, gc_ref,
                  wr_ref, ww_ref, wn_ref, wo_ref, x_ref, m2_ref, gf_ref, m3_ref, m4_ref, wrt_ref,
                  xo_ref, hx_ref, lt_ref):
    y = yf_ref[...] + yb_ref[...]
    parts = []
    for h in range(RET_HEADS):
        yh = y[:, h * RET_D:(h + 1) * RET_D]
        mu = jnp.mean(yh, axis=-1, keepdims=True)
        var = jnp.mean(jnp.square(yh - mu), axis=-1, keepdims=True)
        parts.append((yh - mu) * lax.rsqrt(var + EPS))
    g = gr_ref[...].astype(F32)
    ya = jnp.concatenate(parts, axis=1) * gn_ref[...] * (g * jax.nn.sigmoid(g))
    za = _dot(ya.astype(BF16), wr_ref[...])
    zb = _dot(yw_ref[...], ww_ref[...])
    zc = _dot(yn_ref[...], wn_ref[...])
    sig = lambda r: jax.nn.sigmoid(r[...].astype(F32))
    mix = sig(ga_ref) * za + sig(gb_ref) * zb + sig(gc_ref) * zc
    x_new = x_ref[...] + m2_ref[...] * _dot(mix.astype(BF16), wo_ref[...])
    xo_ref[...] = x_new
    h2 = _norm_mod(x_new, gf_ref[...], m3_ref[...], m4_ref[...])
    for s in range(h2.shape[1] // LANES):
        hx_ref[:, s, :] = h2[:, s * LANES:(s + 1) * LANES]
    lt_ref[...] = _dot_nt(wrt_ref[...], h2, precision=HIGHEST)


def _merge(yf, yb, p, gn, yw, yn, wr, ww, wn, wo, x, m2, gf, m3, m4, wrt, *, tm):
    n, d = x.shape
    row = lambda wdt, t: pl.BlockSpec((tm, wdt), lambda i: (i, t))
    full = lambda a: pl.BlockSpec(a.shape, lambda i: (0,) * a.ndim)
    gate0 = T_GATES * SEG // d
    return pl.pallas_call(
        _merge_kernel,
        grid=(n // tm,),
        in_specs=[row(SEG, 0), row(SEG, 0), row(SEG, T_GR), full(gn), row(SEG, 0), row(SEG, 0),
                  row(d, gate0), row(d, gate0 + 1), row(d, gate0 + 2),
                  full(wr), full(ww), full(wn), full(wo), row(d, 0), full(m2), full(gf), full(m3), full(m4), full(wrt)],
        out_specs=[pl.BlockSpec((tm, d), lambda i: (i, 0)),
                   pl.BlockSpec((tm, d // LANES, LANES), lambda i: (i, 0, 0)),
                   pl.BlockSpec((N_EXPERTS, tm), lambda i: (0, i))],
        out_shape=[jax.ShapeDtypeStruct((n, d), F32), jax.ShapeDtypeStruct((n, d // LANES, LANES), F32),
                   jax.ShapeDtypeStruct((N_EXPERTS, n), F32)],
        compiler_params=_cparams(("arbitrary",)),
        name="merge",
    )(yf, yb, p, gn, yw, yn, p, p, p, wr, ww, wn, wo, x, m2, gf, m3, m4, wrt)


def _route_kernel(lt_ref, idx_ref, gate_ref, pos_ref, goff_ref, aff_ref, thr_ref, *, cap, n_groups):
    e = pl.program_id(0)
    n_exp, ag, _ = lt_ref.shape
    capp = idx_ref.shape[-1]

    @pl.when(e == 0)
    def _():
        lt = lt_ref[...]
        ex = jnp.exp(lt - jnp.max(lt, axis=0, keepdims=True))
        aff = ex / jnp.sum(ex, axis=0, keepdims=True)
        real = _iota(aff.shape, 1) < n_groups
        aff = jnp.where(real, aff, 0.0)
        aff_ref[...] = aff
        bits = pltpu.bitcast(aff, I32)

        def body(k, t):
            cand = t | (jnp.int32(1) << (30 - k))
            cnt = jnp.sum(jnp.sum((bits >= cand).astype(F32), axis=2, keepdims=True), axis=1, keepdims=True)
            return jnp.where(cnt >= float(cap), cand, t)

        t = lax.fori_loop(0, 31, body, jnp.zeros((n_exp, 1, 1), I32))
        thr_ref[...] = jnp.broadcast_to(t, thr_ref.shape)

    a = aff_ref[e]
    bits = pltpu.bitcast(a, I32)
    t = thr_ref[e][0:1, :]
    gt = bits > t
    eq = bits == t
    tri_lane_strict = (_iota((LANES, LANES), 0) < _iota((LANES, LANES), 1)).astype(BF16)
    tri_lane_incl = (_iota((LANES, LANES), 0) <= _iota((LANES, LANES), 1)).astype(BF16)
    tri_grp_strict = (_iota((ag, ag), 1) < _iota((ag, ag), 0)).astype(BF16)
    tri_grp_incl = (_iota((ag, ag), 1) <= _iota((ag, ag), 0)).astype(BF16)

    def total(mask_f):
        return jnp.sum(jnp.sum(mask_f, axis=1, keepdims=True), axis=0, keepdims=True)

    def group_sum(mask_f):
        return jnp.broadcast_to(jnp.sum(mask_f, axis=1, keepdims=True), (ag, LANES)).astype(BF16)

    eq_f = eq.astype(F32)
    need = float(cap) - total(gt.astype(F32))
    rank_eq = _dot(tri_grp_strict, group_sum(eq_f)) + _dot(eq_f.astype(BF16), tri_lane_strict)
    sel = gt | (eq & (rank_eq < need))
    sel_f = sel.astype(F32)
    cl = _dot(sel_f.astype(BF16), tri_lane_incl)
    cg = _dot(tri_grp_incl, group_sum(sel_f))
    goff = cg - jnp.broadcast_to(jnp.sum(sel_f, axis=1, keepdims=True), (ag, LANES))
    pos_ref[...] = jnp.where(sel, goff + cl - 1.0, -1.0)
    diag = _iota((ag, ag), 0) == _iota((ag, ag), 1)
    goff_sq = goff if ag == LANES else goff[:, :ag]
    goff_ref[...] = jnp.sum(jnp.where(diag, goff_sq, 0.0), axis=0, keepdims=True).astype(I32)

    pp = _iota((ag, capp), 1, F32)
    cg_b = jnp.broadcast_to(cg[:, 0:1], (ag, capp))
    below = cg_b <= pp
    grp = jnp.sum(below.astype(F32), axis=0, keepdims=True)
    off = jnp.max(jnp.where(below, cg_b, 0.0), axis=0, keepdims=True)
    onehot = _iota((ag, capp), 0, F32) == grp
    in_grp = _dot_tn(cl.astype(BF16), onehot.astype(BF16))
    local = pp[0:1, :] - off
    lane_of = jnp.sum((in_grp <= local).astype(F32), axis=0, keepdims=True)
    live = pp[0:1, :] < float(cap)
    idx = jnp.where(live, grp * float(LANES) + lane_of, 0.0)
    idx_ref[...] = idx.astype(I32)
    aff_grp = _dot_tn(a, onehot.astype(F32), precision=HIGHEST)
    pick = _iota((LANES, capp), 0, F32) == lane_of
    gate = jnp.sum(jnp.where(pick, aff_grp, 0.0), axis=0, keepdims=True)
    gate_ref[...] = jnp.where(live, gate, 0.0)


def _route(logits_t, *, cap, n_groups):
    n_exp, ag, _ = logits_t.shape
    capp = -(-cap // LANES) * LANES
    return pl.pallas_call(
        functools.partial(_route_kernel, cap=cap, n_groups=n_groups),
        grid=(n_exp,),
        in_specs=[pl.BlockSpec(logits_t.shape, lambda e: (0, 0, 0))],
        out_specs=[pl.BlockSpec((None, 1, capp), lambda e: (e, 0, 0)),
                   pl.BlockSpec((None, 1, capp), lambda e: (e, 0, 0)),
                   pl.BlockSpec((None, ag, LANES), lambda e: (e, 0, 0)),
                   pl.BlockSpec((None, 1, ag), lambda e: (e, 0, 0))],
        out_shape=[jax.ShapeDtypeStruct((n_exp, 1, capp), I32), jax.ShapeDtypeStruct((n_exp, 1, capp), F32),
                   jax.ShapeDtypeStruct((n_exp, ag, LANES), F32), jax.ShapeDtypeStruct((n_exp, 1, ag), I32)],
        scratch_shapes=[pltpu.VMEM((n_exp, ag, LANES), F32), pltpu.VMEM((n_exp, SUBLANES, LANES), I32)],
        compiler_params=_cparams(("arbitrary",)),
        name="route",
    )(logits_t)


def _ffn_kernel(idx_ref, idxn_ref, gate_ref, hx_ref, wg_ref, wu_ref, wd_ref, o_ref, xbuf, sem, wgb, wub, wdb, *, tm):
    e = pl.program_id(0)
    j = pl.program_id(1)
    nt = pl.num_programs(1)
    step = e * nt + j
    last = pl.num_programs(0) * nt - 1

    def gather(rows_ref, slot):
        def body(r, carry):
            pltpu.make_async_copy(hx_ref.at[rows_ref[0, 0, r]], xbuf.at[slot, r], sem.at[slot]).start()
            return carry
        lax.fori_loop(0, tm, body, 0, unroll=8)

    @pl.when(step == 0)
    def _():
        gather(idx_ref, 0)

    @pl.when(step < last)
    def _():
        gather(idxn_ref, (step + 1) % 2)

    @pl.when(j == 0)
    def _():
        wgb[...] = wg_ref[...].astype(BF16)
        wub[...] = wu_ref[...].astype(BF16)
        wdb[...] = wd_ref[...].astype(BF16)

    slot = step % 2
    pltpu.make_async_copy(hx_ref.at[pl.ds(0, tm)], xbuf.at[slot], sem.at[slot]).wait()
    xb = xbuf.at[slot]
    x = jnp.concatenate([xb[:, s, :] for s in range(xbuf.shape[2])], axis=1).astype(BF16)
    g = _dot(x, wgb[...])
    u = _dot(x, wub[...])
    hid = (g * jax.nn.sigmoid(g)) * u
    y = _dot(hid.astype(BF16), wdb[...])
    eye = _iota((tm, tm), 0) == _iota((tm, tm), 1)
    gcol = jnp.sum(jnp.where(eye, jnp.broadcast_to(gate_ref[0], (tm, tm)), 0.0), axis=1, keepdims=True)
    o_ref[...] = (y * gcol).astype(BF16)


def _expert_ffn(idx, gate, hx3, w_gate, w_up, w_down, *, layer, cap, tm):
    n_exp = idx.shape[0]
    d = w_gate.shape[2]
    f = w_gate.shape[3]
    nt = cap // tm
    idx_t = idx[:, :, :cap].reshape(n_exp * nt, 1, tm)
    gate_t = gate[:, :, :cap].reshape(n_exp * nt, 1, tm)
    n_steps = n_exp * nt
    wspec = lambda a, b: pl.BlockSpec((None, None, a, b), lambda e, j: (layer, e, 0, 0))
    return pl.pallas_call(
        functools.partial(_ffn_kernel, tm=tm),
        grid=(n_exp, nt),
        in_specs=[pl.BlockSpec((1, 1, tm), lambda e, j: (e * nt + j, 0, 0), memory_space=pltpu.SMEM),
                  pl.BlockSpec((1, 1, tm), lambda e, j: (jnp.minimum(e * nt + j + 1, n_steps - 1), 0, 0), memory_space=pltpu.SMEM),
                  pl.BlockSpec((1, 1, tm), lambda e, j: (e * nt + j, 0, 0)),
                  pl.BlockSpec(memory_space=pl.ANY),
                  wspec(d, f), wspec(d, f), wspec(f, d)],
        out_specs=pl.BlockSpec((tm, d), lambda e, j: (e * nt + j, 0)),
        out_shape=jax.ShapeDtypeStruct((n_exp * cap, d), BF16),
        scratch_shapes=[pltpu.VMEM((2, tm, d // LANES, LANES), F32), pltpu.SemaphoreType.DMA((2,)),
                        pltpu.VMEM((d, f), BF16), pltpu.VMEM((d, f), BF16), pltpu.VMEM((f, d), BF16)],
        compiler_params=_cparams(("arbitrary", "arbitrary")),
        name="expert_ffn",
    )(idx_t, idx_t, gate_t, hx3, w_gate, w_up, w_down)


def _combine_kernel(ws_ref, *refs, n_exp, cap, win, final_norm):
    y_refs = refs[:n_exp]
    pos_ref, x_ref, m5_ref, gf_ref, o_ref = refs[n_exp:]
    a = pl.program_id(0)
    ng = pl.num_programs(0)
    lp = _iota((win, LANES), 0, F32)
    acc = None
    for e in range(n_exp):
        start = (ws_ref[e * ng + a] - e * cap).astype(F32)
        onehot = (pos_ref[e, pl.ds(a, 1), :] - start) == lp
        part = _dot_tn(onehot.astype(BF16), y_refs[e][...])
        acc = part if acc is None else acc + part
    x_new = x_ref[...] + m5_ref[...] * acc
    if final_norm:
        ms = jnp.mean(x_new * x_new, axis=-1, keepdims=True)
        x_new = x_new * lax.rsqrt(ms + EPS) * gf_ref[...]
    o_ref[...] = x_new


def _combine(ye, posmap, goff, x, m5, g_final, *, cap, final_norm):
    n, d = x.shape
    n_exp = posmap.shape[0]
    ng = n // LANES
    win = min(COMBINE_WINDOW, cap)
    goff = goff[:, 0, :ng]
    start = jnp.minimum((goff // BF16_ROWS) * BF16_ROWS, cap - win)
    wstart = (start + jnp.arange(n_exp, dtype=I32)[:, None] * cap).reshape(-1)
    yspec = lambda e: pl.BlockSpec((pl.Element(win), pl.Element(d)),
                                   lambda a, ws: (pl.multiple_of(ws[e * ng + a], BF16_ROWS), 0))
    gs = pltpu.PrefetchScalarGridSpec(
        num_scalar_prefetch=1,
        grid=(ng,),
        in_specs=[yspec(e) for e in range(n_exp)]
        + [pl.BlockSpec(posmap.shape, lambda a, ws: (0, 0, 0)),
           pl.BlockSpec((LANES, d), lambda a, ws: (a, 0)),
           pl.BlockSpec((1, d), lambda a, ws: (0, 0)), pl.BlockSpec((1, d), lambda a, ws: (0, 0))],
        out_specs=pl.BlockSpec((LANES, d), lambda a, ws: (a, 0)),
    )
    return pl.pallas_call(
        functools.partial(_combine_kernel, n_exp=n_exp, cap=cap, win=win, final_norm=final_norm),
        grid_spec=gs,
        out_shape=jax.ShapeDtypeStruct((n, d), F32),
        compiler_params=_cparams(("arbitrary",)),
        name="combine_final" if final_norm else "combine",
    )(wstart, *([ye] * n_exp), posmap, x, m5, g_final)


def _rope_tables(n):
    t = jnp.arange(n)
    row = (t // GRID_W).astype(F32)
    col = (t % GRID_W).astype(F32)

    def cs(d):
        nf = d // 4
        inv = ROPE_BASE ** (-jnp.arange(nf, dtype=F32) / nf)
        ang = jnp.concatenate([row[:, None] * inv, col[:, None] * inv], axis=-1)
        return jnp.cos(ang), jnp.sin(ang)

    c, s = cs(RET_D)
    c128 = jnp.concatenate([c, c], axis=1)
    s128 = jnp.concatenate([-s, s], axis=1)
    c, s = cs(HEAD_D)
    z = jnp.zeros_like(s)
    c64 = jnp.concatenate([c, c, c, c], axis=1)
    sa = jnp.concatenate([-s, z, -s, z], axis=1)
    sb = jnp.concatenate([z, s, z, s], axis=1)
    return c128, s128, c64, sa, sb


WIN_HEAD_ORDER = (0, 4, 1, 5, 2, 6, 3, 7)


def _permute_in_weight(w):
    d = w.shape[0]
    sizes = (("q_r", 512), ("k_r", 512), ("v_r", 512), ("g_r", 512), ("q_w", 512), ("k_w", 128), ("v_w", 128),
             ("q_n", 512), ("k_n", 512), ("v_n", 512), ("gates", 3 * d))
    off, lay = 0, {}
    for name, size in sizes:
        lay[name] = (off, off + size)
        off += size
    seg = lambda name: w[:, lay[name][0]:lay[name][1]]
    q0 = lay["q_w"][0]
    q_w = [w[:, q0 + h * HEAD_D:q0 + (h + 1) * HEAD_D] for h in WIN_HEAD_ORDER]
    parts = [seg("q_r"), seg("k_r"), seg("v_r"), seg("g_r")] + q_w + [seg("q_n"), seg("k_n"), seg("v_n"), seg("gates"),
                                                                       seg("k_w"), seg("v_w")]
    parts.append(jnp.zeros((d, P_COLS - off), w.dtype))
    return jnp.concatenate(parts, axis=1).astype(BF16)


def _permute_win_rows(w):
    return jnp.concatenate([w[h * HEAD_D:(h + 1) * HEAD_D] for h in WIN_HEAD_ORDER], axis=0).astype(BF16)


def kernel(x, c, ctx, c_ctx, w_mod, b_mod, g_mix, g_ffn, w_in, ret_decay_logit, ret_gn, w_ret, win_sink, w_win, na_rpb,
           w_na, w_out, w_router, w_exp_gate, w_exp_up, w_exp_down, g_final):
    _, n, d = x.shape
    l = ctx.shape[1]
    depth = w_in.shape[0]
    xs, cs = x[0], ctx[0]
    cc = jnp.zeros((SUBLANES, d), F32).at[0].set(c[0]).at[1].set(c_ctx)
    mods = _modulation(cc, w_mod, b_mod)
    tabs_x = _rope_tables(n)
    tabs_c = tuple(jnp.zeros((l, LANES), F32) for _ in range(5))
    cap_x = CAPACITY_FACTOR * n // N_EXPERTS
    cap_c = CAPACITY_FACTOR * l // N_EXPERTS
    grp_c = -(-(l // LANES) // BF16_ROWS) * BF16_ROWS
    vec = lambda v: v.reshape(1, -1)

    for layer in range(depth):
        need_ctx = layer < depth - 1
        last = layer == depth - 1
        mx = [vec(mods[layer, 0, k * d:(k + 1) * d]) for k in range(N_MOD)]
        mc = [vec(mods[layer, 1, k * d:(k + 1) * d]) for k in range(N_MOD)]
        w_p = _permute_in_weight(w_in[layer])
        wr = w_ret[layer].astype(BF16)
        ww = _permute_win_rows(w_win[layer])
        wn = w_na[layer].astype(BF16)
        wo = w_out[layer].astype(BF16)
        wrt = w_router[layer].T
        sink = win_sink[layer].astype(F32)
        lg_rows = jnp.broadcast_to(ret_decay_logit[layer].astype(F32).reshape(2 * RET_HEADS, 1), (2 * RET_HEADS, LANES))
        gn = vec(ret_gn[layer])
        bias_tab = _na_bias_table(na_rpb[layer])
        gmix, gffn = vec(g_mix[layer]), vec(g_ffn[layer])

        p_x = _inproj(xs, gmix, mx[0], mx[1], tabs_x, w_p, rope=True, tm=2048)
        p_c = _inproj(cs, gmix, mc[0], mc[1], tabs_c, w_p, rope=False, tm=l)

        yf, yb = _retention(p_x, p_c, lg_rows, zero_init=False)
        yw = _window_attention(p_x, p_c, sink)
        yn = _na_attention(p_x, p_c, bias_tab)
        xs, hx3, lt = _merge(yf, yb, p_x, gn, yw, yn, wr, ww, wn, wo, xs, mx[2], gffn, mx[3], mx[4], wrt, tm=512)

        idx, gate, posmap, goff = _route(lt.reshape(N_EXPERTS, n // LANES, LANES), cap=cap_x, n_groups=n // LANES)
        ye = _expert_ffn(idx, gate, hx3, w_exp_gate, w_exp_up, w_exp_down, layer=layer, cap=cap_x, tm=256)
        xs = _combine(ye, posmap, goff, xs, mx[5], vec(g_final), cap=cap_x, final_norm=last)

        if need_ctx:
            yfc, ybc = _retention(p_c, p_c, lg_rows, zero_init=True)
            ywc, ync = _ctx_attention(p_c, sink)
            cs, hc3, ltc = _merge(yfc, ybc, p_c, gn, ywc, ync, wr, ww, wn, wo, cs, mc[2], gffn, mc[3], mc[4], wrt, tm=l)
            ltc = jnp.pad(ltc.reshape(N_EXPERTS, l // LANES, LANES), ((0, 0), (0, grp_c - l // LANES), (0, 0)))
            idc, gtc, posc, goffc = _route(ltc, cap=cap_c, n_groups=l // LANES)
            yec = _expert_ffn(idc, gtc, hc3, w_exp_gate, w_exp_up, w_exp_down, layer=layer, cap=cap_c, tm=cap_c)
            cs = _combine(yec, posc, goffc, cs, mc[5], vec(g_final), cap=cap_c, final_norm=False)

    return xs[None]
```

```python
import functools

import jax
import jax.numpy as jnp
from jax import lax
from jax.experimental import pallas as pl
from jax.experimental.pallas import tpu as pltpu

F32, BF16, I32 = jnp.float32, jnp.bfloat16, jnp.int32
HIGHEST = lax.Precision.HIGHEST

GRID_W = 64
RET_HEADS, RET_D, RET_CHUNK = 4, 128, 128
WIN_HEADS, WIN_KV_HEADS, HEAD_D, WINDOW, WIN_BLOCK = 8, 2, 64, 128, 128
NA_HEADS, NA_ROWS, NA_COLS = 8, 8, 16
N_EXPERTS, CAPACITY_FACTOR = 16, 2
N_MOD = 6
ROPE_BASE = 10000.0
EPS = 1e-6
NEG_INF = -1e30

LANES = 128
SUBLANES = 8
BF16_ROWS = 16
VMEM_LIMIT = 56 * 1024 * 1024

SEG = 512
T_QR, T_KR, T_VR, T_GR, T_QW, T_QN, T_KN, T_VN, T_GATES, T_KVW = 0, 1, 2, 3, 4, 5, 6, 7, 8, 14
P_TILES = 15
P_COLS = P_TILES * SEG
C_KW = T_KVW * SEG // LANES
C_VW = C_KW + 1
NA_SLAB_ROWS = 10
COMBINE_WINDOW = 64
MXU_DEPTH = 256


def _cparams(sem):
    return pltpu.CompilerParams(dimension_semantics=sem, vmem_limit_bytes=VMEM_LIMIT)


def _dot(a, b):
    return jnp.dot(a, b, preferred_element_type=F32)


def _dot_nt(a, b, precision=None):
    return lax.dot_general(a, b, (((1,), (1,)), ((), ())), precision=precision, preferred_element_type=F32)


def _dot_tn(a, b, precision=None):
    return lax.dot_general(a, b, (((0,), (0,)), ((), ())), precision=precision, preferred_element_type=F32)


def _iota(shape, dim, dtype=I32):
    return lax.broadcasted_iota(I32, shape, dim).astype(dtype)


def _mod_kernel(s_ref, w_ref, b_ref, o_ref):
    s = s_ref[...]
    s = s * jax.nn.sigmoid(s)
    o_ref[...] = jnp.dot(s, w_ref[...], precision=HIGHEST, preferred_element_type=F32) + b_ref[...]


def _modulation(cc, w_mod, b_mod):
    depth, d, md = w_mod.shape
    tn = 1536
    return pl.pallas_call(
        _mod_kernel,
        grid=(depth, md // tn),
        in_specs=[
            pl.BlockSpec((SUBLANES, d), lambda l, j: (0, 0)),
            pl.BlockSpec((None, d, tn), lambda l, j: (l, 0, j)),
            pl.BlockSpec((None, 1, tn), lambda l, j: (l, 0, j)),
        ],
        out_specs=pl.BlockSpec((None, SUBLANES, tn), lambda l, j: (l, 0, j)),
        out_shape=jax.ShapeDtypeStruct((depth, SUBLANES, md), F32),
        compiler_params=_cparams(("arbitrary", "arbitrary")),
        name="modulation",
    )(cc, w_mod, b_mod.reshape(depth, 1, md))


def _norm_mod(x, g, shift, scale):
    ms = jnp.mean(x * x, axis=-1, keepdims=True)
    y = x * lax.rsqrt(ms + EPS) * g
    return y * (1.0 + scale) + shift


def _inproj_kernel(x_ref, g_ref, sh_ref, sc_ref, c128_ref, s128_ref, c64_ref, sa_ref, sb_ref, w_ref, o_ref, hx_ref, *, rope):
    j = pl.program_id(1)

    @pl.when(j == 0)
    def _():
        hx_ref[...] = _norm_mod(x_ref[...], g_ref[...], sh_ref[...], sc_ref[...]).astype(BF16)

    acc = _dot(hx_ref[...], w_ref[...])
    groups = [acc[:, k * LANES:(k + 1) * LANES] for k in range(SEG // LANES)]

    def rope128(a):
        return a * c128_ref[...] + pltpu.roll(a, 64, 1) * s128_ref[...]

    def rope64(a):
        return a * c64_ref[...] + pltpu.roll(a, 96, 1) * sa_ref[...] + pltpu.roll(a, 32, 1) * sb_ref[...]

    def put(gs):
        o_ref[...] = jnp.concatenate(gs, axis=1).astype(BF16)

    k_scale = RET_D ** -0.5
    a_scale = HEAD_D ** -0.5
    ident = lambda a: a
    f_qr = rope128 if rope else ident
    f_kr = (lambda a: rope128(a) * k_scale) if rope else (lambda a: a * k_scale)
    f_qw = (lambda a: rope64(a) * a_scale) if rope else (lambda a: a * a_scale)
    f_kw = rope64 if rope else ident

    @pl.when(j == T_QR)
    def _():
        put([f_qr(a) for a in groups])

    @pl.when(j == T_KR)
    def _():
        put([f_kr(a) for a in groups])

    @pl.when(j == T_QW)
    def _():
        put([f_qw(a) for a in groups])

    @pl.when(j == T_QN)
    def _():
        put([a * a_scale for a in groups])

    @pl.when(j == T_KVW)
    def _():
        put([f_kw(groups[0])] + groups[1:])

    plain = (j != T_QR) & (j != T_KR) & (j != T_QW) & (j != T_QN) & (j != T_KVW)

    @pl.when(plain)
    def _():
        put(groups)


def _inproj(x, g, shift, scale, tabs, w, *, rope, tm):
    n, d = x.shape
    vec = lambda: pl.BlockSpec((1, d), lambda i, j: (0, 0))
    tab = lambda: pl.BlockSpec((tm, LANES), lambda i, j: (i, 0))
    return pl.pallas_call(
        functools.partial(_inproj_kernel, rope=rope),
        grid=(n // tm, P_TILES),
        in_specs=[pl.BlockSpec((tm, d), lambda i, j: (i, 0)), vec(), vec(), vec(),
                  tab(), tab(), tab(), tab(), tab(),
                  pl.BlockSpec((d, SEG), lambda i, j: (0, j))],
        out_specs=pl.BlockSpec((tm, SEG), lambda i, j: (i, j)),
        out_shape=jax.ShapeDtypeStruct((n, P_COLS), BF16),
        scratch_shapes=[pltpu.VMEM((tm, d), BF16)],
        compiler_params=_cparams(("arbitrary", "arbitrary")),
        name="inproj_rope" if rope else "inproj_ctx",
    )(x, g, shift, scale, *tabs, w)


def _ret_kernel(lg_ref, qf_ref, kf_ref, vf_ref, qb_ref, kb_ref, vb_ref, kc_ref, vc_ref, yf_ref, yb_ref,
                rf_ref, rb_ref, tab_ref, *, zero_init):
    c = pl.program_id(0)
    ch = RET_CHUNK
    dh = RET_D
    states = (rf_ref, rb_ref)

    @pl.when(c == 0)
    def _():
        lg = jax.nn.log_sigmoid(lg_ref[...])
        ii = _iota((ch, ch), 0, F32)
        jj = _iota((ch, ch), 1, F32)
        n_ctx = kc_ref.shape[0]
        mm = _iota((n_ctx, dh), 0, F32)
        for d in range(2):
            for h in range(RET_HEADS):
                row = lg[RET_HEADS * d + h:RET_HEADS * d + h + 1, :]
                l = jnp.broadcast_to(row, (ch, ch))
                if d == 0:
                    diff = ii - jj
                    dmat = jnp.where(diff >= 0.0, jnp.exp(jnp.maximum(diff, 0.0) * l), 0.0)
                    zeta = jnp.exp((ch - 1.0 - ii) * l)
                    xi = jnp.exp((ii + 1.0) * l)
                else:
                    diff = jj - ii
                    dmat = jnp.where(diff >= 1.0, jnp.exp(jnp.maximum(diff, 0.0) * l), 0.0)
                    zeta = jnp.exp(ii * l)
                    xi = jnp.exp((ch - ii) * l)
                tab_ref[d, h, 0] = dmat
                tab_ref[d, h, 1] = zeta
                tab_ref[d, h, 2] = xi
                tab_ref[d, h, 3] = jnp.exp(ch * l)
                if zero_init:
                    states[d][h] = jnp.zeros((dh, dh), F32)
                else:
                    lc = jnp.broadcast_to(row, (n_ctx, dh))
                    w = jnp.exp((n_ctx - 1.0 - mm) * lc) if d == 0 else jnp.exp(mm * lc)
                    sl = slice(h * dh, (h + 1) * dh)
                    kw = (kc_ref[:, sl].astype(F32) * w).astype(BF16)
                    states[d][h] = _dot_tn(kw, vc_ref[:, sl])

    for d, (q_ref, k_ref, v_ref, y_ref) in enumerate(((qf_ref, kf_ref, vf_ref, yf_ref), (qb_ref, kb_ref, vb_ref, yb_ref))):
        for h in range(RET_HEADS):
            sl = slice(h * dh, (h + 1) * dh)
            q, k, v = q_ref[:, sl], k_ref[:, sl], v_ref[:, sl]
            r = states[d][h]
            a = (_dot_nt(q, k) * tab_ref[d, h, 0]).astype(BF16)
            inner = _dot(a, v)
            cross = _dot(q, r.astype(BF16)) * tab_ref[d, h, 2]
            y_ref[:, sl] = inner + cross
            kz = (k.astype(F32) * tab_ref[d, h, 1]).astype(BF16)
            states[d][h] = tab_ref[d, h, 3] * r + _dot_tn(kz, v)


def _retention(p, p_ctx, lg_rows, *, zero_init):
    n = p.shape[0]
    nc = n // RET_CHUNK
    w = RET_HEADS * RET_D
    fwd = lambda t: pl.BlockSpec((RET_CHUNK, w), lambda c: (c, t))
    bwd = lambda t: pl.BlockSpec((RET_CHUNK, w), lambda c: (nc - 1 - c, t))
    ctx = lambda t: pl.BlockSpec((p_ctx.shape[0], w), lambda c: (0, t))
    return pl.pallas_call(
        functools.partial(_ret_kernel, zero_init=zero_init),
        grid=(nc,),
        in_specs=[pl.BlockSpec((SUBLANES, LANES), lambda c: (0, 0)),
                  fwd(T_QR), fwd(T_KR), fwd(T_VR), bwd(T_QR), bwd(T_KR), bwd(T_VR), ctx(T_KR), ctx(T_VR)],
        out_specs=[pl.BlockSpec((RET_CHUNK, w), lambda c: (c, 0)), pl.BlockSpec((RET_CHUNK, w), lambda c: (nc - 1 - c, 0))],
        out_shape=[jax.ShapeDtypeStruct((n, w), F32), jax.ShapeDtypeStruct((n, w), F32)],
        scratch_shapes=[pltpu.VMEM((RET_HEADS, RET_D, RET_D), F32), pltpu.VMEM((RET_HEADS, RET_D, RET_D), F32),
                        pltpu.VMEM((2, RET_HEADS, 4, RET_CHUNK, RET_CHUNK), F32)],
        compiler_params=_cparams(("arbitrary",)),
        name="retention_ctx" if zero_init else "retention",
    )(lg_rows, p, p, p, p, p, p, p_ctx, p_ctx)


def _half_mask(x, half):
    lane = _iota(x.shape, 1)
    keep = (lane < HEAD_D) if half == 0 else (lane >= HEAD_D)
    return jnp.where(keep, x, jnp.zeros_like(x))


def _softmax_pv(s, v, extra=None):
    m = jnp.max(s, axis=1, keepdims=True)
    if extra is not None:
        m = jnp.maximum(m, extra)
    p = jnp.exp(s - m)
    den = jnp.sum(p, axis=1, keepdims=True)
    if extra is not None:
        den = den + jnp.exp(extra - m)
    return _dot(p.astype(BF16), v) / den


def _win_kernel(sink_ref, q_ref, kp_ref, kc_ref, kn_ref, vp_ref, vc_ref, vn_ref, kx_ref, vx_ref, o_ref, *, n_tokens):
    nb = pl.program_id(0)
    wb = WIN_BLOCK
    k_all = jnp.concatenate([kp_ref[...], kc_ref[...], kn_ref[...], kx_ref[...]], axis=0)
    v_all = jnp.concatenate([vp_ref[...], vc_ref[...], vn_ref[...], vx_ref[...]], axis=0)
    n_cols = SEG // LANES
    qs = jnp.concatenate([q_ref[:, c * LANES:(c + 1) * LANES] for c in range(n_cols)], axis=0)
    m_rows = n_cols * wb
    qpos = _iota((m_rows, 3 * wb), 0) & (wb - 1)
    kpos = _iota((m_rows, 3 * wb), 1) - wb
    kabs = nb * wb + kpos
    valid = (jnp.abs(kpos - qpos) <= WINDOW) & (kabs >= 0) & (kabs < n_tokens)
    outs = []
    for g in range(WIN_KV_HEADS):
        s = _dot_nt(qs, _half_mask(k_all, g))
        s = jnp.concatenate([jnp.where(valid, s[:, :3 * wb], NEG_INF), s[:, 3 * wb:]], axis=1)
        sink = jnp.concatenate([jnp.full((wb, 1), sink_ref[n_cols * g + c], F32) for c in range(n_cols)], axis=0)
        outs.append(_softmax_pv(s, _half_mask(v_all, g), sink))
    o = outs[0] + outs[1]
    o_ref[...] = jnp.concatenate([o[c * wb:(c + 1) * wb, :] for c in range(n_cols)], axis=1).astype(BF16)


def _window_attention(p, p_ctx, sink):
    n = p.shape[0]
    nb = n // WIN_BLOCK
    l = p_ctx.shape[0]
    blk = lambda col, f: pl.BlockSpec((WIN_BLOCK, LANES), lambda i: (f(i), col))
    prev = lambda i: jnp.maximum(i - 1, 0)
    cur = lambda i: i
    nxt = lambda i: jnp.minimum(i + 1, nb - 1)
    return pl.pallas_call(
        functools.partial(_win_kernel, n_tokens=n),
        grid=(nb,),
        in_specs=[pl.BlockSpec(memory_space=pltpu.SMEM),
                  pl.BlockSpec((WIN_BLOCK, SEG), lambda i: (i, T_QW)),
                  blk(C_KW, prev), blk(C_KW, cur), blk(C_KW, nxt),
                  blk(C_VW, prev), blk(C_VW, cur), blk(C_VW, nxt),
                  pl.BlockSpec((l, LANES), lambda i: (0, C_KW)), pl.BlockSpec((l, LANES), lambda i: (0, C_VW))],
        out_specs=pl.BlockSpec((WIN_BLOCK, SEG), lambda i: (i, 0)),
        out_shape=jax.ShapeDtypeStruct((n, SEG), BF16),
        compiler_params=_cparams(("arbitrary",)),
        name="window_attention",
    )(sink, p, p, p, p, p, p, p, p_ctx, p_ctx)


def _na_slab_start(s, half_rows):
    return jnp.clip(s - 2, 0, half_rows - NA_SLAB_ROWS // 2)


def _na_kernel(q_ref, k0, k1, k2, k3, k4, v0, v1, v2, v3, v4, kx_ref, vx_ref, bias_ref, o_ref, *, rows):
    s = pl.program_id(0)
    w = GRID_W
    nq = 2 * w
    nk = NA_SLAB_ROWS * w
    r0 = 2 * s
    sb = 2 * _na_slab_start(s, rows // 2)
    delta = sb - r0
    k_refs = (k0, k1, k2, k3, k4)
    v_refs = (v0, v1, v2, v3, v4)
    sub = _iota((nq, nk), 0)
    lane = _iota((nq, nk), 1)
    shift = w.bit_length() - 1
    r_q = r0 + (sub >> shift)
    c_q = sub & (w - 1)
    r_k = sb + (lane >> shift)
    c_k = lane & (w - 1)
    r_start = jnp.clip(r_q - NA_ROWS // 2, 0, rows - NA_ROWS)
    c_start = jnp.clip(c_q - NA_COLS // 2, 0, w - NA_COLS)
    valid = (r_k >= r_start) & (r_k < r_start + NA_ROWS) & (c_k >= c_start) & (c_k < c_start + NA_COLS)
    for pair in range(NA_HEADS // 2):
        sl = slice(pair * LANES, (pair + 1) * LANES)
        q = q_ref[:, sl]
        k_all = jnp.concatenate([r[:, sl] for r in k_refs] + [kx_ref[:, sl]], axis=0)
        v_all = jnp.concatenate([r[:, sl] for r in v_refs] + [vx_ref[:, sl]], axis=0)
        out = None
        for u in range(2):
            h = 2 * pair + u
            sc = _dot_nt(q, _half_mask(k_all, u))
            bias = jnp.concatenate(
                [jnp.concatenate([bias_ref[h, delta + 2 * i - a + NA_ROWS + 1] for i in range(NA_SLAB_ROWS // 2)], axis=1)
                 for a in range(2)], axis=0)
            s_loc = jnp.where(valid, sc[:, :nk] + bias, NEG_INF)
            o = _softmax_pv(jnp.concatenate([s_loc, sc[:, nk:]], axis=1), _half_mask(v_all, u))
            out = o if out is None else out + o
        o_ref[:, sl] = out.astype(BF16)


def _na_attention(p, p_ctx, bias_tab):
    n = p.shape[0]
    rows = n // GRID_W
    steps = rows // 2
    l = p_ctx.shape[0]
    slab = lambda t, i: pl.BlockSpec((2 * GRID_W, SEG), lambda s: (_na_slab_start(s, steps) + i, t))
    nslab = NA_SLAB_ROWS // 2
    return pl.pallas_call(
        functools.partial(_na_kernel, rows=rows),
        grid=(steps,),
        in_specs=[pl.BlockSpec((2 * GRID_W, SEG), lambda s: (s, T_QN))]
        + [slab(T_KN, i) for i in range(nslab)] + [slab(T_VN, i) for i in range(nslab)]
        + [pl.BlockSpec((l, SEG), lambda s: (0, T_KN)), pl.BlockSpec((l, SEG), lambda s: (0, T_VN)),
           pl.BlockSpec(bias_tab.shape, lambda s: (0, 0, 0, 0))],
        out_specs=pl.BlockSpec((2 * GRID_W, SEG), lambda s: (s, 0)),
        out_shape=jax.ShapeDtypeStruct((n, SEG), BF16),
        compiler_params=_cparams(("arbitrary",)),
        name="neighbourhood_attention",
    )(p, *([p] * (2 * nslab)), p_ctx, p_ctx, bias_tab)


def _na_bias_table(rpb):
    n_pairs = 2 * NA_ROWS + 2
    n_r, n_c = 2 * NA_ROWS - 1, 2 * NA_COLS - 1
    rpb = rpb.astype(F32)
    rows = jnp.stack([jnp.stack([rpb[:, min(max(d + half - 2, 0), n_r - 1), :] for half in range(2)], axis=1)
                      for d in range(n_pairs)], axis=1)
    cq = jnp.arange(GRID_W)[:, None]
    ck = jnp.arange(GRID_W)[None, :]
    ci = jnp.clip(ck - cq, -(NA_COLS - 1), NA_COLS - 1) + NA_COLS - 1
    sel = (ci[None] == jnp.arange(n_c)[:, None, None]).astype(F32)
    tab = jnp.einsum("hdaj,jqk->hdqak", rows, sel, precision=HIGHEST)
    return tab.reshape(rpb.shape[0], n_pairs, GRID_W, 2 * GRID_W)


def _ctx_attn_kernel(sink_ref, p_ref, ow_ref, on_ref):
    l = p_ref.shape[0]
    n_cols = SEG // LANES
    k_all = p_ref[:, T_KVW * SEG:T_KVW * SEG + LANES]
    v_all = p_ref[:, T_KVW * SEG + LANES:T_KVW * SEG + 2 * LANES]
    qs = jnp.concatenate([p_ref[:, T_QW * SEG + c * LANES:T_QW * SEG + (c + 1) * LANES] for c in range(n_cols)], axis=0)
    outs = []
    for g in range(WIN_KV_HEADS):
        s = _dot_nt(qs, _half_mask(k_all, g))
        sink = jnp.concatenate([jnp.full((l, 1), sink_ref[n_cols * g + c], F32) for c in range(n_cols)], axis=0)
        outs.append(_softmax_pv(s, _half_mask(v_all, g), sink))
    o = outs[0] + outs[1]
    ow_ref[...] = jnp.concatenate([o[c * l:(c + 1) * l, :] for c in range(n_cols)], axis=1).astype(BF16)
    for pair in range(NA_HEADS // 2):
        sl = lambda t: slice(t * SEG + pair * LANES, t * SEG + (pair + 1) * LANES)
        q, k, v = p_ref[:, sl(T_QN)], p_ref[:, sl(T_KN)], p_ref[:, sl(T_VN)]
        out = None
        for u in range(2):
            o = _softmax_pv(_dot_nt(q, _half_mask(k, u)), _half_mask(v, u))
            out = o if out is None else out + o
        on_ref[:, pair * LANES:(pair + 1) * LANES] = out.astype(BF16)


def _ctx_attention(p_ctx, sink):
    l = p_ctx.shape[0]
    return pl.pallas_call(
        _ctx_attn_kernel,
        in_specs=[pl.BlockSpec(memory_space=pltpu.SMEM), pl.BlockSpec(p_ctx.shape, lambda: (0, 0))],
        out_specs=[pl.BlockSpec((l, SEG), lambda: (0, 0)), pl.BlockSpec((l, SEG), lambda: (0, 0))],
        out_shape=[jax.ShapeDtypeStruct((l, SEG), BF16), jax.ShapeDtypeStruct((l, SEG), BF16)],
        compiler_params=pltpu.CompilerParams(vmem_limit_bytes=VMEM_LIMIT),
        name="context_attention",
    )(sink, p_ctx)


def _merge_kernel(yf_ref, yb_ref, gr_ref, gn_ref, yw_ref, yn_ref, ga_ref, gb_ref, gc_ref,
                  wr_ref, ww_ref, wn_ref, wo_ref, x_ref, m2_ref, gf_ref, m3_ref, m4_ref, wrt_ref,
                  xo_ref, hx_ref, lt_ref):
    y = yf_ref[...] + yb_ref[...]
    parts = []
    for h in range(RET_HEADS):
        yh = y[:, h * RET_D:(h + 1) * RET_D]
        mu = jnp.mean(yh, axis=-1, keepdims=True)
        var = jnp.mean(jnp.square(yh - mu), axis=-1, keepdims=True)
        parts.append((yh - mu) * lax.rsqrt(var + EPS))
    g = gr_ref[...].astype(F32)
    ya = jnp.concatenate(parts, axis=1) * gn_ref[...] * (g * jax.nn.sigmoid(g))
    za = _dot(ya.astype(BF16), wr_ref[...])
    zb = _dot(yw_ref[...], ww_ref[...])
    zc = _dot(yn_ref[...], wn_ref[...])
    sig = lambda r: jax.nn.sigmoid(r[...].astype(F32))
    mix = sig(ga_ref) * za + sig(gb_ref) * zb + sig(gc_ref) * zc
    x_new = x_ref[...] + m2_ref[...] * _dot(mix.astype(BF16), wo_ref[...])
    xo_ref[...] = x_new
    h2 = _norm_mod(x_new, gf_ref[...], m3_ref[...], m4_ref[...])
    n_sub = h2.shape[1] // LANES
    for s in range(n_sub):
        hx_ref[pl.ds(s, h2.shape[0], stride=n_sub), :] = h2[:, s * LANES:(s + 1) * LANES]
    lt_ref[...] = _dot_nt(wrt_ref[...], h2, precision=HIGHEST)


def _merge(yf, yb, p, gn, yw, yn, wr, ww, wn, wo, x, m2, gf, m3, m4, wrt, *, tm):
    n, d = x.shape
    row = lambda wdt, t: pl.BlockSpec((tm, wdt), lambda i: (i, t))
    full = lambda a: pl.BlockSpec(a.shape, lambda i: (0,) * a.ndim)
    gate0 = T_GATES * SEG // d
    return pl.pallas_call(
        _merge_kernel,
        grid=(n // tm,),
        in_specs=[row(SEG, 0), row(SEG, 0), row(SEG, T_GR), full(gn), row(SEG, 0), row(SEG, 0),
                  row(d, gate0), row(d, gate0 + 1), row(d, gate0 + 2),
                  full(wr), full(ww), full(wn), full(wo), row(d, 0), full(m2), full(gf), full(m3), full(m4), full(wrt)],
        out_specs=[pl.BlockSpec((tm, d), lambda i: (i, 0)),
                   pl.BlockSpec((tm * (d // LANES), LANES), lambda i: (i, 0)),
                   pl.BlockSpec((N_EXPERTS, tm), lambda i: (0, i))],
        out_shape=[jax.ShapeDtypeStruct((n, d), F32), jax.ShapeDtypeStruct((n * (d // LANES), LANES), F32),
                   jax.ShapeDtypeStruct((N_EXPERTS, n), F32)],
        compiler_params=_cparams(("arbitrary",)),
        name="merge",
    )(yf, yb, p, gn, yw, yn, p, p, p, wr, ww, wn, wo, x, m2, gf, m3, m4, wrt)


def _route_kernel(lt_ref, idx_ref, gate_ref, pos_ref, goff_ref, aff_ref, thr_ref, *, cap, n_groups):
    e = pl.program_id(0)
    n_exp, ag, _ = lt_ref.shape
    capp = idx_ref.shape[-1]

    @pl.when(e == 0)
    def _():
        lt = lt_ref[...]
        ex = jnp.exp(lt - jnp.max(lt, axis=0, keepdims=True))
        aff = ex / jnp.sum(ex, axis=0, keepdims=True)
        real = _iota(aff.shape, 1) < n_groups
        aff = jnp.where(real, aff, 0.0)
        aff_ref[...] = aff
        bits = pltpu.bitcast(aff, I32)

        def body(k, t):
            cand = t | (jnp.int32(1) << (30 - k))
            cnt = jnp.sum(jnp.sum((bits >= cand).astype(F32), axis=2, keepdims=True), axis=1, keepdims=True)
            return jnp.where(cnt >= float(cap), cand, t)

        t = lax.fori_loop(0, 31, body, jnp.zeros((n_exp, 1, 1), I32))
        thr_ref[...] = jnp.broadcast_to(t, thr_ref.shape)

    a = aff_ref[e]
    bits = pltpu.bitcast(a, I32)
    t = thr_ref[e][0:1, :]
    gt = bits > t
    eq = bits == t
    tri_lane_strict = (_iota((LANES, LANES), 0) < _iota((LANES, LANES), 1)).astype(BF16)
    tri_lane_incl = (_iota((LANES, LANES), 0) <= _iota((LANES, LANES), 1)).astype(BF16)
    tri_grp_strict = (_iota((ag, ag), 1) < _iota((ag, ag), 0)).astype(BF16)
    tri_grp_incl = (_iota((ag, ag), 1) <= _iota((ag, ag), 0)).astype(BF16)

    def total(mask_f):
        return jnp.sum(jnp.sum(mask_f, axis=1, keepdims=True), axis=0, keepdims=True)

    def group_sum(mask_f):
        return jnp.broadcast_to(jnp.sum(mask_f, axis=1, keepdims=True), (ag, LANES)).astype(BF16)

    eq_f = eq.astype(F32)
    need = float(cap) - total(gt.astype(F32))
    rank_eq = _dot(tri_grp_strict, group_sum(eq_f)) + _dot(eq_f.astype(BF16), tri_lane_strict)
    sel = gt | (eq & (rank_eq < need))
    sel_f = sel.astype(F32)
    cl = _dot(sel_f.astype(BF16), tri_lane_incl)
    cg = _dot(tri_grp_incl, group_sum(sel_f))
    goff = cg - jnp.broadcast_to(jnp.sum(sel_f, axis=1, keepdims=True), (ag, LANES))
    pos_ref[...] = jnp.where(sel, goff + cl - 1.0, -1.0)
    diag = _iota((ag, ag), 0) == _iota((ag, ag), 1)
    goff_sq = goff if ag == LANES else goff[:, :ag]
    goff_ref[...] = jnp.sum(jnp.where(diag, goff_sq, 0.0), axis=0, keepdims=True).astype(I32)

    pp = _iota((ag, capp), 1, F32)
    cg_b = jnp.broadcast_to(cg[:, 0:1], (ag, capp))
    below = cg_b <= pp
    grp = jnp.sum(below.astype(F32), axis=0, keepdims=True)
    off = jnp.max(jnp.where(below, cg_b, 0.0), axis=0, keepdims=True)
    onehot = _iota((ag, capp), 0, F32) == grp
    in_grp = _dot_tn(cl.astype(BF16), onehot.astype(BF16))
    local = pp[0:1, :] - off
    lane_of = jnp.sum((in_grp <= local).astype(F32), axis=0, keepdims=True)
    live = pp[0:1, :] < float(cap)
    idx = jnp.where(live, grp * float(LANES) + lane_of, 0.0)
    idx_ref[...] = idx.astype(I32)
    aff_grp = _dot_tn(a, onehot.astype(F32), precision=HIGHEST)
    pick = _iota((LANES, capp), 0, F32) == lane_of
    gate = jnp.sum(jnp.where(pick, aff_grp, 0.0), axis=0, keepdims=True)
    gate_ref[...] = jnp.where(live, gate, 0.0)


def _route(logits_t, *, cap, n_groups):
    n_exp, ag, _ = logits_t.shape
    capp = -(-cap // LANES) * LANES
    return pl.pallas_call(
        functools.partial(_route_kernel, cap=cap, n_groups=n_groups),
        grid=(n_exp,),
        in_specs=[pl.BlockSpec(logits_t.shape, lambda e: (0, 0, 0))],
        out_specs=[pl.BlockSpec((None, 1, capp), lambda e: (e, 0, 0)),
                   pl.BlockSpec((None, 1, capp), lambda e: (e, 0, 0)),
                   pl.BlockSpec((None, ag, LANES), lambda e: (e, 0, 0)),
                   pl.BlockSpec((None, 1, ag), lambda e: (e, 0, 0))],
        out_shape=[jax.ShapeDtypeStruct((n_exp, 1, capp), I32), jax.ShapeDtypeStruct((n_exp, 1, capp), F32),
                   jax.ShapeDtypeStruct((n_exp, ag, LANES), F32), jax.ShapeDtypeStruct((n_exp, 1, ag), I32)],
        scratch_shapes=[pltpu.VMEM((n_exp, ag, LANES), F32), pltpu.VMEM((n_exp, SUBLANES, LANES), I32)],
        compiler_params=_cparams(("arbitrary",)),
        name="route",
    )(logits_t)


def _ffn_kernel(idx_ref, idxn_ref, gate_ref, hx_ref, wg_ref, wu_ref, wd_ref, o_ref, xbuf, sem, wgb, wub, wdb, *, tm):
    e = pl.program_id(0)
    j = pl.program_id(1)
    nt = pl.num_programs(1)
    step = e * nt + j
    last = pl.num_programs(0) * nt - 1

    sub = SUBLANES
    rows = tm * sub

    def row_copy(rows_ref, r, slot):
        src = hx_ref.at[pl.ds(pl.multiple_of(rows_ref[0, 0, r] * sub, sub), sub)]
        dst = xbuf.at[pl.ds(pl.multiple_of(slot * rows + r * sub, sub), sub)]
        return pltpu.make_async_copy(src, dst, sem.at[slot])

    def slot_copy(slot):
        return pltpu.make_async_copy(hx_ref.at[pl.ds(0, rows)], xbuf.at[pl.ds(pl.multiple_of(slot * rows, sub), rows)],
                                     sem.at[slot])

    @pl.when(step == 0)
    def _():
        def body(r, carry):
            row_copy(idx_ref, r, 0).start()
            return carry
        lax.fori_loop(0, tm, body, 0, unroll=8)

    @pl.when(j == 0)
    def _():
        wgb[...] = wg_ref[...].astype(BF16)
        wub[...] = wu_ref[...].astype(BF16)
        wdb[...] = wd_ref[...].astype(BF16)

    slot = step % 2
    nslot = 1 - slot
    for r in range(tm):
        row_copy(idxn_ref, r, nslot).start()
    slot_copy(slot).wait()
    base = slot * rows
    x = jnp.concatenate([xbuf[pl.ds(base + s, tm, stride=sub), :] for s in range(sub)], axis=1).astype(BF16)
    g = _dot(x, wgb[...])
    u = _dot(x, wub[...])
    hid = (g * jax.nn.sigmoid(g)) * u
    y = _dot(hid.astype(BF16), wdb[...])
    eye = _iota((tm, tm), 0) == _iota((tm, tm), 1)
    gcol = jnp.sum(jnp.where(eye, jnp.broadcast_to(gate_ref[0], (tm, tm)), 0.0), axis=1, keepdims=True)
    o_ref[...] = (y * gcol).astype(BF16)

    @pl.when(step == last)
    def _():
        slot_copy(nslot).wait()


def _expert_ffn(idx, gate, hx3, w_gate, w_up, w_down, *, layer, cap, tm):
    n_exp = idx.shape[0]
    d = w_gate.shape[2]
    f = w_gate.shape[3]
    assert d == SUBLANES * LANES, "a token row must be exactly one (8, 128) f32 tile"
    nt = cap // tm
    idx_t = idx[:, :, :cap].reshape(n_exp * nt, 1, tm)
    gate_t = gate[:, :, :cap].reshape(n_exp * nt, 1, tm)
    n_steps = n_exp * nt
    wspec = lambda a, b: pl.BlockSpec((None, None, a, b), lambda e, j: (layer, e, 0, 0))
    return pl.pallas_call(
        functools.partial(_ffn_kernel, tm=tm),
        grid=(n_exp, nt),
        in_specs=[pl.BlockSpec((1, 1, tm), lambda e, j: (e * nt + j, 0, 0), memory_space=pltpu.SMEM),
                  pl.BlockSpec((1, 1, tm), lambda e, j: (jnp.minimum(e * nt + j + 1, n_steps - 1), 0, 0), memory_space=pltpu.SMEM),
                  pl.BlockSpec((1, 1, tm), lambda e, j: (e * nt + j, 0, 0)),
                  pl.BlockSpec(memory_space=pl.ANY),
                  wspec(d, f), wspec(d, f), wspec(f, d)],
        out_specs=pl.BlockSpec((tm, d), lambda e, j: (e * nt + j, 0)),
        out_shape=jax.ShapeDtypeStruct((n_exp * cap, d), BF16),
        scratch_shapes=[pltpu.VMEM((2 * tm * SUBLANES, LANES), F32), pltpu.SemaphoreType.DMA((2,)),
                        pltpu.VMEM((d, f), BF16), pltpu.VMEM((d, f), BF16), pltpu.VMEM((f, d), BF16)],
        compiler_params=_cparams(("arbitrary", "arbitrary")),
        name="expert_ffn",
    )(idx_t, idx_t, gate_t, hx3, w_gate, w_up, w_down)


def _combine_kernel(ws_ref, nw_ref, *refs, n_exp, cap, win, final_norm):
    y_refs = refs[:n_exp]
    pos_ref, ye_ref, x_ref, m5_ref, gf_ref, o_ref, acc_ref, xwin, sem = refs[n_exp:]
    a = pl.program_id(0)
    ng = pl.num_programs(0)
    lp = _iota((win, LANES), 0, F32)
    per = max(1, MXU_DEPTH // win)

    def onehot(e, first_row):
        return (pos_ref[e, pl.ds(a, 1), :] - first_row.astype(F32)) == lp

    acc = None
    for e0 in range(0, n_exp, per):
        es = range(e0, min(e0 + per, n_exp))
        hot = jnp.concatenate([onehot(e, ws_ref[e * ng + a] - e * cap).astype(BF16) for e in es], axis=0)
        rows = jnp.concatenate([y_refs[e][...] for e in es], axis=0)
        part = _dot_tn(hot, rows)
        acc = part if acc is None else acc + part
    acc_ref[...] = acc

    for e in range(n_exp):
        first = ws_ref[e * ng + a] - e * cap

        def extra(k, carry, e=e, first=first):
            lo = first + k * win
            row = jnp.minimum(lo, cap - win)
            cp = pltpu.make_async_copy(ye_ref.at[pl.ds(pl.multiple_of(e * cap + row, BF16_ROWS), win)], xwin, sem)
            cp.start()
            cp.wait()
            pos = pos_ref[e, pl.ds(a, 1), :]
            hot = ((pos - row.astype(F32)) == lp) & (pos >= lo.astype(F32))
            acc_ref[...] += _dot_tn(hot.astype(BF16), xwin[...])
            return carry

        lax.fori_loop(1, nw_ref[e * ng + a], extra, 0)

    x_new = x_ref[...] + m5_ref[...] * acc_ref[...]
    if final_norm:
        ms = jnp.mean(x_new * x_new, axis=-1, keepdims=True)
        x_new = x_new * lax.rsqrt(ms + EPS) * gf_ref[...]
    o_ref[...] = x_new


def _combine(ye, posmap, goff, x, m5, g_final, *, cap, final_norm):
    n, d = x.shape
    n_exp = posmap.shape[0]
    ng = n // LANES
    win = min(COMBINE_WINDOW, cap)
    goff = goff[:, 0, :ng]
    end = jnp.concatenate([goff[:, 1:], jnp.full((n_exp, 1), cap, I32)], axis=1)
    start = jnp.minimum((goff // BF16_ROWS) * BF16_ROWS, cap - win)
    n_win = jnp.maximum((end - start + win - 1) // win, 1).reshape(-1)
    wstart = (start + jnp.arange(n_exp, dtype=I32)[:, None] * cap).reshape(-1)
    yspec = lambda e: pl.BlockSpec((pl.Element(win), pl.Element(d)),
                                   lambda a, ws, nw: (pl.multiple_of(ws[e * ng + a], BF16_ROWS), 0))
    gs = pltpu.PrefetchScalarGridSpec(
        num_scalar_prefetch=2,
        grid=(ng,),
        in_specs=[yspec(e) for e in range(n_exp)]
        + [pl.BlockSpec(posmap.shape, lambda a, ws, nw: (0, 0, 0)),
           pl.BlockSpec(memory_space=pl.ANY),
           pl.BlockSpec((LANES, d), lambda a, ws, nw: (a, 0)),
           pl.BlockSpec((1, d), lambda a, ws, nw: (0, 0)), pl.BlockSpec((1, d), lambda a, ws, nw: (0, 0))],
        out_specs=pl.BlockSpec((LANES, d), lambda a, ws, nw: (a, 0)),
        scratch_shapes=[pltpu.VMEM((LANES, d), F32), pltpu.VMEM((win, d), BF16), pltpu.SemaphoreType.DMA(())],
    )
    return pl.pallas_call(
        functools.partial(_combine_kernel, n_exp=n_exp, cap=cap, win=win, final_norm=final_norm),
        grid_spec=gs,
        out_shape=jax.ShapeDtypeStruct((n, d), F32),
        compiler_params=_cparams(("arbitrary",)),
        name="combine_final" if final_norm else "combine",
    )(wstart, n_win, *([ye] * n_exp), posmap, ye, x, m5, g_final)


def _rope_tables(n):
    t = jnp.arange(n)
    row = (t // GRID_W).astype(F32)
    col = (t % GRID_W).astype(F32)

    def cs(d):
        nf = d // 4
        inv = ROPE_BASE ** (-jnp.arange(nf, dtype=F32) / nf)
        ang = jnp.concatenate([row[:, None] * inv, col[:, None] * inv], axis=-1)
        return jnp.cos(ang), jnp.sin(ang)

    c, s = cs(RET_D)
    c128 = jnp.concatenate([c, c], axis=1)
    s128 = jnp.concatenate([-s, s], axis=1)
    c, s = cs(HEAD_D)
    z = jnp.zeros_like(s)
    c64 = jnp.concatenate([c, c, c, c], axis=1)
    sa = jnp.concatenate([-s, z, -s, z], axis=1)
    sb = jnp.concatenate([z, s, z, s], axis=1)
    return c128, s128, c64, sa, sb


WIN_HEAD_ORDER = (0, 4, 1, 5, 2, 6, 3, 7)


def _permute_in_weight(w):
    d = w.shape[0]
    sizes = (("q_r", 512), ("k_r", 512), ("v_r", 512), ("g_r", 512), ("q_w", 512), ("k_w", 128), ("v_w", 128),
             ("q_n", 512), ("k_n", 512), ("v_n", 512), ("gates", 3 * d))
    off, lay = 0, {}
    for name, size in sizes:
        lay[name] = (off, off + size)
        off += size
    seg = lambda name: w[:, lay[name][0]:lay[name][1]]
    q0 = lay["q_w"][0]
    q_w = [w[:, q0 + h * HEAD_D:q0 + (h + 1) * HEAD_D] for h in WIN_HEAD_ORDER]
    parts = [seg("q_r"), seg("k_r"), seg("v_r"), seg("g_r")] + q_w + [seg("q_n"), seg("k_n"), seg("v_n"), seg("gates"),
                                                                       seg("k_w"), seg("v_w")]
    parts.append(jnp.zeros((d, P_COLS - off), w.dtype))
    return jnp.concatenate(parts, axis=1).astype(BF16)


def _permute_win_rows(w):
    return jnp.concatenate([w[h * HEAD_D:(h + 1) * HEAD_D] for h in WIN_HEAD_ORDER], axis=0).astype(BF16)


def kernel(x, c, ctx, c_ctx, w_mod, b_mod, g_mix, g_ffn, w_in, ret_decay_logit, ret_gn, w_ret, win_sink, w_win, na_rpb,
           w_na, w_out, w_router, w_exp_gate, w_exp_up, w_exp_down, g_final):
    _, n, d = x.shape
    l = ctx.shape[1]
    depth = w_in.shape[0]
    xs, cs = x[0], ctx[0]
    cc = jnp.zeros((SUBLANES, d), F32).at[0].set(c[0]).at[1].set(c_ctx)
    mods = _modulation(cc, w_mod, b_mod)
    tabs_x = _rope_tables(n)
    tabs_c = tuple(jnp.zeros((l, LANES), F32) for _ in range(5))
    cap_x = CAPACITY_FACTOR * n // N_EXPERTS
    cap_c = CAPACITY_FACTOR * l // N_EXPERTS
    grp_c = -(-(l // LANES) // BF16_ROWS) * BF16_ROWS
    vec = lambda v: v.reshape(1, -1)

    for layer in range(depth):
        need_ctx = layer < depth - 1
        last = layer == depth - 1
        mx = [vec(mods[layer, 0, k * d:(k + 1) * d]) for k in range(N_MOD)]
        mc = [vec(mods[layer, 1, k * d:(k + 1) * d]) for k in range(N_MOD)]
        w_p = _permute_in_weight(w_in[layer])
        wr = w_ret[layer].astype(BF16)
        ww = _permute_win_rows(w_win[layer])
        wn = w_na[layer].astype(BF16)
        wo = w_out[layer].astype(BF16)
        wrt = w_router[layer].T
        sink = win_sink[layer].astype(F32)
        lg_rows = jnp.broadcast_to(ret_decay_logit[layer].astype(F32).reshape(2 * RET_HEADS, 1), (2 * RET_HEADS, LANES))
        gn = vec(ret_gn[layer])
        bias_tab = _na_bias_table(na_rpb[layer])
        gmix, gffn = vec(g_mix[layer]), vec(g_ffn[layer])

        p_x = _inproj(xs, gmix, mx[0], mx[1], tabs_x, w_p, rope=True, tm=2048)
        p_c = _inproj(cs, gmix, mc[0], mc[1], tabs_c, w_p, rope=False, tm=l)

        yf, yb = _retention(p_x, p_c, lg_rows, zero_init=False)
        yw = _window_attention(p_x, p_c, sink)
        yn = _na_attention(p_x, p_c, bias_tab)
        xs, hx3, lt = _merge(yf, yb, p_x, gn, yw, yn, wr, ww, wn, wo, xs, mx[2], gffn, mx[3], mx[4], wrt, tm=512)

        idx, gate, posmap, goff = _route(lt.reshape(N_EXPERTS, n // LANES, LANES), cap=cap_x, n_groups=n // LANES)
        ye = _expert_ffn(idx, gate, hx3, w_exp_gate, w_exp_up, w_exp_down, layer=layer, cap=cap_x, tm=256)
        xs = _combine(ye, posmap, goff, xs, mx[5], vec(g_final), cap=cap_x, final_norm=last)

        if need_ctx:
            yfc, ybc = _retention(p_c, p_c, lg_rows, zero_init=True)
            ywc, ync = _ctx_attention(p_c, sink)
            cs, hc3, ltc = _merge(yfc, ybc, p_c, gn, ywc, ync, wr, ww, wn, wo, cs, mc[2], gffn, mc[3], mc[4], wrt, tm=l)
            ltc = jnp.pad(ltc.reshape(N_EXPERTS, l // LANES, LANES), ((0, 0), (0, grp_c - l // LANES), (0, 0)))
            idc, gtc, posc, goffc = _route(ltc, cap=cap_c, n_groups=l // LANES)
            yec = _expert_ffn(idc, gtc, hc3, w_exp_gate, w_exp_up, w_exp_down, layer=layer, cap=cap_c, tm=cap_c)
            cs = _combine(yec, posc, goffc, cs, mc[5], vec(g_final), cap=cap_c, final_norm=False)

    return xs[None]
```

```python
import functools

import jax
import jax.numpy as jnp
from jax import lax
from jax.experimental import pallas as pl
from jax.experimental.pallas import tpu as pltpu

F32, BF16, I32 = jnp.float32, jnp.bfloat16, jnp.int32
HIGHEST = lax.Precision.HIGHEST

GRID_W = 64
RET_HEADS, RET_D, RET_CHUNK = 4, 128, 128
WIN_HEADS, WIN_KV_HEADS, HEAD_D, WINDOW, WIN_BLOCK = 8, 2, 64, 128, 128
NA_HEADS, NA_ROWS, NA_COLS = 8, 8, 16
N_EXPERTS, CAPACITY_FACTOR = 16, 2
N_MOD = 6
ROPE_BASE = 10000.0
EPS = 1e-6
NEG_INF = -1e30

LANES = 128
SUBLANES = 8
BF16_ROWS = 16
VMEM_LIMIT = 56 * 1024 * 1024

SEG = 512
T_QR, T_KR, T_VR, T_GR, T_QW, T_QN, T_KN, T_VN, T_GATES, T_KVW = 0, 1, 2, 3, 4, 5, 6, 7, 8, 14
P_TILES = 15
P_COLS = P_TILES * SEG
C_KW = T_KVW * SEG // LANES
C_VW = C_KW + 1
WIN_QBLOCKS = 2
NA_QROWS = 4
NA_SLAB_ROWS = 12
NA_BIAS_ZERO = NA_SLAB_ROWS - 2
NA_BIAS_PAIRS = NA_BIAS_ZERO + NA_SLAB_ROWS - 1
COMBINE_GROUPS = 4
COMBINE_WINDOW = 128
MXU_DEPTH = 256


def _cparams(sem):
    return pltpu.CompilerParams(dimension_semantics=sem, vmem_limit_bytes=VMEM_LIMIT)


def _dot(a, b):
    return jnp.dot(a, b, preferred_element_type=F32)


def _dot_nt(a, b, precision=None):
    return lax.dot_general(a, b, (((1,), (1,)), ((), ())), precision=precision, preferred_element_type=F32)


def _dot_tn(a, b, precision=None):
    return lax.dot_general(a, b, (((0,), (0,)), ((), ())), precision=precision, preferred_element_type=F32)


def _iota(shape, dim, dtype=I32):
    return lax.broadcasted_iota(I32, shape, dim).astype(dtype)


def _mod_kernel(s_ref, w_ref, b_ref, o_ref):
    s = s_ref[...]
    s = s * jax.nn.sigmoid(s)
    o_ref[...] = jnp.dot(s, w_ref[...], precision=HIGHEST, preferred_element_type=F32) + b_ref[...]


def _modulation(cc, w_mod, b_mod):
    depth, d, md = w_mod.shape
    tn = 1536
    return pl.pallas_call(
        _mod_kernel,
        grid=(depth, md // tn),
        in_specs=[
            pl.BlockSpec((SUBLANES, d), lambda l, j: (0, 0)),
            pl.BlockSpec((None, d, tn), lambda l, j: (l, 0, j)),
            pl.BlockSpec((None, 1, tn), lambda l, j: (l, 0, j)),
        ],
        out_specs=pl.BlockSpec((None, SUBLANES, tn), lambda l, j: (l, 0, j)),
        out_shape=jax.ShapeDtypeStruct((depth, SUBLANES, md), F32),
        compiler_params=_cparams(("arbitrary", "arbitrary")),
        name="modulation",
    )(cc, w_mod, b_mod.reshape(depth, 1, md))


def _norm_mod(x, g, shift, scale):
    ms = jnp.mean(x * x, axis=-1, keepdims=True)
    y = x * lax.rsqrt(ms + EPS) * g
    return y * (1.0 + scale) + shift


def _inproj_kernel(x_ref, g_ref, sh_ref, sc_ref, c128_ref, s128_ref, c64_ref, sa_ref, sb_ref, w_ref, o_ref, hx_ref, *, rope):
    j = pl.program_id(1)

    @pl.when(j == 0)
    def _():
        hx_ref[...] = _norm_mod(x_ref[...], g_ref[...], sh_ref[...], sc_ref[...]).astype(BF16)

    acc = _dot(hx_ref[...], w_ref[...])
    groups = [acc[:, k * LANES:(k + 1) * LANES] for k in range(SEG // LANES)]

    def rope128(a):
        return a * c128_ref[...] + pltpu.roll(a, 64, 1) * s128_ref[...]

    def rope64(a):
        return a * c64_ref[...] + pltpu.roll(a, 96, 1) * sa_ref[...] + pltpu.roll(a, 32, 1) * sb_ref[...]

    def put(gs):
        o_ref[...] = jnp.concatenate(gs, axis=1).astype(BF16)

    k_scale = RET_D ** -0.5
    a_scale = HEAD_D ** -0.5
    ident = lambda a: a
    f_qr = rope128 if rope else ident
    f_kr = (lambda a: rope128(a) * k_scale) if rope else (lambda a: a * k_scale)
    f_qw = (lambda a: rope64(a) * a_scale) if rope else (lambda a: a * a_scale)
    f_kw = rope64 if rope else ident

    @pl.when(j == T_QR)
    def _():
        put([f_qr(a) for a in groups])

    @pl.when(j == T_KR)
    def _():
        put([f_kr(a) for a in groups])

    @pl.when(j == T_QW)
    def _():
        put([f_qw(a) for a in groups])

    @pl.when(j == T_QN)
    def _():
        put([a * a_scale for a in groups])

    @pl.when(j == T_KVW)
    def _():
        put([f_kw(groups[0])] + groups[1:])

    plain = (j != T_QR) & (j != T_KR) & (j != T_QW) & (j != T_QN) & (j != T_KVW)

    @pl.when(plain)
    def _():
        put(groups)


def _inproj(x, g, shift, scale, tabs, w, *, rope, tm):
    n, d = x.shape
    vec = lambda: pl.BlockSpec((1, d), lambda i, j: (0, 0))
    tab = lambda: pl.BlockSpec((tm, LANES), lambda i, j: (i, 0))
    return pl.pallas_call(
        functools.partial(_inproj_kernel, rope=rope),
        grid=(n // tm, P_TILES),
        in_specs=[pl.BlockSpec((tm, d), lambda i, j: (i, 0)), vec(), vec(), vec(),
                  tab(), tab(), tab(), tab(), tab(),
                  pl.BlockSpec((d, SEG), lambda i, j: (0, j))],
        out_specs=pl.BlockSpec((tm, SEG), lambda i, j: (i, j)),
        out_shape=jax.ShapeDtypeStruct((n, P_COLS), BF16),
        scratch_shapes=[pltpu.VMEM((tm, d), BF16)],
        compiler_params=_cparams(("arbitrary", "arbitrary")),
        name="inproj_rope" if rope else "inproj_ctx",
    )(x, g, shift, scale, *tabs, w)


def _ret_kernel(lg_ref, qf_ref, kf_ref, vf_ref, qb_ref, kb_ref, vb_ref, kc_ref, vc_ref, yf_ref, yb_ref,
                rf_ref, rb_ref, tab_ref, *, zero_init):
    c = pl.program_id(0)
    ch = RET_CHUNK
    dh = RET_D
    states = (rf_ref, rb_ref)

    @pl.when(c == 0)
    def _():
        lg = jax.nn.log_sigmoid(lg_ref[...])
        ii = _iota((ch, ch), 0, F32)
        jj = _iota((ch, ch), 1, F32)
        n_ctx = kc_ref.shape[0]
        mm = _iota((n_ctx, dh), 0, F32)
        for d in range(2):
            for h in range(RET_HEADS):
                row = lg[RET_HEADS * d + h:RET_HEADS * d + h + 1, :]
                l = jnp.broadcast_to(row, (ch, ch))
                if d == 0:
                    diff = ii - jj
                    dmat = jnp.where(diff >= 0.0, jnp.exp(jnp.maximum(diff, 0.0) * l), 0.0)
                    zeta = jnp.exp((ch - 1.0 - ii) * l)
                    xi = jnp.exp((ii + 1.0) * l)
                else:
                    diff = jj - ii
                    dmat = jnp.where(diff >= 1.0, jnp.exp(jnp.maximum(diff, 0.0) * l), 0.0)
                    zeta = jnp.exp(ii * l)
                    xi = jnp.exp((ch - ii) * l)
                tab_ref[d, h, 0] = dmat
                tab_ref[d, h, 1] = zeta
                tab_ref[d, h, 2] = xi
                tab_ref[d, h, 3] = jnp.exp(ch * l)
                if zero_init:
                    states[d][h] = jnp.zeros((dh, dh), F32)
                else:
                    lc = jnp.broadcast_to(row, (n_ctx, dh))
                    w = jnp.exp((n_ctx - 1.0 - mm) * lc) if d == 0 else jnp.exp(mm * lc)
                    sl = slice(h * dh, (h + 1) * dh)
                    kw = (kc_ref[:, sl].astype(F32) * w).astype(BF16)
                    states[d][h] = _dot_tn(kw, vc_ref[:, sl])

    for d, (q_ref, k_ref, v_ref, y_ref) in enumerate(((qf_ref, kf_ref, vf_ref, yf_ref), (qb_ref, kb_ref, vb_ref, yb_ref))):
        for h in range(RET_HEADS):
            sl = slice(h * dh, (h + 1) * dh)
            q, k, v = q_ref[:, sl], k_ref[:, sl], v_ref[:, sl]
            r = states[d][h]
            a = (_dot_nt(q, k) * tab_ref[d, h, 0]).astype(BF16)
            inner = _dot(a, v)
            cross = _dot(q, r.astype(BF16)) * tab_ref[d, h, 2]
            y_ref[:, sl] = inner + cross
            kz = (k.astype(F32) * tab_ref[d, h, 1]).astype(BF16)
            states[d][h] = tab_ref[d, h, 3] * r + _dot_tn(kz, v)


def _retention(p, p_ctx, lg_rows, *, zero_init):
    n = p.shape[0]
    nc = n // RET_CHUNK
    w = RET_HEADS * RET_D
    fwd = lambda t: pl.BlockSpec((RET_CHUNK, w), lambda c: (c, t))
    bwd = lambda t: pl.BlockSpec((RET_CHUNK, w), lambda c: (nc - 1 - c, t))
    ctx = lambda t: pl.BlockSpec((p_ctx.shape[0], w), lambda c: (0, t))
    return pl.pallas_call(
        functools.partial(_ret_kernel, zero_init=zero_init),
        grid=(nc,),
        in_specs=[pl.BlockSpec((SUBLANES, LANES), lambda c: (0, 0)),
                  fwd(T_QR), fwd(T_KR), fwd(T_VR), bwd(T_QR), bwd(T_KR), bwd(T_VR), ctx(T_KR), ctx(T_VR)],
        out_specs=[pl.BlockSpec((RET_CHUNK, w), lambda c: (c, 0)), pl.BlockSpec((RET_CHUNK, w), lambda c: (nc - 1 - c, 0))],
        out_shape=[jax.ShapeDtypeStruct((n, w), F32), jax.ShapeDtypeStruct((n, w), F32)],
        scratch_shapes=[pltpu.VMEM((RET_HEADS, RET_D, RET_D), F32), pltpu.VMEM((RET_HEADS, RET_D, RET_D), F32),
                        pltpu.VMEM((2, RET_HEADS, 4, RET_CHUNK, RET_CHUNK), F32)],
        compiler_params=_cparams(("arbitrary",)),
        name="retention_ctx" if zero_init else "retention",
    )(lg_rows, p, p, p, p, p, p, p_ctx, p_ctx)


def _half_mask(x, half):
    lane = _iota(x.shape, 1)
    keep = (lane < HEAD_D) if half == 0 else (lane >= HEAD_D)
    return jnp.where(keep, x, jnp.zeros_like(x))


def _softmax_pv(s, v, extra=None):
    m = jnp.max(s, axis=1, keepdims=True)
    if extra is not None:
        m = jnp.maximum(m, extra)
    p = jnp.exp(s - m)
    den = jnp.sum(p, axis=1, keepdims=True)
    if extra is not None:
        den = den + jnp.exp(extra - m)
    return _dot(p.astype(BF16), v) / den


def _softmax_pv_t(st, v, extra=None):
    m = jnp.max(st, axis=0, keepdims=True)
    if extra is not None:
        m = jnp.maximum(m, extra)
    p = jnp.exp(st - m)
    den = jnp.sum(p, axis=0, keepdims=True)
    if extra is not None:
        den = den + jnp.exp(extra - m)
    return _dot_tn(v, p.astype(BF16)) / den


def _win_kernel(sink_ref, q_ref, *refs):
    nkb = WIN_QBLOCKS + 2
    k_refs, v_refs = refs[:nkb], refs[nkb:2 * nkb]
    kx_ref, vx_ref, mask_ref, o_ref = refs[2 * nkb:]
    s = pl.program_id(0)
    last = pl.num_programs(0) - 1
    nq = WIN_QBLOCKS * WIN_BLOCK
    nk = nkb * WIN_BLOCK
    k_all = jnp.concatenate([r[...] for r in k_refs] + [kx_ref[...]], axis=0)
    v_all = jnp.concatenate([r[...] for r in v_refs] + [vx_ref[...]], axis=0)
    n_cols = SEG // LANES
    qs = jnp.concatenate([q_ref[:, c * LANES:(c + 1) * LANES] for c in range(n_cols)], axis=0)
    variant = jnp.where(s == 0, 1, 0) + jnp.where(s == last, 2, 0)
    valid = mask_ref[variant] > 0.5
    valid = jnp.concatenate([valid] * n_cols, axis=1)
    out_t = None
    for g in range(WIN_KV_HEADS):
        st = _dot_nt(_half_mask(k_all, g), qs)
        st = jnp.concatenate([jnp.where(valid, st[:nk], NEG_INF), st[nk:]], axis=0)
        sink = jnp.concatenate([jnp.full((1, nq), sink_ref[n_cols * g + c], F32) for c in range(n_cols)], axis=1)
        o = _softmax_pv_t(st, _half_mask(v_all, g), sink)
        out_t = o if out_t is None else out_t + o
    o = out_t.T
    o_ref[...] = jnp.concatenate([o[c * nq:(c + 1) * nq, :] for c in range(n_cols)], axis=1).astype(BF16)


def _win_valid_table():
    import numpy as np
    nkb = WIN_QBLOCKS + 2
    qpos = np.arange(WIN_QBLOCKS * WIN_BLOCK)[None, :]
    kpos = np.arange(nkb * WIN_BLOCK)[:, None] - WIN_BLOCK
    band = np.abs(kpos - qpos) <= WINDOW
    tabs = []
    for variant in range(4):
        ok = band.copy()
        if variant & 1:
            ok &= kpos >= 0
        if variant & 2:
            ok &= kpos < WIN_QBLOCKS * WIN_BLOCK
        tabs.append(ok)
    return jnp.asarray(np.stack(tabs).astype(np.float32))


def _window_attention(p, p_ctx, sink):
    n = p.shape[0]
    nb = n // WIN_BLOCK
    assert nb % WIN_QBLOCKS == 0
    l = p_ctx.shape[0]
    nkb = WIN_QBLOCKS + 2
    blk = lambda col, j: pl.BlockSpec((WIN_BLOCK, LANES), lambda i: (jnp.clip(WIN_QBLOCKS * i - 1 + j, 0, nb - 1), col))
    mask_tab = _win_valid_table()
    return pl.pallas_call(
        _win_kernel,
        grid=(nb // WIN_QBLOCKS,),
        in_specs=[pl.BlockSpec(memory_space=pltpu.SMEM),
                  pl.BlockSpec((WIN_QBLOCKS * WIN_BLOCK, SEG), lambda i: (i, T_QW))]
        + [blk(C_KW, j) for j in range(nkb)] + [blk(C_VW, j) for j in range(nkb)]
        + [pl.BlockSpec((l, LANES), lambda i: (0, C_KW)), pl.BlockSpec((l, LANES), lambda i: (0, C_VW)),
           pl.BlockSpec(mask_tab.shape, lambda i: (0, 0, 0))],
        out_specs=pl.BlockSpec((WIN_QBLOCKS * WIN_BLOCK, SEG), lambda i: (i, 0)),
        out_shape=jax.ShapeDtypeStruct((n, SEG), BF16),
        compiler_params=_cparams(("arbitrary",)),
        name="window_attention",
    )(sink, p, *([p] * (2 * nkb)), p_ctx, p_ctx, mask_tab)


def _na_slab_start(s, half_rows):
    return jnp.clip((NA_QROWS // 2) * s - NA_ROWS // 4, 0, half_rows - NA_SLAB_ROWS // 2)


def _na_kernel(q_ref, *refs, rows):
    nslab = NA_SLAB_ROWS // 2
    k_refs, v_refs = refs[:nslab], refs[nslab:2 * nslab]
    kx_ref, vx_ref, bias_ref, mask_ref, o_ref = refs[2 * nslab:]
    s = pl.program_id(0)
    last = pl.num_programs(0) - 1
    w = GRID_W
    nq = NA_QROWS * w
    nk = NA_SLAB_ROWS * w
    delta = 2 * _na_slab_start(s, rows // 2) - NA_QROWS * s
    variant = jnp.where(s == 0, 0, jnp.where(s == last, 2, 1))
    valid = mask_ref[variant] > 0.5
    valid = jnp.concatenate([valid, valid], axis=1)
    row = _iota((LANES, nq), 0)
    for pair in range(NA_HEADS // 2):
        sl = slice(pair * LANES, (pair + 1) * LANES)
        q = q_ref[:, sl]
        qs = jnp.concatenate([_half_mask(q, 0), _half_mask(q, 1)], axis=0)
        k_all = jnp.concatenate([r[:, sl] for r in k_refs] + [kx_ref[:, sl]], axis=0)
        v_all = jnp.concatenate([r[:, sl] for r in v_refs] + [vx_ref[:, sl]], axis=0)
        st = _dot_nt(k_all, qs)
        bias = jnp.concatenate(
            [jnp.concatenate([bias_ref[2 * pair + u, delta + 2 * i - a + NA_BIAS_ZERO]
                              for u in range(2) for a in range(0, NA_QROWS, 2)], axis=1) for i in range(nslab)], axis=0)
        s_loc = jnp.where(valid, st[:nk] + bias, NEG_INF)
        o = _softmax_pv_t(jnp.concatenate([s_loc, st[nk:]], axis=0), v_all)
        o_ref[:, sl] = jnp.where(row < HEAD_D, o[:, :nq], o[:, nq:]).T.astype(BF16)


def _na_valid_table():
    import numpy as np
    kk = np.arange(NA_SLAB_ROWS)[:, None, None, None]
    ck = np.arange(GRID_W)[None, :, None, None]
    a = np.arange(NA_QROWS)[None, None, :, None]
    cq = np.arange(GRID_W)[None, None, None, :]
    c_start = np.clip(cq - NA_COLS // 2, 0, GRID_W - NA_COLS)
    col_ok = (ck >= c_start) & (ck < c_start + NA_COLS)
    first_row = (0 * a, a, 0 * a + NA_SLAB_ROWS - NA_ROWS)
    tabs = [(col_ok & (kk >= f) & (kk < f + NA_ROWS)).reshape(NA_SLAB_ROWS * GRID_W, NA_QROWS * GRID_W) for f in first_row]
    return jnp.asarray(np.stack(tabs).astype(np.float32))


def _na_attention(p, p_ctx, bias_tab):
    n = p.shape[0]
    rows = n // GRID_W
    assert rows % NA_QROWS == 0 and rows >= NA_SLAB_ROWS
    steps = rows // NA_QROWS
    l = p_ctx.shape[0]
    slab = lambda t, i: pl.BlockSpec((2 * GRID_W, SEG), lambda s: (_na_slab_start(s, rows // 2) + i, t))
    nslab = NA_SLAB_ROWS // 2
    mask_tab = _na_valid_table()
    return pl.pallas_call(
        functools.partial(_na_kernel, rows=rows),
        grid=(steps,),
        in_specs=[pl.BlockSpec((NA_QROWS * GRID_W, SEG), lambda s: (s, T_QN))]
        + [slab(T_KN, i) for i in range(nslab)] + [slab(T_VN, i) for i in range(nslab)]
        + [pl.BlockSpec((l, SEG), lambda s: (0, T_KN)), pl.BlockSpec((l, SEG), lambda s: (0, T_VN)),
           pl.BlockSpec(bias_tab.shape, lambda s: (0, 0, 0, 0)), pl.BlockSpec(mask_tab.shape, lambda s: (0, 0, 0))],
        out_specs=pl.BlockSpec((NA_QROWS * GRID_W, SEG), lambda s: (s, 0)),
        out_shape=jax.ShapeDtypeStruct((n, SEG), BF16),
        compiler_params=_cparams(("arbitrary",)),
        name="neighbourhood_attention",
    )(p, *([p] * (2 * nslab)), p_ctx, p_ctx, bias_tab, mask_tab)


def _na_bias_table(rpb):
    n_r, n_c = 2 * NA_ROWS - 1, 2 * NA_COLS - 1
    rpb = rpb.astype(F32)
    pick = lambda e, uk, wq: rpb[:, min(max(e - NA_BIAS_ZERO + uk - wq + NA_ROWS - 1, 0), n_r - 1), :]
    rows = jnp.stack([jnp.stack([jnp.stack([pick(e, uk, wq) for wq in range(2)], axis=1) for uk in range(2)], axis=1)
                      for e in range(NA_BIAS_PAIRS)], axis=1)
    ck = jnp.arange(GRID_W)[:, None]
    cq = jnp.arange(GRID_W)[None, :]
    ci = jnp.clip(ck - cq, -(NA_COLS - 1), NA_COLS - 1) + NA_COLS - 1
    sel = (ci[None] == jnp.arange(n_c)[:, None, None]).astype(F32)
    tab = jnp.einsum("heuwj,jkq->heukwq", rows, sel, precision=HIGHEST)
    return tab.reshape(rpb.shape[0], NA_BIAS_PAIRS, 2 * GRID_W, 2 * GRID_W)


def _ctx_attn_kernel(sink_ref, p_ref, ow_ref, on_ref):
    l = p_ref.shape[0]
    n_cols = SEG // LANES
    k_all = p_ref[:, T_KVW * SEG:T_KVW * SEG + LANES]
    v_all = p_ref[:, T_KVW * SEG + LANES:T_KVW * SEG + 2 * LANES]
    qs = jnp.concatenate([p_ref[:, T_QW * SEG + c * LANES:T_QW * SEG + (c + 1) * LANES] for c in range(n_cols)], axis=0)
    outs = []
    for g in range(WIN_KV_HEADS):
        s = _dot_nt(qs, _half_mask(k_all, g))
        sink = jnp.concatenate([jnp.full((l, 1), sink_ref[n_cols * g + c], F32) for c in range(n_cols)], axis=0)
        outs.append(_softmax_pv(s, _half_mask(v_all, g), sink))
    o = outs[0] + outs[1]
    ow_ref[...] = jnp.concatenate([o[c * l:(c + 1) * l, :] for c in range(n_cols)], axis=1).astype(BF16)
    for pair in range(NA_HEADS // 2):
        sl = lambda t: slice(t * SEG + pair * LANES, t * SEG + (pair + 1) * LANES)
        q, k, v = p_ref[:, sl(T_QN)], p_ref[:, sl(T_KN)], p_ref[:, sl(T_VN)]
        out = None
        for u in range(2):
            o = _softmax_pv(_dot_nt(q, _half_mask(k, u)), _half_mask(v, u))
            out = o if out is None else out + o
        on_ref[:, pair * LANES:(pair + 1) * LANES] = out.astype(BF16)


def _ctx_attention(p_ctx, sink):
    l = p_ctx.shape[0]
    return pl.pallas_call(
        _ctx_attn_kernel,
        in_specs=[pl.BlockSpec(memory_space=pltpu.SMEM), pl.BlockSpec(p_ctx.shape, lambda: (0, 0))],
        out_specs=[pl.BlockSpec((l, SEG), lambda: (0, 0)), pl.BlockSpec((l, SEG), lambda: (0, 0))],
        out_shape=[jax.ShapeDtypeStruct((l, SEG), BF16), jax.ShapeDtypeStruct((l, SEG), BF16)],
        compiler_params=pltpu.CompilerParams(vmem_limit_bytes=VMEM_LIMIT),
        name="context_attention",
    )(sink, p_ctx)


def _merge_kernel(yf_ref, yb_ref, gr_ref, gn_ref, yw_ref, yn_ref, ga_ref, gb_ref, gc_ref,
                  wr_ref, ww_ref, wn_ref, wo_ref, x_ref, m2_ref, gf_ref, m3_ref, m4_ref, wrt_ref,
                  xo_ref, hx_ref, lt_ref):
    y = yf_ref[...] + yb_ref[...]
    parts = []
    for h in range(RET_HEADS):
        yh = y[:, h * RET_D:(h + 1) * RET_D]
        mu = jnp.mean(yh, axis=-1, keepdims=True)
        var = jnp.mean(jnp.square(yh - mu), axis=-1, keepdims=True)
        parts.append((yh - mu) * lax.rsqrt(var + EPS))
    g = gr_ref[...].astype(F32)
    ya = jnp.concatenate(parts, axis=1) * gn_ref[...] * (g * jax.nn.sigmoid(g))
    za = _dot(ya.astype(BF16), wr_ref[...])
    zb = _dot(yw_ref[...], ww_ref[...])
    zc = _dot(yn_ref[...], wn_ref[...])
    sig = lambda r: jax.nn.sigmoid(r[...].astype(F32))
    mix = sig(ga_ref) * za + sig(gb_ref) * zb + sig(gc_ref) * zc
    x_new = x_ref[...] + m2_ref[...] * _dot(mix.astype(BF16), wo_ref[...])
    xo_ref[...] = x_new
    h2 = _norm_mod(x_new, gf_ref[...], m3_ref[...], m4_ref[...])
    n_sub = h2.shape[1] // LANES
    for s in range(n_sub):
        hx_ref[pl.ds(s, h2.shape[0], stride=n_sub), :] = h2[:, s * LANES:(s + 1) * LANES]
    lt_ref[...] = _dot_nt(wrt_ref[...], h2, precision=HIGHEST)


def _merge(yf, yb, p, gn, yw, yn, wr, ww, wn, wo, x, m2, gf, m3, m4, wrt, *, tm):
    n, d = x.shape
    row = lambda wdt, t: pl.BlockSpec((tm, wdt), lambda i: (i, t))
    full = lambda a: pl.BlockSpec(a.shape, lambda i: (0,) * a.ndim)
    gate0 = T_GATES * SEG // d
    return pl.pallas_call(
        _merge_kernel,
        grid=(n // tm,),
        in_specs=[row(SEG, 0), row(SEG, 0), row(SEG, T_GR), full(gn), row(SEG, 0), row(SEG, 0),
                  row(d, gate0), row(d, gate0 + 1), row(d, gate0 + 2),
                  full(wr), full(ww), full(wn), full(wo), row(d, 0), full(m2), full(gf), full(m3), full(m4), full(wrt)],
        out_specs=[pl.BlockSpec((tm, d), lambda i: (i, 0)),
                   pl.BlockSpec((tm * (d // LANES), LANES), lambda i: (i, 0)),
                   pl.BlockSpec((N_EXPERTS, tm), lambda i: (0, i))],
        out_shape=[jax.ShapeDtypeStruct((n, d), F32), jax.ShapeDtypeStruct((n * (d // LANES), LANES), F32),
                   jax.ShapeDtypeStruct((N_EXPERTS, n), F32)],
        compiler_params=_cparams(("arbitrary",)),
        name="merge",
    )(yf, yb, p, gn, yw, yn, p, p, p, wr, ww, wn, wo, x, m2, gf, m3, m4, wrt)


def _route_kernel(lt_ref, idx_ref, gate_ref, pos_ref, goff_ref, aff_ref, thr_ref, *, cap, n_groups):
    e = pl.program_id(0)
    n_exp, ag, _ = lt_ref.shape
    capp = idx_ref.shape[-1]

    @pl.when(e == 0)
    def _():
        lt = lt_ref[...]
        ex = jnp.exp(lt - jnp.max(lt, axis=0, keepdims=True))
        aff = ex / jnp.sum(ex, axis=0, keepdims=True)
        real = _iota(aff.shape, 1) < n_groups
        aff = jnp.where(real, aff, 0.0)
        aff_ref[...] = aff
        bits = pltpu.bitcast(aff, I32)

        def body(k, t):
            cand = t | (jnp.int32(1) << (30 - k))
            cnt = jnp.sum(jnp.sum((bits >= cand).astype(F32), axis=2, keepdims=True), axis=1, keepdims=True)
            return jnp.where(cnt >= float(cap), cand, t)

        t = lax.fori_loop(0, 31, body, jnp.zeros((n_exp, 1, 1), I32))
        thr_ref[...] = jnp.broadcast_to(t, thr_ref.shape)

    a = aff_ref[e]
    bits = pltpu.bitcast(a, I32)
    t = thr_ref[e][0:1, :]
    gt = bits > t
    eq = bits == t
    tri_lane_strict = (_iota((LANES, LANES), 0) < _iota((LANES, LANES), 1)).astype(BF16)
    tri_lane_incl = (_iota((LANES, LANES), 0) <= _iota((LANES, LANES), 1)).astype(BF16)
    tri_grp_strict = (_iota((ag, ag), 1) < _iota((ag, ag), 0)).astype(BF16)
    tri_grp_incl = (_iota((ag, ag), 1) <= _iota((ag, ag), 0)).astype(BF16)

    def total(mask_f):
        return jnp.sum(jnp.sum(mask_f, axis=1, keepdims=True), axis=0, keepdims=True)

    def group_sum(mask_f):
        return jnp.broadcast_to(jnp.sum(mask_f, axis=1, keepdims=True), (ag, LANES)).astype(BF16)

    eq_f = eq.astype(F32)
    need = float(cap) - total(gt.astype(F32))
    rank_eq = _dot(tri_grp_strict, group_sum(eq_f)) + _dot(eq_f.astype(BF16), tri_lane_strict)
    sel = gt | (eq & (rank_eq < need))
    sel_f = sel.astype(F32)
    cl = _dot(sel_f.astype(BF16), tri_lane_incl)
    cg = _dot(tri_grp_incl, group_sum(sel_f))
    goff = cg - jnp.broadcast_to(jnp.sum(sel_f, axis=1, keepdims=True), (ag, LANES))
    pos_ref[...] = jnp.where(sel, goff + cl - 1.0, -1.0)
    diag = _iota((ag, ag), 0) == _iota((ag, ag), 1)
    goff_sq = goff if ag == LANES else goff[:, :ag]
    goff_ref[...] = jnp.sum(jnp.where(diag, goff_sq, 0.0), axis=0, keepdims=True).astype(I32)

    pp = _iota((ag, capp), 1, F32)
    cg_b = jnp.broadcast_to(cg[:, 0:1], (ag, capp))
    below = cg_b <= pp
    grp = jnp.sum(below.astype(F32), axis=0, keepdims=True)
    off = jnp.max(jnp.where(below, cg_b, 0.0), axis=0, keepdims=True)
    onehot = _iota((ag, capp), 0, F32) == grp
    in_grp = _dot_tn(cl.astype(BF16), onehot.astype(BF16))
    local = pp[0:1, :] - off
    lane_of = jnp.sum((in_grp <= local).astype(F32), axis=0, keepdims=True)
    live = pp[0:1, :] < float(cap)
    idx = jnp.where(live, grp * float(LANES) + lane_of, 0.0)
    idx_ref[...] = idx.astype(I32)
    aff_grp = _dot_tn(a, onehot.astype(F32), precision=HIGHEST)
    pick = _iota((LANES, capp), 0, F32) == lane_of
    gate = jnp.sum(jnp.where(pick, aff_grp, 0.0), axis=0, keepdims=True)
    gate_ref[...] = jnp.where(live, gate, 0.0)


def _route(logits_t, *, cap, n_groups):
    n_exp, ag, _ = logits_t.shape
    capp = -(-cap // LANES) * LANES
    return pl.pallas_call(
        functools.partial(_route_kernel, cap=cap, n_groups=n_groups),
        grid=(n_exp,),
        in_specs=[pl.BlockSpec(logits_t.shape, lambda e: (0, 0, 0))],
        out_specs=[pl.BlockSpec((None, 1, capp), lambda e: (e, 0, 0)),
                   pl.BlockSpec((None, 1, capp), lambda e: (e, 0, 0)),
                   pl.BlockSpec((None, ag, LANES), lambda e: (e, 0, 0)),
                   pl.BlockSpec((None, 1, ag), lambda e: (e, 0, 0))],
        out_shape=[jax.ShapeDtypeStruct((n_exp, 1, capp), I32), jax.ShapeDtypeStruct((n_exp, 1, capp), F32),
                   jax.ShapeDtypeStruct((n_exp, ag, LANES), F32), jax.ShapeDtypeStruct((n_exp, 1, ag), I32)],
        scratch_shapes=[pltpu.VMEM((n_exp, ag, LANES), F32), pltpu.VMEM((n_exp, SUBLANES, LANES), I32)],
        compiler_params=_cparams(("arbitrary",)),
        name="route",
    )(logits_t)


def _ffn_kernel(idx_ref, idxn_ref, gate_ref, hx_ref, wg_ref, wu_ref, wd_ref, o_ref, xbuf, sem, wgb, wub, wdb, *, tm):
    e = pl.program_id(0)
    j = pl.program_id(1)
    nt = pl.num_programs(1)
    step = e * nt + j
    last = pl.num_programs(0) * nt - 1

    sub = SUBLANES
    rows = tm * sub

    def row_copy(rows_ref, r, slot):
        src = hx_ref.at[pl.ds(pl.multiple_of(rows_ref[0, 0, r] * sub, sub), sub)]
        dst = xbuf.at[pl.ds(pl.multiple_of(slot * rows + r * sub, sub), sub)]
        return pltpu.make_async_copy(src, dst, sem.at[slot])

    def slot_copy(slot):
        return pltpu.make_async_copy(hx_ref.at[pl.ds(0, rows)], xbuf.at[pl.ds(pl.multiple_of(slot * rows, sub), rows)],
                                     sem.at[slot])

    @pl.when(step == 0)
    def _():
        def body(r, carry):
            row_copy(idx_ref, r, 0).start()
            return carry
        lax.fori_loop(0, tm, body, 0, unroll=8)

    @pl.when(j == 0)
    def _():
        wgb[...] = wg_ref[...].astype(BF16)
        wub[...] = wu_ref[...].astype(BF16)
        wdb[...] = wd_ref[...].astype(BF16)

    slot = step % 2
    nslot = 1 - slot
    for r in range(tm):
        row_copy(idxn_ref, r, nslot).start()
    slot_copy(slot).wait()
    base = slot * rows
    x = jnp.concatenate([xbuf[pl.ds(base + s, tm, stride=sub), :] for s in range(sub)], axis=1).astype(BF16)
    g = _dot(x, wgb[...])
    u = _dot(x, wub[...])
    hid = (g * jax.nn.sigmoid(g)) * u
    y = _dot(hid.astype(BF16), wdb[...])
    eye = _iota((tm, tm), 0) == _iota((tm, tm), 1)
    gcol = jnp.sum(jnp.where(eye, jnp.broadcast_to(gate_ref[0], (tm, tm)), 0.0), axis=1, keepdims=True)
    o_ref[...] = (y * gcol).astype(BF16)

    @pl.when(step == last)
    def _():
        slot_copy(nslot).wait()


def _expert_ffn(idx, gate, hx3, w_gate, w_up, w_down, *, layer, cap, tm):
    n_exp = idx.shape[0]
    d = w_gate.shape[2]
    f = w_gate.shape[3]
    assert d == SUBLANES * LANES, "a token row must be exactly one (8, 128) f32 tile"
    nt = cap // tm
    idx_t = idx[:, :, :cap].reshape(n_exp * nt, 1, tm)
    gate_t = gate[:, :, :cap].reshape(n_exp * nt, 1, tm)
    n_steps = n_exp * nt
    wspec = lambda a, b: pl.BlockSpec((None, None, a, b), lambda e, j: (layer, e, 0, 0))
    return pl.pallas_call(
        functools.partial(_ffn_kernel, tm=tm),
        grid=(n_exp, nt),
        in_specs=[pl.BlockSpec((1, 1, tm), lambda e, j: (e * nt + j, 0, 0), memory_space=pltpu.SMEM),
                  pl.BlockSpec((1, 1, tm), lambda e, j: (jnp.minimum(e * nt + j + 1, n_steps - 1), 0, 0), memory_space=pltpu.SMEM),
                  pl.BlockSpec((1, 1, tm), lambda e, j: (e * nt + j, 0, 0)),
                  pl.BlockSpec(memory_space=pl.ANY),
                  wspec(d, f), wspec(d, f), wspec(f, d)],
        out_specs=pl.BlockSpec((tm, d), lambda e, j: (e * nt + j, 0)),
        out_shape=jax.ShapeDtypeStruct((n_exp * cap, d), BF16),
        scratch_shapes=[pltpu.VMEM((2 * tm * SUBLANES, LANES), F32), pltpu.SemaphoreType.DMA((2,)),
                        pltpu.VMEM((d, f), BF16), pltpu.VMEM((d, f), BF16), pltpu.VMEM((f, d), BF16)],
        compiler_params=_cparams(("arbitrary", "arbitrary")),
        name="expert_ffn",
    )(idx_t, idx_t, gate_t, hx3, w_gate, w_up, w_down)


def _combine_kernel(ws_ref, nw_ref, *refs, n_exp, cap, win, gpt, final_norm):
    y_refs = refs[:n_exp]
    pos_ref, ye_ref, x_ref, m5_ref, gf_ref, o_ref, acc_ref, xwin, sem = refs[n_exp:]
    a = pl.program_id(0)
    ng = pl.num_programs(0)
    per = max(1, MXU_DEPTH // win)

    def slots(e):
        pos = pos_ref[e, pl.ds(a * gpt, gpt), :]
        return jnp.concatenate([jnp.broadcast_to(pos[j:j + 1, :], (win, LANES)) for j in range(gpt)], axis=1)

    lp = _iota((win, gpt * LANES), 0, F32)
    acc = None
    for e0 in range(0, n_exp, per):
        es = range(e0, min(e0 + per, n_exp))
        hot = jnp.concatenate([((slots(e) - (ws_ref[e * ng + a] - e * cap).astype(F32)) == lp).astype(BF16) for e in es],
                              axis=0)
        rows = jnp.concatenate([y_refs[e][...] for e in es], axis=0)
        part = _dot_tn(hot, rows)
        acc = part if acc is None else acc + part
    acc_ref[...] = acc

    for e in range(n_exp):
        first = ws_ref[e * ng + a] - e * cap

        def extra(k, carry, e=e, first=first):
            lo = first + k * win
            row = jnp.minimum(lo, cap - win)
            cp = pltpu.make_async_copy(ye_ref.at[pl.ds(pl.multiple_of(e * cap + row, BF16_ROWS), win)], xwin, sem)
            cp.start()
            cp.wait()
            pos = slots(e)
            hot = ((pos - row.astype(F32)) == lp) & (pos >= lo.astype(F32))
            acc_ref[...] += _dot_tn(hot.astype(BF16), xwin[...])
            return carry

        lax.fori_loop(1, nw_ref[e * ng + a], extra, 0)

    x_new = x_ref[...] + m5_ref[...] * acc_ref[...]
    if final_norm:
        ms = jnp.mean(x_new * x_new, axis=-1, keepdims=True)
        x_new = x_new * lax.rsqrt(ms + EPS) * gf_ref[...]
    o_ref[...] = x_new


def _combine(ye, posmap, goff, x, m5, g_final, *, cap, final_norm):
    n, d = x.shape
    n_exp = posmap.shape[0]
    gpt = min(COMBINE_GROUPS, n // LANES)
    ng = n // (gpt * LANES)
    tok = gpt * LANES
    win = min(COMBINE_WINDOW, cap)
    goff = goff[:, 0, :ng * gpt:gpt]
    end = jnp.concatenate([goff[:, 1:], jnp.full((n_exp, 1), cap, I32)], axis=1)
    start = jnp.minimum((goff // BF16_ROWS) * BF16_ROWS, cap - win)
    n_win = jnp.maximum((end - start + win - 1) // win, 1).reshape(-1)
    wstart = (start + jnp.arange(n_exp, dtype=I32)[:, None] * cap).reshape(-1)
    yspec = lambda e: pl.BlockSpec((pl.Element(win), pl.Element(d)),
                                   lambda a, ws, nw: (pl.multiple_of(ws[e * ng + a], BF16_ROWS), 0))
    gs = pltpu.PrefetchScalarGridSpec(
        num_scalar_prefetch=2,
        grid=(ng,),
        in_specs=[yspec(e) for e in range(n_exp)]
        + [pl.BlockSpec(posmap.shape, lambda a, ws, nw: (0, 0, 0)),
           pl.BlockSpec(memory_space=pl.ANY),
           pl.BlockSpec((tok, d), lambda a, ws, nw: (a, 0)),
           pl.BlockSpec((1, d), lambda a, ws, nw: (0, 0)), pl.BlockSpec((1, d), lambda a, ws, nw: (0, 0))],
        out_specs=pl.BlockSpec((tok, d), lambda a, ws, nw: (a, 0)),
        scratch_shapes=[pltpu.VMEM((tok, d), F32), pltpu.VMEM((win, d), BF16), pltpu.SemaphoreType.DMA(())],
    )
    return pl.pallas_call(
        functools.partial(_combine_kernel, n_exp=n_exp, cap=cap, win=win, gpt=gpt, final_norm=final_norm),
        grid_spec=gs,
        out_shape=jax.ShapeDtypeStruct((n, d), F32),
        compiler_params=_cparams(("arbitrary",)),
        name="combine_final" if final_norm else "combine",
    )(wstart, n_win, *([ye] * n_exp), posmap, ye, x, m5, g_final)


def _rope_tables(n):
    t = jnp.arange(n)
    row = (t // GRID_W).astype(F32)
    col = (t % GRID_W).astype(F32)

    def cs(d):
        nf = d // 4
        inv = ROPE_BASE ** (-jnp.arange(nf, dtype=F32) / nf)
        ang = jnp.concatenate([row[:, None] * inv, col[:, None] * inv], axis=-1)
        return jnp.cos(ang), jnp.sin(ang)

    c, s = cs(RET_D)
    c128 = jnp.concatenate([c, c], axis=1)
    s128 = jnp.concatenate([-s, s], axis=1)
    c, s = cs(HEAD_D)
    z = jnp.zeros_like(s)
    c64 = jnp.concatenate([c, c, c, c], axis=1)
    sa = jnp.concatenate([-s, z, -s, z], axis=1)
    sb = jnp.concatenate([z, s, z, s], axis=1)
    return c128, s128, c64, sa, sb


WIN_HEAD_ORDER = (0, 4, 1, 5, 2, 6, 3, 7)


def _permute_in_weight(w):
    d = w.shape[0]
    sizes = (("q_r", 512), ("k_r", 512), ("v_r", 512), ("g_r", 512), ("q_w", 512), ("k_w", 128), ("v_w", 128),
             ("q_n", 512), ("k_n", 512), ("v_n", 512), ("gates", 3 * d))
    off, lay = 0, {}
    for name, size in sizes:
        lay[name] = (off, off + size)
        off += size
    seg = lambda name: w[:, lay[name][0]:lay[name][1]]
    q0 = lay["q_w"][0]
    q_w = [w[:, q0 + h * HEAD_D:q0 + (h + 1) * HEAD_D] for h in WIN_HEAD_ORDER]
    parts = [seg("q_r"), seg("k_r"), seg("v_r"), seg("g_r")] + q_w + [seg("q_n"), seg("k_n"), seg("v_n"), seg("gates"),
                                                                       seg("k_w"), seg("v_w")]
    parts.append(jnp.zeros((d, P_COLS - off), w.dtype))
    return jnp.concatenate(parts, axis=1).astype(BF16)


def _permute_win_rows(w):
    return jnp.concatenate([w[h * HEAD_D:(h + 1) * HEAD_D] for h in WIN_HEAD_ORDER], axis=0).astype(BF16)


def kernel(x, c, ctx, c_ctx, w_mod, b_mod, g_mix, g_ffn, w_in, ret_decay_logit, ret_gn, w_ret, win_sink, w_win, na_rpb,
           w_na, w_out, w_router, w_exp_gate, w_exp_up, w_exp_down, g_final):
    _, n, d = x.shape
    l = ctx.shape[1]
    depth = w_in.shape[0]
    xs, cs = x[0], ctx[0]
    cc = jnp.zeros((SUBLANES, d), F32).at[0].set(c[0]).at[1].set(c_ctx)
    mods = _modulation(cc, w_mod, b_mod)
    tabs_x = _rope_tables(n)
    tabs_c = tuple(jnp.zeros((l, LANES), F32) for _ in range(5))
    cap_x = CAPACITY_FACTOR * n // N_EXPERTS
    cap_c = CAPACITY_FACTOR * l // N_EXPERTS
    grp_c = -(-(l // LANES) // BF16_ROWS) * BF16_ROWS
    vec = lambda v: v.reshape(1, -1)

    for layer in range(depth):
        need_ctx = layer < depth - 1
        last = layer == depth - 1
        mx = [vec(mods[layer, 0, k * d:(k + 1) * d]) for k in range(N_MOD)]
        mc = [vec(mods[layer, 1, k * d:(k + 1) * d]) for k in range(N_MOD)]
        w_p = _permute_in_weight(w_in[layer])
        wr = w_ret[layer].astype(BF16)
        ww = _permute_win_rows(w_win[layer])
        wn = w_na[layer].astype(BF16)
        wo = w_out[layer].astype(BF16)
        wrt = w_router[layer].T
        sink = win_sink[layer].astype(F32)
        lg_rows = jnp.broadcast_to(ret_decay_logit[layer].astype(F32).reshape(2 * RET_HEADS, 1), (2 * RET_HEADS, LANES))
        gn = vec(ret_gn[layer])
        bias_tab = _na_bias_table(na_rpb[layer])
        gmix, gffn = vec(g_mix[layer]), vec(g_ffn[layer])

        p_x = _inproj(xs, gmix, mx[0], mx[1], tabs_x, w_p, rope=True, tm=2048)
        p_c = _inproj(cs, gmix, mc[0], mc[1], tabs_c, w_p, rope=False, tm=l)

        yf, yb = _retention(p_x, p_c, lg_rows, zero_init=False)
        yw = _window_attention(p_x, p_c, sink)
        yn = _na_attention(p_x, p_c, bias_tab)
        xs, hx3, lt = _merge(yf, yb, p_x, gn, yw, yn, wr, ww, wn, wo, xs, mx[2], gffn, mx[3], mx[4], wrt, tm=512)

        idx, gate, posmap, goff = _route(lt.reshape(N_EXPERTS, n // LANES, LANES), cap=cap_x, n_groups=n // LANES)
        ye = _expert_ffn(idx, gate, hx3, w_exp_gate, w_exp_up, w_exp_down, layer=layer, cap=cap_x, tm=256)
        xs = _combine(ye, posmap, goff, xs, mx[5], vec(g_final), cap=cap_x, final_norm=last)

        if need_ctx:
            yfc, ybc = _retention(p_c, p_c, lg_rows, zero_init=True)
            ywc, ync = _ctx_attention(p_c, sink)
            cs, hc3, ltc = _merge(yfc, ybc, p_c, gn, ywc, ync, wr, ww, wn, wo, cs, mc[2], gffn, mc[3], mc[4], wrt, tm=l)
            ltc = jnp.pad(ltc.reshape(N_EXPERTS, l // LANES, LANES), ((0, 0), (0, grp_c - l // LANES), (0, 0)))
            idc, gtc, posc, goffc = _route(ltc, cap=cap_c, n_groups=l // LANES)
            yec = _expert_ffn(idc, gtc, hc3, w_exp_gate, w_exp_up, w_exp_down, layer=layer, cap=cap_c, tm=cap_c)
            cs = _combine(yec, posc, goffc, cs, mc[5], vec(g_final), cap=cap_c, final_norm=False)

    return xs[None]
```

```python
import functools

import jax
import jax.numpy as jnp
from jax import lax
from jax.experimental import pallas as pl
from jax.experimental.pallas import tpu as pltpu

F32, BF16, I32 = jnp.float32, jnp.bfloat16, jnp.int32
HIGHEST = lax.Precision.HIGHEST

GRID_W = 64
RET_HEADS, RET_D, RET_CHUNK = 4, 128, 128
WIN_HEADS, WIN_KV_HEADS, HEAD_D, WINDOW, WIN_BLOCK = 8, 2, 64, 128, 128
NA_HEADS, NA_ROWS, NA_COLS = 8, 8, 16
N_EXPERTS, CAPACITY_FACTOR = 16, 2
N_MOD = 6
ROPE_BASE = 10000.0
EPS = 1e-6
NEG_INF = -1e30

LANES = 128
SUBLANES = 8
BF16_ROWS = 16
VMEM_LIMIT = 56 * 1024 * 1024

SEG = 512
T_QR, T_KR, T_VR, T_GR, T_QW, T_QN, T_KN, T_VN, T_GATES, T_KVW = 0, 1, 2, 3, 4, 5, 6, 7, 8, 14
P_TILES = 15
P_COLS = P_TILES * SEG
C_KW = T_KVW * SEG // LANES
C_VW = C_KW + 1
RET_BLOCK = 4 * RET_CHUNK
WIN_QBLOCKS = 2
NA_QROWS = 4
NA_SLAB_ROWS = 12
NA_BIAS_ZERO = NA_SLAB_ROWS - 2
NA_BIAS_PAIRS = NA_BIAS_ZERO + NA_SLAB_ROWS - 1
COMBINE_GROUPS = 4
COMBINE_WINDOW = 128
MXU_DEPTH = 256


def _cparams(sem):
    return pltpu.CompilerParams(dimension_semantics=sem, vmem_limit_bytes=VMEM_LIMIT)


def _dot(a, b):
    return jnp.dot(a, b, preferred_element_type=F32)


def _dot_nt(a, b, precision=None):
    return lax.dot_general(a, b, (((1,), (1,)), ((), ())), precision=precision, preferred_element_type=F32)


def _dot_tn(a, b, precision=None):
    return lax.dot_general(a, b, (((0,), (0,)), ((), ())), precision=precision, preferred_element_type=F32)


def _iota(shape, dim, dtype=I32):
    return lax.broadcasted_iota(I32, shape, dim).astype(dtype)


def _mod_kernel(s_ref, w_ref, b_ref, o_ref):
    s = s_ref[...]
    s = s * jax.nn.sigmoid(s)
    o_ref[...] = jnp.dot(s, w_ref[...], precision=HIGHEST, preferred_element_type=F32) + b_ref[...]


def _modulation(cc, w_mod, b_mod):
    depth, d, md = w_mod.shape
    tn = 1536
    return pl.pallas_call(
        _mod_kernel,
        grid=(depth, md // tn),
        in_specs=[
            pl.BlockSpec((SUBLANES, d), lambda l, j: (0, 0)),
            pl.BlockSpec((None, d, tn), lambda l, j: (l, 0, j)),
            pl.BlockSpec((None, 1, tn), lambda l, j: (l, 0, j)),
        ],
        out_specs=pl.BlockSpec((None, SUBLANES, tn), lambda l, j: (l, 0, j)),
        out_shape=jax.ShapeDtypeStruct((depth, SUBLANES, md), F32),
        compiler_params=_cparams(("arbitrary", "arbitrary")),
        name="modulation",
    )(cc, w_mod, b_mod.reshape(depth, 1, md))


def _norm_mod(x, g, shift, scale):
    ms = jnp.mean(x * x, axis=-1, keepdims=True)
    y = x * lax.rsqrt(ms + EPS) * g
    return y * (1.0 + scale) + shift


def _inproj_kernel(x_ref, g_ref, sh_ref, sc_ref, c128_ref, s128_ref, c64_ref, sa_ref, sb_ref, w_ref, o_ref, hx_ref, *, rope):
    j = pl.program_id(1)

    @pl.when(j == 0)
    def _():
        hx_ref[...] = _norm_mod(x_ref[...], g_ref[...], sh_ref[...], sc_ref[...]).astype(BF16)

    acc = _dot(hx_ref[...], w_ref[...])
    groups = [acc[:, k * LANES:(k + 1) * LANES] for k in range(SEG // LANES)]

    def rope128(a):
        return a * c128_ref[...] + pltpu.roll(a, 64, 1) * s128_ref[...]

    def rope64(a):
        return a * c64_ref[...] + pltpu.roll(a, 96, 1) * sa_ref[...] + pltpu.roll(a, 32, 1) * sb_ref[...]

    def put(gs):
        o_ref[...] = jnp.concatenate(gs, axis=1).astype(BF16)

    k_scale = RET_D ** -0.5
    a_scale = HEAD_D ** -0.5
    ident = lambda a: a
    f_qr = rope128 if rope else ident
    f_kr = (lambda a: rope128(a) * k_scale) if rope else (lambda a: a * k_scale)
    f_qw = (lambda a: rope64(a) * a_scale) if rope else (lambda a: a * a_scale)
    f_kw = rope64 if rope else ident

    @pl.when(j == T_QR)
    def _():
        put([f_qr(a) for a in groups])

    @pl.when(j == T_KR)
    def _():
        put([f_kr(a) for a in groups])

    @pl.when(j == T_QW)
    def _():
        put([f_qw(a) for a in groups])

    @pl.when(j == T_QN)
    def _():
        put([a * a_scale for a in groups])

    @pl.when(j == T_KVW)
    def _():
        put([f_kw(groups[0])] + groups[1:])

    plain = (j != T_QR) & (j != T_KR) & (j != T_QW) & (j != T_QN) & (j != T_KVW)

    @pl.when(plain)
    def _():
        put(groups)


def _inproj(x, g, shift, scale, tabs, w, *, rope, tm):
    n, d = x.shape
    vec = lambda: pl.BlockSpec((1, d), lambda i, j: (0, 0))
    tab = lambda: pl.BlockSpec((tm, LANES), lambda i, j: (i, 0))
    return pl.pallas_call(
        functools.partial(_inproj_kernel, rope=rope),
        grid=(n // tm, P_TILES),
        in_specs=[pl.BlockSpec((tm, d), lambda i, j: (i, 0)), vec(), vec(), vec(),
                  tab(), tab(), tab(), tab(), tab(),
                  pl.BlockSpec((d, SEG), lambda i, j: (0, j))],
        out_specs=pl.BlockSpec((tm, SEG), lambda i, j: (i, j)),
        out_shape=jax.ShapeDtypeStruct((n, P_COLS), BF16),
        scratch_shapes=[pltpu.VMEM((tm, d), BF16)],
        compiler_params=_cparams(("arbitrary", "arbitrary")),
        name="inproj_rope" if rope else "inproj_ctx",
    )(x, g, shift, scale, *tabs, w)


def _ret_kernel(lg_ref, qf_ref, kf_ref, vf_ref, qb_ref, kb_ref, vb_ref, kc_ref, vc_ref, yf_ref, yb_ref,
                rf_ref, rb_ref, tab_ref, *, zero_init):
    c = pl.program_id(0)
    ch = RET_CHUNK
    dh = RET_D
    states = (rf_ref, rb_ref)

    @pl.when(c == 0)
    def _():
        lg = jax.nn.log_sigmoid(lg_ref[...])
        ii = _iota((ch, ch), 0, F32)
        jj = _iota((ch, ch), 1, F32)
        n_ctx = kc_ref.shape[0]
        mm = _iota((n_ctx, dh), 0, F32)
        for d in range(2):
            for h in range(RET_HEADS):
                row = lg[RET_HEADS * d + h:RET_HEADS * d + h + 1, :]
                l = jnp.broadcast_to(row, (ch, ch))
                if d == 0:
                    diff = ii - jj
                    dmat = jnp.where(diff >= 0.0, jnp.exp(jnp.maximum(diff, 0.0) * l), 0.0)
                    zeta = jnp.exp((ch - 1.0 - ii) * l)
                    xi = jnp.exp((ii + 1.0) * l)
                else:
                    diff = jj - ii
                    dmat = jnp.where(diff >= 1.0, jnp.exp(jnp.maximum(diff, 0.0) * l), 0.0)
                    zeta = jnp.exp(ii * l)
                    xi = jnp.exp((ch - ii) * l)
                tab_ref[d, h, 0] = dmat
                tab_ref[d, h, 1] = zeta
                tab_ref[d, h, 2] = xi
                tab_ref[d, h, 3] = jnp.exp(ch * l)
                if zero_init:
                    states[d][h] = jnp.zeros((dh, dh), F32)
                else:
                    lc = jnp.broadcast_to(row, (n_ctx, dh))
                    w = jnp.exp((n_ctx - 1.0 - mm) * lc) if d == 0 else jnp.exp(mm * lc)
                    sl = slice(h * dh, (h + 1) * dh)
                    kw = (kc_ref[:, sl].astype(F32) * w).astype(BF16)
                    states[d][h] = _dot_tn(kw, vc_ref[:, sl])

    n_sub = qf_ref.shape[0] // ch
    for d, (q_ref, k_ref, v_ref, y_ref) in enumerate(((qf_ref, kf_ref, vf_ref, yf_ref), (qb_ref, kb_ref, vb_ref, yb_ref))):
        order = range(n_sub) if d == 0 else range(n_sub - 1, -1, -1)
        for h in range(RET_HEADS):
            sl = slice(h * dh, (h + 1) * dh)
            r = states[d][h]
            for j in order:
                rows = slice(j * ch, (j + 1) * ch)
                q, k, v = q_ref[rows, sl], k_ref[rows, sl], v_ref[rows, sl]
                a = (_dot_nt(q, k) * tab_ref[d, h, 0]).astype(BF16)
                inner = _dot(a, v)
                cross = _dot(q, r.astype(BF16)) * tab_ref[d, h, 2]
                y_ref[rows, sl] = inner + cross
                kz = (k.astype(F32) * tab_ref[d, h, 1]).astype(BF16)
                r = tab_ref[d, h, 3] * r + _dot_tn(kz, v)
            states[d][h] = r


def _retention(p, p_ctx, lg_rows, *, zero_init):
    n = p.shape[0]
    blk = min(RET_BLOCK, n)
    nc = n // blk
    w = RET_HEADS * RET_D
    fwd = lambda t: pl.BlockSpec((blk, w), lambda c: (c, t))
    bwd = lambda t: pl.BlockSpec((blk, w), lambda c: (nc - 1 - c, t))
    ctx = lambda t: pl.BlockSpec((p_ctx.shape[0], w), lambda c: (0, t))
    return pl.pallas_call(
        functools.partial(_ret_kernel, zero_init=zero_init),
        grid=(nc,),
        in_specs=[pl.BlockSpec((SUBLANES, LANES), lambda c: (0, 0)),
                  fwd(T_QR), fwd(T_KR), fwd(T_VR), bwd(T_QR), bwd(T_KR), bwd(T_VR), ctx(T_KR), ctx(T_VR)],
        out_specs=[pl.BlockSpec((blk, w), lambda c: (c, 0)), pl.BlockSpec((blk, w), lambda c: (nc - 1 - c, 0))],
        out_shape=[jax.ShapeDtypeStruct((n, w), F32), jax.ShapeDtypeStruct((n, w), F32)],
        scratch_shapes=[pltpu.VMEM((RET_HEADS, RET_D, RET_D), F32), pltpu.VMEM((RET_HEADS, RET_D, RET_D), F32),
                        pltpu.VMEM((2, RET_HEADS, 4, RET_CHUNK, RET_CHUNK), F32)],
        compiler_params=_cparams(("arbitrary",)),
        name="retention_ctx" if zero_init else "retention",
    )(lg_rows, p, p, p, p, p, p, p_ctx, p_ctx)


def _half_mask(x, half):
    lane = _iota(x.shape, 1)
    keep = (lane < HEAD_D) if half == 0 else (lane >= HEAD_D)
    return jnp.where(keep, x, jnp.zeros_like(x))


def _softmax_pv(s, v, extra=None):
    m = jnp.max(s, axis=1, keepdims=True)
    if extra is not None:
        m = jnp.maximum(m, extra)
    p = jnp.exp(s - m)
    den = jnp.sum(p, axis=1, keepdims=True)
    if extra is not None:
        den = den + jnp.exp(extra - m)
    return _dot(p.astype(BF16), v) / den


def _softmax_pv_t(st, v, extra=None):
    m = jnp.max(st, axis=0, keepdims=True)
    if extra is not None:
        m = jnp.maximum(m, extra)
    p = jnp.exp(st - m)
    den = jnp.sum(p, axis=0, keepdims=True)
    if extra is not None:
        den = den + jnp.exp(extra - m)
    return _dot_tn(v, p.astype(BF16)) / den


def _win_kernel(sink_ref, q_ref, *refs):
    nkb = WIN_QBLOCKS + 2
    k_refs, v_refs = refs[:nkb], refs[nkb:2 * nkb]
    kx_ref, vx_ref, mask_ref, o_ref = refs[2 * nkb:]
    s = pl.program_id(0)
    last = pl.num_programs(0) - 1
    nq = WIN_QBLOCKS * WIN_BLOCK
    nk = nkb * WIN_BLOCK
    k_all = jnp.concatenate([r[...] for r in k_refs] + [kx_ref[...]], axis=0)
    v_all = jnp.concatenate([r[...] for r in v_refs] + [vx_ref[...]], axis=0)
    n_cols = SEG // LANES
    qs = jnp.concatenate([q_ref[:, c * LANES:(c + 1) * LANES] for c in range(n_cols)], axis=0)
    variant = jnp.where(s == 0, 1, 0) + jnp.where(s == last, 2, 0)
    valid = mask_ref[variant] > 0.5
    valid = jnp.concatenate([valid] * n_cols, axis=1)
    out_t = None
    for g in range(WIN_KV_HEADS):
        st = _dot_nt(_half_mask(k_all, g), qs)
        st = jnp.concatenate([jnp.where(valid, st[:nk], NEG_INF), st[nk:]], axis=0)
        sink = jnp.concatenate([jnp.full((1, nq), sink_ref[n_cols * g + c], F32) for c in range(n_cols)], axis=1)
        o = _softmax_pv_t(st, _half_mask(v_all, g), sink)
        out_t = o if out_t is None else out_t + o
    o = out_t.T
    o_ref[...] = jnp.concatenate([o[c * nq:(c + 1) * nq, :] for c in range(n_cols)], axis=1).astype(BF16)


def _win_valid_table():
    import numpy as np
    nkb = WIN_QBLOCKS + 2
    qpos = np.arange(WIN_QBLOCKS * WIN_BLOCK)[None, :]
    kpos = np.arange(nkb * WIN_BLOCK)[:, None] - WIN_BLOCK
    band = np.abs(kpos - qpos) <= WINDOW
    tabs = []
    for variant in range(4):
        ok = band.copy()
        if variant & 1:
            ok &= kpos >= 0
        if variant & 2:
            ok &= kpos < WIN_QBLOCKS * WIN_BLOCK
        tabs.append(ok)
    return jnp.asarray(np.stack(tabs).astype(np.float32))


def _window_attention(p, p_ctx, sink):
    n = p.shape[0]
    nb = n // WIN_BLOCK
    assert nb % WIN_QBLOCKS == 0
    l = p_ctx.shape[0]
    nkb = WIN_QBLOCKS + 2
    blk = lambda col, j: pl.BlockSpec((WIN_BLOCK, LANES), lambda i: (jnp.clip(WIN_QBLOCKS * i - 1 + j, 0, nb - 1), col))
    mask_tab = _win_valid_table()
    return pl.pallas_call(
        _win_kernel,
        grid=(nb // WIN_QBLOCKS,),
        in_specs=[pl.BlockSpec(memory_space=pltpu.SMEM),
                  pl.BlockSpec((WIN_QBLOCKS * WIN_BLOCK, SEG), lambda i: (i, T_QW))]
        + [blk(C_KW, j) for j in range(nkb)] + [blk(C_VW, j) for j in range(nkb)]
        + [pl.BlockSpec((l, LANES), lambda i: (0, C_KW)), pl.BlockSpec((l, LANES), lambda i: (0, C_VW)),
           pl.BlockSpec(mask_tab.shape, lambda i: (0, 0, 0))],
        out_specs=pl.BlockSpec((WIN_QBLOCKS * WIN_BLOCK, SEG), lambda i: (i, 0)),
        out_shape=jax.ShapeDtypeStruct((n, SEG), BF16),
        compiler_params=_cparams(("arbitrary",)),
        name="window_attention",
    )(sink, p, *([p] * (2 * nkb)), p_ctx, p_ctx, mask_tab)


def _na_slab_start(s, half_rows):
    return jnp.clip((NA_QROWS // 2) * s - NA_ROWS // 4, 0, half_rows - NA_SLAB_ROWS // 2)


def _na_kernel(q_ref, *refs, rows):
    nslab = NA_SLAB_ROWS // 2
    k_refs, v_refs = refs[:nslab], refs[nslab:2 * nslab]
    kx_ref, vx_ref, bias_ref, mask_ref, o_ref = refs[2 * nslab:]
    s = pl.program_id(0)
    last = pl.num_programs(0) - 1
    w = GRID_W
    nq = NA_QROWS * w
    nk = NA_SLAB_ROWS * w
    delta = 2 * _na_slab_start(s, rows // 2) - NA_QROWS * s
    variant = jnp.where(s == 0, 0, jnp.where(s == last, 2, 1))
    valid = mask_ref[variant] > 0.5
    valid = jnp.concatenate([valid, valid], axis=1)
    row = _iota((LANES, nq), 0)
    for pair in range(NA_HEADS // 2):
        sl = slice(pair * LANES, (pair + 1) * LANES)
        q = q_ref[:, sl]
        qs = jnp.concatenate([_half_mask(q, 0), _half_mask(q, 1)], axis=0)
        k_all = jnp.concatenate([r[:, sl] for r in k_refs] + [kx_ref[:, sl]], axis=0)
        v_all = jnp.concatenate([r[:, sl] for r in v_refs] + [vx_ref[:, sl]], axis=0)
        st = _dot_nt(k_all, qs)
        bias = jnp.concatenate(
            [jnp.concatenate([bias_ref[2 * pair + u, delta + 2 * i - a + NA_BIAS_ZERO]
                              for u in range(2) for a in range(0, NA_QROWS, 2)], axis=1) for i in range(nslab)], axis=0)
        s_loc = jnp.where(valid, st[:nk] + bias, NEG_INF)
        o = _softmax_pv_t(jnp.concatenate([s_loc, st[nk:]], axis=0), v_all)
        o_ref[:, sl] = jnp.where(row < HEAD_D, o[:, :nq], o[:, nq:]).T.astype(BF16)


def _na_valid_table():
    import numpy as np
    kk = np.arange(NA_SLAB_ROWS)[:, None, None, None]
    ck = np.arange(GRID_W)[None, :, None, None]
    a = np.arange(NA_QROWS)[None, None, :, None]
    cq = np.arange(GRID_W)[None, None, None, :]
    c_start = np.clip(cq - NA_COLS // 2, 0, GRID_W - NA_COLS)
    col_ok = (ck >= c_start) & (ck < c_start + NA_COLS)
    first_row = (0 * a, a, 0 * a + NA_SLAB_ROWS - NA_ROWS)
    tabs = [(col_ok & (kk >= f) & (kk < f + NA_ROWS)).reshape(NA_SLAB_ROWS * GRID_W, NA_QROWS * GRID_W) for f in first_row]
    return jnp.asarray(np.stack(tabs).astype(np.float32))


def _na_attention(p, p_ctx, bias_tab):
    n = p.shape[0]
    rows = n // GRID_W
    assert rows % NA_QROWS == 0 and rows >= NA_SLAB_ROWS
    steps = rows // NA_QROWS
    l = p_ctx.shape[0]
    slab = lambda t, i: pl.BlockSpec((2 * GRID_W, SEG), lambda s: (_na_slab_start(s, rows // 2) + i, t))
    nslab = NA_SLAB_ROWS // 2
    mask_tab = _na_valid_table()
    return pl.pallas_call(
        functools.partial(_na_kernel, rows=rows),
        grid=(steps,),
        in_specs=[pl.BlockSpec((NA_QROWS * GRID_W, SEG), lambda s: (s, T_QN))]
        + [slab(T_KN, i) for i in range(nslab)] + [slab(T_VN, i) for i in range(nslab)]
        + [pl.BlockSpec((l, SEG), lambda s: (0, T_KN)), pl.BlockSpec((l, SEG), lambda s: (0, T_VN)),
           pl.BlockSpec(bias_tab.shape, lambda s: (0, 0, 0, 0)), pl.BlockSpec(mask_tab.shape, lambda s: (0, 0, 0))],
        out_specs=pl.BlockSpec((NA_QROWS * GRID_W, SEG), lambda s: (s, 0)),
        out_shape=jax.ShapeDtypeStruct((n, SEG), BF16),
        compiler_params=_cparams(("arbitrary",)),
        name="neighbourhood_attention",
    )(p, *([p] * (2 * nslab)), p_ctx, p_ctx, bias_tab, mask_tab)


def _na_bias_table(rpb):
    n_r, n_c = 2 * NA_ROWS - 1, 2 * NA_COLS - 1
    rpb = rpb.astype(F32)
    pick = lambda e, uk, wq: rpb[:, min(max(e - NA_BIAS_ZERO + uk - wq + NA_ROWS - 1, 0), n_r - 1), :]
    rows = jnp.stack([jnp.stack([jnp.stack([pick(e, uk, wq) for wq in range(2)], axis=1) for uk in range(2)], axis=1)
                      for e in range(NA_BIAS_PAIRS)], axis=1)
    import numpy as np
    ck = np.arange(GRID_W)[:, None]
    cq = np.arange(GRID_W)[None, :]
    ci = np.clip(ck - cq, -(NA_COLS - 1), NA_COLS - 1) + NA_COLS - 1
    sel = (ci[None] == np.arange(n_c)[:, None, None]).astype(np.float32)
    sel2 = np.zeros((2, n_c, GRID_W, 2, GRID_W), np.float32)
    for wq in range(2):
        sel2[wq, :, :, wq, :] = sel
    sel2 = jnp.asarray(sel2.reshape(2 * n_c, GRID_W, 2 * GRID_W))
    rows = rows.reshape(rpb.shape[0], NA_BIAS_PAIRS, 2, 2 * n_c)
    tab = jnp.einsum("heuj,jkl->heukl", rows, sel2, precision=HIGHEST)
    return tab.reshape(rpb.shape[0], NA_BIAS_PAIRS, 2 * GRID_W, 2 * GRID_W)


def _ctx_attn_kernel(sink_ref, p_ref, ow_ref, on_ref):
    l = p_ref.shape[0]
    n_cols = SEG // LANES
    k_all = p_ref[:, T_KVW * SEG:T_KVW * SEG + LANES]
    v_all = p_ref[:, T_KVW * SEG + LANES:T_KVW * SEG + 2 * LANES]
    qs = jnp.concatenate([p_ref[:, T_QW * SEG + c * LANES:T_QW * SEG + (c + 1) * LANES] for c in range(n_cols)], axis=0)
    outs = []
    for g in range(WIN_KV_HEADS):
        s = _dot_nt(qs, _half_mask(k_all, g))
        sink = jnp.concatenate([jnp.full((l, 1), sink_ref[n_cols * g + c], F32) for c in range(n_cols)], axis=0)
        outs.append(_softmax_pv(s, _half_mask(v_all, g), sink))
    o = outs[0] + outs[1]
    ow_ref[...] = jnp.concatenate([o[c * l:(c + 1) * l, :] for c in range(n_cols)], axis=1).astype(BF16)
    for pair in range(NA_HEADS // 2):
        sl = lambda t: slice(t * SEG + pair * LANES, t * SEG + (pair + 1) * LANES)
        q, k, v = p_ref[:, sl(T_QN)], p_ref[:, sl(T_KN)], p_ref[:, sl(T_VN)]
        out = None
        for u in range(2):
            o = _softmax_pv(_dot_nt(q, _half_mask(k, u)), _half_mask(v, u))
            out = o if out is None else out + o
        on_ref[:, pair * LANES:(pair + 1) * LANES] = out.astype(BF16)


def _ctx_attention(p_ctx, sink):
    l = p_ctx.shape[0]
    return pl.pallas_call(
        _ctx_attn_kernel,
        in_specs=[pl.BlockSpec(memory_space=pltpu.SMEM), pl.BlockSpec(p_ctx.shape, lambda: (0, 0))],
        out_specs=[pl.BlockSpec((l, SEG), lambda: (0, 0)), pl.BlockSpec((l, SEG), lambda: (0, 0))],
        out_shape=[jax.ShapeDtypeStruct((l, SEG), BF16), jax.ShapeDtypeStruct((l, SEG), BF16)],
        compiler_params=pltpu.CompilerParams(vmem_limit_bytes=VMEM_LIMIT),
        name="context_attention",
    )(sink, p_ctx)


def _merge_kernel(yf_ref, yb_ref, gr_ref, gn_ref, yw_ref, yn_ref, ga_ref, gb_ref, gc_ref,
                  wr_ref, ww_ref, wn_ref, wo_ref, x_ref, m2_ref, gf_ref, m3_ref, m4_ref, wrt_ref,
                  xo_ref, hx_ref, lt_ref):
    y = yf_ref[...] + yb_ref[...]
    parts = []
    for h in range(RET_HEADS):
        yh = y[:, h * RET_D:(h + 1) * RET_D]
        mu = jnp.mean(yh, axis=-1, keepdims=True)
        var = jnp.mean(jnp.square(yh - mu), axis=-1, keepdims=True)
        parts.append((yh - mu) * lax.rsqrt(var + EPS))
    g = gr_ref[...].astype(F32)
    ya = jnp.concatenate(parts, axis=1) * gn_ref[...] * (g * jax.nn.sigmoid(g))
    za = _dot(ya.astype(BF16), wr_ref[...])
    zb = _dot(yw_ref[...], ww_ref[...])
    zc = _dot(yn_ref[...], wn_ref[...])
    sig = lambda r: jax.nn.sigmoid(r[...].astype(F32))
    mix = sig(ga_ref) * za + sig(gb_ref) * zb + sig(gc_ref) * zc
    x_new = x_ref[...] + m2_ref[...] * _dot(mix.astype(BF16), wo_ref[...])
    xo_ref[...] = x_new
    h2 = _norm_mod(x_new, gf_ref[...], m3_ref[...], m4_ref[...])
    n_sub = h2.shape[1] // LANES
    for s in range(n_sub):
        hx_ref[pl.ds(s, h2.shape[0], stride=n_sub), :] = h2[:, s * LANES:(s + 1) * LANES]
    lt_ref[...] = _dot_nt(wrt_ref[...], h2, precision=HIGHEST)


def _merge(yf, yb, p, gn, yw, yn, wr, ww, wn, wo, x, m2, gf, m3, m4, wrt, *, tm):
    n, d = x.shape
    row = lambda wdt, t: pl.BlockSpec((tm, wdt), lambda i: (i, t))
    full = lambda a: pl.BlockSpec(a.shape, lambda i: (0,) * a.ndim)
    gate0 = T_GATES * SEG // d
    return pl.pallas_call(
        _merge_kernel,
        grid=(n // tm,),
        in_specs=[row(SEG, 0), row(SEG, 0), row(SEG, T_GR), full(gn), row(SEG, 0), row(SEG, 0),
                  row(d, gate0), row(d, gate0 + 1), row(d, gate0 + 2),
                  full(wr), full(ww), full(wn), full(wo), row(d, 0), full(m2), full(gf), full(m3), full(m4), full(wrt)],
        out_specs=[pl.BlockSpec((tm, d), lambda i: (i, 0)),
                   pl.BlockSpec((tm * (d // LANES), LANES), lambda i: (i, 0)),
                   pl.BlockSpec((N_EXPERTS, tm), lambda i: (0, i))],
        out_shape=[jax.ShapeDtypeStruct((n, d), F32), jax.ShapeDtypeStruct((n * (d // LANES), LANES), F32),
                   jax.ShapeDtypeStruct((N_EXPERTS, n), F32)],
        compiler_params=_cparams(("arbitrary",)),
        name="merge",
    )(yf, yb, p, gn, yw, yn, p, p, p, wr, ww, wn, wo, x, m2, gf, m3, m4, wrt)


def _route_kernel(lt_ref, idx_ref, gate_ref, pos_ref, goff_ref, aff_ref, thr_ref, *, cap, n_groups):
    e = pl.program_id(0)
    n_exp, ag, _ = lt_ref.shape
    capp = idx_ref.shape[-1]

    @pl.when(e == 0)
    def _():
        lt = lt_ref[...]
        ex = jnp.exp(lt - jnp.max(lt, axis=0, keepdims=True))
        aff = ex / jnp.sum(ex, axis=0, keepdims=True)
        real = _iota(aff.shape, 1) < n_groups
        aff = jnp.where(real, aff, 0.0)
        aff_ref[...] = aff
        bits = pltpu.bitcast(aff, I32)

        def body(k, t):
            cand = t | (jnp.int32(1) << (30 - k))
            cnt = jnp.sum(jnp.sum((bits >= cand).astype(F32), axis=2, keepdims=True), axis=1, keepdims=True)
            return jnp.where(cnt >= float(cap), cand, t)

        t = lax.fori_loop(0, 31, body, jnp.zeros((n_exp, 1, 1), I32))
        thr_ref[...] = jnp.broadcast_to(t, thr_ref.shape)

    a = aff_ref[e]
    bits = pltpu.bitcast(a, I32)
    t = thr_ref[e][0:1, :]
    gt = bits > t
    eq = bits == t
    tri_lane_strict = (_iota((LANES, LANES), 0) < _iota((LANES, LANES), 1)).astype(BF16)
    tri_lane_incl = (_iota((LANES, LANES), 0) <= _iota((LANES, LANES), 1)).astype(BF16)
    tri_grp_strict = (_iota((ag, ag), 1) < _iota((ag, ag), 0)).astype(BF16)
    tri_grp_incl = (_iota((ag, ag), 1) <= _iota((ag, ag), 0)).astype(BF16)

    def total(mask_f):
        return jnp.sum(jnp.sum(mask_f, axis=1, keepdims=True), axis=0, keepdims=True)

    def group_sum(mask_f):
        return jnp.broadcast_to(jnp.sum(mask_f, axis=1, keepdims=True), (ag, LANES)).astype(BF16)

    eq_f = eq.astype(F32)
    need = float(cap) - total(gt.astype(F32))
    rank_eq = _dot(tri_grp_strict, group_sum(eq_f)) + _dot(eq_f.astype(BF16), tri_lane_strict)
    sel = gt | (eq & (rank_eq < need))
    sel_f = sel.astype(F32)
    cl = _dot(sel_f.astype(BF16), tri_lane_incl)
    cg = _dot(tri_grp_incl, group_sum(sel_f))
    goff = cg - jnp.broadcast_to(jnp.sum(sel_f, axis=1, keepdims=True), (ag, LANES))
    pos_ref[...] = jnp.where(sel, goff + cl - 1.0, -1.0)
    diag = _iota((ag, ag), 0) == _iota((ag, ag), 1)
    goff_sq = goff if ag == LANES else goff[:, :ag]
    goff_ref[...] = jnp.sum(jnp.where(diag, goff_sq, 0.0), axis=0, keepdims=True).astype(I32)

    pp = _iota((ag, capp), 1, F32)
    cg_b = jnp.broadcast_to(cg[:, 0:1], (ag, capp))
    below = cg_b <= pp
    grp = jnp.sum(below.astype(F32), axis=0, keepdims=True)
    off = jnp.max(jnp.where(below, cg_b, 0.0), axis=0, keepdims=True)
    onehot = _iota((ag, capp), 0, F32) == grp
    in_grp = _dot_tn(cl.astype(BF16), onehot.astype(BF16))
    local = pp[0:1, :] - off
    lane_of = jnp.sum((in_grp <= local).astype(F32), axis=0, keepdims=True)
    live = pp[0:1, :] < float(cap)
    idx = jnp.where(live, grp * float(LANES) + lane_of, 0.0)
    idx_ref[...] = idx.astype(I32)
    aff_grp = _dot_tn(a, onehot.astype(F32), precision=HIGHEST)
    pick = _iota((LANES, capp), 0, F32) == lane_of
    gate = jnp.sum(jnp.where(pick, aff_grp, 0.0), axis=0, keepdims=True)
    gate_ref[...] = jnp.where(live, gate, 0.0)


def _route(logits_t, *, cap, n_groups):
    n_exp, ag, _ = logits_t.shape
    capp = -(-cap // LANES) * LANES
    return pl.pallas_call(
        functools.partial(_route_kernel, cap=cap, n_groups=n_groups),
        grid=(n_exp,),
        in_specs=[pl.BlockSpec(logits_t.shape, lambda e: (0, 0, 0))],
        out_specs=[pl.BlockSpec((None, 1, capp), lambda e: (e, 0, 0)),
                   pl.BlockSpec((None, 1, capp), lambda e: (e, 0, 0)),
                   pl.BlockSpec((None, ag, LANES), lambda e: (e, 0, 0)),
                   pl.BlockSpec((None, 1, ag), lambda e: (e, 0, 0))],
        out_shape=[jax.ShapeDtypeStruct((n_exp, 1, capp), I32), jax.ShapeDtypeStruct((n_exp, 1, capp), F32),
                   jax.ShapeDtypeStruct((n_exp, ag, LANES), F32), jax.ShapeDtypeStruct((n_exp, 1, ag), I32)],
        scratch_shapes=[pltpu.VMEM((n_exp, ag, LANES), F32), pltpu.VMEM((n_exp, SUBLANES, LANES), I32)],
        compiler_params=_cparams(("arbitrary",)),
        name="route",
    )(logits_t)


def _ffn_kernel(idx_ref, idxn_ref, gate_ref, hx_ref, wg_ref, wu_ref, wd_ref, o_ref, xbuf, sem, wgb, wub, wdb, *, tm):
    e = pl.program_id(0)
    j = pl.program_id(1)
    nt = pl.num_programs(1)
    step = e * nt + j
    last = pl.num_programs(0) * nt - 1

    sub = SUBLANES
    rows = tm * sub

    def row_copy(rows_ref, r, slot):
        src = hx_ref.at[pl.ds(pl.multiple_of(rows_ref[0, 0, r] * sub, sub), sub)]
        dst = xbuf.at[pl.ds(pl.multiple_of(slot * rows + r * sub, sub), sub)]
        return pltpu.make_async_copy(src, dst, sem.at[slot])

    def slot_copy(slot):
        return pltpu.make_async_copy(hx_ref.at[pl.ds(0, rows)], xbuf.at[pl.ds(pl.multiple_of(slot * rows, sub), rows)],
                                     sem.at[slot])

    @pl.when(step == 0)
    def _():
        def body(r, carry):
            row_copy(idx_ref, r, 0).start()
            return carry
        lax.fori_loop(0, tm, body, 0, unroll=8)

    @pl.when(j == 0)
    def _():
        wgb[...] = wg_ref[...].astype(BF16)
        wub[...] = wu_ref[...].astype(BF16)
        wdb[...] = wd_ref[...].astype(BF16)

    slot = step % 2
    nslot = 1 - slot
    for r in range(tm):
        row_copy(idxn_ref, r, nslot).start()
    slot_copy(slot).wait()
    base = slot * rows
    x = jnp.concatenate([xbuf[pl.ds(base + s, tm, stride=sub), :] for s in range(sub)], axis=1).astype(BF16)
    g = _dot(x, wgb[...])
    u = _dot(x, wub[...])
    hid = (g * jax.nn.sigmoid(g)) * u
    y = _dot(hid.astype(BF16), wdb[...])
    eye = _iota((tm, tm), 0) == _iota((tm, tm), 1)
    gcol = jnp.sum(jnp.where(eye, jnp.broadcast_to(gate_ref[0], (tm, tm)), 0.0), axis=1, keepdims=True)
    o_ref[...] = (y * gcol).astype(BF16)

    @pl.when(step == last)
    def _():
        slot_copy(nslot).wait()


def _expert_ffn(idx, gate, hx3, w_gate, w_up, w_down, *, layer, cap, tm):
    n_exp = idx.shape[0]
    d = w_gate.shape[2]
    f = w_gate.shape[3]
    assert d == SUBLANES * LANES, "a token row must be exactly one (8, 128) f32 tile"
    nt = cap // tm
    idx_t = idx[:, :, :cap].reshape(n_exp * nt, 1, tm)
    gate_t = gate[:, :, :cap].reshape(n_exp * nt, 1, tm)
    n_steps = n_exp * nt
    wspec = lambda a, b: pl.BlockSpec((None, None, a, b), lambda e, j: (layer, e, 0, 0))
    return pl.pallas_call(
        functools.partial(_ffn_kernel, tm=tm),
        grid=(n_exp, nt),
        in_specs=[pl.BlockSpec((1, 1, tm), lambda e, j: (e * nt + j, 0, 0), memory_space=pltpu.SMEM),
                  pl.BlockSpec((1, 1, tm), lambda e, j: (jnp.minimum(e * nt + j + 1, n_steps - 1), 0, 0), memory_space=pltpu.SMEM),
                  pl.BlockSpec((1, 1, tm), lambda e, j: (e * nt + j, 0, 0)),
                  pl.BlockSpec(memory_space=pl.ANY),
                  wspec(d, f), wspec(d, f), wspec(f, d)],
        out_specs=pl.BlockSpec((tm, d), lambda e, j: (e * nt + j, 0)),
        out_shape=jax.ShapeDtypeStruct((n_exp * cap, d), BF16),
        scratch_shapes=[pltpu.VMEM((2 * tm * SUBLANES, LANES), F32), pltpu.SemaphoreType.DMA((2,)),
                        pltpu.VMEM((d, f), BF16), pltpu.VMEM((d, f), BF16), pltpu.VMEM((f, d), BF16)],
        compiler_params=_cparams(("arbitrary", "arbitrary")),
        name="expert_ffn",
    )(idx_t, idx_t, gate_t, hx3, w_gate, w_up, w_down)


def _combine_kernel(ws_ref, nw_ref, *refs, n_exp, cap, win, gpt, final_norm):
    y_refs = refs[:n_exp]
    pos_ref, ye_ref, x_ref, m5_ref, gf_ref, o_ref, acc_ref, xwin, sem = refs[n_exp:]
    a = pl.program_id(0)
    ng = pl.num_programs(0)
    per = max(1, MXU_DEPTH // win)

    def slots(e):
        pos = pos_ref[e, pl.ds(a * gpt, gpt), :]
        return jnp.concatenate([jnp.broadcast_to(pos[j:j + 1, :], (win, LANES)) for j in range(gpt)], axis=1)

    lp = _iota((win, gpt * LANES), 0, F32)
    acc = None
    for e0 in range(0, n_exp, per):
        es = range(e0, min(e0 + per, n_exp))
        hot = jnp.concatenate([((slots(e) - (ws_ref[e * ng + a] - e * cap).astype(F32)) == lp).astype(BF16) for e in es],
                              axis=0)
        rows = jnp.concatenate([y_refs[e][...] for e in es], axis=0)
        part = _dot_tn(hot, rows)
        acc = part if acc is None else acc + part
    acc_ref[...] = acc

    for e in range(n_exp):
        first = ws_ref[e * ng + a] - e * cap

        def extra(k, carry, e=e, first=first):
            lo = first + k * win
            row = jnp.minimum(lo, cap - win)
            cp = pltpu.make_async_copy(ye_ref.at[pl.ds(pl.multiple_of(e * cap + row, BF16_ROWS), win)], xwin, sem)
            cp.start()
            cp.wait()
            pos = slots(e)
            hot = ((pos - row.astype(F32)) == lp) & (pos >= lo.astype(F32))
            acc_ref[...] += _dot_tn(hot.astype(BF16), xwin[...])
            return carry

        lax.fori_loop(1, nw_ref[e * ng + a], extra, 0)

    x_new = x_ref[...] + m5_ref[...] * acc_ref[...]
    if final_norm:
        ms = jnp.mean(x_new * x_new, axis=-1, keepdims=True)
        x_new = x_new * lax.rsqrt(ms + EPS) * gf_ref[...]
    o_ref[...] = x_new


def _combine(ye, posmap, goff, x, m5, g_final, *, cap, final_norm):
    n, d = x.shape
    n_exp = posmap.shape[0]
    gpt = min(COMBINE_GROUPS, n // LANES)
    ng = n // (gpt * LANES)
    tok = gpt * LANES
    win = min(COMBINE_WINDOW, cap)
    goff = goff[:, 0, :ng * gpt:gpt]
    end = jnp.concatenate([goff[:, 1:], jnp.full((n_exp, 1), cap, I32)], axis=1)
    start = jnp.minimum((goff // BF16_ROWS) * BF16_ROWS, cap - win)
    n_win = jnp.maximum((end - start + win - 1) // win, 1).reshape(-1)
    wstart = (start + jnp.arange(n_exp, dtype=I32)[:, None] * cap).reshape(-1)
    yspec = lambda e: pl.BlockSpec((pl.Element(win), pl.Element(d)),
                                   lambda a, ws, nw: (pl.multiple_of(ws[e * ng + a], BF16_ROWS), 0))
    gs = pltpu.PrefetchScalarGridSpec(
        num_scalar_prefetch=2,
        grid=(ng,),
        in_specs=[yspec(e) for e in range(n_exp)]
        + [pl.BlockSpec(posmap.shape, lambda a, ws, nw: (0, 0, 0)),
           pl.BlockSpec(memory_space=pl.ANY),
           pl.BlockSpec((tok, d), lambda a, ws, nw: (a, 0)),
           pl.BlockSpec((1, d), lambda a, ws, nw: (0, 0)), pl.BlockSpec((1, d), lambda a, ws, nw: (0, 0))],
        out_specs=pl.BlockSpec((tok, d), lambda a, ws, nw: (a, 0)),
        scratch_shapes=[pltpu.VMEM((tok, d), F32), pltpu.VMEM((win, d), BF16), pltpu.SemaphoreType.DMA(())],
    )
    return pl.pallas_call(
        functools.partial(_combine_kernel, n_exp=n_exp, cap=cap, win=win, gpt=gpt, final_norm=final_norm),
        grid_spec=gs,
        out_shape=jax.ShapeDtypeStruct((n, d), F32),
        compiler_params=_cparams(("arbitrary",)),
        name="combine_final" if final_norm else "combine",
    )(wstart, n_win, *([ye] * n_exp), posmap, ye, x, m5, g_final)


def _rope_tables(n):
    n_rows = n // GRID_W
    row = jnp.arange(n_rows).astype(F32)
    col = jnp.arange(GRID_W).astype(F32)

    def cs(d):
        nf = d // 4
        inv = ROPE_BASE ** (-jnp.arange(nf, dtype=F32) / nf)
        ang_r, ang_c = row[:, None] * inv, col[:, None] * inv
        grid = lambda f: jnp.concatenate(
            [jnp.broadcast_to(f(ang_r)[:, None, :], (n_rows, GRID_W, nf)),
             jnp.broadcast_to(f(ang_c)[None, :, :], (n_rows, GRID_W, nf))], axis=-1).reshape(n, 2 * nf)
        return grid(jnp.cos), grid(jnp.sin)

    c, s = cs(RET_D)
    c128 = jnp.concatenate([c, c], axis=1)
    s128 = jnp.concatenate([-s, s], axis=1)
    c, s = cs(HEAD_D)
    z = jnp.zeros_like(s)
    c64 = jnp.concatenate([c, c, c, c], axis=1)
    sa = jnp.concatenate([-s, z, -s, z], axis=1)
    sb = jnp.concatenate([z, s, z, s], axis=1)
    return c128, s128, c64, sa, sb


WIN_HEAD_ORDER = (0, 4, 1, 5, 2, 6, 3, 7)


def _permute_in_weight(w):
    d = w.shape[0]
    sizes = (("q_r", 512), ("k_r", 512), ("v_r", 512), ("g_r", 512), ("q_w", 512), ("k_w", 128), ("v_w", 128),
             ("q_n", 512), ("k_n", 512), ("v_n", 512), ("gates", 3 * d))
    off, lay = 0, {}
    for name, size in sizes:
        lay[name] = (off, off + size)
        off += size
    seg = lambda name: w[:, lay[name][0]:lay[name][1]]
    q0 = lay["q_w"][0]
    q_w = [w[:, q0 + h * HEAD_D:q0 + (h + 1) * HEAD_D] for h in WIN_HEAD_ORDER]
    parts = [seg("q_r"), seg("k_r"), seg("v_r"), seg("g_r")] + q_w + [seg("q_n"), seg("k_n"), seg("v_n"), seg("gates"),
                                                                       seg("k_w"), seg("v_w")]
    parts.append(jnp.zeros((d, P_COLS - off), w.dtype))
    return jnp.concatenate(parts, axis=1).astype(BF16)


def _permute_win_rows(w):
    return jnp.concatenate([w[h * HEAD_D:(h + 1) * HEAD_D] for h in WIN_HEAD_ORDER], axis=0).astype(BF16)


def kernel(x, c, ctx, c_ctx, w_mod, b_mod, g_mix, g_ffn, w_in, ret_decay_logit, ret_gn, w_ret, win_sink, w_win, na_rpb,
           w_na, w_out, w_router, w_exp_gate, w_exp_up, w_exp_down, g_final):
    _, n, d = x.shape
    l = ctx.shape[1]
    depth = w_in.shape[0]
    xs, cs = x[0], ctx[0]
    cc = jnp.zeros((SUBLANES, d), F32).at[0].set(c[0]).at[1].set(c_ctx)
    mods = _modulation(cc, w_mod, b_mod)
    tabs_x = _rope_tables(n)
    tabs_c = tuple(jnp.zeros((l, LANES), F32) for _ in range(5))
    cap_x = CAPACITY_FACTOR * n // N_EXPERTS
    cap_c = CAPACITY_FACTOR * l // N_EXPERTS
    grp_c = -(-(l // LANES) // BF16_ROWS) * BF16_ROWS
    vec = lambda v: v.reshape(1, -1)

    for layer in range(depth):
        need_ctx = layer < depth - 1
        last = layer == depth - 1
        mx = [vec(mods[layer, 0, k * d:(k + 1) * d]) for k in range(N_MOD)]
        mc = [vec(mods[layer, 1, k * d:(k + 1) * d]) for k in range(N_MOD)]
        w_p = _permute_in_weight(w_in[layer])
        wr = w_ret[layer].astype(BF16)
        ww = _permute_win_rows(w_win[layer])
        wn = w_na[layer].astype(BF16)
        wo = w_out[layer].astype(BF16)
        wrt = w_router[layer].T
        sink = win_sink[layer].astype(F32)
        lg_rows = jnp.broadcast_to(ret_decay_logit[layer].astype(F32).reshape(2 * RET_HEADS, 1), (2 * RET_HEADS, LANES))
        gn = vec(ret_gn[layer])
        bias_tab = _na_bias_table(na_rpb[layer])
        gmix, gffn = vec(g_mix[layer]), vec(g_ffn[layer])

        p_x = _inproj(xs, gmix, mx[0], mx[1], tabs_x, w_p, rope=True, tm=2048)
        p_c = _inproj(cs, gmix, mc[0], mc[1], tabs_c, w_p, rope=False, tm=l)

        yf, yb = _retention(p_x, p_c, lg_rows, zero_init=False)
        yw = _window_attention(p_x, p_c, sink)
        yn = _na_attention(p_x, p_c, bias_tab)
        xs, hx3, lt = _merge(yf, yb, p_x, gn, yw, yn, wr, ww, wn, wo, xs, mx[2], gffn, mx[3], mx[4], wrt, tm=512)

        idx, gate, posmap, goff = _route(lt.reshape(N_EXPERTS, n // LANES, LANES), cap=cap_x, n_groups=n // LANES)
        ye = _expert_ffn(idx, gate, hx3, w_exp_gate, w_exp_up, w_exp_down, layer=layer, cap=cap_x, tm=512)
        xs = _combine(ye, posmap, goff, xs, mx[5], vec(g_final), cap=cap_x, final_norm=last)

        if need_ctx:
            yfc, ybc = _retention(p_c, p_c, lg_rows, zero_init=True)
            ywc, ync = _ctx_attention(p_c, sink)
            cs, hc3, ltc = _merge(yfc, ybc, p_c, gn, ywc, ync, wr, ww, wn, wo, cs, mc[2], gffn, mc[3], mc[4], wrt, tm=l)
            ltc = jnp.pad(ltc.reshape(N_EXPERTS, l // LANES, LANES), ((0, 0), (0, grp_c - l // LANES), (0, 0)))
            idc, gtc, posc, goffc = _route(ltc, cap=cap_c, n_groups=l // LANES)
            yec = _expert_ffn(idc, gtc, hc3, w_exp_gate, w_exp_up, w_exp_down, layer=layer, cap=cap_c, tm=cap_c)
            cs = _combine(yec, posc, goffc, cs, mc[5], vec(g_final), cap=cap_c, final_norm=False)

    return xs[None]
```

```python
import functools

import jax
import jax.numpy as jnp
from jax import lax
from jax.experimental import pallas as pl
from jax.experimental.pallas import tpu as pltpu

F32, BF16, I32 = jnp.float32, jnp.bfloat16, jnp.int32
HIGHEST = lax.Precision.HIGHEST

GRID_W = 64
RET_HEADS, RET_D, RET_CHUNK = 4, 128, 128
WIN_HEADS, WIN_KV_HEADS, HEAD_D, WINDOW, WIN_BLOCK = 8, 2, 64, 128, 128
NA_HEADS, NA_ROWS, NA_COLS = 8, 8, 16
N_EXPERTS, CAPACITY_FACTOR = 16, 2
N_MOD = 6
ROPE_BASE = 10000.0
EPS = 1e-6
NEG_INF = -1e30

LANES = 128
SUBLANES = 8
BF16_ROWS = 16
VMEM_LIMIT = 56 * 1024 * 1024

SEG = 512
T_GATES, T_VR, T_GR, T_QN, T_KN, T_VN = 0, 6, 7, 8, 9, 10
PLAIN_TILES = 11
R_QR, R_KR, R_QW, R_KVW = 0, 1, 2, 3
LOG2E = 1.4426950408889634
Q_SCALE = HEAD_D ** -0.5 * LOG2E
RET_BLOCK = 4 * RET_CHUNK
WIN_QBLOCKS = 2
NA_QROWS = 4
NA_SLAB_ROWS = 12
NA_BIAS_ZERO = NA_SLAB_ROWS - 2
NA_BIAS_PAIRS = NA_BIAS_ZERO + NA_SLAB_ROWS - 1
COMBINE_GROUPS = 4
COMBINE_WINDOW = 128
MXU_DEPTH = 256


def _cparams(sem):
    return pltpu.CompilerParams(dimension_semantics=sem, vmem_limit_bytes=VMEM_LIMIT)


def _dot(a, b):
    return jnp.dot(a, b, preferred_element_type=F32)


def _dot_nt(a, b, precision=None):
    return lax.dot_general(a, b, (((1,), (1,)), ((), ())), precision=precision, preferred_element_type=F32)


def _dot_tn(a, b, precision=None):
    return lax.dot_general(a, b, (((0,), (0,)), ((), ())), precision=precision, preferred_element_type=F32)


def _iota(shape, dim, dtype=I32):
    return lax.broadcasted_iota(I32, shape, dim).astype(dtype)


def _mod_kernel(s_ref, w_ref, b_ref, o_ref):
    s = s_ref[...]
    s = s * jax.nn.sigmoid(s)
    o_ref[...] = jnp.dot(s, w_ref[...], precision=HIGHEST, preferred_element_type=F32) + b_ref[...]


def _modulation(cc, w_mod, b_mod):
    depth, d, md = w_mod.shape
    tn = 1536
    return pl.pallas_call(
        _mod_kernel,
        grid=(depth, md // tn),
        in_specs=[
            pl.BlockSpec((SUBLANES, d), lambda l, j: (0, 0)),
            pl.BlockSpec((None, d, tn), lambda l, j: (l, 0, j)),
            pl.BlockSpec((None, 1, tn), lambda l, j: (l, 0, j)),
        ],
        out_specs=pl.BlockSpec((None, SUBLANES, tn), lambda l, j: (l, 0, j)),
        out_shape=jax.ShapeDtypeStruct((depth, SUBLANES, md), F32),
        compiler_params=_cparams(("arbitrary", "arbitrary")),
        name="modulation",
    )(cc, w_mod, b_mod.reshape(depth, 1, md))


def _norm_mod(x, g, shift, scale):
    ms = jnp.mean(x * x, axis=-1, keepdims=True)
    y = x * lax.rsqrt(ms + EPS) * g
    return y * (1.0 + scale) + shift


def _norm_kernel(x_ref, g_ref, sh_ref, sc_ref, o_ref):
    o_ref[...] = _norm_mod(x_ref[...], g_ref[...], sh_ref[...], sc_ref[...]).astype(BF16)


def _norm_call(x, g, shift, scale, *, tm):
    n, d = x.shape
    vec = lambda: pl.BlockSpec((1, d), lambda i: (0, 0))
    return pl.pallas_call(
        _norm_kernel,
        grid=(n // tm,),
        in_specs=[pl.BlockSpec((tm, d), lambda i: (i, 0)), vec(), vec(), vec()],
        out_specs=pl.BlockSpec((tm, d), lambda i: (i, 0)),
        out_shape=jax.ShapeDtypeStruct((n, d), BF16),
        compiler_params=_cparams(("arbitrary",)),
        name="norm_modulate",
    )(x, g, shift, scale)


def _proj_kernel(scale_ref, hx_ref, w_ref, o_ref):
    o_ref[...] = (_dot(hx_ref[...], w_ref[...]) * scale_ref[pl.program_id(1)]).astype(BF16)


def _proj(hx, w, scales, *, tm):
    n, d = hx.shape
    tiles = w.shape[1] // SEG
    return pl.pallas_call(
        _proj_kernel,
        grid=(n // tm, tiles),
        in_specs=[pl.BlockSpec(memory_space=pltpu.SMEM),
                  pl.BlockSpec((tm, d), lambda i, j: (i, 0)), pl.BlockSpec((d, SEG), lambda i, j: (0, j))],
        out_specs=pl.BlockSpec((tm, SEG), lambda i, j: (i, j)),
        out_shape=jax.ShapeDtypeStruct((n, tiles * SEG), BF16),
        compiler_params=_cparams(("arbitrary", "arbitrary")),
        name="projection",
    )(scales, hx, w)


def _proj_rope_kernel(hx_ref, w_ref, *refs, head_d, scale, n_rot):
    tabs, o_ref = refs[:-1], refs[-1]
    acc = _dot(hx_ref[...], w_ref[...])
    if head_d == RET_D:
        c = tabs[0][...] * scale
        s = tabs[1][...] * scale
        rot = lambda a: a * c + pltpu.roll(a, 64, 1) * s
    else:
        c = tabs[0][...] * scale
        sa = tabs[1][...] * scale
        sb = tabs[2][...] * scale
        rot = lambda a: a * c + pltpu.roll(a, 96, 1) * sa + pltpu.roll(a, 32, 1) * sb
    groups = [acc[:, k * LANES:(k + 1) * LANES] for k in range(SEG // LANES)]
    o_ref[...] = jnp.concatenate([rot(a) if k < n_rot else a for k, a in enumerate(groups)], axis=1).astype(BF16)


def _proj_rope(hx, w, tile, tabs, *, head_d, scale, n_rot, tm, name):
    n, d = hx.shape
    return pl.pallas_call(
        functools.partial(_proj_rope_kernel, head_d=head_d, scale=scale, n_rot=n_rot),
        grid=(n // tm,),
        in_specs=[pl.BlockSpec((tm, d), lambda i: (i, 0)), pl.BlockSpec((d, SEG), lambda i: (0, tile))]
        + [pl.BlockSpec((tm, LANES), lambda i: (i, 0)) for _ in tabs],
        out_specs=pl.BlockSpec((tm, SEG), lambda i: (i, 0)),
        out_shape=jax.ShapeDtypeStruct((n, SEG), BF16),
        compiler_params=_cparams(("arbitrary",)),
        name=name,
    )(hx, w, *tabs)


def _ret_kernel(lg_ref, qf_ref, kf_ref, vf_ref, qb_ref, kb_ref, vb_ref, kc_ref, vc_ref, yf_ref, yb_ref,
                rf_ref, rb_ref, tab_ref, *, zero_init):
    c = pl.program_id(0)
    ch = RET_CHUNK
    dh = RET_D
    states = (rf_ref, rb_ref)

    @pl.when(c == 0)
    def _():
        lg = jax.nn.log_sigmoid(lg_ref[...])
        ii = _iota((ch, ch), 0, F32)
        jj = _iota((ch, ch), 1, F32)
        n_ctx = kc_ref.shape[0]
        mm = _iota((n_ctx, dh), 0, F32)
        for d in range(2):
            for h in range(RET_HEADS):
                row = lg[RET_HEADS * d + h:RET_HEADS * d + h + 1, :]
                l = jnp.broadcast_to(row, (ch, ch))
                if d == 0:
                    diff = ii - jj
                    dmat = jnp.where(diff >= 0.0, jnp.exp(jnp.maximum(diff, 0.0) * l), 0.0)
                    zeta = jnp.exp((ch - 1.0 - ii) * l)
                    xi = jnp.exp((ii + 1.0) * l)
                else:
                    diff = jj - ii
                    dmat = jnp.where(diff >= 1.0, jnp.exp(jnp.maximum(diff, 0.0) * l), 0.0)
                    zeta = jnp.exp(ii * l)
                    xi = jnp.exp((ch - ii) * l)
                tab_ref[d, h, 0] = dmat
                tab_ref[d, h, 1] = zeta
                tab_ref[d, h, 2] = xi
                tab_ref[d, h, 3] = jnp.exp(ch * l)
                if zero_init:
                    states[d][h] = jnp.zeros((dh, dh), F32)
                else:
                    lc = jnp.broadcast_to(row, (n_ctx, dh))
                    w = jnp.exp((n_ctx - 1.0 - mm) * lc) if d == 0 else jnp.exp(mm * lc)
                    sl = slice(h * dh, (h + 1) * dh)
                    kw = (kc_ref[:, sl].astype(F32) * w).astype(BF16)
                    states[d][h] = _dot_tn(kw, vc_ref[:, sl])

    n_sub = qf_ref.shape[0] // ch
    for d, (q_ref, k_ref, v_ref, y_ref) in enumerate(((qf_ref, kf_ref, vf_ref, yf_ref), (qb_ref, kb_ref, vb_ref, yb_ref))):
        order = range(n_sub) if d == 0 else range(n_sub - 1, -1, -1)
        for h in range(RET_HEADS):
            sl = slice(h * dh, (h + 1) * dh)
            r = states[d][h]
            for j in order:
                rows = slice(j * ch, (j + 1) * ch)
                q, k, v = q_ref[rows, sl], k_ref[rows, sl], v_ref[rows, sl]
                a = (_dot_nt(q, k) * tab_ref[d, h, 0]).astype(BF16)
                inner = _dot(a, v)
                cross = _dot(q, r.astype(BF16)) * tab_ref[d, h, 2]
                y_ref[rows, sl] = inner + cross
                kz = (k.astype(F32) * tab_ref[d, h, 1]).astype(BF16)
                r = tab_ref[d, h, 3] * r + _dot_tn(kz, v)
            states[d][h] = r


def _retention(q, k, v, kc, vc, lg_rows, *, zero_init):
    n = q[0].shape[0]
    blk = min(RET_BLOCK, n)
    nc = n // blk
    w = RET_HEADS * RET_D
    fwd = lambda sg: pl.BlockSpec((blk, w), lambda c: (c, sg[1]))
    bwd = lambda sg: pl.BlockSpec((blk, w), lambda c: (nc - 1 - c, sg[1]))
    ctx = lambda sg: pl.BlockSpec((sg[0].shape[0], w), lambda c: (0, sg[1]))
    return pl.pallas_call(
        functools.partial(_ret_kernel, zero_init=zero_init),
        grid=(nc,),
        in_specs=[pl.BlockSpec((SUBLANES, LANES), lambda c: (0, 0)),
                  fwd(q), fwd(k), fwd(v), bwd(q), bwd(k), bwd(v), ctx(kc), ctx(vc)],
        out_specs=[pl.BlockSpec((blk, w), lambda c: (c, 0)), pl.BlockSpec((blk, w), lambda c: (nc - 1 - c, 0))],
        out_shape=[jax.ShapeDtypeStruct((n, w), F32), jax.ShapeDtypeStruct((n, w), F32)],
        scratch_shapes=[pltpu.VMEM((RET_HEADS, RET_D, RET_D), F32), pltpu.VMEM((RET_HEADS, RET_D, RET_D), F32),
                        pltpu.VMEM((2, RET_HEADS, 4, RET_CHUNK, RET_CHUNK), F32)],
        compiler_params=_cparams(("arbitrary",)),
        name="retention_ctx" if zero_init else "retention",
    )(lg_rows, q[0], k[0], v[0], q[0], k[0], v[0], kc[0], vc[0])


def _half_mask(x, half):
    lane = _iota(x.shape, 1)
    keep = (lane < HEAD_D) if half == 0 else (lane >= HEAD_D)
    return jnp.where(keep, x, jnp.zeros_like(x))


def _softmax_pv(s, v, extra=None):
    m = jnp.max(s, axis=1, keepdims=True)
    if extra is not None:
        m = jnp.maximum(m, extra)
    p = jnp.exp2(s - m)
    den = jnp.sum(p, axis=1, keepdims=True)
    if extra is not None:
        den = den + jnp.exp2(extra - m)
    return _dot(p.astype(BF16), v) / den


def _softmax_pv_t(st, v, extra=None):
    m = jnp.max(st, axis=0, keepdims=True)
    if extra is not None:
        m = jnp.maximum(m, extra)
    p = jnp.exp2(st - m)
    den = jnp.sum(p, axis=0, keepdims=True)
    if extra is not None:
        den = den + jnp.exp2(extra - m)
    return _dot_tn(v, p.astype(BF16)) / den


def _win_kernel(sink_ref, q_ref, *refs):
    nkb = WIN_QBLOCKS + 2
    k_refs, v_refs = refs[:nkb], refs[nkb:2 * nkb]
    kx_ref, vx_ref, mask_ref, o_ref = refs[2 * nkb:]
    s = pl.program_id(0)
    last = pl.num_programs(0) - 1
    nq = WIN_QBLOCKS * WIN_BLOCK
    nk = nkb * WIN_BLOCK
    k_all = jnp.concatenate([r[...] for r in k_refs] + [kx_ref[...]], axis=0)
    v_all = jnp.concatenate([r[...] for r in v_refs] + [vx_ref[...]], axis=0)
    n_cols = SEG // LANES
    qs = jnp.concatenate([q_ref[:, c * LANES:(c + 1) * LANES] for c in range(n_cols)], axis=0)
    variant = jnp.where(s == 0, 1, 0) + jnp.where(s == last, 2, 0)
    valid = mask_ref[variant] > 0.5
    valid = jnp.concatenate([valid] * n_cols, axis=1)
    out_t = None
    for g in range(WIN_KV_HEADS):
        st = _dot_nt(_half_mask(k_all, g), qs)
        st = jnp.concatenate([jnp.where(valid, st[:nk], NEG_INF), st[nk:]], axis=0)
        sink = jnp.concatenate([jnp.full((1, nq), sink_ref[n_cols * g + c], F32) for c in range(n_cols)], axis=1)
        o = _softmax_pv_t(st, _half_mask(v_all, g), sink)
        out_t = o if out_t is None else out_t + o
    o = out_t.T
    o_ref[...] = jnp.concatenate([o[c * nq:(c + 1) * nq, :] for c in range(n_cols)], axis=1).astype(BF16)


def _win_valid_table():
    import numpy as np
    nkb = WIN_QBLOCKS + 2
    qpos = np.arange(WIN_QBLOCKS * WIN_BLOCK)[None, :]
    kpos = np.arange(nkb * WIN_BLOCK)[:, None] - WIN_BLOCK
    band = np.abs(kpos - qpos) <= WINDOW
    tabs = []
    for variant in range(4):
        ok = band.copy()
        if variant & 1:
            ok &= kpos >= 0
        if variant & 2:
            ok &= kpos < WIN_QBLOCKS * WIN_BLOCK
        tabs.append(ok)
    return jnp.asarray(np.stack(tabs).astype(np.float32))


def _window_attention(q, kv, kv_ctx, sink):
    n = q[0].shape[0]
    nb = n // WIN_BLOCK
    assert nb % WIN_QBLOCKS == 0
    l = kv_ctx[0].shape[0]
    nkb = WIN_QBLOCKS + 2
    per = SEG // LANES
    blk = lambda off, j: pl.BlockSpec((WIN_BLOCK, LANES),
                                      lambda i: (jnp.clip(WIN_QBLOCKS * i - 1 + j, 0, nb - 1), per * kv[1] + off))
    mask_tab = _win_valid_table()
    return pl.pallas_call(
        _win_kernel,
        grid=(nb // WIN_QBLOCKS,),
        in_specs=[pl.BlockSpec(memory_space=pltpu.SMEM),
                  pl.BlockSpec((WIN_QBLOCKS * WIN_BLOCK, SEG), lambda i: (i, q[1]))]
        + [blk(0, j) for j in range(nkb)] + [blk(1, j) for j in range(nkb)]
        + [pl.BlockSpec((l, LANES), lambda i: (0, per * kv_ctx[1])), pl.BlockSpec((l, LANES), lambda i: (0, per * kv_ctx[1] + 1)),
           pl.BlockSpec(mask_tab.shape, lambda i: (0, 0, 0))],
        out_specs=pl.BlockSpec((WIN_QBLOCKS * WIN_BLOCK, SEG), lambda i: (i, 0)),
        out_shape=jax.ShapeDtypeStruct((n, SEG), BF16),
        compiler_params=_cparams(("arbitrary",)),
        name="window_attention",
    )(sink, q[0], *([kv[0]] * (2 * nkb)), kv_ctx[0], kv_ctx[0], mask_tab)


def _na_slab_start(s, half_rows):
    return jnp.clip((NA_QROWS // 2) * s - NA_ROWS // 4, 0, half_rows - NA_SLAB_ROWS // 2)


def _na_kernel(q_ref, *refs, rows):
    nslab = NA_SLAB_ROWS // 2
    k_refs, v_refs = refs[:nslab], refs[nslab:2 * nslab]
    kx_ref, vx_ref, bias_ref, mask_ref, o_ref = refs[2 * nslab:]
    s = pl.program_id(0)
    last = pl.num_programs(0) - 1
    w = GRID_W
    nq = NA_QROWS * w
    nk = NA_SLAB_ROWS * w
    delta = 2 * _na_slab_start(s, rows // 2) - NA_QROWS * s
    variant = jnp.where(s == 0, 0, jnp.where(s == last, 2, 1))
    valid = mask_ref[variant] > 0.5
    valid = jnp.concatenate([valid, valid], axis=1)
    row = _iota((LANES, nq), 0)
    for pair in range(NA_HEADS // 2):
        sl = slice(pair * LANES, (pair + 1) * LANES)
        q = q_ref[:, sl]
        qs = jnp.concatenate([_half_mask(q, 0), _half_mask(q, 1)], axis=0)
        k_all = jnp.concatenate([r[:, sl] for r in k_refs] + [kx_ref[:, sl]], axis=0)
        v_all = jnp.concatenate([r[:, sl] for r in v_refs] + [vx_ref[:, sl]], axis=0)
        st = _dot_nt(k_all, qs)
        bias = jnp.concatenate(
            [jnp.concatenate([bias_ref[2 * pair + u, delta + 2 * i - a + NA_BIAS_ZERO]
                              for u in range(2) for a in range(0, NA_QROWS, 2)], axis=1) for i in range(nslab)], axis=0)
        s_loc = jnp.where(valid, st[:nk] + bias, NEG_INF)
        o = _softmax_pv_t(jnp.concatenate([s_loc, st[nk:]], axis=0), v_all)
        o_ref[:, sl] = jnp.where(row < HEAD_D, o[:, :nq], o[:, nq:]).T.astype(BF16)


def _na_valid_table():
    import numpy as np
    kk = np.arange(NA_SLAB_ROWS)[:, None, None, None]
    ck = np.arange(GRID_W)[None, :, None, None]
    a = np.arange(NA_QROWS)[None, None, :, None]
    cq = np.arange(GRID_W)[None, None, None, :]
    c_start = np.clip(cq - NA_COLS // 2, 0, GRID_W - NA_COLS)
    col_ok = (ck >= c_start) & (ck < c_start + NA_COLS)
    first_row = (0 * a, a, 0 * a + NA_SLAB_ROWS - NA_ROWS)
    tabs = [(col_ok & (kk >= f) & (kk < f + NA_ROWS)).reshape(NA_SLAB_ROWS * GRID_W, NA_QROWS * GRID_W) for f in first_row]
    return jnp.asarray(np.stack(tabs).astype(np.float32))


def _na_attention(p, p_ctx, bias_tab):
    n = p.shape[0]
    rows = n // GRID_W
    assert rows % NA_QROWS == 0 and rows >= NA_SLAB_ROWS
    steps = rows // NA_QROWS
    l = p_ctx.shape[0]
    slab = lambda t, i: pl.BlockSpec((2 * GRID_W, SEG), lambda s: (_na_slab_start(s, rows // 2) + i, t))
    nslab = NA_SLAB_ROWS // 2
    mask_tab = _na_valid_table()
    return pl.pallas_call(
        functools.partial(_na_kernel, rows=rows),
        grid=(steps,),
        in_specs=[pl.BlockSpec((NA_QROWS * GRID_W, SEG), lambda s: (s, T_QN))]
        + [slab(T_KN, i) for i in range(nslab)] + [slab(T_VN, i) for i in range(nslab)]
        + [pl.BlockSpec((l, SEG), lambda s: (0, T_KN)), pl.BlockSpec((l, SEG), lambda s: (0, T_VN)),
           pl.BlockSpec(bias_tab.shape, lambda s: (0, 0, 0, 0)), pl.BlockSpec(mask_tab.shape, lambda s: (0, 0, 0))],
        out_specs=pl.BlockSpec((NA_QROWS * GRID_W, SEG), lambda s: (s, 0)),
        out_shape=jax.ShapeDtypeStruct((n, SEG), BF16),
        compiler_params=_cparams(("arbitrary",)),
        name="neighbourhood_attention",
    )(p, *([p] * (2 * nslab)), p_ctx, p_ctx, bias_tab, mask_tab)


def _na_bias_table(rpb):
    n_r, n_c = 2 * NA_ROWS - 1, 2 * NA_COLS - 1
    rpb = rpb.astype(F32)
    pick = lambda e, uk, wq: rpb[:, min(max(e - NA_BIAS_ZERO + uk - wq + NA_ROWS - 1, 0), n_r - 1), :]
    rows = jnp.stack([jnp.stack([jnp.stack([pick(e, uk, wq) for wq in range(2)], axis=1) for uk in range(2)], axis=1)
                      for e in range(NA_BIAS_PAIRS)], axis=1)
    import numpy as np
    ck = np.arange(GRID_W)[:, None]
    cq = np.arange(GRID_W)[None, :]
    ci = np.clip(ck - cq, -(NA_COLS - 1), NA_COLS - 1) + NA_COLS - 1
    sel = (ci[None] == np.arange(n_c)[:, None, None]).astype(np.float32)
    sel2 = np.zeros((2, n_c, GRID_W, 2, GRID_W), np.float32)
    for wq in range(2):
        sel2[wq, :, :, wq, :] = sel
    sel2 = jnp.asarray(sel2.reshape(2 * n_c, GRID_W, 2 * GRID_W))
    rows = rows.reshape(rpb.shape[0], NA_BIAS_PAIRS, 2, 2 * n_c)
    tab = jnp.einsum("heuj,jkl->heukl", rows, sel2, precision=HIGHEST)
    return tab.reshape(rpb.shape[0], NA_BIAS_PAIRS, 2 * GRID_W, 2 * GRID_W)


def _ctx_attn_kernel(sink_ref, r_ref, p_ref, ow_ref, on_ref):
    l = p_ref.shape[0]
    n_cols = SEG // LANES
    k_all = r_ref[:, R_KVW * SEG:R_KVW * SEG + LANES]
    v_all = r_ref[:, R_KVW * SEG + LANES:R_KVW * SEG + 2 * LANES]
    qs = jnp.concatenate([r_ref[:, R_QW * SEG + c * LANES:R_QW * SEG + (c + 1) * LANES] for c in range(n_cols)], axis=0)
    outs = []
    for g in range(WIN_KV_HEADS):
        s = _dot_nt(qs, _half_mask(k_all, g))
        sink = jnp.concatenate([jnp.full((l, 1), sink_ref[n_cols * g + c], F32) for c in range(n_cols)], axis=0)
        outs.append(_softmax_pv(s, _half_mask(v_all, g), sink))
    o = outs[0] + outs[1]
    ow_ref[...] = jnp.concatenate([o[c * l:(c + 1) * l, :] for c in range(n_cols)], axis=1).astype(BF16)
    for pair in range(NA_HEADS // 2):
        sl = lambda t: slice(t * SEG + pair * LANES, t * SEG + (pair + 1) * LANES)
        q, k, v = p_ref[:, sl(T_QN)], p_ref[:, sl(T_KN)], p_ref[:, sl(T_VN)]
        out = None
        for u in range(2):
            o = _softmax_pv(_dot_nt(q, _half_mask(k, u)), _half_mask(v, u))
            out = o if out is None else out + o
        on_ref[:, pair * LANES:(pair + 1) * LANES] = out.astype(BF16)


def _ctx_attention(r_ctx, p_ctx, sink):
    l = p_ctx.shape[0]
    return pl.pallas_call(
        _ctx_attn_kernel,
        in_specs=[pl.BlockSpec(memory_space=pltpu.SMEM), pl.BlockSpec(r_ctx.shape, lambda: (0, 0)),
                  pl.BlockSpec(p_ctx.shape, lambda: (0, 0))],
        out_specs=[pl.BlockSpec((l, SEG), lambda: (0, 0)), pl.BlockSpec((l, SEG), lambda: (0, 0))],
        out_shape=[jax.ShapeDtypeStruct((l, SEG), BF16), jax.ShapeDtypeStruct((l, SEG), BF16)],
        compiler_params=pltpu.CompilerParams(vmem_limit_bytes=VMEM_LIMIT),
        name="context_attention",
    )(sink, r_ctx, p_ctx)


def _merge_kernel(yf_ref, yb_ref, gr_ref, gn_ref, yw_ref, yn_ref, ga_ref, gb_ref, gc_ref,
                  wr_ref, ww_ref, wn_ref, wo_ref, x_ref, m2_ref, gf_ref, m3_ref, m4_ref, wrt_ref,
                  xo_ref, hx_ref, lt_ref):
    y = yf_ref[...] + yb_ref[...]
    parts = []
    for h in range(RET_HEADS):
        yh = y[:, h * RET_D:(h + 1) * RET_D]
        mu = jnp.mean(yh, axis=-1, keepdims=True)
        var = jnp.mean(jnp.square(yh - mu), axis=-1, keepdims=True)
        parts.append((yh - mu) * lax.rsqrt(var + EPS))
    g = gr_ref[...].astype(F32)
    ya = jnp.concatenate(parts, axis=1) * gn_ref[...] * (g * jax.nn.sigmoid(g))
    za = _dot(ya.astype(BF16), wr_ref[...])
    zb = _dot(yw_ref[...], ww_ref[...])
    zc = _dot(yn_ref[...], wn_ref[...])
    sig = lambda r: jax.nn.sigmoid(r[...].astype(F32))
    mix = sig(ga_ref) * za + sig(gb_ref) * zb + sig(gc_ref) * zc
    x_new = x_ref[...] + m2_ref[...] * _dot(mix.astype(BF16), wo_ref[...])
    xo_ref[...] = x_new
    h2 = _norm_mod(x_new, gf_ref[...], m3_ref[...], m4_ref[...])
    n_sub = h2.shape[1] // LANES
    for s in range(n_sub):
        hx_ref[pl.ds(s, h2.shape[0], stride=n_sub), :] = h2[:, s * LANES:(s + 1) * LANES]
    lt_ref[...] = _dot_nt(wrt_ref[...], h2, precision=HIGHEST)


def _merge(yf, yb, p, gn, yw, yn, wr, ww, wn, wo, x, m2, gf, m3, m4, wrt, *, tm):
    n, d = x.shape
    row = lambda wdt, t: pl.BlockSpec((tm, wdt), lambda i: (i, t))
    full = lambda a: pl.BlockSpec(a.shape, lambda i: (0,) * a.ndim)
    gate0 = T_GATES * SEG // d
    return pl.pallas_call(
        _merge_kernel,
        grid=(n // tm,),
        in_specs=[row(SEG, 0), row(SEG, 0), row(SEG, T_GR), full(gn), row(SEG, 0), row(SEG, 0),
                  row(d, gate0), row(d, gate0 + 1), row(d, gate0 + 2),
                  full(wr), full(ww), full(wn), full(wo), row(d, 0), full(m2), full(gf), full(m3), full(m4), full(wrt)],
        out_specs=[pl.BlockSpec((tm, d), lambda i: (i, 0)),
                   pl.BlockSpec((tm * (d // LANES), LANES), lambda i: (i, 0)),
                   pl.BlockSpec((N_EXPERTS, tm), lambda i: (0, i))],
        out_shape=[jax.ShapeDtypeStruct((n, d), F32), jax.ShapeDtypeStruct((n * (d // LANES), LANES), F32),
                   jax.ShapeDtypeStruct((N_EXPERTS, n), F32)],
        compiler_params=_cparams(("arbitrary",)),
        name="merge",
    )(yf, yb, p, gn, yw, yn, p, p, p, wr, ww, wn, wo, x, m2, gf, m3, m4, wrt)


def _route_kernel(lt_ref, idx_ref, gate_ref, pos_ref, goff_ref, aff_ref, thr_ref, *, cap, n_groups):
    e = pl.program_id(0)
    n_exp, ag, _ = lt_ref.shape
    capp = idx_ref.shape[-1]

    @pl.when(e == 0)
    def _():
        lt = lt_ref[...]
        ex = jnp.exp(lt - jnp.max(lt, axis=0, keepdims=True))
        aff = ex / jnp.sum(ex, axis=0, keepdims=True)
        real = _iota(aff.shape, 1) < n_groups
        aff = jnp.where(real, aff, 0.0)
        aff_ref[...] = aff
        bits = pltpu.bitcast(aff, I32)

        def body(k, t):
            cand = t | (jnp.int32(1) << (30 - k))
            cnt = jnp.sum(jnp.sum((bits >= cand).astype(F32), axis=2, keepdims=True), axis=1, keepdims=True)
            return jnp.where(cnt >= float(cap), cand, t)

        t = lax.fori_loop(0, 31, body, jnp.zeros((n_exp, 1, 1), I32))
        thr_ref[...] = jnp.broadcast_to(t, thr_ref.shape)

    a = aff_ref[e]
    bits = pltpu.bitcast(a, I32)
    t = thr_ref[e][0:1, :]
    gt = bits > t
    eq = bits == t
    tri_lane_strict = (_iota((LANES, LANES), 0) < _iota((LANES, LANES), 1)).astype(BF16)
    tri_lane_incl = (_iota((LANES, LANES), 0) <= _iota((LANES, LANES), 1)).astype(BF16)
    tri_grp_strict = (_iota((ag, ag), 1) < _iota((ag, ag), 0)).astype(BF16)
    tri_grp_incl = (_iota((ag, ag), 1) <= _iota((ag, ag), 0)).astype(BF16)

    def total(mask_f):
        return jnp.sum(jnp.sum(mask_f, axis=1, keepdims=True), axis=0, keepdims=True)

    def group_sum(mask_f):
        return jnp.broadcast_to(jnp.sum(mask_f, axis=1, keepdims=True), (ag, LANES)).astype(BF16)

    eq_f = eq.astype(F32)
    need = float(cap) - total(gt.astype(F32))
    rank_eq = _dot(tri_grp_strict, group_sum(eq_f)) + _dot(eq_f.astype(BF16), tri_lane_strict)
    sel = gt | (eq & (rank_eq < need))
    sel_f = sel.astype(F32)
    cl = _dot(sel_f.astype(BF16), tri_lane_incl)
    cg = _dot(tri_grp_incl, group_sum(sel_f))
    goff = cg - jnp.broadcast_to(jnp.sum(sel_f, axis=1, keepdims=True), (ag, LANES))
    pos_ref[...] = jnp.where(sel, goff + cl - 1.0, -1.0)
    diag = _iota((ag, ag), 0) == _iota((ag, ag), 1)
    goff_sq = goff if ag == LANES else goff[:, :ag]
    goff_ref[...] = jnp.sum(jnp.where(diag, goff_sq, 0.0), axis=0, keepdims=True).astype(I32)

    pp = _iota((ag, capp), 1, F32)
    cg_b = jnp.broadcast_to(cg[:, 0:1], (ag, capp))
    below = cg_b <= pp
    grp = jnp.sum(below.astype(F32), axis=0, keepdims=True)
    off = jnp.max(jnp.where(below, cg_b, 0.0), axis=0, keepdims=True)
    onehot = _iota((ag, capp), 0, F32) == grp
    in_grp = _dot_tn(cl.astype(BF16), onehot.astype(BF16))
    local = pp[0:1, :] - off
    lane_of = jnp.sum((in_grp <= local).astype(F32), axis=0, keepdims=True)
    live = pp[0:1, :] < float(cap)
    idx = jnp.where(live, grp * float(LANES) + lane_of, 0.0)
    idx_ref[...] = idx.astype(I32)
    aff_grp = _dot_tn(a, onehot.astype(F32), precision=HIGHEST)
    pick = _iota((LANES, capp), 0, F32) == lane_of
    gate = jnp.sum(jnp.where(pick, aff_grp, 0.0), axis=0, keepdims=True)
    gate_ref[...] = jnp.where(live, gate, 0.0)


def _route(logits_t, *, cap, n_groups):
    n_exp, ag, _ = logits_t.shape
    capp = -(-cap // LANES) * LANES
    return pl.pallas_call(
        functools.partial(_route_kernel, cap=cap, n_groups=n_groups),
        grid=(n_exp,),
        in_specs=[pl.BlockSpec(logits_t.shape, lambda e: (0, 0, 0))],
        out_specs=[pl.BlockSpec((None, 1, capp), lambda e: (e, 0, 0)),
                   pl.BlockSpec((None, 1, capp), lambda e: (e, 0, 0)),
                   pl.BlockSpec((None, ag, LANES), lambda e: (e, 0, 0)),
                   pl.BlockSpec((None, 1, ag), lambda e: (e, 0, 0))],
        out_shape=[jax.ShapeDtypeStruct((n_exp, 1, capp), I32), jax.ShapeDtypeStruct((n_exp, 1, capp), F32),
                   jax.ShapeDtypeStruct((n_exp, ag, LANES), F32), jax.ShapeDtypeStruct((n_exp, 1, ag), I32)],
        scratch_shapes=[pltpu.VMEM((n_exp, ag, LANES), F32), pltpu.VMEM((n_exp, SUBLANES, LANES), I32)],
        compiler_params=_cparams(("arbitrary",)),
        name="route",
    )(logits_t)


def _ffn_kernel(idx_ref, idxn_ref, gate_ref, hx_ref, wg_ref, wu_ref, wd_ref, o_ref, xbuf, sem, wgb, wub, wdb, *, tm):
    e = pl.program_id(0)
    j = pl.program_id(1)
    nt = pl.num_programs(1)
    step = e * nt + j
    last = pl.num_programs(0) * nt - 1

    sub = SUBLANES
    rows = tm * sub

    def row_copy(rows_ref, r, slot):
        src = hx_ref.at[pl.ds(pl.multiple_of(rows_ref[0, 0, r] * sub, sub), sub)]
        dst = xbuf.at[pl.ds(pl.multiple_of(slot * rows + r * sub, sub), sub)]
        return pltpu.make_async_copy(src, dst, sem.at[slot])

    def slot_copy(slot):
        return pltpu.make_async_copy(hx_ref.at[pl.ds(0, rows)], xbuf.at[pl.ds(pl.multiple_of(slot * rows, sub), rows)],
                                     sem.at[slot])

    @pl.when(step == 0)
    def _():
        def body(r, carry):
            row_copy(idx_ref, r, 0).start()
            return carry
        lax.fori_loop(0, tm, body, 0, unroll=8)

    @pl.when(j == 0)
    def _():
        wgb[...] = wg_ref[...].astype(BF16)
        wub[...] = wu_ref[...].astype(BF16)
        wdb[...] = wd_ref[...].astype(BF16)

    slot = step % 2
    nslot = 1 - slot
    for r in range(tm):
        row_copy(idxn_ref, r, nslot).start()
    slot_copy(slot).wait()
    base = slot * rows
    x = jnp.concatenate([xbuf[pl.ds(base + s, tm, stride=sub), :] for s in range(sub)], axis=1).astype(BF16)
    g = _dot(x, wgb[...])
    u = _dot(x, wub[...])
    hid = (g * jax.nn.sigmoid(g)) * u
    y = _dot(hid.astype(BF16), wdb[...])
    eye = _iota((tm, tm), 0) == _iota((tm, tm), 1)
    gcol = jnp.sum(jnp.where(eye, jnp.broadcast_to(gate_ref[0], (tm, tm)), 0.0), axis=1, keepdims=True)
    o_ref[...] = (y * gcol).astype(BF16)

    @pl.when(step == last)
    def _():
        slot_copy(nslot).wait()


def _expert_ffn(idx, gate, hx3, w_gate, w_up, w_down, *, layer, cap, tm):
    n_exp = idx.shape[0]
    d = w_gate.shape[2]
    f = w_gate.shape[3]
    assert d == SUBLANES * LANES, "a token row must be exactly one (8, 128) f32 tile"
    nt = cap // tm
    idx_t = idx[:, :, :cap].reshape(n_exp * nt, 1, tm)
    gate_t = gate[:, :, :cap].reshape(n_exp * nt, 1, tm)
    n_steps = n_exp * nt
    wspec = lambda a, b: pl.BlockSpec((None, None, a, b), lambda e, j: (layer, e, 0, 0))
    return pl.pallas_call(
        functools.partial(_ffn_kernel, tm=tm),
        grid=(n_exp, nt),
        in_specs=[pl.BlockSpec((1, 1, tm), lambda e, j: (e * nt + j, 0, 0), memory_space=pltpu.SMEM),
                  pl.BlockSpec((1, 1, tm), lambda e, j: (jnp.minimum(e * nt + j + 1, n_steps - 1), 0, 0), memory_space=pltpu.SMEM),
                  pl.BlockSpec((1, 1, tm), lambda e, j: (e * nt + j, 0, 0)),
                  pl.BlockSpec(memory_space=pl.ANY),
                  wspec(d, f), wspec(d, f), wspec(f, d)],
        out_specs=pl.BlockSpec((tm, d), lambda e, j: (e * nt + j, 0)),
        out_shape=jax.ShapeDtypeStruct((n_exp * cap, d), BF16),
        scratch_shapes=[pltpu.VMEM((2 * tm * SUBLANES, LANES), F32), pltpu.SemaphoreType.DMA((2,)),
                        pltpu.VMEM((d, f), BF16), pltpu.VMEM((d, f), BF16), pltpu.VMEM((f, d), BF16)],
        compiler_params=_cparams(("arbitrary", "arbitrary")),
        name="expert_ffn",
    )(idx_t, idx_t, gate_t, hx3, w_gate, w_up, w_down)


def _combine_kernel(ws_ref, nw_ref, *refs, n_exp, cap, win, gpt, final_norm):
    y_refs = refs[:n_exp]
    pos_ref, ye_ref, x_ref, m5_ref, gf_ref, o_ref, acc_ref, xwin, sem = refs[n_exp:]
    a = pl.program_id(0)
    ng = pl.num_programs(0)
    per = max(1, MXU_DEPTH // win)

    def slots(e):
        pos = pos_ref[e, pl.ds(a * gpt, gpt), :]
        return jnp.concatenate([jnp.broadcast_to(pos[j:j + 1, :], (win, LANES)) for j in range(gpt)], axis=1)

    lp = _iota((win, gpt * LANES), 0, F32)
    acc = None
    for e0 in range(0, n_exp, per):
        es = range(e0, min(e0 + per, n_exp))
        hot = jnp.concatenate([((slots(e) - (ws_ref[e * ng + a] - e * cap).astype(F32)) == lp).astype(BF16) for e in es],
                              axis=0)
        rows = jnp.concatenate([y_refs[e][...] for e in es], axis=0)
        part = _dot_tn(hot, rows)
        acc = part if acc is None else acc + part
    acc_ref[...] = acc

    for e in range(n_exp):
        first = ws_ref[e * ng + a] - e * cap

        def extra(k, carry, e=e, first=first):
            lo = first + k * win
            row = jnp.minimum(lo, cap - win)
            cp = pltpu.make_async_copy(ye_ref.at[pl.ds(pl.multiple_of(e * cap + row, BF16_ROWS), win)], xwin, sem)
            cp.start()
            cp.wait()
            pos = slots(e)
            hot = ((pos - row.astype(F32)) == lp) & (pos >= lo.astype(F32))
            acc_ref[...] += _dot_tn(hot.astype(BF16), xwin[...])
            return carry

        lax.fori_loop(1, nw_ref[e * ng + a], extra, 0)

    x_new = x_ref[...] + m5_ref[...] * acc_ref[...]
    if final_norm:
        ms = jnp.mean(x_new * x_new, axis=-1, keepdims=True)
        x_new = x_new * lax.rsqrt(ms + EPS) * gf_ref[...]
    o_ref[...] = x_new


def _combine(ye, posmap, goff, x, m5, g_final, *, cap, final_norm):
    n, d = x.shape
    n_exp = posmap.shape[0]
    gpt = min(COMBINE_GROUPS, n // LANES)
    ng = n // (gpt * LANES)
    tok = gpt * LANES
    win = min(COMBINE_WINDOW, cap)
    goff = goff[:, 0, :ng * gpt:gpt]
    end = jnp.concatenate([goff[:, 1:], jnp.full((n_exp, 1), cap, I32)], axis=1)
    start = jnp.minimum((goff // BF16_ROWS) * BF16_ROWS, cap - win)
    n_win = jnp.maximum((end - start + win - 1) // win, 1).reshape(-1)
    wstart = (start + jnp.arange(n_exp, dtype=I32)[:, None] * cap).reshape(-1)
    yspec = lambda e: pl.BlockSpec((pl.Element(win), pl.Element(d)),
                                   lambda a, ws, nw: (pl.multiple_of(ws[e * ng + a], BF16_ROWS), 0))
    gs = pltpu.PrefetchScalarGridSpec(
        num_scalar_prefetch=2,
        grid=(ng,),
        in_specs=[yspec(e) for e in range(n_exp)]
        + [pl.BlockSpec(posmap.shape, lambda a, ws, nw: (0, 0, 0)),
           pl.BlockSpec(memory_space=pl.ANY),
           pl.BlockSpec((tok, d), lambda a, ws, nw: (a, 0)),
           pl.BlockSpec((1, d), lambda a, ws, nw: (0, 0)), pl.BlockSpec((1, d), lambda a, ws, nw: (0, 0))],
        out_specs=pl.BlockSpec((tok, d), lambda a, ws, nw: (a, 0)),
        scratch_shapes=[pltpu.VMEM((tok, d), F32), pltpu.VMEM((win, d), BF16), pltpu.SemaphoreType.DMA(())],
    )
    return pl.pallas_call(
        functools.partial(_combine_kernel, n_exp=n_exp, cap=cap, win=win, gpt=gpt, final_norm=final_norm),
        grid_spec=gs,
        out_shape=jax.ShapeDtypeStruct((n, d), F32),
        compiler_params=_cparams(("arbitrary",)),
        name="combine_final" if final_norm else "combine",
    )(wstart, n_win, *([ye] * n_exp), posmap, ye, x, m5, g_final)


def _rope_tables(n):
    n_rows = n // GRID_W
    row = jnp.arange(n_rows).astype(F32)
    col = jnp.arange(GRID_W).astype(F32)

    def cs(d):
        nf = d // 4
        inv = ROPE_BASE ** (-jnp.arange(nf, dtype=F32) / nf)
        ang_r, ang_c = row[:, None] * inv, col[:, None] * inv
        grid = lambda f: jnp.concatenate(
            [jnp.broadcast_to(f(ang_r)[:, None, :], (n_rows, GRID_W, nf)),
             jnp.broadcast_to(f(ang_c)[None, :, :], (n_rows, GRID_W, nf))], axis=-1).reshape(n, 2 * nf)
        return grid(jnp.cos), grid(jnp.sin)

    c, s = cs(RET_D)
    c128 = jnp.concatenate([c, c], axis=1)
    s128 = jnp.concatenate([-s, s], axis=1)
    c, s = cs(HEAD_D)
    z = jnp.zeros_like(s)
    c64 = jnp.concatenate([c, c, c, c], axis=1)
    sa = jnp.concatenate([-s, z, -s, z], axis=1)
    sb = jnp.concatenate([z, s, z, s], axis=1)
    return c128, s128, c64, sa, sb


WIN_HEAD_ORDER = (0, 4, 1, 5, 2, 6, 3, 7)


def _permute_in_weight(w):
    d = w.shape[0]
    sizes = (("q_r", 512), ("k_r", 512), ("v_r", 512), ("g_r", 512), ("q_w", 512), ("k_w", 128), ("v_w", 128),
             ("q_n", 512), ("k_n", 512), ("v_n", 512), ("gates", 3 * d))
    off, lay = 0, {}
    for name, size in sizes:
        lay[name] = (off, off + size)
        off += size
    seg = lambda name: w[:, lay[name][0]:lay[name][1]]
    q0 = lay["q_w"][0]
    q_w = [w[:, q0 + h * HEAD_D:q0 + (h + 1) * HEAD_D] for h in WIN_HEAD_ORDER]
    pad = jnp.zeros((d, SEG - 2 * LANES), w.dtype)
    rope = jnp.concatenate([seg("q_r"), seg("k_r")] + q_w + [seg("k_w"), seg("v_w"), pad], axis=1).astype(BF16)
    plain = jnp.concatenate([seg("gates"), seg("v_r"), seg("g_r"), seg("q_n"), seg("k_n"), seg("v_n")], axis=1).astype(BF16)
    return rope, plain


def _permute_win_rows(w):
    return jnp.concatenate([w[h * HEAD_D:(h + 1) * HEAD_D] for h in WIN_HEAD_ORDER], axis=0).astype(BF16)


def kernel(x, c, ctx, c_ctx, w_mod, b_mod, g_mix, g_ffn, w_in, ret_decay_logit, ret_gn, w_ret, win_sink, w_win, na_rpb,
           w_na, w_out, w_router, w_exp_gate, w_exp_up, w_exp_down, g_final):
    _, n, d = x.shape
    l = ctx.shape[1]
    depth = w_in.shape[0]
    xs, cs = x[0], ctx[0]
    cc = jnp.zeros((SUBLANES, d), F32).at[0].set(c[0]).at[1].set(c_ctx)
    mods = _modulation(cc, w_mod, b_mod)
    c128, s128, c64, sa64, sb64 = _rope_tables(n)
    k_scale = RET_D ** -0.5
    plain_scales = jnp.ones((PLAIN_TILES,), F32).at[T_QN].set(Q_SCALE)
    ctx_rope_scales = jnp.ones((R_KVW + 1,), F32).at[R_KR].set(k_scale).at[R_QW].set(Q_SCALE)
    cap_x = CAPACITY_FACTOR * n // N_EXPERTS
    cap_c = CAPACITY_FACTOR * l // N_EXPERTS
    grp_c = -(-(l // LANES) // BF16_ROWS) * BF16_ROWS
    vec = lambda v: v.reshape(1, -1)

    for layer in range(depth):
        need_ctx = layer < depth - 1
        last = layer == depth - 1
        mx = [vec(mods[layer, 0, k * d:(k + 1) * d]) for k in range(N_MOD)]
        mc = [vec(mods[layer, 1, k * d:(k + 1) * d]) for k in range(N_MOD)]
        w_rope, w_plain = _permute_in_weight(w_in[layer])
        wr = w_ret[layer].astype(BF16)
        ww = _permute_win_rows(w_win[layer])
        wn = w_na[layer].astype(BF16)
        wo = w_out[layer].astype(BF16)
        wrt = w_router[layer].T
        sink = win_sink[layer].astype(F32) * LOG2E
        lg_rows = jnp.broadcast_to(ret_decay_logit[layer].astype(F32).reshape(2 * RET_HEADS, 1), (2 * RET_HEADS, LANES))
        gn = vec(ret_gn[layer])
        bias_tab = _na_bias_table(na_rpb[layer].astype(F32) * LOG2E)
        gmix, gffn = vec(g_mix[layer]), vec(g_ffn[layer])

        hx = _norm_call(xs, gmix, mx[0], mx[1], tm=1024)
        hc = _norm_call(cs, gmix, mc[0], mc[1], tm=l)
        p_x = _proj(hx, w_plain, plain_scales, tm=2048)
        p_c = _proj(hc, w_plain, plain_scales, tm=l)
        r_c = _proj(hc, w_rope, ctx_rope_scales, tm=l)
        rope = functools.partial(_proj_rope, hx, w_rope, tm=2048)
        q_r = rope(R_QR, (c128, s128), head_d=RET_D, scale=1.0, n_rot=SEG // LANES, name="proj_q_ret")
        k_r = rope(R_KR, (c128, s128), head_d=RET_D, scale=k_scale, n_rot=SEG // LANES, name="proj_k_ret")
        q_w = rope(R_QW, (c64, sa64, sb64), head_d=HEAD_D, scale=Q_SCALE, n_rot=SEG // LANES, name="proj_q_win")
        kv_w = rope(R_KVW, (c64, sa64, sb64), head_d=HEAD_D, scale=1.0, n_rot=1, name="proj_kv_win")

        yf, yb = _retention((q_r, 0), (k_r, 0), (p_x, T_VR), (r_c, R_KR), (p_c, T_VR), lg_rows, zero_init=False)
        yw = _window_attention((q_w, 0), (kv_w, 0), (r_c, R_KVW), sink)
        yn = _na_attention(p_x, p_c, bias_tab)
        xs, hx3, lt = _merge(yf, yb, p_x, gn, yw, yn, wr, ww, wn, wo, xs, mx[2], gffn, mx[3], mx[4], wrt, tm=512)

        idx, gate, posmap, goff = _route(lt.reshape(N_EXPERTS, n // LANES, LANES), cap=cap_x, n_groups=n // LANES)
        ye = _expert_ffn(idx, gate, hx3, w_exp_gate, w_exp_up, w_exp_down, layer=layer, cap=cap_x, tm=512)
        xs = _combine(ye, posmap, goff, xs, mx[5], vec(g_final), cap=cap_x, final_norm=last)

        if need_ctx:
            yfc, ybc = _retention((r_c, R_QR), (r_c, R_KR), (p_c, T_VR), (r_c, R_KR), (p_c, T_VR), lg_rows, zero_init=True)
            ywc, ync = _ctx_attention(r_c, p_c, sink)
            cs, hc3, ltc = _merge(yfc, ybc, p_c, gn, ywc, ync, wr, ww, wn, wo, cs, mc[2], gffn, mc[3], mc[4], wrt, tm=l)
            ltc = jnp.pad(ltc.reshape(N_EXPERTS, l // LANES, LANES), ((0, 0), (0, grp_c - l // LANES), (0, 0)))
            idc, gtc, posc, goffc = _route(ltc, cap=cap_c, n_groups=l // LANES)
            yec = _expert_ffn(idc, gtc, hc3, w_exp_gate, w_exp_up, w_exp_down, layer=layer, cap=cap_c, tm=cap_c)
            cs = _combine(yec, posc, goffc, cs, mc[5], vec(g_final), cap=cap_c, final_norm=False)

    return xs[None]
```

```python
import functools

import jax
import jax.numpy as jnp
from jax import lax
from jax.experimental import pallas as pl
from jax.experimental.pallas import tpu as pltpu

F32, BF16, I32 = jnp.float32, jnp.bfloat16, jnp.int32
HIGHEST = lax.Precision.HIGHEST

GRID_W = 64
RET_HEADS, RET_D, RET_CHUNK = 4, 128, 128
WIN_HEADS, WIN_KV_HEADS, HEAD_D, WINDOW, WIN_BLOCK = 8, 2, 64, 128, 128
NA_HEADS, NA_ROWS, NA_COLS = 8, 8, 16
N_EXPERTS, CAPACITY_FACTOR = 16, 2
N_MOD = 6
ROPE_BASE = 10000.0
EPS = 1e-6
NEG_INF = -1e30

LANES = 128
SUBLANES = 8
BF16_ROWS = 16
VMEM_LIMIT = 56 * 1024 * 1024

SEG = 512
T_GATES, T_VR, T_GR, T_QN, T_KN, T_VN = 0, 6, 7, 8, 9, 10
PLAIN_TILES = 11
R_QR, R_KR, R_QW, R_KVW = 0, 1, 2, 3
LOG2E = 1.4426950408889634
Q_SCALE = HEAD_D ** -0.5 * LOG2E
RET_BLOCK = 4 * RET_CHUNK
WIN_QBLOCKS = 2
NA_QROWS = 4
NA_SLAB_ROWS = 12
NA_BIAS_ZERO = NA_SLAB_ROWS - 2
NA_BIAS_PAIRS = NA_BIAS_ZERO + NA_SLAB_ROWS - 1
COMBINE_GROUPS = 4
COMBINE_WINDOW = 128
MXU_DEPTH = 256


def _cparams(sem):
    return pltpu.CompilerParams(dimension_semantics=sem, vmem_limit_bytes=VMEM_LIMIT)


def _dot(a, b):
    return jnp.dot(a, b, preferred_element_type=F32)


def _dot_nt(a, b, precision=None):
    return lax.dot_general(a, b, (((1,), (1,)), ((), ())), precision=precision, preferred_element_type=F32)


def _dot_tn(a, b, precision=None):
    return lax.dot_general(a, b, (((0,), (0,)), ((), ())), precision=precision, preferred_element_type=F32)


def _iota(shape, dim, dtype=I32):
    return lax.broadcasted_iota(I32, shape, dim).astype(dtype)


def _mod_kernel(s_ref, w_ref, b_ref, o_ref):
    s = s_ref[...]
    s = s * jax.nn.sigmoid(s)
    o_ref[...] = jnp.dot(s, w_ref[...], precision=HIGHEST, preferred_element_type=F32) + b_ref[...]


def _modulation(cc, w_mod, b_mod):
    depth, d, md = w_mod.shape
    tn = 1536
    return pl.pallas_call(
        _mod_kernel,
        grid=(depth, md // tn),
        in_specs=[
            pl.BlockSpec((SUBLANES, d), lambda l, j: (0, 0)),
            pl.BlockSpec((None, d, tn), lambda l, j: (l, 0, j)),
            pl.BlockSpec((None, 1, tn), lambda l, j: (l, 0, j)),
        ],
        out_specs=pl.BlockSpec((None, SUBLANES, tn), lambda l, j: (l, 0, j)),
        out_shape=jax.ShapeDtypeStruct((depth, SUBLANES, md), F32),
        compiler_params=_cparams(("arbitrary", "arbitrary")),
        name="modulation",
    )(cc, w_mod, b_mod.reshape(depth, 1, md))


def _norm_mod(x, g, shift, scale):
    ms = jnp.mean(x * x, axis=-1, keepdims=True)
    y = x * lax.rsqrt(ms + EPS) * g
    return y * (1.0 + scale) + shift


def _norm_kernel(x_ref, g_ref, sh_ref, sc_ref, o_ref):
    o_ref[...] = _norm_mod(x_ref[...], g_ref[...], sh_ref[...], sc_ref[...]).astype(BF16)


def _norm_call(x, g, shift, scale, *, tm):
    n, d = x.shape
    vec = lambda: pl.BlockSpec((1, d), lambda i: (0, 0))
    return pl.pallas_call(
        _norm_kernel,
        grid=(n // tm,),
        in_specs=[pl.BlockSpec((tm, d), lambda i: (i, 0)), vec(), vec(), vec()],
        out_specs=pl.BlockSpec((tm, d), lambda i: (i, 0)),
        out_shape=jax.ShapeDtypeStruct((n, d), BF16),
        compiler_params=_cparams(("arbitrary",)),
        name="norm_modulate",
    )(x, g, shift, scale)


def _proj_kernel(scale_ref, hx_ref, w_ref, o_ref):
    o_ref[...] = (_dot(hx_ref[...], w_ref[...]) * scale_ref[pl.program_id(1)]).astype(BF16)


def _proj(hx, w, scales, *, tm):
    n, d = hx.shape
    tiles = w.shape[1] // SEG
    return pl.pallas_call(
        _proj_kernel,
        grid=(n // tm, tiles),
        in_specs=[pl.BlockSpec(memory_space=pltpu.SMEM),
                  pl.BlockSpec((tm, d), lambda i, j: (i, 0)), pl.BlockSpec((d, SEG), lambda i, j: (0, j))],
        out_specs=pl.BlockSpec((tm, SEG), lambda i, j: (i, j)),
        out_shape=jax.ShapeDtypeStruct((n, tiles * SEG), BF16),
        compiler_params=_cparams(("arbitrary", "arbitrary")),
        name="projection",
    )(scales, hx, w)


def _proj_rope_kernel(hx_ref, w_ref, *refs, head_d, scale, n_rot):
    tabs, o_ref = refs[:-1], refs[-1]
    acc = _dot(hx_ref[...], w_ref[...])
    if head_d == RET_D:
        c = tabs[0][...] * scale
        s = tabs[1][...] * scale
        rot = lambda a: a * c + pltpu.roll(a, 64, 1) * s
    else:
        c = tabs[0][...] * scale
        sa = tabs[1][...] * scale
        sb = tabs[2][...] * scale
        rot = lambda a: a * c + pltpu.roll(a, 96, 1) * sa + pltpu.roll(a, 32, 1) * sb
    groups = [acc[:, k * LANES:(k + 1) * LANES] for k in range(SEG // LANES)]
    o_ref[...] = jnp.concatenate([rot(a) if k < n_rot else a for k, a in enumerate(groups)], axis=1).astype(BF16)


def _proj_rope(hx, w, tile, tabs, *, head_d, scale, n_rot, tm, name):
    n, d = hx.shape
    return pl.pallas_call(
        functools.partial(_proj_rope_kernel, head_d=head_d, scale=scale, n_rot=n_rot),
        grid=(n // tm,),
        in_specs=[pl.BlockSpec((tm, d), lambda i: (i, 0)), pl.BlockSpec((d, SEG), lambda i: (0, tile))]
        + [pl.BlockSpec((tm, LANES), lambda i: (i, 0)) for _ in tabs],
        out_specs=pl.BlockSpec((tm, SEG), lambda i: (i, 0)),
        out_shape=jax.ShapeDtypeStruct((n, SEG), BF16),
        compiler_params=_cparams(("arbitrary",)),
        name=name,
    )(hx, w, *tabs)


def _ret_kernel(lg_ref, qf_ref, kf_ref, vf_ref, qb_ref, kb_ref, vb_ref, kc_ref, vc_ref, yf_ref, yb_ref,
                rf_ref, rb_ref, tab_ref, *, zero_init):
    c = pl.program_id(0)
    ch = RET_CHUNK
    dh = RET_D
    states = (rf_ref, rb_ref)

    @pl.when(c == 0)
    def _():
        lg = jax.nn.log_sigmoid(lg_ref[...])
        ii = _iota((ch, ch), 0, F32)
        jj = _iota((ch, ch), 1, F32)
        n_ctx = kc_ref.shape[0]
        mm = _iota((n_ctx, dh), 0, F32)
        for d in range(2):
            for h in range(RET_HEADS):
                row = lg[RET_HEADS * d + h:RET_HEADS * d + h + 1, :]
                l = jnp.broadcast_to(row, (ch, ch))
                if d == 0:
                    diff = ii - jj
                    dmat = jnp.where(diff >= 0.0, jnp.exp(jnp.maximum(diff, 0.0) * l), 0.0)
                    zeta = jnp.exp((ch - 1.0 - ii) * l)
                    xi = jnp.exp((ii + 1.0) * l)
                else:
                    diff = jj - ii
                    dmat = jnp.where(diff >= 1.0, jnp.exp(jnp.maximum(diff, 0.0) * l), 0.0)
                    zeta = jnp.exp(ii * l)
                    xi = jnp.exp((ch - ii) * l)
                tab_ref[d, h, 0] = dmat
                tab_ref[d, h, 1] = zeta
                tab_ref[d, h, 2] = xi
                tab_ref[d, h, 3] = jnp.exp(ch * l)
                if zero_init:
                    states[d][h] = jnp.zeros((dh, dh), F32)
                else:
                    lc = jnp.broadcast_to(row, (n_ctx, dh))
                    w = jnp.exp((n_ctx - 1.0 - mm) * lc) if d == 0 else jnp.exp(mm * lc)
                    sl = slice(h * dh, (h + 1) * dh)
                    kw = (kc_ref[:, sl].astype(F32) * w).astype(BF16)
                    states[d][h] = _dot_tn(kw, vc_ref[:, sl])

    n_sub = qf_ref.shape[0] // ch
    for d, (q_ref, k_ref, v_ref, y_ref) in enumerate(((qf_ref, kf_ref, vf_ref, yf_ref), (qb_ref, kb_ref, vb_ref, yb_ref))):
        order = range(n_sub) if d == 0 else range(n_sub - 1, -1, -1)
        for h in range(RET_HEADS):
            sl = slice(h * dh, (h + 1) * dh)
            r = states[d][h]
            for j in order:
                rows = slice(j * ch, (j + 1) * ch)
                q, k, v = q_ref[rows, sl], k_ref[rows, sl], v_ref[rows, sl]
                a = (_dot_nt(q, k) * tab_ref[d, h, 0]).astype(BF16)
                inner = _dot(a, v)
                cross = _dot(q, r.astype(BF16)) * tab_ref[d, h, 2]
                y_ref[rows, sl] = inner + cross
                kz = (k.astype(F32) * tab_ref[d, h, 1]).astype(BF16)
                r = tab_ref[d, h, 3] * r + _dot_tn(kz, v)
            states[d][h] = r


def _retention(q, k, v, kc, vc, lg_rows, *, zero_init):
    n = q[0].shape[0]
    blk = min(RET_BLOCK, n)
    nc = n // blk
    w = RET_HEADS * RET_D
    fwd = lambda sg: pl.BlockSpec((blk, w), lambda c: (c, sg[1]))
    bwd = lambda sg: pl.BlockSpec((blk, w), lambda c: (nc - 1 - c, sg[1]))
    ctx = lambda sg: pl.BlockSpec((sg[0].shape[0], w), lambda c: (0, sg[1]))
    return pl.pallas_call(
        functools.partial(_ret_kernel, zero_init=zero_init),
        grid=(nc,),
        in_specs=[pl.BlockSpec((SUBLANES, LANES), lambda c: (0, 0)),
                  fwd(q), fwd(k), fwd(v), bwd(q), bwd(k), bwd(v), ctx(kc), ctx(vc)],
        out_specs=[pl.BlockSpec((blk, w), lambda c: (c, 0)), pl.BlockSpec((blk, w), lambda c: (nc - 1 - c, 0))],
        out_shape=[jax.ShapeDtypeStruct((n, w), F32), jax.ShapeDtypeStruct((n, w), F32)],
        scratch_shapes=[pltpu.VMEM((RET_HEADS, RET_D, RET_D), F32), pltpu.VMEM((RET_HEADS, RET_D, RET_D), F32),
                        pltpu.VMEM((2, RET_HEADS, 4, RET_CHUNK, RET_CHUNK), F32)],
        compiler_params=_cparams(("arbitrary",)),
        name="retention_ctx" if zero_init else "retention",
    )(lg_rows, q[0], k[0], v[0], q[0], k[0], v[0], kc[0], vc[0])


def _half_mask(x, half):
    lane = _iota(x.shape, 1)
    keep = (lane < HEAD_D) if half == 0 else (lane >= HEAD_D)
    return jnp.where(keep, x, jnp.zeros_like(x))


def _softmax_pv(s, v, extra=None):
    m = jnp.max(s, axis=1, keepdims=True)
    if extra is not None:
        m = jnp.maximum(m, extra)
    p = jnp.exp2(s - m)
    den = jnp.sum(p, axis=1, keepdims=True)
    if extra is not None:
        den = den + jnp.exp2(extra - m)
    return _dot(p.astype(BF16), v) / den


def _softmax_pv_t(st, v, extra=None):
    m = jnp.max(st, axis=0, keepdims=True)
    if extra is not None:
        m = jnp.maximum(m, extra)
    p = jnp.exp2(st - m)
    den = jnp.sum(p, axis=0, keepdims=True)
    if extra is not None:
        den = den + jnp.exp2(extra - m)
    return _dot_tn(v, p.astype(BF16)) / den


def _win_kernel(sink_ref, q_ref, *refs):
    nkb = WIN_QBLOCKS + 2
    k_refs, v_refs = refs[:nkb], refs[nkb:2 * nkb]
    kx_ref, vx_ref, mask_ref, o_ref = refs[2 * nkb:]
    s = pl.program_id(0)
    last = pl.num_programs(0) - 1
    nq = WIN_QBLOCKS * WIN_BLOCK
    nk = nkb * WIN_BLOCK
    k_all = jnp.concatenate([r[...] for r in k_refs] + [kx_ref[...]], axis=0)
    v_all = jnp.concatenate([r[...] for r in v_refs] + [vx_ref[...]], axis=0)
    n_cols = SEG // LANES
    qs = jnp.concatenate([q_ref[:, c * LANES:(c + 1) * LANES] for c in range(n_cols)], axis=0)
    variant = jnp.where(s == 0, 1, 0) + jnp.where(s == last, 2, 0)
    valid = mask_ref[variant] > 0.5
    valid = jnp.concatenate([valid] * n_cols, axis=1)
    out_t = None
    for g in range(WIN_KV_HEADS):
        st = _dot_nt(_half_mask(k_all, g), qs)
        st = jnp.concatenate([jnp.where(valid, st[:nk], NEG_INF), st[nk:]], axis=0)
        sink = jnp.concatenate([jnp.full((1, nq), sink_ref[n_cols * g + c], F32) for c in range(n_cols)], axis=1)
        o = _softmax_pv_t(st, _half_mask(v_all, g), sink)
        out_t = o if out_t is None else out_t + o
    o = out_t.T
    o_ref[...] = jnp.concatenate([o[c * nq:(c + 1) * nq, :] for c in range(n_cols)], axis=1).astype(BF16)


def _win_valid_table():
    import numpy as np
    nkb = WIN_QBLOCKS + 2
    qpos = np.arange(WIN_QBLOCKS * WIN_BLOCK)[None, :]
    kpos = np.arange(nkb * WIN_BLOCK)[:, None] - WIN_BLOCK
    band = np.abs(kpos - qpos) <= WINDOW
    tabs = []
    for variant in range(4):
        ok = band.copy()
        if variant & 1:
            ok &= kpos >= 0
        if variant & 2:
            ok &= kpos < WIN_QBLOCKS * WIN_BLOCK
        tabs.append(ok)
    return jnp.asarray(np.stack(tabs).astype(np.float32))


def _window_attention(q, kv, kv_ctx, sink):
    n = q[0].shape[0]
    nb = n // WIN_BLOCK
    assert nb % WIN_QBLOCKS == 0
    l = kv_ctx[0].shape[0]
    nkb = WIN_QBLOCKS + 2
    per = SEG // LANES
    blk = lambda off, j: pl.BlockSpec((WIN_BLOCK, LANES),
                                      lambda i: (jnp.clip(WIN_QBLOCKS * i - 1 + j, 0, nb - 1), per * kv[1] + off))
    mask_tab = _win_valid_table()
    return pl.pallas_call(
        _win_kernel,
        grid=(nb // WIN_QBLOCKS,),
        in_specs=[pl.BlockSpec(memory_space=pltpu.SMEM),
                  pl.BlockSpec((WIN_QBLOCKS * WIN_BLOCK, SEG), lambda i: (i, q[1]))]
        + [blk(0, j) for j in range(nkb)] + [blk(1, j) for j in range(nkb)]
        + [pl.BlockSpec((l, LANES), lambda i: (0, per * kv_ctx[1])), pl.BlockSpec((l, LANES), lambda i: (0, per * kv_ctx[1] + 1)),
           pl.BlockSpec(mask_tab.shape, lambda i: (0, 0, 0))],
        out_specs=pl.BlockSpec((WIN_QBLOCKS * WIN_BLOCK, SEG), lambda i: (i, 0)),
        out_shape=jax.ShapeDtypeStruct((n, SEG), BF16),
        compiler_params=_cparams(("arbitrary",)),
        name="window_attention",
    )(sink, q[0], *([kv[0]] * (2 * nkb)), kv_ctx[0], kv_ctx[0], mask_tab)


def _na_slab_start(s, half_rows):
    return jnp.clip((NA_QROWS // 2) * s - NA_ROWS // 4, 0, half_rows - NA_SLAB_ROWS // 2)


def _na_kernel(q_ref, *refs, rows):
    nslab = NA_SLAB_ROWS // 2
    k_refs, v_refs = refs[:nslab], refs[nslab:2 * nslab]
    kx_ref, vx_ref, bias_ref, mask_ref, o_ref = refs[2 * nslab:]
    s = pl.program_id(0)
    last = pl.num_programs(0) - 1
    w = GRID_W
    nq = NA_QROWS * w
    nk = NA_SLAB_ROWS * w
    delta = 2 * _na_slab_start(s, rows // 2) - NA_QROWS * s
    variant = jnp.where(s == 0, 0, jnp.where(s == last, 2, 1))
    valid = mask_ref[variant] > 0.5
    valid = jnp.concatenate([valid, valid], axis=1)
    row = _iota((LANES, nq), 0)
    for pair in range(NA_HEADS // 2):
        sl = slice(pair * LANES, (pair + 1) * LANES)
        q = q_ref[:, sl]
        qs = jnp.concatenate([_half_mask(q, 0), _half_mask(q, 1)], axis=0)
        k_all = jnp.concatenate([r[:, sl] for r in k_refs] + [kx_ref[:, sl]], axis=0)
        v_all = jnp.concatenate([r[:, sl] for r in v_refs] + [vx_ref[:, sl]], axis=0)
        st = _dot_nt(k_all, qs)
        bias = jnp.concatenate(
            [jnp.concatenate([bias_ref[2 * pair + u, delta + 2 * i - a + NA_BIAS_ZERO]
                              for u in range(2) for a in range(0, NA_QROWS, 2)], axis=1) for i in range(nslab)], axis=0)
        s_loc = jnp.where(valid, st[:nk] + bias, NEG_INF)
        o = _softmax_pv_t(jnp.concatenate([s_loc, st[nk:]], axis=0), v_all)
        o_ref[:, sl] = jnp.where(row < HEAD_D, o[:, :nq], o[:, nq:]).T.astype(BF16)


def _na_valid_table():
    import numpy as np
    kk = np.arange(NA_SLAB_ROWS)[:, None, None, None]
    ck = np.arange(GRID_W)[None, :, None, None]
    a = np.arange(NA_QROWS)[None, None, :, None]
    cq = np.arange(GRID_W)[None, None, None, :]
    c_start = np.clip(cq - NA_COLS // 2, 0, GRID_W - NA_COLS)
    col_ok = (ck >= c_start) & (ck < c_start + NA_COLS)
    first_row = (0 * a, a, 0 * a + NA_SLAB_ROWS - NA_ROWS)
    tabs = [(col_ok & (kk >= f) & (kk < f + NA_ROWS)).reshape(NA_SLAB_ROWS * GRID_W, NA_QROWS * GRID_W) for f in first_row]
    return jnp.asarray(np.stack(tabs).astype(np.float32))


def _na_attention(p, p_ctx, bias_tab):
    n = p.shape[0]
    rows = n // GRID_W
    assert rows % NA_QROWS == 0 and rows >= NA_SLAB_ROWS
    steps = rows // NA_QROWS
    l = p_ctx.shape[0]
    slab = lambda t, i: pl.BlockSpec((2 * GRID_W, SEG), lambda s: (_na_slab_start(s, rows // 2) + i, t))
    nslab = NA_SLAB_ROWS // 2
    mask_tab = _na_valid_table()
    return pl.pallas_call(
        functools.partial(_na_kernel, rows=rows),
        grid=(steps,),
        in_specs=[pl.BlockSpec((NA_QROWS * GRID_W, SEG), lambda s: (s, T_QN))]
        + [slab(T_KN, i) for i in range(nslab)] + [slab(T_VN, i) for i in range(nslab)]
        + [pl.BlockSpec((l, SEG), lambda s: (0, T_KN)), pl.BlockSpec((l, SEG), lambda s: (0, T_VN)),
           pl.BlockSpec(bias_tab.shape, lambda s: (0, 0, 0, 0)), pl.BlockSpec(mask_tab.shape, lambda s: (0, 0, 0))],
        out_specs=pl.BlockSpec((NA_QROWS * GRID_W, SEG), lambda s: (s, 0)),
        out_shape=jax.ShapeDtypeStruct((n, SEG), BF16),
        compiler_params=_cparams(("arbitrary",)),
        name="neighbourhood_attention",
    )(p, *([p] * (2 * nslab)), p_ctx, p_ctx, bias_tab, mask_tab)


def _na_bias_table(rpb):
    n_r, n_c = 2 * NA_ROWS - 1, 2 * NA_COLS - 1
    rpb = rpb.astype(F32)
    import numpy as np
    e_i, uk_i, wq_i = np.meshgrid(np.arange(NA_BIAS_PAIRS), np.arange(2), np.arange(2), indexing="ij")
    src = np.clip(e_i - NA_BIAS_ZERO + uk_i - wq_i + NA_ROWS - 1, 0, n_r - 1)
    pick = jnp.asarray((src[..., None] == np.arange(n_r)).astype(np.float32))
    rows = jnp.einsum("euwr,hrj->heuwj", pick, rpb, precision=HIGHEST)
    ck = np.arange(GRID_W)[:, None]
    cq = np.arange(GRID_W)[None, :]
    ci = np.clip(ck - cq, -(NA_COLS - 1), NA_COLS - 1) + NA_COLS - 1
    sel = (ci[None] == np.arange(n_c)[:, None, None]).astype(np.float32)
    sel2 = np.zeros((2, n_c, GRID_W, 2, GRID_W), np.float32)
    for wq in range(2):
        sel2[wq, :, :, wq, :] = sel
    sel2 = jnp.asarray(sel2.reshape(2 * n_c, GRID_W, 2 * GRID_W))
    rows = rows.reshape(rpb.shape[0], NA_BIAS_PAIRS, 2, 2 * n_c)
    tab = jnp.einsum("heuj,jkl->heukl", rows, sel2, precision=HIGHEST)
    return tab.reshape(rpb.shape[0], NA_BIAS_PAIRS, 2 * GRID_W, 2 * GRID_W)


def _ctx_attn_kernel(sink_ref, r_ref, p_ref, ow_ref, on_ref):
    l = p_ref.shape[0]
    n_cols = SEG // LANES
    k_all = r_ref[:, R_KVW * SEG:R_KVW * SEG + LANES]
    v_all = r_ref[:, R_KVW * SEG + LANES:R_KVW * SEG + 2 * LANES]
    qs = jnp.concatenate([r_ref[:, R_QW * SEG + c * LANES:R_QW * SEG + (c + 1) * LANES] for c in range(n_cols)], axis=0)
    outs = []
    for g in range(WIN_KV_HEADS):
        s = _dot_nt(qs, _half_mask(k_all, g))
        sink = jnp.concatenate([jnp.full((l, 1), sink_ref[n_cols * g + c], F32) for c in range(n_cols)], axis=0)
        outs.append(_softmax_pv(s, _half_mask(v_all, g), sink))
    o = outs[0] + outs[1]
    ow_ref[...] = jnp.concatenate([o[c * l:(c + 1) * l, :] for c in range(n_cols)], axis=1).astype(BF16)
    for pair in range(NA_HEADS // 2):
        sl = lambda t: slice(t * SEG + pair * LANES, t * SEG + (pair + 1) * LANES)
        q, k, v = p_ref[:, sl(T_QN)], p_ref[:, sl(T_KN)], p_ref[:, sl(T_VN)]
        out = None
        for u in range(2):
            o = _softmax_pv(_dot_nt(q, _half_mask(k, u)), _half_mask(v, u))
            out = o if out is None else out + o
        on_ref[:, pair * LANES:(pair + 1) * LANES] = out.astype(BF16)


def _ctx_attention(r_ctx, p_ctx, sink):
    l = p_ctx.shape[0]
    return pl.pallas_call(
        _ctx_attn_kernel,
        in_specs=[pl.BlockSpec(memory_space=pltpu.SMEM), pl.BlockSpec(r_ctx.shape, lambda: (0, 0)),
                  pl.BlockSpec(p_ctx.shape, lambda: (0, 0))],
        out_specs=[pl.BlockSpec((l, SEG), lambda: (0, 0)), pl.BlockSpec((l, SEG), lambda: (0, 0))],
        out_shape=[jax.ShapeDtypeStruct((l, SEG), BF16), jax.ShapeDtypeStruct((l, SEG), BF16)],
        compiler_params=pltpu.CompilerParams(vmem_limit_bytes=VMEM_LIMIT),
        name="context_attention",
    )(sink, r_ctx, p_ctx)


def _merge_kernel(yf_ref, yb_ref, gr_ref, gn_ref, yw_ref, yn_ref, ga_ref, gb_ref, gc_ref,
                  wr_ref, ww_ref, wn_ref, wo_ref, x_ref, m2_ref, gf_ref, m3_ref, m4_ref, wrt_ref,
                  xo_ref, hx_ref, lt_ref):
    y = yf_ref[...] + yb_ref[...]
    parts = []
    for h in range(RET_HEADS):
        yh = y[:, h * RET_D:(h + 1) * RET_D]
        mu = jnp.mean(yh, axis=-1, keepdims=True)
        var = jnp.mean(jnp.square(yh - mu), axis=-1, keepdims=True)
        parts.append((yh - mu) * lax.rsqrt(var + EPS))
    g = gr_ref[...].astype(F32)
    ya = jnp.concatenate(parts, axis=1) * gn_ref[...] * (g * jax.nn.sigmoid(g))
    za = _dot(ya.astype(BF16), wr_ref[...])
    zb = _dot(yw_ref[...], ww_ref[...])
    zc = _dot(yn_ref[...], wn_ref[...])
    sig = lambda r: jax.nn.sigmoid(r[...].astype(F32))
    mix = sig(ga_ref) * za + sig(gb_ref) * zb + sig(gc_ref) * zc
    x_new = x_ref[...] + m2_ref[...] * _dot(mix.astype(BF16), wo_ref[...])
    xo_ref[...] = x_new
    h2 = _norm_mod(x_new, gf_ref[...], m3_ref[...], m4_ref[...])
    n_sub = h2.shape[1] // LANES
    for s in range(n_sub):
        hx_ref[pl.ds(s, h2.shape[0], stride=n_sub), :] = h2[:, s * LANES:(s + 1) * LANES]
    lt_ref[...] = _dot_nt(wrt_ref[...], h2, precision=HIGHEST)


def _merge(yf, yb, p, gn, yw, yn, wr, ww, wn, wo, x, m2, gf, m3, m4, wrt, *, tm):
    n, d = x.shape
    row = lambda wdt, t: pl.BlockSpec((tm, wdt), lambda i: (i, t))
    full = lambda a: pl.BlockSpec(a.shape, lambda i: (0,) * a.ndim)
    gate0 = T_GATES * SEG // d
    return pl.pallas_call(
        _merge_kernel,
        grid=(n // tm,),
        in_specs=[row(SEG, 0), row(SEG, 0), row(SEG, T_GR), full(gn), row(SEG, 0), row(SEG, 0),
                  row(d, gate0), row(d, gate0 + 1), row(d, gate0 + 2),
                  full(wr), full(ww), full(wn), full(wo), row(d, 0), full(m2), full(gf), full(m3), full(m4), full(wrt)],
        out_specs=[pl.BlockSpec((tm, d), lambda i: (i, 0)),
                   pl.BlockSpec((tm * (d // LANES), LANES), lambda i: (i, 0)),
                   pl.BlockSpec((N_EXPERTS, tm), lambda i: (0, i))],
        out_shape=[jax.ShapeDtypeStruct((n, d), F32), jax.ShapeDtypeStruct((n * (d // LANES), LANES), F32),
                   jax.ShapeDtypeStruct((N_EXPERTS, n), F32)],
        compiler_params=_cparams(("arbitrary",)),
        name="merge",
    )(yf, yb, p, gn, yw, yn, p, p, p, wr, ww, wn, wo, x, m2, gf, m3, m4, wrt)


def _route_kernel(lt_ref, idx_ref, gate_ref, pos_ref, goff_ref, aff_ref, thr_ref, *, cap, n_groups):
    e = pl.program_id(0)
    n_exp, ag, _ = lt_ref.shape
    capp = idx_ref.shape[-1]

    @pl.when(e == 0)
    def _():
        lt = lt_ref[...]
        ex = jnp.exp(lt - jnp.max(lt, axis=0, keepdims=True))
        aff = ex / jnp.sum(ex, axis=0, keepdims=True)
        real = _iota(aff.shape, 1) < n_groups
        aff = jnp.where(real, aff, 0.0)
        aff_ref[...] = aff
        bits = pltpu.bitcast(aff, I32)

        def body(k, t):
            cand = t | (jnp.int32(1) << (30 - k))
            cnt = jnp.sum(jnp.sum((bits >= cand).astype(F32), axis=2, keepdims=True), axis=1, keepdims=True)
            return jnp.where(cnt >= float(cap), cand, t)

        t = lax.fori_loop(0, 31, body, jnp.zeros((n_exp, 1, 1), I32))
        thr_ref[...] = jnp.broadcast_to(t, thr_ref.shape)

    a = aff_ref[e]
    bits = pltpu.bitcast(a, I32)
    t = thr_ref[e][0:1, :]
    gt = bits > t
    eq = bits == t
    tri_lane_strict = (_iota((LANES, LANES), 0) < _iota((LANES, LANES), 1)).astype(BF16)
    tri_lane_incl = (_iota((LANES, LANES), 0) <= _iota((LANES, LANES), 1)).astype(BF16)
    tri_grp_strict = (_iota((ag, ag), 1) < _iota((ag, ag), 0)).astype(BF16)
    tri_grp_incl = (_iota((ag, ag), 1) <= _iota((ag, ag), 0)).astype(BF16)

    def total(mask_f):
        return jnp.sum(jnp.sum(mask_f, axis=1, keepdims=True), axis=0, keepdims=True)

    def group_sum(mask_f):
        return jnp.broadcast_to(jnp.sum(mask_f, axis=1, keepdims=True), (ag, LANES)).astype(BF16)

    eq_f = eq.astype(F32)
    need = float(cap) - total(gt.astype(F32))
    rank_eq = _dot(tri_grp_strict, group_sum(eq_f)) + _dot(eq_f.astype(BF16), tri_lane_strict)
    sel = gt | (eq & (rank_eq < need))
    sel_f = sel.astype(F32)
    cl = _dot(sel_f.astype(BF16), tri_lane_incl)
    cg = _dot(tri_grp_incl, group_sum(sel_f))
    goff = cg - jnp.broadcast_to(jnp.sum(sel_f, axis=1, keepdims=True), (ag, LANES))
    pos_ref[...] = jnp.where(sel, goff + cl - 1.0, -1.0)
    diag = _iota((ag, ag), 0) == _iota((ag, ag), 1)
    goff_sq = goff if ag == LANES else goff[:, :ag]
    goff_ref[...] = jnp.sum(jnp.where(diag, goff_sq, 0.0), axis=0, keepdims=True).astype(I32)

    pp = _iota((ag, capp), 1, F32)
    cg_b = jnp.broadcast_to(cg[:, 0:1], (ag, capp))
    below = cg_b <= pp
    grp = jnp.sum(below.astype(F32), axis=0, keepdims=True)
    off = jnp.max(jnp.where(below, cg_b, 0.0), axis=0, keepdims=True)
    onehot = _iota((ag, capp), 0, F32) == grp
    in_grp = _dot_tn(cl.astype(BF16), onehot.astype(BF16))
    local = pp[0:1, :] - off
    lane_of = jnp.sum((in_grp <= local).astype(F32), axis=0, keepdims=True)
    live = pp[0:1, :] < float(cap)
    idx = jnp.where(live, grp * float(LANES) + lane_of, 0.0)
    idx_ref[...] = idx.astype(I32)
    aff_grp = _dot_tn(a, onehot.astype(F32), precision=HIGHEST)
    pick = _iota((LANES, capp), 0, F32) == lane_of
    gate = jnp.sum(jnp.where(pick, aff_grp, 0.0), axis=0, keepdims=True)
    gate_ref[...] = jnp.where(live, gate, 0.0)


def _route(logits_t, *, cap, n_groups):
    n_exp, ag, _ = logits_t.shape
    capp = -(-cap // LANES) * LANES
    return pl.pallas_call(
        functools.partial(_route_kernel, cap=cap, n_groups=n_groups),
        grid=(n_exp,),
        in_specs=[pl.BlockSpec(logits_t.shape, lambda e: (0, 0, 0))],
        out_specs=[pl.BlockSpec((None, 1, capp), lambda e: (e, 0, 0)),
                   pl.BlockSpec((None, 1, capp), lambda e: (e, 0, 0)),
                   pl.BlockSpec((None, ag, LANES), lambda e: (e, 0, 0)),
                   pl.BlockSpec((None, 1, ag), lambda e: (e, 0, 0))],
        out_shape=[jax.ShapeDtypeStruct((n_exp, 1, capp), I32), jax.ShapeDtypeStruct((n_exp, 1, capp), F32),
                   jax.ShapeDtypeStruct((n_exp, ag, LANES), F32), jax.ShapeDtypeStruct((n_exp, 1, ag), I32)],
        scratch_shapes=[pltpu.VMEM((n_exp, ag, LANES), F32), pltpu.VMEM((n_exp, SUBLANES, LANES), I32)],
        compiler_params=_cparams(("arbitrary",)),
        name="route",
    )(logits_t)


def _ffn_kernel(idx_ref, idxn_ref, gate_ref, hx_ref, wg_ref, wu_ref, wd_ref, o_ref, xbuf, sem, wgb, wub, wdb, *, tm):
    e = pl.program_id(0)
    j = pl.program_id(1)
    nt = pl.num_programs(1)
    step = e * nt + j
    last = pl.num_programs(0) * nt - 1

    sub = SUBLANES
    rows = tm * sub

    def row_copy(rows_ref, r, slot):
        src = hx_ref.at[pl.ds(pl.multiple_of(rows_ref[0, 0, r] * sub, sub), sub)]
        dst = xbuf.at[pl.ds(pl.multiple_of(slot * rows + r * sub, sub), sub)]
        return pltpu.make_async_copy(src, dst, sem.at[slot])

    def slot_copy(slot):
        return pltpu.make_async_copy(hx_ref.at[pl.ds(0, rows)], xbuf.at[pl.ds(pl.multiple_of(slot * rows, sub), rows)],
                                     sem.at[slot])

    @pl.when(step == 0)
    def _():
        def body(r, carry):
            row_copy(idx_ref, r, 0).start()
            return carry
        lax.fori_loop(0, tm, body, 0, unroll=8)

    @pl.when(j == 0)
    def _():
        wgb[...] = wg_ref[...].astype(BF16)
        wub[...] = wu_ref[...].astype(BF16)
        wdb[...] = wd_ref[...].astype(BF16)

    slot = step % 2
    nslot = 1 - slot
    for r in range(tm):
        row_copy(idxn_ref, r, nslot).start()
    slot_copy(slot).wait()
    base = slot * rows
    x = jnp.concatenate([xbuf[pl.ds(base + s, tm, stride=sub), :] for s in range(sub)], axis=1).astype(BF16)
    g = _dot(x, wgb[...])
    u = _dot(x, wub[...])
    hid = (g * jax.nn.sigmoid(g)) * u
    y = _dot(hid.astype(BF16), wdb[...])
    eye = _iota((tm, tm), 0) == _iota((tm, tm), 1)
    gcol = jnp.sum(jnp.where(eye, jnp.broadcast_to(gate_ref[0], (tm, tm)), 0.0), axis=1, keepdims=True)
    o_ref[...] = (y * gcol).astype(BF16)

    @pl.when(step == last)
    def _():
        slot_copy(nslot).wait()


def _expert_ffn(idx, gate, hx3, w_gate, w_up, w_down, *, layer, cap, tm):
    n_exp = idx.shape[0]
    d = w_gate.shape[2]
    f = w_gate.shape[3]
    assert d == SUBLANES * LANES, "a token row must be exactly one (8, 128) f32 tile"
    nt = cap // tm
    idx_t = idx[:, :, :cap].reshape(n_exp * nt, 1, tm)
    gate_t = gate[:, :, :cap].reshape(n_exp * nt, 1, tm)
    n_steps = n_exp * nt
    wspec = lambda a, b: pl.BlockSpec((None, None, a, b), lambda e, j: (layer, e, 0, 0))
    return pl.pallas_call(
        functools.partial(_ffn_kernel, tm=tm),
        grid=(n_exp, nt),
        in_specs=[pl.BlockSpec((1, 1, tm), lambda e, j: (e * nt + j, 0, 0), memory_space=pltpu.SMEM),
                  pl.BlockSpec((1, 1, tm), lambda e, j: (jnp.minimum(e * nt + j + 1, n_steps - 1), 0, 0), memory_space=pltpu.SMEM),
                  pl.BlockSpec((1, 1, tm), lambda e, j: (e * nt + j, 0, 0)),
                  pl.BlockSpec(memory_space=pl.ANY),
                  wspec(d, f), wspec(d, f), wspec(f, d)],
        out_specs=pl.BlockSpec((tm, d), lambda e, j: (e * nt + j, 0)),
        out_shape=jax.ShapeDtypeStruct((n_exp * cap, d), BF16),
        scratch_shapes=[pltpu.VMEM((2 * tm * SUBLANES, LANES), F32), pltpu.SemaphoreType.DMA((2,)),
                        pltpu.VMEM((d, f), BF16), pltpu.VMEM((d, f), BF16), pltpu.VMEM((f, d), BF16)],
        compiler_params=_cparams(("arbitrary", "arbitrary")),
        name="expert_ffn",
    )(idx_t, idx_t, gate_t, hx3, w_gate, w_up, w_down)


def _combine_kernel(ws_ref, nw_ref, *refs, n_exp, cap, win, gpt, final_norm):
    y_refs = refs[:n_exp]
    pos_ref, ye_ref, x_ref, m5_ref, gf_ref, o_ref, acc_ref, xwin, sem = refs[n_exp:]
    a = pl.program_id(0)
    ng = pl.num_programs(0)
    per = max(1, MXU_DEPTH // win)

    def slots(e):
        pos = pos_ref[e, pl.ds(a * gpt, gpt), :]
        return jnp.concatenate([jnp.broadcast_to(pos[j:j + 1, :], (win, LANES)) for j in range(gpt)], axis=1)

    lp = _iota((win, gpt * LANES), 0, F32)
    acc = None
    for e0 in range(0, n_exp, per):
        es = range(e0, min(e0 + per, n_exp))
        hot = jnp.concatenate([((slots(e) - (ws_ref[e * ng + a] - e * cap).astype(F32)) == lp).astype(BF16) for e in es],
                              axis=0)
        rows = jnp.concatenate([y_refs[e][...] for e in es], axis=0)
        part = _dot_tn(hot, rows)
        acc = part if acc is None else acc + part
    acc_ref[...] = acc

    for e in range(n_exp):
        first = ws_ref[e * ng + a] - e * cap

        def extra(k, carry, e=e, first=first):
            lo = first + k * win
            row = jnp.minimum(lo, cap - win)
            cp = pltpu.make_async_copy(ye_ref.at[pl.ds(pl.multiple_of(e * cap + row, BF16_ROWS), win)], xwin, sem)
            cp.start()
            cp.wait()
            pos = slots(e)
            hot = ((pos - row.astype(F32)) == lp) & (pos >= lo.astype(F32))
            acc_ref[...] += _dot_tn(hot.astype(BF16), xwin[...])
            return carry

        lax.fori_loop(1, nw_ref[e * ng + a], extra, 0)

    x_new = x_ref[...] + m5_ref[...] * acc_ref[...]
    if final_norm:
        ms = jnp.mean(x_new * x_new, axis=-1, keepdims=True)
        x_new = x_new * lax.rsqrt(ms + EPS) * gf_ref[...]
    o_ref[...] = x_new


def _combine(ye, posmap, goff, x, m5, g_final, *, cap, final_norm):
    n, d = x.shape
    n_exp = posmap.shape[0]
    gpt = min(COMBINE_GROUPS, n // LANES)
    ng = n // (gpt * LANES)
    tok = gpt * LANES
    win = min(COMBINE_WINDOW, cap)
    goff = goff[:, 0, :ng * gpt:gpt]
    end = jnp.concatenate([goff[:, 1:], jnp.full((n_exp, 1), cap, I32)], axis=1)
    start = jnp.minimum((goff // BF16_ROWS) * BF16_ROWS, cap - win)
    n_win = jnp.maximum((end - start + win - 1) // win, 1).reshape(-1)
    wstart = (start + jnp.arange(n_exp, dtype=I32)[:, None] * cap).reshape(-1)
    yspec = lambda e: pl.BlockSpec((pl.Element(win), pl.Element(d)),
                                   lambda a, ws, nw: (pl.multiple_of(ws[e * ng + a], BF16_ROWS), 0))
    gs = pltpu.PrefetchScalarGridSpec(
        num_scalar_prefetch=2,
        grid=(ng,),
        in_specs=[yspec(e) for e in range(n_exp)]
        + [pl.BlockSpec(posmap.shape, lambda a, ws, nw: (0, 0, 0)),
           pl.BlockSpec(memory_space=pl.ANY),
           pl.BlockSpec((tok, d), lambda a, ws, nw: (a, 0)),
           pl.BlockSpec((1, d), lambda a, ws, nw: (0, 0)), pl.BlockSpec((1, d), lambda a, ws, nw: (0, 0))],
        out_specs=pl.BlockSpec((tok, d), lambda a, ws, nw: (a, 0)),
        scratch_shapes=[pltpu.VMEM((tok, d), F32), pltpu.VMEM((win, d), BF16), pltpu.SemaphoreType.DMA(())],
    )
    return pl.pallas_call(
        functools.partial(_combine_kernel, n_exp=n_exp, cap=cap, win=win, gpt=gpt, final_norm=final_norm),
        grid_spec=gs,
        out_shape=jax.ShapeDtypeStruct((n, d), F32),
        compiler_params=_cparams(("arbitrary",)),
        name="combine_final" if final_norm else "combine",
    )(wstart, n_win, *([ye] * n_exp), posmap, ye, x, m5, g_final)


def _rope_tables(n):
    n_rows = n // GRID_W
    row = jnp.arange(n_rows).astype(F32)
    col = jnp.arange(GRID_W).astype(F32)

    def tables(d, signs):
        nf = d // 4
        inv = ROPE_BASE ** (-jnp.arange(nf, dtype=F32) / nf)
        ang_r, ang_c = row[:, None] * inv, col[:, None] * inv
        reps = LANES // (2 * nf)

        def lanes(f_r, f_c, factors):
            z_r, z_c = jnp.zeros_like(f_r), jnp.zeros_like(f_c)
            pat_r = jnp.concatenate([part for k in range(reps) for part in (f_r * factors[2 * k], z_r)], axis=1)
            pat_c = jnp.concatenate([part for k in range(reps) for part in (z_c, f_c * factors[2 * k + 1])], axis=1)
            return (pat_r[:, None, :] + pat_c[None, :, :]).reshape(n, LANES)

        ones = (1.0,) * (2 * reps)
        cos = lanes(jnp.cos(ang_r), jnp.cos(ang_c), ones)
        return [cos] + [lanes(jnp.sin(ang_r), jnp.sin(ang_c), sg) for sg in signs]

    c128, s128 = tables(RET_D, [(-1.0, -1.0, 1.0, 1.0)])
    c64, sa, sb = tables(HEAD_D, [(-1.0, -1.0, 0.0, 0.0) * 2, (0.0, 0.0, 1.0, 1.0) * 2])
    return c128, s128, c64, sa, sb


WIN_HEAD_ORDER = (0, 4, 1, 5, 2, 6, 3, 7)


def _permute_in_weight(w):
    d = w.shape[0]
    sizes = (("q_r", 512), ("k_r", 512), ("v_r", 512), ("g_r", 512), ("q_w", 512), ("k_w", 128), ("v_w", 128),
             ("q_n", 512), ("k_n", 512), ("v_n", 512), ("gates", 3 * d))
    off, lay = 0, {}
    for name, size in sizes:
        lay[name] = (off, off + size)
        off += size
    seg = lambda name: w[:, lay[name][0]:lay[name][1]]
    q0 = lay["q_w"][0]
    q_w = [w[:, q0 + h * HEAD_D:q0 + (h + 1) * HEAD_D] for h in WIN_HEAD_ORDER]
    pad = jnp.zeros((d, SEG - 2 * LANES), w.dtype)
    rope = jnp.concatenate([seg("q_r"), seg("k_r")] + q_w + [seg("k_w"), seg("v_w"), pad], axis=1).astype(BF16)
    plain = jnp.concatenate([seg("gates"), seg("v_r"), seg("g_r"), seg("q_n"), seg("k_n"), seg("v_n")], axis=1).astype(BF16)
    return rope, plain


def _permute_win_rows(w):
    return jnp.concatenate([w[h * HEAD_D:(h + 1) * HEAD_D] for h in WIN_HEAD_ORDER], axis=0).astype(BF16)


def kernel(x, c, ctx, c_ctx, w_mod, b_mod, g_mix, g_ffn, w_in, ret_decay_logit, ret_gn, w_ret, win_sink, w_win, na_rpb,
           w_na, w_out, w_router, w_exp_gate, w_exp_up, w_exp_down, g_final):
    _, n, d = x.shape
    l = ctx.shape[1]
    depth = w_in.shape[0]
    xs, cs = x[0], ctx[0]
    cc = jnp.zeros((SUBLANES, d), F32).at[0].set(c[0]).at[1].set(c_ctx)
    mods = _modulation(cc, w_mod, b_mod)
    c128, s128, c64, sa64, sb64 = _rope_tables(n)
    k_scale = RET_D ** -0.5
    plain_scales = jnp.ones((PLAIN_TILES,), F32).at[T_QN].set(Q_SCALE)
    ctx_rope_scales = jnp.ones((R_KVW + 1,), F32).at[R_KR].set(k_scale).at[R_QW].set(Q_SCALE)
    cap_x = CAPACITY_FACTOR * n // N_EXPERTS
    cap_c = CAPACITY_FACTOR * l // N_EXPERTS
    grp_c = -(-(l // LANES) // BF16_ROWS) * BF16_ROWS
    vec = lambda v: v.reshape(1, -1)

    for layer in range(depth):
        need_ctx = layer < depth - 1
        last = layer == depth - 1
        mx = [vec(mods[layer, 0, k * d:(k + 1) * d]) for k in range(N_MOD)]
        mc = [vec(mods[layer, 1, k * d:(k + 1) * d]) for k in range(N_MOD)]
        w_rope, w_plain = _permute_in_weight(w_in[layer])
        wr = w_ret[layer].astype(BF16)
        ww = _permute_win_rows(w_win[layer])
        wn = w_na[layer].astype(BF16)
        wo = w_out[layer].astype(BF16)
        wrt = w_router[layer].T
        sink = win_sink[layer].astype(F32) * LOG2E
        lg_rows = jnp.broadcast_to(ret_decay_logit[layer].astype(F32).reshape(2 * RET_HEADS, 1), (2 * RET_HEADS, LANES))
        gn = vec(ret_gn[layer])
        bias_tab = _na_bias_table(na_rpb[layer].astype(F32) * LOG2E)
        gmix, gffn = vec(g_mix[layer]), vec(g_ffn[layer])

        hx = _norm_call(xs, gmix, mx[0], mx[1], tm=1024)
        hc = _norm_call(cs, gmix, mc[0], mc[1], tm=l)
        p_x = _proj(hx, w_plain, plain_scales, tm=2048)
        p_c = _proj(hc, w_plain, plain_scales, tm=l)
        r_c = _proj(hc, w_rope, ctx_rope_scales, tm=l)
        rope = functools.partial(_proj_rope, hx, w_rope, tm=2048)
        q_r = rope(R_QR, (c128, s128), head_d=RET_D, scale=1.0, n_rot=SEG // LANES, name="proj_q_ret")
        k_r = rope(R_KR, (c128, s128), head_d=RET_D, scale=k_scale, n_rot=SEG // LANES, name="proj_k_ret")
        q_w = rope(R_QW, (c64, sa64, sb64), head_d=HEAD_D, scale=Q_SCALE, n_rot=SEG // LANES, name="proj_q_win")
        kv_w = rope(R_KVW, (c64, sa64, sb64), head_d=HEAD_D, scale=1.0, n_rot=1, name="proj_kv_win")

        yf, yb = _retention((q_r, 0), (k_r, 0), (p_x, T_VR), (r_c, R_KR), (p_c, T_VR), lg_rows, zero_init=False)
        yw = _window_attention((q_w, 0), (kv_w, 0), (r_c, R_KVW), sink)
        yn = _na_attention(p_x, p_c, bias_tab)
        xs, hx3, lt = _merge(yf, yb, p_x, gn, yw, yn, wr, ww, wn, wo, xs, mx[2], gffn, mx[3], mx[4], wrt, tm=512)

        idx, gate, posmap, goff = _route(lt.reshape(N_EXPERTS, n // LANES, LANES), cap=cap_x, n_groups=n // LANES)
        ye = _expert_ffn(idx, gate, hx3, w_exp_gate, w_exp_up, w_exp_down, layer=layer, cap=cap_x, tm=512)
        xs = _combine(ye, posmap, goff, xs, mx[5], vec(g_final), cap=cap_x, final_norm=last)

        if need_ctx:
            yfc, ybc = _retention((r_c, R_QR), (r_c, R_KR), (p_c, T_VR), (r_c, R_KR), (p_c, T_VR), lg_rows, zero_init=True)
            ywc, ync = _ctx_attention(r_c, p_c, sink)
            cs, hc3, ltc = _merge(yfc, ybc, p_c, gn, ywc, ync, wr, ww, wn, wo, cs, mc[2], gffn, mc[3], mc[4], wrt, tm=l)
            ltc = jnp.pad(ltc.reshape(N_EXPERTS, l // LANES, LANES), ((0, 0), (0, grp_c - l // LANES), (0, 0)))
            idc, gtc, posc, goffc = _route(ltc, cap=cap_c, n_groups=l // LANES)
            yec = _expert_ffn(idc, gtc, hc3, w_exp_gate, w_exp_up, w_exp_down, layer=layer, cap=cap_c, tm=cap_c)
            cs = _combine(yec, posc, goffc, cs, mc[5], vec(g_final), cap=cap_c, final_norm=False)

    return xs[None]
```

```python
import functools

import jax
import jax.numpy as jnp
from jax import lax
from jax.experimental import pallas as pl
from jax.experimental.pallas import tpu as pltpu

F32, BF16, I32 = jnp.float32, jnp.bfloat16, jnp.int32
HIGHEST = lax.Precision.HIGHEST

GRID_W = 64
RET_HEADS, RET_D, RET_CHUNK = 4, 128, 128
WIN_HEADS, WIN_KV_HEADS, HEAD_D, WINDOW, WIN_BLOCK = 8, 2, 64, 128, 128
NA_HEADS, NA_ROWS, NA_COLS = 8, 8, 16
N_EXPERTS, CAPACITY_FACTOR = 16, 2
N_MOD = 6
ROPE_BASE = 10000.0
EPS = 1e-6
NEG_INF = -1e30

LANES = 128
SUBLANES = 8
BF16_ROWS = 16
VMEM_LIMIT = 56 * 1024 * 1024

SEG = 512
T_GATES, T_VR, T_GR, T_QN, T_KN, T_VN = 0, 6, 7, 8, 9, 10
PLAIN_TILES = 11
R_QR, R_KR, R_QW, R_KVW = 0, 1, 2, 3
LOG2E = 1.4426950408889634
Q_SCALE = HEAD_D ** -0.5 * LOG2E
RET_BLOCK = 4 * RET_CHUNK
WIN_QBLOCKS = 2
NA_QROWS = 4
NA_SLAB_ROWS = 12
NA_BIAS_ZERO = NA_SLAB_ROWS - 2
NA_BIAS_PAIRS = NA_BIAS_ZERO + NA_SLAB_ROWS - 1
COMBINE_GROUPS = 4
COMBINE_WINDOW = 128
MXU_DEPTH = 256
BISECT_STEPS = 160


def _cparams(sem):
    return pltpu.CompilerParams(dimension_semantics=sem, vmem_limit_bytes=VMEM_LIMIT)


def _dot(a, b):
    return jnp.dot(a, b, preferred_element_type=F32)


def _dot_nt(a, b, precision=None):
    return lax.dot_general(a, b, (((1,), (1,)), ((), ())), precision=precision, preferred_element_type=F32)


def _dot_tn(a, b, precision=None):
    return lax.dot_general(a, b, (((0,), (0,)), ((), ())), precision=precision, preferred_element_type=F32)


def _iota(shape, dim, dtype=I32):
    return lax.broadcasted_iota(I32, shape, dim).astype(dtype)


def _mod_kernel(s_ref, w_ref, b_ref, o_ref):
    s = s_ref[...]
    s = s * jax.nn.sigmoid(s)
    o_ref[...] = jnp.dot(s, w_ref[...], precision=HIGHEST, preferred_element_type=F32) + b_ref[...]


def _modulation(cc, w_mod, b_mod):
    depth, d, md = w_mod.shape
    tn = 1536
    return pl.pallas_call(
        _mod_kernel,
        grid=(depth, md // tn),
        in_specs=[
            pl.BlockSpec((SUBLANES, d), lambda l, j: (0, 0)),
            pl.BlockSpec((None, d, tn), lambda l, j: (l, 0, j)),
            pl.BlockSpec((None, 1, tn), lambda l, j: (l, 0, j)),
        ],
        out_specs=pl.BlockSpec((None, SUBLANES, tn), lambda l, j: (l, 0, j)),
        out_shape=jax.ShapeDtypeStruct((depth, SUBLANES, md), F32),
        compiler_params=_cparams(("arbitrary", "arbitrary")),
        name="modulation",
    )(cc, w_mod, b_mod.reshape(depth, 1, md))


def _norm_mod(x, g, shift, scale):
    ms = jnp.mean(x * x, axis=-1, keepdims=True)
    y = x * lax.rsqrt(ms + EPS) * g
    return y * (1.0 + scale) + shift


def _norm_kernel(x_ref, g_ref, sh_ref, sc_ref, o_ref):
    o_ref[...] = _norm_mod(x_ref[...], g_ref[...], sh_ref[...], sc_ref[...]).astype(BF16)


def _norm_call(x, g, shift, scale, *, tm):
    n, d = x.shape
    vec = lambda: pl.BlockSpec((1, d), lambda i: (0, 0))
    return pl.pallas_call(
        _norm_kernel,
        grid=(n // tm,),
        in_specs=[pl.BlockSpec((tm, d), lambda i: (i, 0)), vec(), vec(), vec()],
        out_specs=pl.BlockSpec((tm, d), lambda i: (i, 0)),
        out_shape=jax.ShapeDtypeStruct((n, d), BF16),
        compiler_params=_cparams(("arbitrary",)),
        name="norm_modulate",
    )(x, g, shift, scale)


def _proj_kernel(scale_ref, hx_ref, w_ref, o_ref):
    o_ref[...] = (_dot(hx_ref[...], w_ref[...]) * scale_ref[pl.program_id(1)]).astype(BF16)


def _proj(hx, w, scales, *, tm):
    n, d = hx.shape
    tiles = w.shape[1] // SEG
    return pl.pallas_call(
        _proj_kernel,
        grid=(n // tm, tiles),
        in_specs=[pl.BlockSpec(memory_space=pltpu.SMEM),
                  pl.BlockSpec((tm, d), lambda i, j: (i, 0)), pl.BlockSpec((d, SEG), lambda i, j: (0, j))],
        out_specs=pl.BlockSpec((tm, SEG), lambda i, j: (i, j)),
        out_shape=jax.ShapeDtypeStruct((n, tiles * SEG), BF16),
        compiler_params=_cparams(("arbitrary", "arbitrary")),
        name="projection",
    )(scales, hx, w)


def _proj_rope_kernel(hx_ref, w_ref, *refs, head_d, scale, n_rot):
    tabs, o_ref = refs[:-1], refs[-1]
    acc = _dot(hx_ref[...], w_ref[...])
    if head_d == RET_D:
        c = tabs[0][...] * scale
        s = tabs[1][...] * scale
        rot = lambda a: a * c + pltpu.roll(a, 64, 1) * s
    else:
        c = tabs[0][...] * scale
        sa = tabs[1][...] * scale
        sb = tabs[2][...] * scale
        rot = lambda a: a * c + pltpu.roll(a, 96, 1) * sa + pltpu.roll(a, 32, 1) * sb
    groups = [acc[:, k * LANES:(k + 1) * LANES] for k in range(SEG // LANES)]
    o_ref[...] = jnp.concatenate([rot(a) if k < n_rot else a for k, a in enumerate(groups)], axis=1).astype(BF16)


def _proj_rope(hx, w, tile, tabs, *, head_d, scale, n_rot, tm, name):
    n, d = hx.shape
    return pl.pallas_call(
        functools.partial(_proj_rope_kernel, head_d=head_d, scale=scale, n_rot=n_rot),
        grid=(n // tm,),
        in_specs=[pl.BlockSpec((tm, d), lambda i: (i, 0)), pl.BlockSpec((d, SEG), lambda i: (0, tile))]
        + [pl.BlockSpec((tm, LANES), lambda i: (i, 0)) for _ in tabs],
        out_specs=pl.BlockSpec((tm, SEG), lambda i: (i, 0)),
        out_shape=jax.ShapeDtypeStruct((n, SEG), BF16),
        compiler_params=_cparams(("arbitrary",)),
        name=name,
    )(hx, w, *tabs)


def _ret_kernel(lg_ref, qf_ref, kf_ref, vf_ref, qb_ref, kb_ref, vb_ref, kc_ref, vc_ref, yf_ref, yb_ref,
                rf_ref, rb_ref, dmat_ref, vec_ref, *, zero_init):
    c = pl.program_id(0)
    ch = qf_ref.shape[0]
    dh = RET_D
    states = (rf_ref, rb_ref)

    @pl.when(c == 0)
    def _():
        lg = jax.nn.log_sigmoid(lg_ref[...])
        ii = _iota((ch, ch), 0, F32)
        jj = _iota((ch, ch), 1, F32)
        iv = _iota((ch, dh), 0, F32)
        n_ctx = kc_ref.shape[0]
        mm = _iota((n_ctx, dh), 0, F32)
        for d in range(2):
            for h in range(RET_HEADS):
                row = lg[RET_HEADS * d + h:RET_HEADS * d + h + 1, :]
                l = jnp.concatenate([jnp.broadcast_to(row, (ch, dh))] * (ch // dh), axis=1)
                lv = jnp.broadcast_to(row, (ch, dh))
                if d == 0:
                    diff = ii - jj
                    dmat_ref[d, h] = jnp.where(diff >= 0.0, jnp.exp(jnp.maximum(diff, 0.0) * l), 0.0)
                    vec_ref[d, h, 0] = jnp.exp((ch - 1.0 - iv) * lv)
                    vec_ref[d, h, 1] = jnp.exp((iv + 1.0) * lv)
                else:
                    diff = jj - ii
                    dmat_ref[d, h] = jnp.where(diff >= 1.0, jnp.exp(jnp.maximum(diff, 0.0) * l), 0.0)
                    vec_ref[d, h, 0] = jnp.exp(iv * lv)
                    vec_ref[d, h, 1] = jnp.exp((ch - iv) * lv)
                vec_ref[d, h, 2] = jnp.exp(float(ch) * lv)
                if zero_init:
                    states[d][h] = jnp.zeros((dh, dh), F32)
                else:
                    lc = jnp.broadcast_to(row, (n_ctx, dh))
                    w = jnp.exp((n_ctx - 1.0 - mm) * lc) if d == 0 else jnp.exp(mm * lc)
                    sl = slice(h * dh, (h + 1) * dh)
                    kw = (kc_ref[:, sl].astype(F32) * w).astype(BF16)
                    states[d][h] = _dot_tn(kw, vc_ref[:, sl])

    for d, (q_ref, k_ref, v_ref, y_ref) in enumerate(((qf_ref, kf_ref, vf_ref, yf_ref), (qb_ref, kb_ref, vb_ref, yb_ref))):
        for h in range(RET_HEADS):
            sl = slice(h * dh, (h + 1) * dh)
            q, k, v = q_ref[:, sl], k_ref[:, sl], v_ref[:, sl]
            r = states[d][h]
            a = (_dot_nt(q, k) * dmat_ref[d, h]).astype(BF16)
            inner = _dot(a, v)
            cross = _dot(q, r.astype(BF16)) * vec_ref[d, h, 1]
            y_ref[:, sl] = inner + cross
            kz = (k.astype(F32) * vec_ref[d, h, 0]).astype(BF16)
            states[d][h] = vec_ref[d, h, 2][:dh, :] * r + _dot_tn(kz, v)


def _retention(q, k, v, kc, vc, lg_rows, *, zero_init):
    n = q[0].shape[0]
    blk = min(RET_BLOCK, n)
    nc = n // blk
    w = RET_HEADS * RET_D
    fwd = lambda sg: pl.BlockSpec((blk, w), lambda c: (c, sg[1]))
    bwd = lambda sg: pl.BlockSpec((blk, w), lambda c: (nc - 1 - c, sg[1]))
    ctx = lambda sg: pl.BlockSpec((sg[0].shape[0], w), lambda c: (0, sg[1]))
    return pl.pallas_call(
        functools.partial(_ret_kernel, zero_init=zero_init),
        grid=(nc,),
        in_specs=[pl.BlockSpec((SUBLANES, LANES), lambda c: (0, 0)),
                  fwd(q), fwd(k), fwd(v), bwd(q), bwd(k), bwd(v), ctx(kc), ctx(vc)],
        out_specs=[pl.BlockSpec((blk, w), lambda c: (c, 0)), pl.BlockSpec((blk, w), lambda c: (nc - 1 - c, 0))],
        out_shape=[jax.ShapeDtypeStruct((n, w), F32), jax.ShapeDtypeStruct((n, w), F32)],
        scratch_shapes=[pltpu.VMEM((RET_HEADS, RET_D, RET_D), F32), pltpu.VMEM((RET_HEADS, RET_D, RET_D), F32),
                        pltpu.VMEM((2, RET_HEADS, blk, blk), F32), pltpu.VMEM((2, RET_HEADS, 3, blk, RET_D), F32)],
        compiler_params=_cparams(("arbitrary",)),
        name="retention_ctx" if zero_init else "retention",
    )(lg_rows, q[0], k[0], v[0], q[0], k[0], v[0], kc[0], vc[0])


def _half_mask(x, half):
    lane = _iota(x.shape, 1)
    keep = (lane < HEAD_D) if half == 0 else (lane >= HEAD_D)
    return jnp.where(keep, x, jnp.zeros_like(x))


def _softmax_pv(s, v, extra=None):
    m = jnp.max(s, axis=1, keepdims=True)
    if extra is not None:
        m = jnp.maximum(m, extra)
    p = jnp.exp2(s - m)
    den = jnp.sum(p, axis=1, keepdims=True)
    if extra is not None:
        den = den + jnp.exp2(extra - m)
    return _dot(p.astype(BF16), v) / den


def _softmax_pv_t(st, v, extra=None):
    m = jnp.max(st, axis=0, keepdims=True)
    if extra is not None:
        m = jnp.maximum(m, extra)
    p = jnp.exp2(st - m)
    den = jnp.sum(p, axis=0, keepdims=True)
    if extra is not None:
        den = den + jnp.exp2(extra - m)
    return _dot_tn(v, p.astype(BF16)) / den


def _win_kernel(sink_ref, q_ref, *refs):
    nkb = WIN_QBLOCKS + 2
    k_refs, v_refs = refs[:nkb], refs[nkb:2 * nkb]
    kx_ref, vx_ref, mask_ref, o_ref = refs[2 * nkb:]
    s = pl.program_id(0)
    last = pl.num_programs(0) - 1
    nq = WIN_QBLOCKS * WIN_BLOCK
    nk = nkb * WIN_BLOCK
    k_all = jnp.concatenate([r[...] for r in k_refs] + [kx_ref[...]], axis=0)
    v_all = jnp.concatenate([r[...] for r in v_refs] + [vx_ref[...]], axis=0)
    n_cols = SEG // LANES
    qs = jnp.concatenate([q_ref[:, c * LANES:(c + 1) * LANES] for c in range(n_cols)], axis=0)
    variant = jnp.where(s == 0, 1, 0) + jnp.where(s == last, 2, 0)
    valid = mask_ref[variant] > 0.5
    valid = jnp.concatenate([valid] * n_cols, axis=1)
    out_t = None
    for g in range(WIN_KV_HEADS):
        st = _dot_nt(_half_mask(k_all, g), qs)
        st = jnp.concatenate([jnp.where(valid, st[:nk], NEG_INF), st[nk:]], axis=0)
        sink = jnp.concatenate([jnp.full((1, nq), sink_ref[n_cols * g + c], F32) for c in range(n_cols)], axis=1)
        o = _softmax_pv_t(st, _half_mask(v_all, g), sink)
        out_t = o if out_t is None else out_t + o
    o = out_t.T
    o_ref[...] = jnp.concatenate([o[c * nq:(c + 1) * nq, :] for c in range(n_cols)], axis=1).astype(BF16)


def _win_valid_table():
    import numpy as np
    nkb = WIN_QBLOCKS + 2
    qpos = np.arange(WIN_QBLOCKS * WIN_BLOCK)[None, :]
    kpos = np.arange(nkb * WIN_BLOCK)[:, None] - WIN_BLOCK
    band = np.abs(kpos - qpos) <= WINDOW
    tabs = []
    for variant in range(4):
        ok = band.copy()
        if variant & 1:
            ok &= kpos >= 0
        if variant & 2:
            ok &= kpos < WIN_QBLOCKS * WIN_BLOCK
        tabs.append(ok)
    return jnp.asarray(np.stack(tabs).astype(np.float32))


def _window_attention(q, kv, kv_ctx, sink):
    n = q[0].shape[0]
    nb = n // WIN_BLOCK
    assert nb % WIN_QBLOCKS == 0
    l = kv_ctx[0].shape[0]
    nkb = WIN_QBLOCKS + 2
    per = SEG // LANES
    blk = lambda off, j: pl.BlockSpec((WIN_BLOCK, LANES),
                                      lambda i: (jnp.clip(WIN_QBLOCKS * i - 1 + j, 0, nb - 1), per * kv[1] + off))
    mask_tab = _win_valid_table()
    return pl.pallas_call(
        _win_kernel,
        grid=(nb // WIN_QBLOCKS,),
        in_specs=[pl.BlockSpec(memory_space=pltpu.SMEM),
                  pl.BlockSpec((WIN_QBLOCKS * WIN_BLOCK, SEG), lambda i: (i, q[1]))]
        + [blk(0, j) for j in range(nkb)] + [blk(1, j) for j in range(nkb)]
        + [pl.BlockSpec((l, LANES), lambda i: (0, per * kv_ctx[1])), pl.BlockSpec((l, LANES), lambda i: (0, per * kv_ctx[1] + 1)),
           pl.BlockSpec(mask_tab.shape, lambda i: (0, 0, 0))],
        out_specs=pl.BlockSpec((WIN_QBLOCKS * WIN_BLOCK, SEG), lambda i: (i, 0)),
        out_shape=jax.ShapeDtypeStruct((n, SEG), BF16),
        compiler_params=_cparams(("arbitrary",)),
        name="window_attention",
    )(sink, q[0], *([kv[0]] * (2 * nkb)), kv_ctx[0], kv_ctx[0], mask_tab)


def _na_slab_start(s, half_rows):
    return jnp.clip((NA_QROWS // 2) * s - NA_ROWS // 4, 0, half_rows - NA_SLAB_ROWS // 2)


def _na_kernel(q_ref, *refs, rows):
    nslab = NA_SLAB_ROWS // 2
    k_refs, v_refs = refs[:nslab], refs[nslab:2 * nslab]
    kx_ref, vx_ref, bias_ref, mask_ref, o_ref = refs[2 * nslab:]
    s = pl.program_id(0)
    last = pl.num_programs(0) - 1
    w = GRID_W
    nq = NA_QROWS * w
    nk = NA_SLAB_ROWS * w
    delta = 2 * _na_slab_start(s, rows // 2) - NA_QROWS * s
    variant = jnp.where(s == 0, 0, jnp.where(s == last, 2, 1))
    valid = mask_ref[variant] > 0.5
    valid = jnp.concatenate([valid, valid], axis=1)
    row = _iota((LANES, nq), 0)
    for pair in range(NA_HEADS // 2):
        sl = slice(pair * LANES, (pair + 1) * LANES)
        q = q_ref[:, sl]
        qs = jnp.concatenate([_half_mask(q, 0), _half_mask(q, 1)], axis=0)
        k_all = jnp.concatenate([r[:, sl] for r in k_refs] + [kx_ref[:, sl]], axis=0)
        v_all = jnp.concatenate([r[:, sl] for r in v_refs] + [vx_ref[:, sl]], axis=0)
        st = _dot_nt(k_all, qs)
        bias = jnp.concatenate(
            [jnp.concatenate([bias_ref[2 * pair + u, delta + 2 * i - a + NA_BIAS_ZERO]
                              for u in range(2) for a in range(0, NA_QROWS, 2)], axis=1) for i in range(nslab)], axis=0)
        s_loc = jnp.where(valid, st[:nk] + bias, NEG_INF)
        o = _softmax_pv_t(jnp.concatenate([s_loc, st[nk:]], axis=0), v_all)
        o_ref[:, sl] = jnp.where(row < HEAD_D, o[:, :nq], o[:, nq:]).T.astype(BF16)


def _na_valid_table():
    import numpy as np
    kk = np.arange(NA_SLAB_ROWS)[:, None, None, None]
    ck = np.arange(GRID_W)[None, :, None, None]
    a = np.arange(NA_QROWS)[None, None, :, None]
    cq = np.arange(GRID_W)[None, None, None, :]
    c_start = np.clip(cq - NA_COLS // 2, 0, GRID_W - NA_COLS)
    col_ok = (ck >= c_start) & (ck < c_start + NA_COLS)
    first_row = (0 * a, a, 0 * a + NA_SLAB_ROWS - NA_ROWS)
    tabs = [(col_ok & (kk >= f) & (kk < f + NA_ROWS)).reshape(NA_SLAB_ROWS * GRID_W, NA_QROWS * GRID_W) for f in first_row]
    return jnp.asarray(np.stack(tabs).astype(np.float32))


def _na_attention(p, p_ctx, bias_tab):
    n = p.shape[0]
    rows = n // GRID_W
    assert rows % NA_QROWS == 0 and rows >= NA_SLAB_ROWS
    steps = rows // NA_QROWS
    l = p_ctx.shape[0]
    slab = lambda t, i: pl.BlockSpec((2 * GRID_W, SEG), lambda s: (_na_slab_start(s, rows // 2) + i, t))
    nslab = NA_SLAB_ROWS // 2
    mask_tab = _na_valid_table()
    return pl.pallas_call(
        functools.partial(_na_kernel, rows=rows),
        grid=(steps,),
        in_specs=[pl.BlockSpec((NA_QROWS * GRID_W, SEG), lambda s: (s, T_QN))]
        + [slab(T_KN, i) for i in range(nslab)] + [slab(T_VN, i) for i in range(nslab)]
        + [pl.BlockSpec((l, SEG), lambda s: (0, T_KN)), pl.BlockSpec((l, SEG), lambda s: (0, T_VN)),
           pl.BlockSpec(bias_tab.shape, lambda s: (0, 0, 0, 0)), pl.BlockSpec(mask_tab.shape, lambda s: (0, 0, 0))],
        out_specs=pl.BlockSpec((NA_QROWS * GRID_W, SEG), lambda s: (s, 0)),
        out_shape=jax.ShapeDtypeStruct((n, SEG), BF16),
        compiler_params=_cparams(("arbitrary",)),
        name="neighbourhood_attention",
    )(p, *([p] * (2 * nslab)), p_ctx, p_ctx, bias_tab, mask_tab)


def _na_bias_table(rpb):
    n_r, n_c = 2 * NA_ROWS - 1, 2 * NA_COLS - 1
    rpb = rpb.astype(F32)
    import numpy as np
    e_i, uk_i, wq_i = np.meshgrid(np.arange(NA_BIAS_PAIRS), np.arange(2), np.arange(2), indexing="ij")
    src = np.clip(e_i - NA_BIAS_ZERO + uk_i - wq_i + NA_ROWS - 1, 0, n_r - 1)
    pick = jnp.asarray((src[..., None] == np.arange(n_r)).astype(np.float32))
    rows = jnp.einsum("euwr,hrj->heuwj", pick, rpb, precision=HIGHEST)
    ck = np.arange(GRID_W)[:, None]
    cq = np.arange(GRID_W)[None, :]
    ci = np.clip(ck - cq, -(NA_COLS - 1), NA_COLS - 1) + NA_COLS - 1
    sel = (ci[None] == np.arange(n_c)[:, None, None]).astype(np.float32)
    sel2 = np.zeros((2, n_c, GRID_W, 2, GRID_W), np.float32)
    for wq in range(2):
        sel2[wq, :, :, wq, :] = sel
    sel2 = jnp.asarray(sel2.reshape(2 * n_c, GRID_W, 2 * GRID_W))
    rows = rows.reshape(rpb.shape[0], NA_BIAS_PAIRS, 2, 2 * n_c)
    tab = jnp.einsum("heuj,jkl->heukl", rows, sel2, precision=HIGHEST)
    return tab.reshape(rpb.shape[0], NA_BIAS_PAIRS, 2 * GRID_W, 2 * GRID_W)


def _ctx_attn_kernel(sink_ref, r_ref, p_ref, ow_ref, on_ref):
    l = p_ref.shape[0]
    n_cols = SEG // LANES
    k_all = r_ref[:, R_KVW * SEG:R_KVW * SEG + LANES]
    v_all = r_ref[:, R_KVW * SEG + LANES:R_KVW * SEG + 2 * LANES]
    qs = jnp.concatenate([r_ref[:, R_QW * SEG + c * LANES:R_QW * SEG + (c + 1) * LANES] for c in range(n_cols)], axis=0)
    outs = []
    for g in range(WIN_KV_HEADS):
        s = _dot_nt(qs, _half_mask(k_all, g))
        sink = jnp.concatenate([jnp.full((l, 1), sink_ref[n_cols * g + c], F32) for c in range(n_cols)], axis=0)
        outs.append(_softmax_pv(s, _half_mask(v_all, g), sink))
    o = outs[0] + outs[1]
    ow_ref[...] = jnp.concatenate([o[c * l:(c + 1) * l, :] for c in range(n_cols)], axis=1).astype(BF16)
    for pair in range(NA_HEADS // 2):
        sl = lambda t: slice(t * SEG + pair * LANES, t * SEG + (pair + 1) * LANES)
        q, k, v = p_ref[:, sl(T_QN)], p_ref[:, sl(T_KN)], p_ref[:, sl(T_VN)]
        out = None
        for u in range(2):
            o = _softmax_pv(_dot_nt(q, _half_mask(k, u)), _half_mask(v, u))
            out = o if out is None else out + o
        on_ref[:, pair * LANES:(pair + 1) * LANES] = out.astype(BF16)


def _ctx_attention(r_ctx, p_ctx, sink):
    l = p_ctx.shape[0]
    return pl.pallas_call(
        _ctx_attn_kernel,
        in_specs=[pl.BlockSpec(memory_space=pltpu.SMEM), pl.BlockSpec(r_ctx.shape, lambda: (0, 0)),
                  pl.BlockSpec(p_ctx.shape, lambda: (0, 0))],
        out_specs=[pl.BlockSpec((l, SEG), lambda: (0, 0)), pl.BlockSpec((l, SEG), lambda: (0, 0))],
        out_shape=[jax.ShapeDtypeStruct((l, SEG), BF16), jax.ShapeDtypeStruct((l, SEG), BF16)],
        compiler_params=pltpu.CompilerParams(vmem_limit_bytes=VMEM_LIMIT),
        name="context_attention",
    )(sink, r_ctx, p_ctx)


def _merge_kernel(yf_ref, yb_ref, gr_ref, gn_ref, yw_ref, yn_ref, ga_ref, gb_ref, gc_ref,
                  wr_ref, ww_ref, wn_ref, wo_ref, x_ref, m2_ref, gf_ref, m3_ref, m4_ref, wrt_ref,
                  xo_ref, hx_ref, lt_ref):
    y = yf_ref[...] + yb_ref[...]
    parts = []
    for h in range(RET_HEADS):
        yh = y[:, h * RET_D:(h + 1) * RET_D]
        mu = jnp.mean(yh, axis=-1, keepdims=True)
        var = jnp.mean(jnp.square(yh - mu), axis=-1, keepdims=True)
        parts.append((yh - mu) * lax.rsqrt(var + EPS))
    g = gr_ref[...].astype(F32)
    ya = jnp.concatenate(parts, axis=1) * gn_ref[...] * (g * jax.nn.sigmoid(g))
    za = _dot(ya.astype(BF16), wr_ref[...])
    zb = _dot(yw_ref[...], ww_ref[...])
    zc = _dot(yn_ref[...], wn_ref[...])
    sig = lambda r: jax.nn.sigmoid(r[...].astype(F32))
    mix = sig(ga_ref) * za + sig(gb_ref) * zb + sig(gc_ref) * zc
    x_new = x_ref[...] + m2_ref[...] * _dot(mix.astype(BF16), wo_ref[...])
    xo_ref[...] = x_new
    h2 = _norm_mod(x_new, gf_ref[...], m3_ref[...], m4_ref[...])
    n_sub = h2.shape[1] // LANES
    for s in range(n_sub):
        hx_ref[pl.ds(s, h2.shape[0], stride=n_sub), :] = h2[:, s * LANES:(s + 1) * LANES]
    lt_ref[...] = _dot_nt(wrt_ref[...], h2, precision=HIGHEST)


def _merge(yf, yb, p, gn, yw, yn, wr, ww, wn, wo, x, m2, gf, m3, m4, wrt, *, tm):
    n, d = x.shape
    row = lambda wdt, t: pl.BlockSpec((tm, wdt), lambda i: (i, t))
    full = lambda a: pl.BlockSpec(a.shape, lambda i: (0,) * a.ndim)
    gate0 = T_GATES * SEG // d
    return pl.pallas_call(
        _merge_kernel,
        grid=(n // tm,),
        in_specs=[row(SEG, 0), row(SEG, 0), row(SEG, T_GR), full(gn), row(SEG, 0), row(SEG, 0),
                  row(d, gate0), row(d, gate0 + 1), row(d, gate0 + 2),
                  full(wr), full(ww), full(wn), full(wo), row(d, 0), full(m2), full(gf), full(m3), full(m4), full(wrt)],
        out_specs=[pl.BlockSpec((tm, d), lambda i: (i, 0)),
                   pl.BlockSpec((tm * (d // LANES), LANES), lambda i: (i, 0)),
                   pl.BlockSpec((N_EXPERTS, tm), lambda i: (0, i))],
        out_shape=[jax.ShapeDtypeStruct((n, d), F32), jax.ShapeDtypeStruct((n * (d // LANES), LANES), F32),
                   jax.ShapeDtypeStruct((N_EXPERTS, n), F32)],
        compiler_params=_cparams(("arbitrary",)),
        name="merge",
    )(yf, yb, p, gn, yw, yn, p, p, p, wr, ww, wn, wo, x, m2, gf, m3, m4, wrt)


def _route_kernel(lt_ref, idx_ref, gate_ref, pos_ref, goff_ref, aff_ref, thr_ref, *, cap, n_groups):
    e = pl.program_id(0)
    n_exp, ag, _ = lt_ref.shape
    capp = idx_ref.shape[-1]

    @pl.when(e == 0)
    def _():
        lt = lt_ref[...]
        ex = jnp.exp(lt - jnp.max(lt, axis=0, keepdims=True))
        aff = ex / jnp.sum(ex, axis=0, keepdims=True)
        real = _iota(aff.shape, 1) < n_groups
        aff = jnp.where(real, aff, 0.0)
        aff_ref[...] = aff

        def count_ge(v):
            return jnp.sum(jnp.sum((aff >= v).astype(F32), axis=2, keepdims=True), axis=1, keepdims=True)

        def body(k, carry):
            lo, hi = carry
            mid = 0.5 * (lo + hi)
            ge = count_ge(mid) >= float(cap)
            return jnp.where(ge, mid, lo), jnp.where(ge, hi, mid)

        lo, _ = lax.fori_loop(0, BISECT_STEPS, body, (jnp.zeros((n_exp, 1, 1), F32), jnp.full((n_exp, 1, 1), 2.0, F32)))
        big = jnp.where(aff >= lo, aff, 4.0)
        t = jnp.min(jnp.min(big, axis=2, keepdims=True), axis=1, keepdims=True)
        thr_ref[...] = jnp.broadcast_to(t, thr_ref.shape)

    a = aff_ref[e]
    t = thr_ref[e][0:1, :]
    gt = a > t
    eq = a == t
    tri_lane_strict = (_iota((LANES, LANES), 0) < _iota((LANES, LANES), 1)).astype(BF16)
    tri_lane_incl = (_iota((LANES, LANES), 0) <= _iota((LANES, LANES), 1)).astype(BF16)
    tri_grp_strict = (_iota((ag, ag), 1) < _iota((ag, ag), 0)).astype(BF16)
    tri_grp_incl = (_iota((ag, ag), 1) <= _iota((ag, ag), 0)).astype(BF16)

    def total(mask_f):
        return jnp.sum(jnp.sum(mask_f, axis=1, keepdims=True), axis=0, keepdims=True)

    def group_sum(mask_f):
        return jnp.broadcast_to(jnp.sum(mask_f, axis=1, keepdims=True), (ag, LANES)).astype(BF16)

    eq_f = eq.astype(F32)
    need = float(cap) - total(gt.astype(F32))
    rank_eq = _dot(tri_grp_strict, group_sum(eq_f)) + _dot(eq_f.astype(BF16), tri_lane_strict)
    sel = gt | (eq & (rank_eq < need))
    sel_f = sel.astype(F32)
    cl = _dot(sel_f.astype(BF16), tri_lane_incl)
    cg = _dot(tri_grp_incl, group_sum(sel_f))
    goff = cg - jnp.broadcast_to(jnp.sum(sel_f, axis=1, keepdims=True), (ag, LANES))
    pos_ref[...] = jnp.where(sel, goff + cl - 1.0, -1.0)
    diag = _iota((ag, ag), 0) == _iota((ag, ag), 1)
    goff_sq = goff if ag == LANES else goff[:, :ag]
    goff_ref[...] = jnp.sum(jnp.where(diag, goff_sq, 0.0), axis=0, keepdims=True).astype(I32)

    pp = _iota((ag, capp), 1, F32)
    cg_b = jnp.broadcast_to(cg[:, 0:1], (ag, capp))
    below = cg_b <= pp
    grp = jnp.sum(below.astype(F32), axis=0, keepdims=True)
    off = jnp.max(jnp.where(below, cg_b, 0.0), axis=0, keepdims=True)
    onehot = _iota((ag, capp), 0, F32) == grp
    in_grp = _dot_tn(cl.astype(BF16), onehot.astype(BF16))
    local = pp[0:1, :] - off
    lane_of = jnp.sum((in_grp <= local).astype(F32), axis=0, keepdims=True)
    live = pp[0:1, :] < float(cap)
    idx = jnp.where(live, grp * float(LANES) + lane_of, 0.0)
    idx_ref[...] = idx.astype(I32)
    aff_grp = _dot_tn(a, onehot.astype(F32), precision=HIGHEST)
    pick = _iota((LANES, capp), 0, F32) == lane_of
    gate = jnp.sum(jnp.where(pick, aff_grp, 0.0), axis=0, keepdims=True)
    gate_ref[...] = jnp.where(live, gate, 0.0)


def _route(logits_t, *, cap, n_groups):
    n_exp, ag, _ = logits_t.shape
    capp = -(-cap // LANES) * LANES
    return pl.pallas_call(
        functools.partial(_route_kernel, cap=cap, n_groups=n_groups),
        grid=(n_exp,),
        in_specs=[pl.BlockSpec(logits_t.shape, lambda e: (0, 0, 0))],
        out_specs=[pl.BlockSpec((None, 1, capp), lambda e: (e, 0, 0)),
                   pl.BlockSpec((None, 1, capp), lambda e: (e, 0, 0)),
                   pl.BlockSpec((None, ag, LANES), lambda e: (e, 0, 0)),
                   pl.BlockSpec((None, 1, ag), lambda e: (e, 0, 0))],
        out_shape=[jax.ShapeDtypeStruct((n_exp, 1, capp), I32), jax.ShapeDtypeStruct((n_exp, 1, capp), F32),
                   jax.ShapeDtypeStruct((n_exp, ag, LANES), F32), jax.ShapeDtypeStruct((n_exp, 1, ag), I32)],
        scratch_shapes=[pltpu.VMEM((n_exp, ag, LANES), F32), pltpu.VMEM((n_exp, SUBLANES, LANES), F32)],
        compiler_params=_cparams(("arbitrary",)),
        name="route",
    )(logits_t)


def _ffn_kernel(idx_ref, idxn_ref, gate_ref, hx_ref, wg_ref, wu_ref, wd_ref, o_ref, xbuf, sem, wgb, wub, wdb, *, tm):
    e = pl.program_id(0)
    j = pl.program_id(1)
    nt = pl.num_programs(1)
    step = e * nt + j
    last = pl.num_programs(0) * nt - 1

    sub = SUBLANES
    rows = tm * sub

    def row_copy(rows_ref, r, slot):
        src = hx_ref.at[pl.ds(pl.multiple_of(rows_ref[0, 0, r] * sub, sub), sub)]
        dst = xbuf.at[pl.ds(pl.multiple_of(slot * rows + r * sub, sub), sub)]
        return pltpu.make_async_copy(src, dst, sem.at[slot])

    def slot_copy(slot):
        return pltpu.make_async_copy(hx_ref.at[pl.ds(0, rows)], xbuf.at[pl.ds(pl.multiple_of(slot * rows, sub), rows)],
                                     sem.at[slot])

    @pl.when(step == 0)
    def _():
        def body(r, carry):
            row_copy(idx_ref, r, 0).start()
            return carry
        lax.fori_loop(0, tm, body, 0, unroll=8)

    @pl.when(j == 0)
    def _():
        wgb[...] = wg_ref[...].astype(BF16)
        wub[...] = wu_ref[...].astype(BF16)
        wdb[...] = wd_ref[...].astype(BF16)

    slot = step % 2
    nslot = 1 - slot
    for r in range(tm):
        row_copy(idxn_ref, r, nslot).start()
    slot_copy(slot).wait()
    base = slot * rows
    x = jnp.concatenate([xbuf[pl.ds(base + s, tm, stride=sub), :] for s in range(sub)], axis=1).astype(BF16)
    g = _dot(x, wgb[...])
    u = _dot(x, wub[...])
    hid = (g * jax.nn.sigmoid(g)) * u
    y = _dot(hid.astype(BF16), wdb[...])
    eye = _iota((tm, tm), 0) == _iota((tm, tm), 1)
    gcol = jnp.sum(jnp.where(eye, jnp.broadcast_to(gate_ref[0], (tm, tm)), 0.0), axis=1, keepdims=True)
    o_ref[...] = (y * gcol).astype(BF16)

    @pl.when(step == last)
    def _():
        slot_copy(nslot).wait()


def _expert_ffn(idx, gate, hx3, w_gate, w_up, w_down, *, layer, cap, tm):
    n_exp = idx.shape[0]
    d = w_gate.shape[2]
    f = w_gate.shape[3]
    assert d == SUBLANES * LANES, "a token row must be exactly one (8, 128) f32 tile"
    nt = cap // tm
    idx_t = idx[:, :, :cap].reshape(n_exp * nt, 1, tm)
    gate_t = gate[:, :, :cap].reshape(n_exp * nt, 1, tm)
    n_steps = n_exp * nt
    wspec = lambda a, b: pl.BlockSpec((None, None, a, b), lambda e, j: (layer, e, 0, 0))
    return pl.pallas_call(
        functools.partial(_ffn_kernel, tm=tm),
        grid=(n_exp, nt),
        in_specs=[pl.BlockSpec((1, 1, tm), lambda e, j: (e * nt + j, 0, 0), memory_space=pltpu.SMEM),
                  pl.BlockSpec((1, 1, tm), lambda e, j: (jnp.minimum(e * nt + j + 1, n_steps - 1), 0, 0), memory_space=pltpu.SMEM),
                  pl.BlockSpec((1, 1, tm), lambda e, j: (e * nt + j, 0, 0)),
                  pl.BlockSpec(memory_space=pl.ANY),
                  wspec(d, f), wspec(d, f), wspec(f, d)],
        out_specs=pl.BlockSpec((tm, d), lambda e, j: (e * nt + j, 0)),
        out_shape=jax.ShapeDtypeStruct((n_exp * cap, d), BF16),
        scratch_shapes=[pltpu.VMEM((2 * tm * SUBLANES, LANES), F32), pltpu.SemaphoreType.DMA((2,)),
                        pltpu.VMEM((d, f), BF16), pltpu.VMEM((d, f), BF16), pltpu.VMEM((f, d), BF16)],
        compiler_params=_cparams(("arbitrary", "arbitrary")),
        name="expert_ffn",
    )(idx_t, idx_t, gate_t, hx3, w_gate, w_up, w_down)


def _combine_kernel(ws_ref, nw_ref, *refs, n_exp, cap, win, gpt, final_norm):
    y_refs = refs[:n_exp]
    pos_ref, ye_ref, x_ref, m5_ref, gf_ref, o_ref, acc_ref, xwin, sem = refs[n_exp:]
    a = pl.program_id(0)
    ng = pl.num_programs(0)
    per = max(1, MXU_DEPTH // win)

    def slots(e):
        pos = pos_ref[e, pl.ds(a * gpt, gpt), :]
        return jnp.concatenate([jnp.broadcast_to(pos[j:j + 1, :], (win, LANES)) for j in range(gpt)], axis=1)

    lp = _iota((win, gpt * LANES), 0, F32)
    acc = None
    for e0 in range(0, n_exp, per):
        es = range(e0, min(e0 + per, n_exp))
        hot = jnp.concatenate([((slots(e) - (ws_ref[e * ng + a] - e * cap).astype(F32)) == lp).astype(BF16) for e in es],
                              axis=0)
        rows = jnp.concatenate([y_refs[e][...] for e in es], axis=0)
        part = _dot_tn(hot, rows)
        acc = part if acc is None else acc + part
    acc_ref[...] = acc

    for e in range(n_exp):
        first = ws_ref[e * ng + a] - e * cap

        def extra(k, carry, e=e, first=first):
            lo = first + k * win
            row = jnp.minimum(lo, cap - win)
            cp = pltpu.make_async_copy(ye_ref.at[pl.ds(pl.multiple_of(e * cap + row, BF16_ROWS), win)], xwin, sem)
            cp.start()
            cp.wait()
            pos = slots(e)
            hot = ((pos - row.astype(F32)) == lp) & (pos >= lo.astype(F32))
            acc_ref[...] += _dot_tn(hot.astype(BF16), xwin[...])
            return carry

        lax.fori_loop(1, nw_ref[e * ng + a], extra, 0)

    x_new = x_ref[...] + m5_ref[...] * acc_ref[...]
    if final_norm:
        ms = jnp.mean(x_new * x_new, axis=-1, keepdims=True)
        x_new = x_new * lax.rsqrt(ms + EPS) * gf_ref[...]
    o_ref[...] = x_new


def _combine(ye, posmap, goff, x, m5, g_final, *, cap, final_norm):
    n, d = x.shape
    n_exp = posmap.shape[0]
    gpt = min(COMBINE_GROUPS, n // LANES)
    ng = n // (gpt * LANES)
    tok = gpt * LANES
    win = min(COMBINE_WINDOW, cap)
    goff = goff[:, 0, :ng * gpt:gpt]
    end = jnp.concatenate([goff[:, 1:], jnp.full((n_exp, 1), cap, I32)], axis=1)
    start = jnp.minimum((goff // BF16_ROWS) * BF16_ROWS, cap - win)
    n_win = jnp.maximum((end - start + win - 1) // win, 1).reshape(-1)
    wstart = (start + jnp.arange(n_exp, dtype=I32)[:, None] * cap).reshape(-1)
    yspec = lambda e: pl.BlockSpec((pl.Element(win), pl.Element(d)),
                                   lambda a, ws, nw: (pl.multiple_of(ws[e * ng + a], BF16_ROWS), 0))
    gs = pltpu.PrefetchScalarGridSpec(
        num_scalar_prefetch=2,
        grid=(ng,),
        in_specs=[yspec(e) for e in range(n_exp)]
        + [pl.BlockSpec(posmap.shape, lambda a, ws, nw: (0, 0, 0)),
           pl.BlockSpec(memory_space=pl.ANY),
           pl.BlockSpec((tok, d), lambda a, ws, nw: (a, 0)),
           pl.BlockSpec((1, d), lambda a, ws, nw: (0, 0)), pl.BlockSpec((1, d), lambda a, ws, nw: (0, 0))],
        out_specs=pl.BlockSpec((tok, d), lambda a, ws, nw: (a, 0)),
        scratch_shapes=[pltpu.VMEM((tok, d), F32), pltpu.VMEM((win, d), BF16), pltpu.SemaphoreType.DMA(())],
    )
    return pl.pallas_call(
        functools.partial(_combine_kernel, n_exp=n_exp, cap=cap, win=win, gpt=gpt, final_norm=final_norm),
        grid_spec=gs,
        out_shape=jax.ShapeDtypeStruct((n, d), F32),
        compiler_params=_cparams(("arbitrary",)),
        name="combine_final" if final_norm else "combine",
    )(wstart, n_win, *([ye] * n_exp), posmap, ye, x, m5, g_final)


def _rope_tables(n):
    n_rows = n // GRID_W
    row = jnp.arange(n_rows).astype(F32)
    col = jnp.arange(GRID_W).astype(F32)

    def tables(d, signs):
        nf = d // 4
        inv = ROPE_BASE ** (-jnp.arange(nf, dtype=F32) / nf)
        ang_r, ang_c = row[:, None] * inv, col[:, None] * inv
        reps = LANES // (2 * nf)

        def lanes(f_r, f_c, factors):
            z_r, z_c = jnp.zeros_like(f_r), jnp.zeros_like(f_c)
            pat_r = jnp.concatenate([part for k in range(reps) for part in (f_r * factors[2 * k], z_r)], axis=1)
            pat_c = jnp.concatenate([part for k in range(reps) for part in (z_c, f_c * factors[2 * k + 1])], axis=1)
            return (pat_r[:, None, :] + pat_c[None, :, :]).reshape(n, LANES)

        ones = (1.0,) * (2 * reps)
        cos = lanes(jnp.cos(ang_r), jnp.cos(ang_c), ones)
        return [cos] + [lanes(jnp.sin(ang_r), jnp.sin(ang_c), sg) for sg in signs]

    c128, s128 = tables(RET_D, [(-1.0, -1.0, 1.0, 1.0)])
    c64, sa, sb = tables(HEAD_D, [(-1.0, -1.0, 0.0, 0.0) * 2, (0.0, 0.0, 1.0, 1.0) * 2])
    return c128, s128, c64, sa, sb


WIN_HEAD_ORDER = (0, 4, 1, 5, 2, 6, 3, 7)


def _permute_in_weight(w):
    d = w.shape[0]
    sizes = (("q_r", 512), ("k_r", 512), ("v_r", 512), ("g_r", 512), ("q_w", 512), ("k_w", 128), ("v_w", 128),
             ("q_n", 512), ("k_n", 512), ("v_n", 512), ("gates", 3 * d))
    off, lay = 0, {}
    for name, size in sizes:
        lay[name] = (off, off + size)
        off += size
    seg = lambda name: w[:, lay[name][0]:lay[name][1]]
    q0 = lay["q_w"][0]
    q_w = [w[:, q0 + h * HEAD_D:q0 + (h + 1) * HEAD_D] for h in WIN_HEAD_ORDER]
    pad = jnp.zeros((d, SEG - 2 * LANES), w.dtype)
    rope = jnp.concatenate([seg("q_r"), seg("k_r")] + q_w + [seg("k_w"), seg("v_w"), pad], axis=1).astype(BF16)
    plain = jnp.concatenate([seg("gates"), seg("v_r"), seg("g_r"), seg("q_n"), seg("k_n"), seg("v_n")], axis=1).astype(BF16)
    return rope, plain


def _permute_win_rows(w):
    return jnp.concatenate([w[h * HEAD_D:(h + 1) * HEAD_D] for h in WIN_HEAD_ORDER], axis=0).astype(BF16)


def kernel(x, c, ctx, c_ctx, w_mod, b_mod, g_mix, g_ffn, w_in, ret_decay_logit, ret_gn, w_ret, win_sink, w_win, na_rpb,
           w_na, w_out, w_router, w_exp_gate, w_exp_up, w_exp_down, g_final):
    _, n, d = x.shape
    l = ctx.shape[1]
    depth = w_in.shape[0]
    xs, cs = x[0], ctx[0]
    cc = jnp.zeros((SUBLANES, d), F32).at[0].set(c[0]).at[1].set(c_ctx)
    mods = _modulation(cc, w_mod, b_mod)
    c128, s128, c64, sa64, sb64 = _rope_tables(n)
    k_scale = RET_D ** -0.5
    plain_scales = jnp.ones((PLAIN_TILES,), F32).at[T_QN].set(Q_SCALE)
    ctx_rope_scales = jnp.ones((R_KVW + 1,), F32).at[R_KR].set(k_scale).at[R_QW].set(Q_SCALE)
    cap_x = CAPACITY_FACTOR * n // N_EXPERTS
    cap_c = CAPACITY_FACTOR * l // N_EXPERTS
    grp_c = -(-(l // LANES) // BF16_ROWS) * BF16_ROWS
    vec = lambda v: v.reshape(1, -1)

    for layer in range(depth):
        need_ctx = layer < depth - 1
        last = layer == depth - 1
        mx = [vec(mods[layer, 0, k * d:(k + 1) * d]) for k in range(N_MOD)]
        mc = [vec(mods[layer, 1, k * d:(k + 1) * d]) for k in range(N_MOD)]
        w_rope, w_plain = _permute_in_weight(w_in[layer])
        wr = w_ret[layer].astype(BF16)
        ww = _permute_win_rows(w_win[layer])
        wn = w_na[layer].astype(BF16)
        wo = w_out[layer].astype(BF16)
        wrt = w_router[layer].T
        sink = win_sink[layer].astype(F32) * LOG2E
        lg_rows = jnp.broadcast_to(ret_decay_logit[layer].astype(F32).reshape(2 * RET_HEADS, 1), (2 * RET_HEADS, LANES))
        gn = vec(ret_gn[layer])
        bias_tab = _na_bias_table(na_rpb[layer].astype(F32) * LOG2E)
        gmix, gffn = vec(g_mix[layer]), vec(g_ffn[layer])

        hx = _norm_call(xs, gmix, mx[0], mx[1], tm=1024)
        hc = _norm_call(cs, gmix, mc[0], mc[1], tm=l)
        p_x = _proj(hx, w_plain, plain_scales, tm=2048)
        p_c = _proj(hc, w_plain, plain_scales, tm=l)
        r_c = _proj(hc, w_rope, ctx_rope_scales, tm=l)
        rope = functools.partial(_proj_rope, hx, w_rope, tm=2048)
        q_r = rope(R_QR, (c128, s128), head_d=RET_D, scale=1.0, n_rot=SEG // LANES, name="proj_q_ret")
        k_r = rope(R_KR, (c128, s128), head_d=RET_D, scale=k_scale, n_rot=SEG // LANES, name="proj_k_ret")
        q_w = rope(R_QW, (c64, sa64, sb64), head_d=HEAD_D, scale=Q_SCALE, n_rot=SEG // LANES, name="proj_q_win")
        kv_w = rope(R_KVW, (c64, sa64, sb64), head_d=HEAD_D, scale=1.0, n_rot=1, name="proj_kv_win")

        yf, yb = _retention((q_r, 0), (k_r, 0), (p_x, T_VR), (r_c, R_KR), (p_c, T_VR), lg_rows, zero_init=False)
        yw = _window_attention((q_w, 0), (kv_w, 0), (r_c, R_KVW), sink)
        yn = _na_attention(p_x, p_c, bias_tab)
        xs, hx3, lt = _merge(yf, yb, p_x, gn, yw, yn, wr, ww, wn, wo, xs, mx[2], gffn, mx[3], mx[4], wrt, tm=512)

        idx, gate, posmap, goff = _route(lt.reshape(N_EXPERTS, n // LANES, LANES), cap=cap_x, n_groups=n // LANES)
        ye = _expert_ffn(idx, gate, hx3, w_exp_gate, w_exp_up, w_exp_down, layer=layer, cap=cap_x, tm=512)
        xs = _combine(ye, posmap, goff, xs, mx[5], vec(g_final), cap=cap_x, final_norm=last)

        if need_ctx:
            yfc, ybc = _retention((r_c, R_QR), (r_c, R_KR), (p_c, T_VR), (r_c, R_KR), (p_c, T_VR), lg_rows, zero_init=True)
            ywc, ync = _ctx_attention(r_c, p_c, sink)
            cs, hc3, ltc = _merge(yfc, ybc, p_c, gn, ywc, ync, wr, ww, wn, wo, cs, mc[2], gffn, mc[3], mc[4], wrt, tm=l)
            ltc = jnp.pad(ltc.reshape(N_EXPERTS, l // LANES, LANES), ((0, 0), (0, grp_c - l // LANES), (0, 0)))
            idc, gtc, posc, goffc = _route(ltc, cap=cap_c, n_groups=l // LANES)
            yec = _expert_ffn(idc, gtc, hc3, w_exp_gate, w_exp_up, w_exp_down, layer=layer, cap=cap_c, tm=cap_c)
            cs = _combine(yec, posc, goffc, cs, mc[5], vec(g_final), cap=cap_c, final_norm=False)

    return xs[None]
```

```python
import functools

import jax
import jax.numpy as jnp
from jax import lax
from jax.experimental import pallas as pl
from jax.experimental.pallas import tpu as pltpu

F32, BF16, I32 = jnp.float32, jnp.bfloat16, jnp.int32
HIGHEST = lax.Precision.HIGHEST

GRID_W = 64
RET_HEADS, RET_D, RET_CHUNK = 4, 128, 128
WIN_HEADS, WIN_KV_HEADS, HEAD_D, WINDOW, WIN_BLOCK = 8, 2, 64, 128, 128
NA_HEADS, NA_ROWS, NA_COLS = 8, 8, 16
N_EXPERTS, CAPACITY_FACTOR = 16, 2
N_MOD = 6
ROPE_BASE = 10000.0
EPS = 1e-6
NEG_INF = -1e30

LANES = 128
SUBLANES = 8
BF16_ROWS = 16
VMEM_LIMIT = 56 * 1024 * 1024

SEG = 512
T_GATES, T_VR, T_GR, T_QN, T_KN, T_VN = 0, 6, 7, 8, 9, 10
PLAIN_TILES = 11
R_QR, R_KR, R_QW, R_KVW = 0, 1, 2, 3
LOG2E = 1.4426950408889634
Q_SCALE = HEAD_D ** -0.5 * LOG2E
RET_BLOCK = 4 * RET_CHUNK
WIN_QBLOCKS = 2
NA_QROWS = 4
NA_SLAB_ROWS = 12
NA_BIAS_ZERO = NA_SLAB_ROWS - 2
NA_BIAS_PAIRS = NA_BIAS_ZERO + NA_SLAB_ROWS - 1
COMBINE_GROUPS = 4
COMBINE_WINDOW = 128
MXU_DEPTH = 256
BISECT_STEPS = 48


def _cparams(sem):
    return pltpu.CompilerParams(dimension_semantics=sem, vmem_limit_bytes=VMEM_LIMIT)


def _dot(a, b):
    return jnp.dot(a, b, preferred_element_type=F32)


def _dot_nt(a, b, precision=None):
    return lax.dot_general(a, b, (((1,), (1,)), ((), ())), precision=precision, preferred_element_type=F32)


def _dot_tn(a, b, precision=None):
    return lax.dot_general(a, b, (((0,), (0,)), ((), ())), precision=precision, preferred_element_type=F32)


def _iota(shape, dim, dtype=I32):
    return lax.broadcasted_iota(I32, shape, dim).astype(dtype)


def _mod_kernel(s_ref, w_ref, b_ref, o_ref):
    s = s_ref[...]
    s = s * jax.nn.sigmoid(s)
    o_ref[...] = jnp.dot(s, w_ref[...], precision=HIGHEST, preferred_element_type=F32) + b_ref[...]


def _modulation(cc, w_mod, b_mod):
    depth, d, md = w_mod.shape
    tn = 1536
    return pl.pallas_call(
        _mod_kernel,
        grid=(depth, md // tn),
        in_specs=[
            pl.BlockSpec((SUBLANES, d), lambda l, j: (0, 0)),
            pl.BlockSpec((None, d, tn), lambda l, j: (l, 0, j)),
            pl.BlockSpec((None, 1, tn), lambda l, j: (l, 0, j)),
        ],
        out_specs=pl.BlockSpec((None, SUBLANES, tn), lambda l, j: (l, 0, j)),
        out_shape=jax.ShapeDtypeStruct((depth, SUBLANES, md), F32),
        compiler_params=_cparams(("arbitrary", "arbitrary")),
        name="modulation",
    )(cc, w_mod, b_mod.reshape(depth, 1, md))


def _norm_mod(x, g, shift, scale):
    ms = jnp.mean(x * x, axis=-1, keepdims=True)
    y = x * lax.rsqrt(ms + EPS) * g
    return y * (1.0 + scale) + shift


def _norm_kernel(x_ref, g_ref, sh_ref, sc_ref, o_ref):
    o_ref[...] = _norm_mod(x_ref[...], g_ref[...], sh_ref[...], sc_ref[...]).astype(BF16)


def _norm_call(x, g, shift, scale, *, tm):
    n, d = x.shape
    vec = lambda: pl.BlockSpec((1, d), lambda i: (0, 0))
    return pl.pallas_call(
        _norm_kernel,
        grid=(n // tm,),
        in_specs=[pl.BlockSpec((tm, d), lambda i: (i, 0)), vec(), vec(), vec()],
        out_specs=pl.BlockSpec((tm, d), lambda i: (i, 0)),
        out_shape=jax.ShapeDtypeStruct((n, d), BF16),
        compiler_params=_cparams(("arbitrary",)),
        name="norm_modulate",
    )(x, g, shift, scale)


def _proj_kernel(scale_ref, hx_ref, w_ref, o_ref):
    o_ref[...] = (_dot(hx_ref[...], w_ref[...]) * scale_ref[pl.program_id(1)]).astype(BF16)


def _proj(hx, w, scales, *, tm):
    n, d = hx.shape
    tiles = w.shape[1] // SEG
    return pl.pallas_call(
        _proj_kernel,
        grid=(n // tm, tiles),
        in_specs=[pl.BlockSpec(memory_space=pltpu.SMEM),
                  pl.BlockSpec((tm, d), lambda i, j: (i, 0)), pl.BlockSpec((d, SEG), lambda i, j: (0, j))],
        out_specs=pl.BlockSpec((tm, SEG), lambda i, j: (i, j)),
        out_shape=jax.ShapeDtypeStruct((n, tiles * SEG), BF16),
        compiler_params=_cparams(("arbitrary", "arbitrary")),
        name="projection",
    )(scales, hx, w)


def _proj_rope_kernel(hx_ref, w_ref, *refs, head_d, scale, n_rot):
    tabs, o_ref = refs[:-1], refs[-1]
    acc = _dot(hx_ref[...], w_ref[...])
    if head_d == RET_D:
        c = tabs[0][...] * scale
        s = tabs[1][...] * scale
        rot = lambda a: a * c + pltpu.roll(a, 64, 1) * s
    else:
        c = tabs[0][...] * scale
        sa = tabs[1][...] * scale
        sb = tabs[2][...] * scale
        rot = lambda a: a * c + pltpu.roll(a, 96, 1) * sa + pltpu.roll(a, 32, 1) * sb
    groups = [acc[:, k * LANES:(k + 1) * LANES] for k in range(SEG // LANES)]
    o_ref[...] = jnp.concatenate([rot(a) if k < n_rot else a for k, a in enumerate(groups)], axis=1).astype(BF16)


def _proj_rope(hx, w, tile, tabs, *, head_d, scale, n_rot, tm, name):
    n, d = hx.shape
    return pl.pallas_call(
        functools.partial(_proj_rope_kernel, head_d=head_d, scale=scale, n_rot=n_rot),
        grid=(n // tm,),
        in_specs=[pl.BlockSpec((tm, d), lambda i: (i, 0)), pl.BlockSpec((d, SEG), lambda i: (0, tile))]
        + [pl.BlockSpec((tm, LANES), lambda i: (i, 0)) for _ in tabs],
        out_specs=pl.BlockSpec((tm, SEG), lambda i: (i, 0)),
        out_shape=jax.ShapeDtypeStruct((n, SEG), BF16),
        compiler_params=_cparams(("arbitrary",)),
        name=name,
    )(hx, w, *tabs)


def _ret_kernel(lg_ref, qf_ref, kf_ref, vf_ref, qb_ref, kb_ref, vb_ref, kc_ref, vc_ref, yf_ref, yb_ref,
                rf_ref, rb_ref, dmat_ref, vec_ref, *, zero_init):
    c = pl.program_id(0)
    ch = qf_ref.shape[0]
    dh = RET_D
    states = (rf_ref, rb_ref)

    @pl.when(c == 0)
    def _():
        lg = jax.nn.log_sigmoid(lg_ref[...])
        ii = _iota((ch, ch), 0, F32)
        jj = _iota((ch, ch), 1, F32)
        iv = _iota((ch, dh), 0, F32)
        n_ctx = kc_ref.shape[0]
        mm = _iota((n_ctx, dh), 0, F32)
        for d in range(2):
            for h in range(RET_HEADS):
                row = lg[RET_HEADS * d + h:RET_HEADS * d + h + 1, :]
                l = jnp.concatenate([jnp.broadcast_to(row, (ch, dh))] * (ch // dh), axis=1)
                lv = jnp.broadcast_to(row, (ch, dh))
                if d == 0:
                    diff = ii - jj
                    dmat_ref[d, h] = jnp.where(diff >= 0.0, jnp.exp(jnp.maximum(diff, 0.0) * l), 0.0)
                    vec_ref[d, h, 0] = jnp.exp((ch - 1.0 - iv) * lv)
                    vec_ref[d, h, 1] = jnp.exp((iv + 1.0) * lv)
                else:
                    diff = jj - ii
                    dmat_ref[d, h] = jnp.where(diff >= 1.0, jnp.exp(jnp.maximum(diff, 0.0) * l), 0.0)
                    vec_ref[d, h, 0] = jnp.exp(iv * lv)
                    vec_ref[d, h, 1] = jnp.exp((ch - iv) * lv)
                vec_ref[d, h, 2] = jnp.exp(float(ch) * lv)
                if zero_init:
                    states[d][h] = jnp.zeros((dh, dh), F32)
                else:
                    lc = jnp.broadcast_to(row, (n_ctx, dh))
                    w = jnp.exp((n_ctx - 1.0 - mm) * lc) if d == 0 else jnp.exp(mm * lc)
                    sl = slice(h * dh, (h + 1) * dh)
                    kw = (kc_ref[:, sl].astype(F32) * w).astype(BF16)
                    states[d][h] = _dot_tn(kw, vc_ref[:, sl])

    for d, (q_ref, k_ref, v_ref, y_ref) in enumerate(((qf_ref, kf_ref, vf_ref, yf_ref), (qb_ref, kb_ref, vb_ref, yb_ref))):
        for h in range(RET_HEADS):
            sl = slice(h * dh, (h + 1) * dh)
            q, k, v = q_ref[:, sl], k_ref[:, sl], v_ref[:, sl]
            r = states[d][h]
            a = (_dot_nt(q, k) * dmat_ref[d, h]).astype(BF16)
            inner = _dot(a, v)
            cross = _dot(q, r.astype(BF16)) * vec_ref[d, h, 1]
            y_ref[:, sl] = inner + cross
            kz = (k.astype(F32) * vec_ref[d, h, 0]).astype(BF16)
            states[d][h] = vec_ref[d, h, 2][:dh, :] * r + _dot_tn(kz, v)


def _retention(q, k, v, kc, vc, lg_rows, *, zero_init):
    n = q[0].shape[0]
    blk = min(RET_BLOCK, n)
    nc = n // blk
    w = RET_HEADS * RET_D
    fwd = lambda sg: pl.BlockSpec((blk, w), lambda c: (c, sg[1]))
    bwd = lambda sg: pl.BlockSpec((blk, w), lambda c: (nc - 1 - c, sg[1]))
    ctx = lambda sg: pl.BlockSpec((sg[0].shape[0], w), lambda c: (0, sg[1]))
    return pl.pallas_call(
        functools.partial(_ret_kernel, zero_init=zero_init),
        grid=(nc,),
        in_specs=[pl.BlockSpec((SUBLANES, LANES), lambda c: (0, 0)),
                  fwd(q), fwd(k), fwd(v), bwd(q), bwd(k), bwd(v), ctx(kc), ctx(vc)],
        out_specs=[pl.BlockSpec((blk, w), lambda c: (c, 0)), pl.BlockSpec((blk, w), lambda c: (nc - 1 - c, 0))],
        out_shape=[jax.ShapeDtypeStruct((n, w), F32), jax.ShapeDtypeStruct((n, w), F32)],
        scratch_shapes=[pltpu.VMEM((RET_HEADS, RET_D, RET_D), F32), pltpu.VMEM((RET_HEADS, RET_D, RET_D), F32),
                        pltpu.VMEM((2, RET_HEADS, blk, blk), F32), pltpu.VMEM((2, RET_HEADS, 3, blk, RET_D), F32)],
        compiler_params=_cparams(("arbitrary",)),
        name="retention_ctx" if zero_init else "retention",
    )(lg_rows, q[0], k[0], v[0], q[0], k[0], v[0], kc[0], vc[0])


def _half_mask(x, half):
    lane = _iota(x.shape, 1)
    keep = (lane < HEAD_D) if half == 0 else (lane >= HEAD_D)
    return jnp.where(keep, x, jnp.zeros_like(x))


def _softmax_pv(s, v, extra=None):
    m = jnp.max(s, axis=1, keepdims=True)
    if extra is not None:
        m = jnp.maximum(m, extra)
    p = jnp.exp2(s - m)
    den = jnp.sum(p, axis=1, keepdims=True)
    if extra is not None:
        den = den + jnp.exp2(extra - m)
    return _dot(p.astype(BF16), v) / den


def _softmax_pv_t(st, v, extra=None):
    m = jnp.max(st, axis=0, keepdims=True)
    if extra is not None:
        m = jnp.maximum(m, extra)
    p = jnp.exp2(st - m)
    den = jnp.sum(p, axis=0, keepdims=True)
    if extra is not None:
        den = den + jnp.exp2(extra - m)
    return _dot_tn(v, p.astype(BF16)) / den


def _win_kernel(sink_ref, q_ref, *refs):
    nkb = WIN_QBLOCKS + 2
    k_refs, v_refs = refs[:nkb], refs[nkb:2 * nkb]
    kx_ref, vx_ref, mask_ref, o_ref = refs[2 * nkb:]
    s = pl.program_id(0)
    last = pl.num_programs(0) - 1
    nq = WIN_QBLOCKS * WIN_BLOCK
    nk = nkb * WIN_BLOCK
    k_all = jnp.concatenate([r[...] for r in k_refs] + [kx_ref[...]], axis=0)
    v_all = jnp.concatenate([r[...] for r in v_refs] + [vx_ref[...]], axis=0)
    n_cols = SEG // LANES
    qs = jnp.concatenate([q_ref[:, c * LANES:(c + 1) * LANES] for c in range(n_cols)], axis=0)
    variant = jnp.where(s == 0, 1, 0) + jnp.where(s == last, 2, 0)
    valid = mask_ref[variant] > 0.5
    valid = jnp.concatenate([valid] * n_cols, axis=1)
    out_t = None
    for g in range(WIN_KV_HEADS):
        st = _dot_nt(_half_mask(k_all, g), qs)
        st = jnp.concatenate([jnp.where(valid, st[:nk], NEG_INF), st[nk:]], axis=0)
        sink = jnp.concatenate([jnp.full((1, nq), sink_ref[n_cols * g + c], F32) for c in range(n_cols)], axis=1)
        o = _softmax_pv_t(st, _half_mask(v_all, g), sink)
        out_t = o if out_t is None else out_t + o
    o = out_t.T
    o_ref[...] = jnp.concatenate([o[c * nq:(c + 1) * nq, :] for c in range(n_cols)], axis=1).astype(BF16)


def _win_valid_table():
    import numpy as np
    nkb = WIN_QBLOCKS + 2
    qpos = np.arange(WIN_QBLOCKS * WIN_BLOCK)[None, :]
    kpos = np.arange(nkb * WIN_BLOCK)[:, None] - WIN_BLOCK
    band = np.abs(kpos - qpos) <= WINDOW
    tabs = []
    for variant in range(4):
        ok = band.copy()
        if variant & 1:
            ok &= kpos >= 0
        if variant & 2:
            ok &= kpos < WIN_QBLOCKS * WIN_BLOCK
        tabs.append(ok)
    return jnp.asarray(np.stack(tabs).astype(np.float32))


def _window_attention(q, kv, kv_ctx, sink):
    n = q[0].shape[0]
    nb = n // WIN_BLOCK
    assert nb % WIN_QBLOCKS == 0
    l = kv_ctx[0].shape[0]
    nkb = WIN_QBLOCKS + 2
    per = SEG // LANES
    blk = lambda off, j: pl.BlockSpec((WIN_BLOCK, LANES),
                                      lambda i: (jnp.clip(WIN_QBLOCKS * i - 1 + j, 0, nb - 1), per * kv[1] + off))
    mask_tab = _win_valid_table()
    return pl.pallas_call(
        _win_kernel,
        grid=(nb // WIN_QBLOCKS,),
        in_specs=[pl.BlockSpec(memory_space=pltpu.SMEM),
                  pl.BlockSpec((WIN_QBLOCKS * WIN_BLOCK, SEG), lambda i: (i, q[1]))]
        + [blk(0, j) for j in range(nkb)] + [blk(1, j) for j in range(nkb)]
        + [pl.BlockSpec((l, LANES), lambda i: (0, per * kv_ctx[1])), pl.BlockSpec((l, LANES), lambda i: (0, per * kv_ctx[1] + 1)),
           pl.BlockSpec(mask_tab.shape, lambda i: (0, 0, 0))],
        out_specs=pl.BlockSpec((WIN_QBLOCKS * WIN_BLOCK, SEG), lambda i: (i, 0)),
        out_shape=jax.ShapeDtypeStruct((n, SEG), BF16),
        compiler_params=_cparams(("arbitrary",)),
        name="window_attention",
    )(sink, q[0], *([kv[0]] * (2 * nkb)), kv_ctx[0], kv_ctx[0], mask_tab)


def _na_slab_start(s, half_rows):
    return jnp.clip((NA_QROWS // 2) * s - NA_ROWS // 4, 0, half_rows - NA_SLAB_ROWS // 2)


def _na_kernel(q_ref, *refs, rows):
    nslab = NA_SLAB_ROWS // 2
    k_refs, v_refs = refs[:nslab], refs[nslab:2 * nslab]
    kx_ref, vx_ref, bias_ref, mask_ref, o_ref = refs[2 * nslab:]
    s = pl.program_id(0)
    last = pl.num_programs(0) - 1
    w = GRID_W
    nq = NA_QROWS * w
    nk = NA_SLAB_ROWS * w
    delta = 2 * _na_slab_start(s, rows // 2) - NA_QROWS * s
    variant = jnp.where(s == 0, 0, jnp.where(s == last, 2, 1))
    valid = mask_ref[variant] > 0.5
    valid = jnp.concatenate([valid, valid], axis=1)
    row = _iota((LANES, nq), 0)
    for pair in range(NA_HEADS // 2):
        sl = slice(pair * LANES, (pair + 1) * LANES)
        q = q_ref[:, sl]
        qs = jnp.concatenate([_half_mask(q, 0), _half_mask(q, 1)], axis=0)
        k_all = jnp.concatenate([r[:, sl] for r in k_refs] + [kx_ref[:, sl]], axis=0)
        v_all = jnp.concatenate([r[:, sl] for r in v_refs] + [vx_ref[:, sl]], axis=0)
        st = _dot_nt(k_all, qs)
        bias = jnp.concatenate(
            [jnp.concatenate([bias_ref[2 * pair + u, delta + 2 * i - a + NA_BIAS_ZERO]
                              for u in range(2) for a in range(0, NA_QROWS, 2)], axis=1) for i in range(nslab)], axis=0)
        s_loc = jnp.where(valid, st[:nk] + bias, NEG_INF)
        o = _softmax_pv_t(jnp.concatenate([s_loc, st[nk:]], axis=0), v_all)
        o_ref[:, sl] = jnp.where(row < HEAD_D, o[:, :nq], o[:, nq:]).T.astype(BF16)


def _na_valid_table():
    import numpy as np
    kk = np.arange(NA_SLAB_ROWS)[:, None, None, None]
    ck = np.arange(GRID_W)[None, :, None, None]
    a = np.arange(NA_QROWS)[None, None, :, None]
    cq = np.arange(GRID_W)[None, None, None, :]
    c_start = np.clip(cq - NA_COLS // 2, 0, GRID_W - NA_COLS)
    col_ok = (ck >= c_start) & (ck < c_start + NA_COLS)
    first_row = (0 * a, a, 0 * a + NA_SLAB_ROWS - NA_ROWS)
    tabs = [(col_ok & (kk >= f) & (kk < f + NA_ROWS)).reshape(NA_SLAB_ROWS * GRID_W, NA_QROWS * GRID_W) for f in first_row]
    return jnp.asarray(np.stack(tabs).astype(np.float32))


def _na_attention(p, p_ctx, bias_tab):
    n = p.shape[0]
    rows = n // GRID_W
    assert rows % NA_QROWS == 0 and rows >= NA_SLAB_ROWS
    steps = rows // NA_QROWS
    l = p_ctx.shape[0]
    slab = lambda t, i: pl.BlockSpec((2 * GRID_W, SEG), lambda s: (_na_slab_start(s, rows // 2) + i, t))
    nslab = NA_SLAB_ROWS // 2
    mask_tab = _na_valid_table()
    return pl.pallas_call(
        functools.partial(_na_kernel, rows=rows),
        grid=(steps,),
        in_specs=[pl.BlockSpec((NA_QROWS * GRID_W, SEG), lambda s: (s, T_QN))]
        + [slab(T_KN, i) for i in range(nslab)] + [slab(T_VN, i) for i in range(nslab)]
        + [pl.BlockSpec((l, SEG), lambda s: (0, T_KN)), pl.BlockSpec((l, SEG), lambda s: (0, T_VN)),
           pl.BlockSpec(bias_tab.shape, lambda s: (0, 0, 0, 0)), pl.BlockSpec(mask_tab.shape, lambda s: (0, 0, 0))],
        out_specs=pl.BlockSpec((NA_QROWS * GRID_W, SEG), lambda s: (s, 0)),
        out_shape=jax.ShapeDtypeStruct((n, SEG), BF16),
        compiler_params=_cparams(("arbitrary",)),
        name="neighbourhood_attention",
    )(p, *([p] * (2 * nslab)), p_ctx, p_ctx, bias_tab, mask_tab)


def _na_bias_table(rpb):
    n_r, n_c = 2 * NA_ROWS - 1, 2 * NA_COLS - 1
    rpb = rpb.astype(F32)
    import numpy as np
    e_i, uk_i, wq_i = np.meshgrid(np.arange(NA_BIAS_PAIRS), np.arange(2), np.arange(2), indexing="ij")
    src = np.clip(e_i - NA_BIAS_ZERO + uk_i - wq_i + NA_ROWS - 1, 0, n_r - 1)
    pick = jnp.asarray((src[..., None] == np.arange(n_r)).astype(np.float32))
    rows = jnp.einsum("euwr,hrj->heuwj", pick, rpb, precision=HIGHEST)
    ck = np.arange(GRID_W)[:, None]
    cq = np.arange(GRID_W)[None, :]
    ci = np.clip(ck - cq, -(NA_COLS - 1), NA_COLS - 1) + NA_COLS - 1
    sel = (ci[None] == np.arange(n_c)[:, None, None]).astype(np.float32)
    sel2 = np.zeros((2, n_c, GRID_W, 2, GRID_W), np.float32)
    for wq in range(2):
        sel2[wq, :, :, wq, :] = sel
    sel2 = jnp.asarray(sel2.reshape(2 * n_c, GRID_W, 2 * GRID_W))
    rows = rows.reshape(rpb.shape[0], NA_BIAS_PAIRS, 2, 2 * n_c)
    tab = jnp.einsum("heuj,jkl->heukl", rows, sel2, precision=HIGHEST)
    return tab.reshape(rpb.shape[0], NA_BIAS_PAIRS, 2 * GRID_W, 2 * GRID_W)


def _ctx_attn_kernel(sink_ref, r_ref, p_ref, ow_ref, on_ref):
    l = p_ref.shape[0]
    n_cols = SEG // LANES
    k_all = r_ref[:, R_KVW * SEG:R_KVW * SEG + LANES]
    v_all = r_ref[:, R_KVW * SEG + LANES:R_KVW * SEG + 2 * LANES]
    qs = jnp.concatenate([r_ref[:, R_QW * SEG + c * LANES:R_QW * SEG + (c + 1) * LANES] for c in range(n_cols)], axis=0)
    outs = []
    for g in range(WIN_KV_HEADS):
        s = _dot_nt(qs, _half_mask(k_all, g))
        sink = jnp.concatenate([jnp.full((l, 1), sink_ref[n_cols * g + c], F32) for c in range(n_cols)], axis=0)
        outs.append(_softmax_pv(s, _half_mask(v_all, g), sink))
    o = outs[0] + outs[1]
    ow_ref[...] = jnp.concatenate([o[c * l:(c + 1) * l, :] for c in range(n_cols)], axis=1).astype(BF16)
    for pair in range(NA_HEADS // 2):
        sl = lambda t: slice(t * SEG + pair * LANES, t * SEG + (pair + 1) * LANES)
        q, k, v = p_ref[:, sl(T_QN)], p_ref[:, sl(T_KN)], p_ref[:, sl(T_VN)]
        out = None
        for u in range(2):
            o = _softmax_pv(_dot_nt(q, _half_mask(k, u)), _half_mask(v, u))
            out = o if out is None else out + o
        on_ref[:, pair * LANES:(pair + 1) * LANES] = out.astype(BF16)


def _ctx_attention(r_ctx, p_ctx, sink):
    l = p_ctx.shape[0]
    return pl.pallas_call(
        _ctx_attn_kernel,
        in_specs=[pl.BlockSpec(memory_space=pltpu.SMEM), pl.BlockSpec(r_ctx.shape, lambda: (0, 0)),
                  pl.BlockSpec(p_ctx.shape, lambda: (0, 0))],
        out_specs=[pl.BlockSpec((l, SEG), lambda: (0, 0)), pl.BlockSpec((l, SEG), lambda: (0, 0))],
        out_shape=[jax.ShapeDtypeStruct((l, SEG), BF16), jax.ShapeDtypeStruct((l, SEG), BF16)],
        compiler_params=pltpu.CompilerParams(vmem_limit_bytes=VMEM_LIMIT),
        name="context_attention",
    )(sink, r_ctx, p_ctx)


def _merge_kernel(yf_ref, yb_ref, gr_ref, gn_ref, yw_ref, yn_ref, ga_ref, gb_ref, gc_ref,
                  wr_ref, ww_ref, wn_ref, wo_ref, x_ref, m2_ref, gf_ref, m3_ref, m4_ref, wrt_ref,
                  xo_ref, hx_ref, lt_ref):
    y = yf_ref[...] + yb_ref[...]
    parts = []
    for h in range(RET_HEADS):
        yh = y[:, h * RET_D:(h + 1) * RET_D]
        mu = jnp.mean(yh, axis=-1, keepdims=True)
        var = jnp.mean(jnp.square(yh - mu), axis=-1, keepdims=True)
        parts.append((yh - mu) * lax.rsqrt(var + EPS))
    g = gr_ref[...].astype(F32)
    ya = jnp.concatenate(parts, axis=1) * gn_ref[...] * (g * jax.nn.sigmoid(g))
    za = _dot(ya.astype(BF16), wr_ref[...])
    zb = _dot(yw_ref[...], ww_ref[...])
    zc = _dot(yn_ref[...], wn_ref[...])
    sig = lambda r: jax.nn.sigmoid(r[...].astype(F32))
    mix = sig(ga_ref) * za + sig(gb_ref) * zb + sig(gc_ref) * zc
    x_new = x_ref[...] + m2_ref[...] * _dot(mix.astype(BF16), wo_ref[...])
    xo_ref[...] = x_new
    h2 = _norm_mod(x_new, gf_ref[...], m3_ref[...], m4_ref[...])
    n_sub = h2.shape[1] // LANES
    for s in range(n_sub):
        hx_ref[pl.ds(s, h2.shape[0], stride=n_sub), :] = h2[:, s * LANES:(s + 1) * LANES]
    lt_ref[...] = _dot_nt(wrt_ref[...], h2, precision=HIGHEST)


def _merge(yf, yb, p, gn, yw, yn, wr, ww, wn, wo, x, m2, gf, m3, m4, wrt, *, tm):
    n, d = x.shape
    row = lambda wdt, t: pl.BlockSpec((tm, wdt), lambda i: (i, t))
    full = lambda a: pl.BlockSpec(a.shape, lambda i: (0,) * a.ndim)
    gate0 = T_GATES * SEG // d
    return pl.pallas_call(
        _merge_kernel,
        grid=(n // tm,),
        in_specs=[row(SEG, 0), row(SEG, 0), row(SEG, T_GR), full(gn), row(SEG, 0), row(SEG, 0),
                  row(d, gate0), row(d, gate0 + 1), row(d, gate0 + 2),
                  full(wr), full(ww), full(wn), full(wo), row(d, 0), full(m2), full(gf), full(m3), full(m4), full(wrt)],
        out_specs=[pl.BlockSpec((tm, d), lambda i: (i, 0)),
                   pl.BlockSpec((tm * (d // LANES), LANES), lambda i: (i, 0)),
                   pl.BlockSpec((N_EXPERTS, tm), lambda i: (0, i))],
        out_shape=[jax.ShapeDtypeStruct((n, d), F32), jax.ShapeDtypeStruct((n * (d // LANES), LANES), F32),
                   jax.ShapeDtypeStruct((N_EXPERTS, n), F32)],
        compiler_params=_cparams(("arbitrary",)),
        name="merge",
    )(yf, yb, p, gn, yw, yn, p, p, p, wr, ww, wn, wo, x, m2, gf, m3, m4, wrt)


def _route_kernel(lt_ref, idx_ref, gate_ref, pos_ref, goff_ref, aff_ref, thr_ref, *, cap, n_groups):
    e = pl.program_id(0)
    n_exp, ag, _ = lt_ref.shape
    capp = idx_ref.shape[-1]

    @pl.when(e == 0)
    def _():
        lt = lt_ref[...]
        ex = jnp.exp(lt - jnp.max(lt, axis=0, keepdims=True))
        aff = ex / jnp.sum(ex, axis=0, keepdims=True)
        real = _iota(aff.shape, 1) < n_groups
        aff = jnp.where(real, aff, 0.0)
        aff_ref[...] = aff

        def count_ge(v):
            return jnp.sum(jnp.sum((aff >= v).astype(F32), axis=2, keepdims=True), axis=1, keepdims=True)

        def body(k, carry):
            lo, hi = carry
            mid = jnp.where(lo > 0.0, jnp.sqrt(lo) * jnp.sqrt(hi), hi * 2.0 ** -16)
            ge = count_ge(mid) >= float(cap)
            return jnp.where(ge, mid, lo), jnp.where(ge, hi, mid)

        lo, _ = lax.fori_loop(0, BISECT_STEPS, body, (jnp.zeros((n_exp, 1, 1), F32), jnp.full((n_exp, 1, 1), 2.0, F32)))
        big = jnp.where(aff >= lo, aff, 4.0)
        t = jnp.min(jnp.min(big, axis=2, keepdims=True), axis=1, keepdims=True)
        thr_ref[...] = jnp.broadcast_to(t, thr_ref.shape)

    a = aff_ref[e]
    t = thr_ref[e][0:1, :]
    gt = a > t
    eq = (a == t) & (_iota(a.shape, 0) < n_groups)
    tri_lane_strict = (_iota((LANES, LANES), 0) < _iota((LANES, LANES), 1)).astype(BF16)
    tri_lane_incl = (_iota((LANES, LANES), 0) <= _iota((LANES, LANES), 1)).astype(BF16)
    tri_grp_strict = (_iota((ag, ag), 1) < _iota((ag, ag), 0)).astype(BF16)
    tri_grp_incl = (_iota((ag, ag), 1) <= _iota((ag, ag), 0)).astype(BF16)

    def total(mask_f):
        return jnp.sum(jnp.sum(mask_f, axis=1, keepdims=True), axis=0, keepdims=True)

    def group_sum(mask_f):
        return jnp.broadcast_to(jnp.sum(mask_f, axis=1, keepdims=True), (ag, LANES)).astype(BF16)

    eq_f = eq.astype(F32)
    need = float(cap) - total(gt.astype(F32))
    rank_eq = _dot(tri_grp_strict, group_sum(eq_f)) + _dot(eq_f.astype(BF16), tri_lane_strict)
    sel = gt | (eq & (rank_eq < need))
    sel_f = sel.astype(F32)
    cl = _dot(sel_f.astype(BF16), tri_lane_incl)
    cg = _dot(tri_grp_incl, group_sum(sel_f))
    goff = cg - jnp.broadcast_to(jnp.sum(sel_f, axis=1, keepdims=True), (ag, LANES))
    pos_ref[...] = jnp.where(sel, goff + cl - 1.0, -1.0)
    diag = _iota((ag, ag), 0) == _iota((ag, ag), 1)
    goff_sq = goff if ag == LANES else goff[:, :ag]
    goff_ref[...] = jnp.sum(jnp.where(diag, goff_sq, 0.0), axis=0, keepdims=True).astype(I32)

    pp = _iota((ag, capp), 1, F32)
    cg_b = jnp.broadcast_to(cg[:, 0:1], (ag, capp))
    below = cg_b <= pp
    grp = jnp.sum(below.astype(F32), axis=0, keepdims=True)
    off = jnp.max(jnp.where(below, cg_b, 0.0), axis=0, keepdims=True)
    onehot = _iota((ag, capp), 0, F32) == grp
    in_grp = _dot_tn(cl.astype(BF16), onehot.astype(BF16))
    local = pp[0:1, :] - off
    lane_of = jnp.sum((in_grp <= local).astype(F32), axis=0, keepdims=True)
    live = pp[0:1, :] < float(cap)
    idx = jnp.where(live, grp * float(LANES) + lane_of, 0.0)
    idx_ref[...] = idx.astype(I32)
    aff_grp = _dot_tn(a, onehot.astype(F32), precision=HIGHEST)
    pick = _iota((LANES, capp), 0, F32) == lane_of
    gate = jnp.sum(jnp.where(pick, aff_grp, 0.0), axis=0, keepdims=True)
    gate_ref[...] = jnp.where(live, gate, 0.0)


def _route(logits_t, *, cap, n_groups):
    n_exp, ag, _ = logits_t.shape
    capp = -(-cap // LANES) * LANES
    return pl.pallas_call(
        functools.partial(_route_kernel, cap=cap, n_groups=n_groups),
        grid=(n_exp,),
        in_specs=[pl.BlockSpec(logits_t.shape, lambda e: (0, 0, 0))],
        out_specs=[pl.BlockSpec((None, 1, capp), lambda e: (e, 0, 0)),
                   pl.BlockSpec((None, 1, capp), lambda e: (e, 0, 0)),
                   pl.BlockSpec((None, ag, LANES), lambda e: (e, 0, 0)),
                   pl.BlockSpec((None, 1, ag), lambda e: (e, 0, 0))],
        out_shape=[jax.ShapeDtypeStruct((n_exp, 1, capp), I32), jax.ShapeDtypeStruct((n_exp, 1, capp), F32),
                   jax.ShapeDtypeStruct((n_exp, ag, LANES), F32), jax.ShapeDtypeStruct((n_exp, 1, ag), I32)],
        scratch_shapes=[pltpu.VMEM((n_exp, ag, LANES), F32), pltpu.VMEM((n_exp, SUBLANES, LANES), F32)],
        compiler_params=_cparams(("arbitrary",)),
        name="route",
    )(logits_t)


def _ffn_kernel(idx_ref, idxn_ref, gate_ref, hx_ref, wg_ref, wu_ref, wd_ref, o_ref, xbuf, sem, wgb, wub, wdb, *, tm):
    e = pl.program_id(0)
    j = pl.program_id(1)
    nt = pl.num_programs(1)
    step = e * nt + j
    last = pl.num_programs(0) * nt - 1

    sub = SUBLANES
    rows = tm * sub

    def row_copy(rows_ref, r, slot):
        src = hx_ref.at[pl.ds(pl.multiple_of(rows_ref[0, 0, r] * sub, sub), sub)]
        dst = xbuf.at[pl.ds(pl.multiple_of(slot * rows + r * sub, sub), sub)]
        return pltpu.make_async_copy(src, dst, sem.at[slot])

    def slot_copy(slot):
        return pltpu.make_async_copy(hx_ref.at[pl.ds(0, rows)], xbuf.at[pl.ds(pl.multiple_of(slot * rows, sub), rows)],
                                     sem.at[slot])

    @pl.when(step == 0)
    def _():
        def body(r, carry):
            row_copy(idx_ref, r, 0).start()
            return carry
        lax.fori_loop(0, tm, body, 0, unroll=8)

    @pl.when(j == 0)
    def _():
        wgb[...] = wg_ref[...].astype(BF16)
        wub[...] = wu_ref[...].astype(BF16)
        wdb[...] = wd_ref[...].astype(BF16)

    slot = step % 2
    nslot = 1 - slot
    for r in range(tm):
        row_copy(idxn_ref, r, nslot).start()
    slot_copy(slot).wait()
    base = slot * rows
    x = jnp.concatenate([xbuf[pl.ds(base + s, tm, stride=sub), :] for s in range(sub)], axis=1).astype(BF16)
    g = _dot(x, wgb[...])
    u = _dot(x, wub[...])
    hid = (g * jax.nn.sigmoid(g)) * u
    y = _dot(hid.astype(BF16), wdb[...])
    eye = _iota((tm, tm), 0) == _iota((tm, tm), 1)
    gcol = jnp.sum(jnp.where(eye, jnp.broadcast_to(gate_ref[0], (tm, tm)), 0.0), axis=1, keepdims=True)
    o_ref[...] = (y * gcol).astype(BF16)

    @pl.when(step == last)
    def _():
        slot_copy(nslot).wait()


def _expert_ffn(idx, gate, hx3, w_gate, w_up, w_down, *, layer, cap, tm):
    n_exp = idx.shape[0]
    d = w_gate.shape[2]
    f = w_gate.shape[3]
    assert d == SUBLANES * LANES, "a token row must be exactly one (8, 128) f32 tile"
    nt = cap // tm
    idx_t = idx[:, :, :cap].reshape(n_exp * nt, 1, tm)
    gate_t = gate[:, :, :cap].reshape(n_exp * nt, 1, tm)
    n_steps = n_exp * nt
    wspec = lambda a, b: pl.BlockSpec((None, None, a, b), lambda e, j: (layer, e, 0, 0))
    return pl.pallas_call(
        functools.partial(_ffn_kernel, tm=tm),
        grid=(n_exp, nt),
        in_specs=[pl.BlockSpec((1, 1, tm), lambda e, j: (e * nt + j, 0, 0), memory_space=pltpu.SMEM),
                  pl.BlockSpec((1, 1, tm), lambda e, j: (jnp.minimum(e * nt + j + 1, n_steps - 1), 0, 0), memory_space=pltpu.SMEM),
                  pl.BlockSpec((1, 1, tm), lambda e, j: (e * nt + j, 0, 0)),
                  pl.BlockSpec(memory_space=pl.ANY),
                  wspec(d, f), wspec(d, f), wspec(f, d)],
        out_specs=pl.BlockSpec((tm, d), lambda e, j: (e * nt + j, 0)),
        out_shape=jax.ShapeDtypeStruct((n_exp * cap, d), BF16),
        scratch_shapes=[pltpu.VMEM((2 * tm * SUBLANES, LANES), F32), pltpu.SemaphoreType.DMA((2,)),
                        pltpu.VMEM((d, f), BF16), pltpu.VMEM((d, f), BF16), pltpu.VMEM((f, d), BF16)],
        compiler_params=_cparams(("arbitrary", "arbitrary")),
        name="expert_ffn",
    )(idx_t, idx_t, gate_t, hx3, w_gate, w_up, w_down)


def _combine_kernel(ws_ref, nw_ref, *refs, n_exp, cap, win, gpt, final_norm):
    y_refs = refs[:n_exp]
    pos_ref, ye_ref, x_ref, m5_ref, gf_ref, o_ref, acc_ref, xwin, sem = refs[n_exp:]
    a = pl.program_id(0)
    ng = pl.num_programs(0)
    per = max(1, MXU_DEPTH // win)

    def slots(e):
        pos = pos_ref[e, pl.ds(a * gpt, gpt), :]
        return jnp.concatenate([jnp.broadcast_to(pos[j:j + 1, :], (win, LANES)) for j in range(gpt)], axis=1)

    lp = _iota((win, gpt * LANES), 0, F32)
    acc = None
    for e0 in range(0, n_exp, per):
        es = range(e0, min(e0 + per, n_exp))
        hot = jnp.concatenate([((slots(e) - (ws_ref[e * ng + a] - e * cap).astype(F32)) == lp).astype(BF16) for e in es],
                              axis=0)
        rows = jnp.concatenate([y_refs[e][...] for e in es], axis=0)
        part = _dot_tn(hot, rows)
        acc = part if acc is None else acc + part
    acc_ref[...] = acc

    for e in range(n_exp):
        first = ws_ref[e * ng + a] - e * cap

        def extra(k, carry, e=e, first=first):
            lo = first + k * win
            row = jnp.minimum(lo, cap - win)
            cp = pltpu.make_async_copy(ye_ref.at[pl.ds(pl.multiple_of(e * cap + row, BF16_ROWS), win)], xwin, sem)
            cp.start()
            cp.wait()
            pos = slots(e)
            hot = ((pos - row.astype(F32)) == lp) & (pos >= lo.astype(F32))
            acc_ref[...] += _dot_tn(hot.astype(BF16), xwin[...])
            return carry

        lax.fori_loop(1, nw_ref[e * ng + a], extra, 0)

    x_new = x_ref[...] + m5_ref[...] * acc_ref[...]
    if final_norm:
        ms = jnp.mean(x_new * x_new, axis=-1, keepdims=True)
        x_new = x_new * lax.rsqrt(ms + EPS) * gf_ref[...]
    o_ref[...] = x_new


def _combine(ye, posmap, goff, x, m5, g_final, *, cap, final_norm):
    n, d = x.shape
    n_exp = posmap.shape[0]
    gpt = min(COMBINE_GROUPS, n // LANES)
    ng = n // (gpt * LANES)
    tok = gpt * LANES
    win = min(COMBINE_WINDOW, cap)
    goff = goff[:, 0, :ng * gpt:gpt]
    end = jnp.concatenate([goff[:, 1:], jnp.full((n_exp, 1), cap, I32)], axis=1)
    start = jnp.minimum((goff // BF16_ROWS) * BF16_ROWS, cap - win)
    n_win = jnp.maximum((end - start + win - 1) // win, 1).reshape(-1)
    wstart = (start + jnp.arange(n_exp, dtype=I32)[:, None] * cap).reshape(-1)
    yspec = lambda e: pl.BlockSpec((pl.Element(win), pl.Element(d)),
                                   lambda a, ws, nw: (pl.multiple_of(ws[e * ng + a], BF16_ROWS), 0))
    gs = pltpu.PrefetchScalarGridSpec(
        num_scalar_prefetch=2,
        grid=(ng,),
        in_specs=[yspec(e) for e in range(n_exp)]
        + [pl.BlockSpec(posmap.shape, lambda a, ws, nw: (0, 0, 0)),
           pl.BlockSpec(memory_space=pl.ANY),
           pl.BlockSpec((tok, d), lambda a, ws, nw: (a, 0)),
           pl.BlockSpec((1, d), lambda a, ws, nw: (0, 0)), pl.BlockSpec((1, d), lambda a, ws, nw: (0, 0))],
        out_specs=pl.BlockSpec((tok, d), lambda a, ws, nw: (a, 0)),
        scratch_shapes=[pltpu.VMEM((tok, d), F32), pltpu.VMEM((win, d), BF16), pltpu.SemaphoreType.DMA(())],
    )
    return pl.pallas_call(
        functools.partial(_combine_kernel, n_exp=n_exp, cap=cap, win=win, gpt=gpt, final_norm=final_norm),
        grid_spec=gs,
        out_shape=jax.ShapeDtypeStruct((n, d), F32),
        compiler_params=_cparams(("arbitrary",)),
        name="combine_final" if final_norm else "combine",
    )(wstart, n_win, *([ye] * n_exp), posmap, ye, x, m5, g_final)


def _rope_tables(n):
    n_rows = n // GRID_W
    row = jnp.arange(n_rows).astype(F32)
    col = jnp.arange(GRID_W).astype(F32)

    def tables(d, signs):
        nf = d // 4
        inv = ROPE_BASE ** (-jnp.arange(nf, dtype=F32) / nf)
        ang_r, ang_c = row[:, None] * inv, col[:, None] * inv
        reps = LANES // (2 * nf)

        def lanes(f_r, f_c, factors):
            z_r, z_c = jnp.zeros_like(f_r), jnp.zeros_like(f_c)
            pat_r = jnp.concatenate([part for k in range(reps) for part in (f_r * factors[2 * k], z_r)], axis=1)
            pat_c = jnp.concatenate([part for k in range(reps) for part in (z_c, f_c * factors[2 * k + 1])], axis=1)
            return (pat_r[:, None, :] + pat_c[None, :, :]).reshape(n, LANES)

        ones = (1.0,) * (2 * reps)
        cos = lanes(jnp.cos(ang_r), jnp.cos(ang_c), ones)
        return [cos] + [lanes(jnp.sin(ang_r), jnp.sin(ang_c), sg) for sg in signs]

    c128, s128 = tables(RET_D, [(-1.0, -1.0, 1.0, 1.0)])
    c64, sa, sb = tables(HEAD_D, [(-1.0, -1.0, 0.0, 0.0) * 2, (0.0, 0.0, 1.0, 1.0) * 2])
    return c128, s128, c64, sa, sb


WIN_HEAD_ORDER = (0, 4, 1, 5, 2, 6, 3, 7)


def _permute_in_weight(w):
    d = w.shape[0]
    sizes = (("q_r", 512), ("k_r", 512), ("v_r", 512), ("g_r", 512), ("q_w", 512), ("k_w", 128), ("v_w", 128),
             ("q_n", 512), ("k_n", 512), ("v_n", 512), ("gates", 3 * d))
    off, lay = 0, {}
    for name, size in sizes:
        lay[name] = (off, off + size)
        off += size
    seg = lambda name: w[:, lay[name][0]:lay[name][1]]
    q0 = lay["q_w"][0]
    q_w = [w[:, q0 + h * HEAD_D:q0 + (h + 1) * HEAD_D] for h in WIN_HEAD_ORDER]
    pad = jnp.zeros((d, SEG - 2 * LANES), w.dtype)
    rope = jnp.concatenate([seg("q_r"), seg("k_r")] + q_w + [seg("k_w"), seg("v_w"), pad], axis=1).astype(BF16)
    plain = jnp.concatenate([seg("gates"), seg("v_r"), seg("g_r"), seg("q_n"), seg("k_n"), seg("v_n")], axis=1).astype(BF16)
    return rope, plain


def _permute_win_rows(w):
    return jnp.concatenate([w[h * HEAD_D:(h + 1) * HEAD_D] for h in WIN_HEAD_ORDER], axis=0).astype(BF16)


def kernel(x, c, ctx, c_ctx, w_mod, b_mod, g_mix, g_ffn, w_in, ret_decay_logit, ret_gn, w_ret, win_sink, w_win, na_rpb,
           w_na, w_out, w_router, w_exp_gate, w_exp_up, w_exp_down, g_final):
    _, n, d = x.shape
    l = ctx.shape[1]
    depth = w_in.shape[0]
    xs, cs = x[0], ctx[0]
    cc = jnp.zeros((SUBLANES, d), F32).at[0].set(c[0]).at[1].set(c_ctx)
    mods = _modulation(cc, w_mod, b_mod)
    c128, s128, c64, sa64, sb64 = _rope_tables(n)
    k_scale = RET_D ** -0.5
    plain_scales = jnp.ones((PLAIN_TILES,), F32).at[T_QN].set(Q_SCALE)
    ctx_rope_scales = jnp.ones((R_KVW + 1,), F32).at[R_KR].set(k_scale).at[R_QW].set(Q_SCALE)
    cap_x = CAPACITY_FACTOR * n // N_EXPERTS
    cap_c = CAPACITY_FACTOR * l // N_EXPERTS
    grp_c = -(-(l // LANES) // BF16_ROWS) * BF16_ROWS
    vec = lambda v: v.reshape(1, -1)

    for layer in range(depth):
        need_ctx = layer < depth - 1
        last = layer == depth - 1
        mx = [vec(mods[layer, 0, k * d:(k + 1) * d]) for k in range(N_MOD)]
        mc = [vec(mods[layer, 1, k * d:(k + 1) * d]) for k in range(N_MOD)]
        w_rope, w_plain = _permute_in_weight(w_in[layer])
        wr = w_ret[layer].astype(BF16)
        ww = _permute_win_rows(w_win[layer])
        wn = w_na[layer].astype(BF16)
        wo = w_out[layer].astype(BF16)
        wrt = w_router[layer].T
        sink = win_sink[layer].astype(F32) * LOG2E
        lg_rows = jnp.broadcast_to(ret_decay_logit[layer].astype(F32).reshape(2 * RET_HEADS, 1), (2 * RET_HEADS, LANES))
        gn = vec(ret_gn[layer])
        bias_tab = _na_bias_table(na_rpb[layer].astype(F32) * LOG2E)
        gmix, gffn = vec(g_mix[layer]), vec(g_ffn[layer])

        hx = _norm_call(xs, gmix, mx[0], mx[1], tm=1024)
        hc = _norm_call(cs, gmix, mc[0], mc[1], tm=l)
        p_x = _proj(hx, w_plain, plain_scales, tm=2048)
        p_c = _proj(hc, w_plain, plain_scales, tm=l)
        r_c = _proj(hc, w_rope, ctx_rope_scales, tm=l)
        rope = functools.partial(_proj_rope, hx, w_rope, tm=2048)
        q_r = rope(R_QR, (c128, s128), head_d=RET_D, scale=1.0, n_rot=SEG // LANES, name="proj_q_ret")
        k_r = rope(R_KR, (c128, s128), head_d=RET_D, scale=k_scale, n_rot=SEG // LANES, name="proj_k_ret")
        q_w = rope(R_QW, (c64, sa64, sb64), head_d=HEAD_D, scale=Q_SCALE, n_rot=SEG // LANES, name="proj_q_win")
        kv_w = rope(R_KVW, (c64, sa64, sb64), head_d=HEAD_D, scale=1.0, n_rot=1, name="proj_kv_win")

        yf, yb = _retention((q_r, 0), (k_r, 0), (p_x, T_VR), (r_c, R_KR), (p_c, T_VR), lg_rows, zero_init=False)
        yw = _window_attention((q_w, 0), (kv_w, 0), (r_c, R_KVW), sink)
        yn = _na_attention(p_x, p_c, bias_tab)
        xs, hx3, lt = _merge(yf, yb, p_x, gn, yw, yn, wr, ww, wn, wo, xs, mx[2], gffn, mx[3], mx[4], wrt, tm=512)

        idx, gate, posmap, goff = _route(lt.reshape(N_EXPERTS, n // LANES, LANES), cap=cap_x, n_groups=n // LANES)
        ye = _expert_ffn(idx, gate, hx3, w_exp_gate, w_exp_up, w_exp_down, layer=layer, cap=cap_x, tm=512)
        xs = _combine(ye, posmap, goff, xs, mx[5], vec(g_final), cap=cap_x, final_norm=last)

        if need_ctx:
            yfc, ybc = _retention((r_c, R_QR), (r_c, R_KR), (p_c, T_VR), (r_c, R_KR), (p_c, T_VR), lg_rows, zero_init=True)
            ywc, ync = _ctx_attention(r_c, p_c, sink)
            cs, hc3, ltc = _merge(yfc, ybc, p_c, gn, ywc, ync, wr, ww, wn, wo, cs, mc[2], gffn, mc[3], mc[4], wrt, tm=l)
            ltc = jnp.pad(ltc.reshape(N_EXPERTS, l // LANES, LANES), ((0, 0), (0, grp_c - l // LANES), (0, 0)))
            idc, gtc, posc, goffc = _route(ltc, cap=cap_c, n_groups=l // LANES)
            yec = _expert_ffn(idc, gtc, hc3, w_exp_gate, w_exp_up, w_exp_down, layer=layer, cap=cap_c, tm=cap_c)
            cs = _combine(yec, posc, goffc, cs, mc[5], vec(g_final), cap=cap_c, final_norm=False)

    return xs[None]
```

```python
import functools

import jax
import jax.numpy as jnp
import numpy as np
from jax import lax
from jax.experimental import pallas as pl
from jax.experimental.pallas import tpu as pltpu

F32, BF16, I32 = jnp.float32, jnp.bfloat16, jnp.int32
HIGHEST = lax.Precision.HIGHEST

GRID_W = 64
RET_HEADS, RET_D, RET_CHUNK = 4, 128, 128
WIN_HEADS, WIN_KV_HEADS, HEAD_D, WINDOW, WIN_BLOCK = 8, 2, 64, 128, 128
NA_HEADS, NA_ROWS, NA_COLS = 8, 8, 16
N_EXPERTS, CAPACITY_FACTOR = 16, 2
N_MOD = 6
ROPE_BASE = 10000.0
EPS = 1e-6
NEG_INF = -1e30

LANES = 128
SUBLANES = 8
BF16_ROWS = 16
VMEM_LIMIT = 56 * 1024 * 1024

SEG = 512
T_GATES, T_VR, T_GR, T_QN, T_KN, T_VN = 0, 6, 7, 8, 9, 10
PLAIN_TILES = 11
R_QR, R_KR, R_QW, R_KVW = 0, 1, 2, 3
LOG2E = 1.4426950408889634
Q_SCALE = HEAD_D ** -0.5 * LOG2E
NORM_ROWS = 1024
PROJ_ROWS = 2048
MERGE_ROWS = 512
FFN_ROWS = 512
RET_BLOCK = 4 * RET_CHUNK
WIN_QBLOCKS = 2
NA_QROWS = 4
NA_SLAB_ROWS = 12
NA_BIAS_ZERO = NA_SLAB_ROWS - 2
NA_BIAS_PAIRS = NA_BIAS_ZERO + NA_SLAB_ROWS - 1
COMBINE_GROUPS = 4
COMBINE_WINDOW = 128
MXU_DEPTH = 256
BISECT_STEPS = 48


def _cparams(sem):
    return pltpu.CompilerParams(dimension_semantics=sem, vmem_limit_bytes=VMEM_LIMIT)


def _dot(a, b):
    return jnp.dot(a, b, preferred_element_type=F32)


def _dot_nt(a, b, precision=None):
    return lax.dot_general(a, b, (((1,), (1,)), ((), ())), precision=precision, preferred_element_type=F32)


def _dot_tn(a, b, precision=None):
    return lax.dot_general(a, b, (((0,), (0,)), ((), ())), precision=precision, preferred_element_type=F32)


def _iota(shape, dim, dtype=I32):
    return lax.broadcasted_iota(I32, shape, dim).astype(dtype)


def _mod_kernel(s_ref, w_ref, b_ref, o_ref):
    s = s_ref[...]
    s = s * jax.nn.sigmoid(s)
    o_ref[...] = jnp.dot(s, w_ref[...], precision=HIGHEST, preferred_element_type=F32) + b_ref[...]


def _modulation(cc, w_mod, b_mod):
    depth, d, md = w_mod.shape
    tn = 1536
    return pl.pallas_call(
        _mod_kernel,
        grid=(depth, md // tn),
        in_specs=[
            pl.BlockSpec((SUBLANES, d), lambda l, j: (0, 0)),
            pl.BlockSpec((None, d, tn), lambda l, j: (l, 0, j)),
            pl.BlockSpec((None, 1, tn), lambda l, j: (l, 0, j)),
        ],
        out_specs=pl.BlockSpec((None, SUBLANES, tn), lambda l, j: (l, 0, j)),
        out_shape=jax.ShapeDtypeStruct((depth, SUBLANES, md), F32),
        compiler_params=_cparams(("arbitrary", "arbitrary")),
        name="modulation",
    )(cc, w_mod, b_mod.reshape(depth, 1, md))


def _norm_mod(x, g, shift, scale):
    ms = jnp.mean(x * x, axis=-1, keepdims=True)
    y = x * lax.rsqrt(ms + EPS) * g
    return y * (1.0 + scale) + shift


def _norm_kernel(x_ref, g_ref, sh_ref, sc_ref, o_ref):
    o_ref[...] = _norm_mod(x_ref[...], g_ref[...], sh_ref[...], sc_ref[...]).astype(BF16)


def _norm_call(x, g, shift, scale, *, tm):
    n, d = x.shape
    vec = lambda: pl.BlockSpec((1, d), lambda i: (0, 0))
    return pl.pallas_call(
        _norm_kernel,
        grid=(n // tm,),
        in_specs=[pl.BlockSpec((tm, d), lambda i: (i, 0)), vec(), vec(), vec()],
        out_specs=pl.BlockSpec((tm, d), lambda i: (i, 0)),
        out_shape=jax.ShapeDtypeStruct((n, d), BF16),
        compiler_params=_cparams(("arbitrary",)),
        name="norm_modulate",
    )(x, g, shift, scale)


def _proj_kernel(scale_ref, hx_ref, w_ref, o_ref):
    o_ref[...] = (_dot(hx_ref[...], w_ref[...]) * scale_ref[pl.program_id(1)]).astype(BF16)


def _proj(hx, w, scales, *, tm):
    n, d = hx.shape
    tiles = w.shape[1] // SEG
    return pl.pallas_call(
        _proj_kernel,
        grid=(n // tm, tiles),
        in_specs=[pl.BlockSpec(memory_space=pltpu.SMEM),
                  pl.BlockSpec((tm, d), lambda i, j: (i, 0)), pl.BlockSpec((d, SEG), lambda i, j: (0, j))],
        out_specs=pl.BlockSpec((tm, SEG), lambda i, j: (i, j)),
        out_shape=jax.ShapeDtypeStruct((n, tiles * SEG), BF16),
        compiler_params=_cparams(("arbitrary", "arbitrary")),
        name="projection",
    )(scales, hx, w)


def _proj_rope_kernel(hx_ref, w_ref, *refs, head_d, scale, n_rot):
    tabs, o_ref = refs[:-1], refs[-1]
    acc = _dot(hx_ref[...], w_ref[...])
    if head_d == RET_D:
        c = tabs[0][...] * scale
        s = tabs[1][...] * scale
        rot = lambda a: a * c + pltpu.roll(a, 64, 1) * s
    else:
        c = tabs[0][...] * scale
        sa = tabs[1][...] * scale
        sb = tabs[2][...] * scale
        rot = lambda a: a * c + pltpu.roll(a, 96, 1) * sa + pltpu.roll(a, 32, 1) * sb
    groups = [acc[:, k * LANES:(k + 1) * LANES] for k in range(SEG // LANES)]
    o_ref[...] = jnp.concatenate([rot(a) if k < n_rot else a for k, a in enumerate(groups)], axis=1).astype(BF16)


def _proj_rope(hx, w, tile, tabs, *, head_d, scale, n_rot, tm, name):
    n, d = hx.shape
    return pl.pallas_call(
        functools.partial(_proj_rope_kernel, head_d=head_d, scale=scale, n_rot=n_rot),
        grid=(n // tm,),
        in_specs=[pl.BlockSpec((tm, d), lambda i: (i, 0)), pl.BlockSpec((d, SEG), lambda i: (0, tile))]
        + [pl.BlockSpec((tm, LANES), lambda i: (i, 0)) for _ in tabs],
        out_specs=pl.BlockSpec((tm, SEG), lambda i: (i, 0)),
        out_shape=jax.ShapeDtypeStruct((n, SEG), BF16),
        compiler_params=_cparams(("arbitrary",)),
        name=name,
    )(hx, w, *tabs)


def _ret_kernel(lg_ref, qf_ref, kf_ref, vf_ref, qb_ref, kb_ref, vb_ref, kc_ref, vc_ref, yf_ref, yb_ref,
                rf_ref, rb_ref, dmat_ref, vec_ref, *, zero_init):
    c = pl.program_id(0)
    ch = qf_ref.shape[0]
    dh = RET_D
    states = (rf_ref, rb_ref)

    @pl.when(c == 0)
    def _():
        lg = jax.nn.log_sigmoid(lg_ref[...])
        ii = _iota((ch, ch), 0, F32)
        jj = _iota((ch, ch), 1, F32)
        iv = _iota((ch, dh), 0, F32)
        n_ctx = kc_ref.shape[0]
        mm = _iota((n_ctx, dh), 0, F32)
        for d in range(2):
            for h in range(RET_HEADS):
                row = lg[RET_HEADS * d + h:RET_HEADS * d + h + 1, :]
                l = jnp.concatenate([jnp.broadcast_to(row, (ch, dh))] * (ch // dh), axis=1)
                lv = jnp.broadcast_to(row, (ch, dh))
                if d == 0:
                    diff = ii - jj
                    dmat_ref[d, h] = jnp.where(diff >= 0.0, jnp.exp(jnp.maximum(diff, 0.0) * l), 0.0)
                    vec_ref[d, h, 0] = jnp.exp((ch - 1.0 - iv) * lv)
                    vec_ref[d, h, 1] = jnp.exp((iv + 1.0) * lv)
                else:
                    diff = jj - ii
                    dmat_ref[d, h] = jnp.where(diff >= 1.0, jnp.exp(jnp.maximum(diff, 0.0) * l), 0.0)
                    vec_ref[d, h, 0] = jnp.exp(iv * lv)
                    vec_ref[d, h, 1] = jnp.exp((ch - iv) * lv)
                vec_ref[d, h, 2] = jnp.exp(float(ch) * lv)
                if zero_init:
                    states[d][h] = jnp.zeros((dh, dh), F32)
                else:
                    lc = jnp.broadcast_to(row, (n_ctx, dh))
                    w = jnp.exp((n_ctx - 1.0 - mm) * lc) if d == 0 else jnp.exp(mm * lc)
                    sl = slice(h * dh, (h + 1) * dh)
                    kw = (kc_ref[:, sl].astype(F32) * w).astype(BF16)
                    states[d][h] = _dot_tn(kw, vc_ref[:, sl])

    for d, (q_ref, k_ref, v_ref, y_ref) in enumerate(((qf_ref, kf_ref, vf_ref, yf_ref), (qb_ref, kb_ref, vb_ref, yb_ref))):
        for h in range(RET_HEADS):
            sl = slice(h * dh, (h + 1) * dh)
            q, k, v = q_ref[:, sl], k_ref[:, sl], v_ref[:, sl]
            r = states[d][h]
            a = (_dot_nt(q, k) * dmat_ref[d, h]).astype(BF16)
            inner = _dot(a, v)
            cross = _dot(q, r.astype(BF16)) * vec_ref[d, h, 1]
            y_ref[:, sl] = inner + cross
            kz = (k.astype(F32) * vec_ref[d, h, 0]).astype(BF16)
            states[d][h] = vec_ref[d, h, 2][:dh, :] * r + _dot_tn(kz, v)


def _retention(q, k, v, kc, vc, lg_rows, *, zero_init):
    n = q[0].shape[0]
    blk = min(RET_BLOCK, n)
    nc = n // blk
    w = RET_HEADS * RET_D
    fwd = lambda sg: pl.BlockSpec((blk, w), lambda c: (c, sg[1]))
    bwd = lambda sg: pl.BlockSpec((blk, w), lambda c: (nc - 1 - c, sg[1]))
    ctx = lambda sg: pl.BlockSpec((sg[0].shape[0], w), lambda c: (0, sg[1]))
    return pl.pallas_call(
        functools.partial(_ret_kernel, zero_init=zero_init),
        grid=(nc,),
        in_specs=[pl.BlockSpec((SUBLANES, LANES), lambda c: (0, 0)),
                  fwd(q), fwd(k), fwd(v), bwd(q), bwd(k), bwd(v), ctx(kc), ctx(vc)],
        out_specs=[pl.BlockSpec((blk, w), lambda c: (c, 0)), pl.BlockSpec((blk, w), lambda c: (nc - 1 - c, 0))],
        out_shape=[jax.ShapeDtypeStruct((n, w), F32), jax.ShapeDtypeStruct((n, w), F32)],
        scratch_shapes=[pltpu.VMEM((RET_HEADS, RET_D, RET_D), F32), pltpu.VMEM((RET_HEADS, RET_D, RET_D), F32),
                        pltpu.VMEM((2, RET_HEADS, blk, blk), F32), pltpu.VMEM((2, RET_HEADS, 3, blk, RET_D), F32)],
        compiler_params=_cparams(("arbitrary",)),
        name="retention_ctx" if zero_init else "retention",
    )(lg_rows, q[0], k[0], v[0], q[0], k[0], v[0], kc[0], vc[0])


def _half_mask(x, half):
    lane = _iota(x.shape, 1)
    keep = (lane < HEAD_D) if half == 0 else (lane >= HEAD_D)
    return jnp.where(keep, x, jnp.zeros_like(x))


def _softmax_pv(s, v, extra=None):
    m = jnp.max(s, axis=1, keepdims=True)
    if extra is not None:
        m = jnp.maximum(m, extra)
    p = jnp.exp2(s - m)
    den = jnp.sum(p, axis=1, keepdims=True)
    if extra is not None:
        den = den + jnp.exp2(extra - m)
    return _dot(p.astype(BF16), v) / den


def _softmax_pv_t(st, v, extra=None):
    m = jnp.max(st, axis=0, keepdims=True)
    if extra is not None:
        m = jnp.maximum(m, extra)
    p = jnp.exp2(st - m)
    den = jnp.sum(p, axis=0, keepdims=True)
    if extra is not None:
        den = den + jnp.exp2(extra - m)
    return _dot_tn(v, p.astype(BF16)) / den


def _win_kernel(sink_ref, q_ref, *refs):
    nkb = WIN_QBLOCKS + 2
    k_refs, v_refs = refs[:nkb], refs[nkb:2 * nkb]
    kx_ref, vx_ref, mask_ref, o_ref = refs[2 * nkb:]
    s = pl.program_id(0)
    last = pl.num_programs(0) - 1
    nq = WIN_QBLOCKS * WIN_BLOCK
    nk = nkb * WIN_BLOCK
    k_all = jnp.concatenate([r[...] for r in k_refs] + [kx_ref[...]], axis=0)
    v_all = jnp.concatenate([r[...] for r in v_refs] + [vx_ref[...]], axis=0)
    n_cols = SEG // LANES
    qs = jnp.concatenate([q_ref[:, c * LANES:(c + 1) * LANES] for c in range(n_cols)], axis=0)
    variant = jnp.where(s == 0, 1, 0) + jnp.where(s == last, 2, 0)
    valid = mask_ref[variant] > 0.5
    valid = jnp.concatenate([valid] * n_cols, axis=1)
    out_t = None
    for g in range(WIN_KV_HEADS):
        st = _dot_nt(_half_mask(k_all, g), qs)
        st = jnp.concatenate([jnp.where(valid, st[:nk], NEG_INF), st[nk:]], axis=0)
        sink = jnp.concatenate([jnp.full((1, nq), sink_ref[n_cols * g + c], F32) for c in range(n_cols)], axis=1)
        o = _softmax_pv_t(st, _half_mask(v_all, g), sink)
        out_t = o if out_t is None else out_t + o
    o = out_t.T
    o_ref[...] = jnp.concatenate([o[c * nq:(c + 1) * nq, :] for c in range(n_cols)], axis=1).astype(BF16)


def _win_valid_table():
    nkb = WIN_QBLOCKS + 2
    qpos = np.arange(WIN_QBLOCKS * WIN_BLOCK)[None, :]
    kpos = np.arange(nkb * WIN_BLOCK)[:, None] - WIN_BLOCK
    band = np.abs(kpos - qpos) <= WINDOW
    tabs = []
    for variant in range(4):
        ok = band.copy()
        if variant & 1:
            ok &= kpos >= 0
        if variant & 2:
            ok &= kpos < WIN_QBLOCKS * WIN_BLOCK
        tabs.append(ok)
    return jnp.asarray(np.stack(tabs).astype(np.float32))


def _window_attention(q, kv, kv_ctx, sink):
    n = q[0].shape[0]
    nb = n // WIN_BLOCK
    assert nb % WIN_QBLOCKS == 0
    l = kv_ctx[0].shape[0]
    nkb = WIN_QBLOCKS + 2
    per = SEG // LANES
    blk = lambda off, j: pl.BlockSpec((WIN_BLOCK, LANES),
                                      lambda i: (jnp.clip(WIN_QBLOCKS * i - 1 + j, 0, nb - 1), per * kv[1] + off))
    mask_tab = _win_valid_table()
    return pl.pallas_call(
        _win_kernel,
        grid=(nb // WIN_QBLOCKS,),
        in_specs=[pl.BlockSpec(memory_space=pltpu.SMEM),
                  pl.BlockSpec((WIN_QBLOCKS * WIN_BLOCK, SEG), lambda i: (i, q[1]))]
        + [blk(0, j) for j in range(nkb)] + [blk(1, j) for j in range(nkb)]
        + [pl.BlockSpec((l, LANES), lambda i: (0, per * kv_ctx[1])), pl.BlockSpec((l, LANES), lambda i: (0, per * kv_ctx[1] + 1)),
           pl.BlockSpec(mask_tab.shape, lambda i: (0, 0, 0))],
        out_specs=pl.BlockSpec((WIN_QBLOCKS * WIN_BLOCK, SEG), lambda i: (i, 0)),
        out_shape=jax.ShapeDtypeStruct((n, SEG), BF16),
        compiler_params=_cparams(("arbitrary",)),
        name="window_attention",
    )(sink, q[0], *([kv[0]] * (2 * nkb)), kv_ctx[0], kv_ctx[0], mask_tab)


def _na_slab_start(s, half_rows):
    return jnp.clip((NA_QROWS // 2) * s - NA_ROWS // 4, 0, half_rows - NA_SLAB_ROWS // 2)


def _na_kernel(q_ref, *refs, rows):
    nslab = NA_SLAB_ROWS // 2
    k_refs, v_refs = refs[:nslab], refs[nslab:2 * nslab]
    kx_ref, vx_ref, bias_ref, mask_ref, o_ref = refs[2 * nslab:]
    s = pl.program_id(0)
    last = pl.num_programs(0) - 1
    w = GRID_W
    nq = NA_QROWS * w
    nk = NA_SLAB_ROWS * w
    delta = 2 * _na_slab_start(s, rows // 2) - NA_QROWS * s
    variant = jnp.where(s == 0, 0, jnp.where(s == last, 2, 1))
    valid = mask_ref[variant] > 0.5
    valid = jnp.concatenate([valid, valid], axis=1)
    row = _iota((LANES, nq), 0)
    for pair in range(NA_HEADS // 2):
        sl = slice(pair * LANES, (pair + 1) * LANES)
        q = q_ref[:, sl]
        qs = jnp.concatenate([_half_mask(q, 0), _half_mask(q, 1)], axis=0)
        k_all = jnp.concatenate([r[:, sl] for r in k_refs] + [kx_ref[:, sl]], axis=0)
        v_all = jnp.concatenate([r[:, sl] for r in v_refs] + [vx_ref[:, sl]], axis=0)
        st = _dot_nt(k_all, qs)
        bias = jnp.concatenate(
            [jnp.concatenate([bias_ref[2 * pair + u, delta + 2 * i - a + NA_BIAS_ZERO]
                              for u in range(2) for a in range(0, NA_QROWS, 2)], axis=1) for i in range(nslab)], axis=0)
        s_loc = jnp.where(valid, st[:nk] + bias, NEG_INF)
        o = _softmax_pv_t(jnp.concatenate([s_loc, st[nk:]], axis=0), v_all)
        o_ref[:, sl] = jnp.where(row < HEAD_D, o[:, :nq], o[:, nq:]).T.astype(BF16)


def _na_valid_table():
    kk = np.arange(NA_SLAB_ROWS)[:, None, None, None]
    ck = np.arange(GRID_W)[None, :, None, None]
    a = np.arange(NA_QROWS)[None, None, :, None]
    cq = np.arange(GRID_W)[None, None, None, :]
    c_start = np.clip(cq - NA_COLS // 2, 0, GRID_W - NA_COLS)
    col_ok = (ck >= c_start) & (ck < c_start + NA_COLS)
    first_row = (0 * a, a, 0 * a + NA_SLAB_ROWS - NA_ROWS)
    tabs = [(col_ok & (kk >= f) & (kk < f + NA_ROWS)).reshape(NA_SLAB_ROWS * GRID_W, NA_QROWS * GRID_W) for f in first_row]
    return jnp.asarray(np.stack(tabs).astype(np.float32))


def _na_attention(p, p_ctx, bias_tab):
    n = p.shape[0]
    rows = n // GRID_W
    assert rows % NA_QROWS == 0 and rows >= NA_SLAB_ROWS
    steps = rows // NA_QROWS
    l = p_ctx.shape[0]
    slab = lambda t, i: pl.BlockSpec((2 * GRID_W, SEG), lambda s: (_na_slab_start(s, rows // 2) + i, t))
    nslab = NA_SLAB_ROWS // 2
    mask_tab = _na_valid_table()
    return pl.pallas_call(
        functools.partial(_na_kernel, rows=rows),
        grid=(steps,),
        in_specs=[pl.BlockSpec((NA_QROWS * GRID_W, SEG), lambda s: (s, T_QN))]
        + [slab(T_KN, i) for i in range(nslab)] + [slab(T_VN, i) for i in range(nslab)]
        + [pl.BlockSpec((l, SEG), lambda s: (0, T_KN)), pl.BlockSpec((l, SEG), lambda s: (0, T_VN)),
           pl.BlockSpec(bias_tab.shape, lambda s: (0, 0, 0, 0)), pl.BlockSpec(mask_tab.shape, lambda s: (0, 0, 0))],
        out_specs=pl.BlockSpec((NA_QROWS * GRID_W, SEG), lambda s: (s, 0)),
        out_shape=jax.ShapeDtypeStruct((n, SEG), BF16),
        compiler_params=_cparams(("arbitrary",)),
        name="neighbourhood_attention",
    )(p, *([p] * (2 * nslab)), p_ctx, p_ctx, bias_tab, mask_tab)


def _na_bias_table(rpb):
    n_r, n_c = 2 * NA_ROWS - 1, 2 * NA_COLS - 1
    rpb = rpb.astype(F32)
    e_i, uk_i, wq_i = np.meshgrid(np.arange(NA_BIAS_PAIRS), np.arange(2), np.arange(2), indexing="ij")
    src = np.clip(e_i - NA_BIAS_ZERO + uk_i - wq_i + NA_ROWS - 1, 0, n_r - 1)
    pick = jnp.asarray((src[..., None] == np.arange(n_r)).astype(np.float32))
    rows = jnp.einsum("euwr,hrj->heuwj", pick, rpb, precision=HIGHEST)
    ck = np.arange(GRID_W)[:, None]
    cq = np.arange(GRID_W)[None, :]
    ci = np.clip(ck - cq, -(NA_COLS - 1), NA_COLS - 1) + NA_COLS - 1
    sel = (ci[None] == np.arange(n_c)[:, None, None]).astype(np.float32)
    sel2 = np.zeros((2, n_c, GRID_W, 2, GRID_W), np.float32)
    for wq in range(2):
        sel2[wq, :, :, wq, :] = sel
    sel2 = jnp.asarray(sel2.reshape(2 * n_c, GRID_W, 2 * GRID_W))
    rows = rows.reshape(rpb.shape[0], NA_BIAS_PAIRS, 2, 2 * n_c)
    tab = jnp.einsum("heuj,jkl->heukl", rows, sel2, precision=HIGHEST)
    return tab.reshape(rpb.shape[0], NA_BIAS_PAIRS, 2 * GRID_W, 2 * GRID_W)


def _ctx_attn_kernel(sink_ref, r_ref, p_ref, ow_ref, on_ref):
    l = p_ref.shape[0]
    n_cols = SEG // LANES
    k_all = r_ref[:, R_KVW * SEG:R_KVW * SEG + LANES]
    v_all = r_ref[:, R_KVW * SEG + LANES:R_KVW * SEG + 2 * LANES]
    qs = jnp.concatenate([r_ref[:, R_QW * SEG + c * LANES:R_QW * SEG + (c + 1) * LANES] for c in range(n_cols)], axis=0)
    outs = []
    for g in range(WIN_KV_HEADS):
        s = _dot_nt(qs, _half_mask(k_all, g))
        sink = jnp.concatenate([jnp.full((l, 1), sink_ref[n_cols * g + c], F32) for c in range(n_cols)], axis=0)
        outs.append(_softmax_pv(s, _half_mask(v_all, g), sink))
    o = outs[0] + outs[1]
    ow_ref[...] = jnp.concatenate([o[c * l:(c + 1) * l, :] for c in range(n_cols)], axis=1).astype(BF16)
    for pair in range(NA_HEADS // 2):
        sl = lambda t: slice(t * SEG + pair * LANES, t * SEG + (pair + 1) * LANES)
        q, k, v = p_ref[:, sl(T_QN)], p_ref[:, sl(T_KN)], p_ref[:, sl(T_VN)]
        out = None
        for u in range(2):
            o = _softmax_pv(_dot_nt(q, _half_mask(k, u)), _half_mask(v, u))
            out = o if out is None else out + o
        on_ref[:, pair * LANES:(pair + 1) * LANES] = out.astype(BF16)


def _ctx_attention(r_ctx, p_ctx, sink):
    l = p_ctx.shape[0]
    return pl.pallas_call(
        _ctx_attn_kernel,
        in_specs=[pl.BlockSpec(memory_space=pltpu.SMEM), pl.BlockSpec(r_ctx.shape, lambda: (0, 0)),
                  pl.BlockSpec(p_ctx.shape, lambda: (0, 0))],
        out_specs=[pl.BlockSpec((l, SEG), lambda: (0, 0)), pl.BlockSpec((l, SEG), lambda: (0, 0))],
        out_shape=[jax.ShapeDtypeStruct((l, SEG), BF16), jax.ShapeDtypeStruct((l, SEG), BF16)],
        compiler_params=pltpu.CompilerParams(vmem_limit_bytes=VMEM_LIMIT),
        name="context_attention",
    )(sink, r_ctx, p_ctx)


def _merge_kernel(yf_ref, yb_ref, gr_ref, gn_ref, yw_ref, yn_ref, ga_ref, gb_ref, gc_ref,
                  wr_ref, ww_ref, wn_ref, wo_ref, x_ref, m2_ref, gf_ref, m3_ref, m4_ref, wrt_ref,
                  xo_ref, hx_ref, lt_ref):
    y = yf_ref[...] + yb_ref[...]
    parts = []
    for h in range(RET_HEADS):
        yh = y[:, h * RET_D:(h + 1) * RET_D]
        mu = jnp.mean(yh, axis=-1, keepdims=True)
        var = jnp.mean(jnp.square(yh - mu), axis=-1, keepdims=True)
        parts.append((yh - mu) * lax.rsqrt(var + EPS))
    g = gr_ref[...].astype(F32)
    ya = jnp.concatenate(parts, axis=1) * gn_ref[...] * (g * jax.nn.sigmoid(g))
    za = _dot(ya.astype(BF16), wr_ref[...])
    zb = _dot(yw_ref[...], ww_ref[...])
    zc = _dot(yn_ref[...], wn_ref[...])
    sig = lambda r: jax.nn.sigmoid(r[...].astype(F32))
    mix = sig(ga_ref) * za + sig(gb_ref) * zb + sig(gc_ref) * zc
    x_new = x_ref[...] + m2_ref[...] * _dot(mix.astype(BF16), wo_ref[...])
    xo_ref[...] = x_new
    h2 = _norm_mod(x_new, gf_ref[...], m3_ref[...], m4_ref[...])
    n_sub = h2.shape[1] // LANES
    for s in range(n_sub):
        hx_ref[pl.ds(s, h2.shape[0], stride=n_sub), :] = h2[:, s * LANES:(s + 1) * LANES]
    lt_ref[...] = _dot_nt(wrt_ref[...], h2, precision=HIGHEST)


def _merge(yf, yb, p, gn, yw, yn, wr, ww, wn, wo, x, m2, gf, m3, m4, wrt, *, tm):
    n, d = x.shape
    row = lambda wdt, t: pl.BlockSpec((tm, wdt), lambda i: (i, t))
    full = lambda a: pl.BlockSpec(a.shape, lambda i: (0,) * a.ndim)
    gate0 = T_GATES * SEG // d
    return pl.pallas_call(
        _merge_kernel,
        grid=(n // tm,),
        in_specs=[row(SEG, 0), row(SEG, 0), row(SEG, T_GR), full(gn), row(SEG, 0), row(SEG, 0),
                  row(d, gate0), row(d, gate0 + 1), row(d, gate0 + 2),
                  full(wr), full(ww), full(wn), full(wo), row(d, 0), full(m2), full(gf), full(m3), full(m4), full(wrt)],
        out_specs=[pl.BlockSpec((tm, d), lambda i: (i, 0)),
                   pl.BlockSpec((tm * (d // LANES), LANES), lambda i: (i, 0)),
                   pl.BlockSpec((N_EXPERTS, tm), lambda i: (0, i))],
        out_shape=[jax.ShapeDtypeStruct((n, d), F32), jax.ShapeDtypeStruct((n * (d // LANES), LANES), F32),
                   jax.ShapeDtypeStruct((N_EXPERTS, n), F32)],
        compiler_params=_cparams(("arbitrary",)),
        name="merge",
    )(yf, yb, p, gn, yw, yn, p, p, p, wr, ww, wn, wo, x, m2, gf, m3, m4, wrt)


def _route_kernel(lt_ref, idx_ref, gate_ref, pos_ref, goff_ref, aff_ref, thr_ref, *, cap, n_groups):
    e = pl.program_id(0)
    n_exp, ag, _ = lt_ref.shape
    capp = idx_ref.shape[-1]

    @pl.when(e == 0)
    def _():
        lt = lt_ref[...]
        ex = jnp.exp(lt - jnp.max(lt, axis=0, keepdims=True))
        aff = ex / jnp.sum(ex, axis=0, keepdims=True)
        real = _iota(aff.shape, 1) < n_groups
        aff = jnp.where(real, aff, 0.0)
        aff_ref[...] = aff

        def count_ge(v):
            return jnp.sum(jnp.sum((aff >= v).astype(F32), axis=2, keepdims=True), axis=1, keepdims=True)

        def body(k, carry):
            lo, hi = carry
            mid = jnp.where(lo > 0.0, jnp.sqrt(lo) * jnp.sqrt(hi), hi * 2.0 ** -16)
            ge = count_ge(mid) >= float(cap)
            return jnp.where(ge, mid, lo), jnp.where(ge, hi, mid)

        lo, _ = lax.fori_loop(0, BISECT_STEPS, body, (jnp.zeros((n_exp, 1, 1), F32), jnp.full((n_exp, 1, 1), 2.0, F32)))
        big = jnp.where(aff >= lo, aff, 4.0)
        t = jnp.min(jnp.min(big, axis=2, keepdims=True), axis=1, keepdims=True)
        thr_ref[...] = jnp.broadcast_to(t, thr_ref.shape)

    a = aff_ref[e]
    t = thr_ref[e][0:1, :]
    gt = a > t
    eq = (a == t) & (_iota(a.shape, 0) < n_groups)
    tri_lane_strict = (_iota((LANES, LANES), 0) < _iota((LANES, LANES), 1)).astype(BF16)
    tri_lane_incl = (_iota((LANES, LANES), 0) <= _iota((LANES, LANES), 1)).astype(BF16)
    tri_grp_strict = (_iota((ag, ag), 1) < _iota((ag, ag), 0)).astype(BF16)
    tri_grp_incl = (_iota((ag, ag), 1) <= _iota((ag, ag), 0)).astype(BF16)

    def total(mask_f):
        return jnp.sum(jnp.sum(mask_f, axis=1, keepdims=True), axis=0, keepdims=True)

    def group_sum(mask_f):
        return jnp.broadcast_to(jnp.sum(mask_f, axis=1, keepdims=True), (ag, LANES)).astype(BF16)

    eq_f = eq.astype(F32)
    need = float(cap) - total(gt.astype(F32))
    rank_eq = _dot(tri_grp_strict, group_sum(eq_f)) + _dot(eq_f.astype(BF16), tri_lane_strict)
    sel = gt | (eq & (rank_eq < need))
    sel_f = sel.astype(F32)
    cl = _dot(sel_f.astype(BF16), tri_lane_incl)
    cg = _dot(tri_grp_incl, group_sum(sel_f))
    goff = cg - jnp.broadcast_to(jnp.sum(sel_f, axis=1, keepdims=True), (ag, LANES))
    pos_ref[...] = jnp.where(sel, goff + cl - 1.0, -1.0)
    diag = _iota((ag, ag), 0) == _iota((ag, ag), 1)
    goff_sq = goff if ag == LANES else goff[:, :ag]
    goff_ref[...] = jnp.sum(jnp.where(diag, goff_sq, 0.0), axis=0, keepdims=True).astype(I32)

    pp = _iota((ag, capp), 1, F32)
    cg_b = jnp.broadcast_to(cg[:, 0:1], (ag, capp))
    below = cg_b <= pp
    grp = jnp.sum(below.astype(F32), axis=0, keepdims=True)
    off = jnp.max(jnp.where(below, cg_b, 0.0), axis=0, keepdims=True)
    onehot = _iota((ag, capp), 0, F32) == grp
    in_grp = _dot_tn(cl.astype(BF16), onehot.astype(BF16))
    local = pp[0:1, :] - off
    lane_of = jnp.sum((in_grp <= local).astype(F32), axis=0, keepdims=True)
    live = pp[0:1, :] < float(cap)
    idx = jnp.where(live, grp * float(LANES) + lane_of, 0.0)
    idx_ref[...] = idx.astype(I32)
    aff_grp = _dot_tn(a, onehot.astype(F32), precision=HIGHEST)
    pick = _iota((LANES, capp), 0, F32) == lane_of
    gate = jnp.sum(jnp.where(pick, aff_grp, 0.0), axis=0, keepdims=True)
    gate_ref[...] = jnp.where(live, gate, 0.0)


def _route(logits_t, *, cap, n_groups):
    n_exp, ag, _ = logits_t.shape
    capp = -(-cap // LANES) * LANES
    return pl.pallas_call(
        functools.partial(_route_kernel, cap=cap, n_groups=n_groups),
        grid=(n_exp,),
        in_specs=[pl.BlockSpec(logits_t.shape, lambda e: (0, 0, 0))],
        out_specs=[pl.BlockSpec((None, 1, capp), lambda e: (e, 0, 0)),
                   pl.BlockSpec((None, 1, capp), lambda e: (e, 0, 0)),
                   pl.BlockSpec((None, ag, LANES), lambda e: (e, 0, 0)),
                   pl.BlockSpec((None, 1, ag), lambda e: (e, 0, 0))],
        out_shape=[jax.ShapeDtypeStruct((n_exp, 1, capp), I32), jax.ShapeDtypeStruct((n_exp, 1, capp), F32),
                   jax.ShapeDtypeStruct((n_exp, ag, LANES), F32), jax.ShapeDtypeStruct((n_exp, 1, ag), I32)],
        scratch_shapes=[pltpu.VMEM((n_exp, ag, LANES), F32), pltpu.VMEM((n_exp, SUBLANES, LANES), F32)],
        compiler_params=_cparams(("arbitrary",)),
        name="route",
    )(logits_t)


def _ffn_kernel(idx_ref, idxn_ref, gate_ref, hx_ref, wg_ref, wu_ref, wd_ref, o_ref, xbuf, sem, wgb, wub, wdb, *, tm):
    e = pl.program_id(0)
    j = pl.program_id(1)
    nt = pl.num_programs(1)
    step = e * nt + j
    last = pl.num_programs(0) * nt - 1

    sub = SUBLANES
    rows = tm * sub

    def row_copy(rows_ref, r, slot):
        src = hx_ref.at[pl.ds(pl.multiple_of(rows_ref[0, 0, r] * sub, sub), sub)]
        dst = xbuf.at[pl.ds(pl.multiple_of(slot * rows + r * sub, sub), sub)]
        return pltpu.make_async_copy(src, dst, sem.at[slot])

    def slot_copy(slot):
        return pltpu.make_async_copy(hx_ref.at[pl.ds(0, rows)], xbuf.at[pl.ds(pl.multiple_of(slot * rows, sub), rows)],
                                     sem.at[slot])

    @pl.when(step == 0)
    def _():
        def body(r, carry):
            row_copy(idx_ref, r, 0).start()
            return carry
        lax.fori_loop(0, tm, body, 0, unroll=8)

    @pl.when(j == 0)
    def _():
        wgb[...] = wg_ref[...].astype(BF16)
        wub[...] = wu_ref[...].astype(BF16)
        wdb[...] = wd_ref[...].astype(BF16)

    slot = step % 2
    nslot = 1 - slot
    for r in range(tm):
        row_copy(idxn_ref, r, nslot).start()
    slot_copy(slot).wait()
    base = slot * rows
    x = jnp.concatenate([xbuf[pl.ds(base + s, tm, stride=sub), :] for s in range(sub)], axis=1).astype(BF16)
    g = _dot(x, wgb[...])
    u = _dot(x, wub[...])
    hid = (g * jax.nn.sigmoid(g)) * u
    y = _dot(hid.astype(BF16), wdb[...])
    eye = _iota((tm, tm), 0) == _iota((tm, tm), 1)
    gcol = jnp.sum(jnp.where(eye, jnp.broadcast_to(gate_ref[0], (tm, tm)), 0.0), axis=1, keepdims=True)
    o_ref[...] = (y * gcol).astype(BF16)

    @pl.when(step == last)
    def _():
        slot_copy(nslot).wait()


def _expert_ffn(idx, gate, hx3, w_gate, w_up, w_down, *, layer, cap, tm):
    n_exp = idx.shape[0]
    d = w_gate.shape[2]
    f = w_gate.shape[3]
    assert d == SUBLANES * LANES, "a token row must be exactly one (8, 128) f32 tile"
    nt = cap // tm
    idx_t = idx[:, :, :cap].reshape(n_exp * nt, 1, tm)
    gate_t = gate[:, :, :cap].reshape(n_exp * nt, 1, tm)
    n_steps = n_exp * nt
    wspec = lambda a, b: pl.BlockSpec((None, None, a, b), lambda e, j: (layer, e, 0, 0))
    return pl.pallas_call(
        functools.partial(_ffn_kernel, tm=tm),
        grid=(n_exp, nt),
        in_specs=[pl.BlockSpec((1, 1, tm), lambda e, j: (e * nt + j, 0, 0), memory_space=pltpu.SMEM),
                  pl.BlockSpec((1, 1, tm), lambda e, j: (jnp.minimum(e * nt + j + 1, n_steps - 1), 0, 0), memory_space=pltpu.SMEM),
                  pl.BlockSpec((1, 1, tm), lambda e, j: (e * nt + j, 0, 0)),
                  pl.BlockSpec(memory_space=pl.ANY),
                  wspec(d, f), wspec(d, f), wspec(f, d)],
        out_specs=pl.BlockSpec((tm, d), lambda e, j: (e * nt + j, 0)),
        out_shape=jax.ShapeDtypeStruct((n_exp * cap, d), BF16),
        scratch_shapes=[pltpu.VMEM((2 * tm * SUBLANES, LANES), F32), pltpu.SemaphoreType.DMA((2,)),
                        pltpu.VMEM((d, f), BF16), pltpu.VMEM((d, f), BF16), pltpu.VMEM((f, d), BF16)],
        compiler_params=_cparams(("arbitrary", "arbitrary")),
        name="expert_ffn",
    )(idx_t, idx_t, gate_t, hx3, w_gate, w_up, w_down)


def _combine_kernel(ws_ref, nw_ref, *refs, n_exp, cap, win, gpt, final_norm):
    y_refs = refs[:n_exp]
    pos_ref, ye_ref, x_ref, m5_ref, gf_ref, o_ref, acc_ref, xwin, sem = refs[n_exp:]
    a = pl.program_id(0)
    ng = pl.num_programs(0)
    per = max(1, MXU_DEPTH // win)

    def slots(e):
        pos = pos_ref[e, pl.ds(a * gpt, gpt), :]
        return jnp.concatenate([jnp.broadcast_to(pos[j:j + 1, :], (win, LANES)) for j in range(gpt)], axis=1)

    lp = _iota((win, gpt * LANES), 0, F32)
    acc = None
    for e0 in range(0, n_exp, per):
        es = range(e0, min(e0 + per, n_exp))
        hot = jnp.concatenate([((slots(e) - (ws_ref[e * ng + a] - e * cap).astype(F32)) == lp).astype(BF16) for e in es],
                              axis=0)
        rows = jnp.concatenate([y_refs[e][...] for e in es], axis=0)
        part = _dot_tn(hot, rows)
        acc = part if acc is None else acc + part
    acc_ref[...] = acc

    for e in range(n_exp):
        first = ws_ref[e * ng + a] - e * cap

        def extra(k, carry, e=e, first=first):
            lo = first + k * win
            row = jnp.minimum(lo, cap - win)
            cp = pltpu.make_async_copy(ye_ref.at[pl.ds(pl.multiple_of(e * cap + row, BF16_ROWS), win)], xwin, sem)
            cp.start()
            cp.wait()
            pos = slots(e)
            hot = ((pos - row.astype(F32)) == lp) & (pos >= lo.astype(F32))
            acc_ref[...] += _dot_tn(hot.astype(BF16), xwin[...])
            return carry

        lax.fori_loop(1, nw_ref[e * ng + a], extra, 0)

    x_new = x_ref[...] + m5_ref[...] * acc_ref[...]
    if final_norm:
        ms = jnp.mean(x_new * x_new, axis=-1, keepdims=True)
        x_new = x_new * lax.rsqrt(ms + EPS) * gf_ref[...]
    o_ref[...] = x_new


def _combine(ye, posmap, goff, x, m5, g_final, *, cap, final_norm):
    n, d = x.shape
    n_exp = posmap.shape[0]
    gpt = min(COMBINE_GROUPS, n // LANES)
    ng = n // (gpt * LANES)
    tok = gpt * LANES
    win = min(COMBINE_WINDOW, cap)
    goff = goff[:, 0, :ng * gpt:gpt]
    end = jnp.concatenate([goff[:, 1:], jnp.full((n_exp, 1), cap, I32)], axis=1)
    start = jnp.minimum((goff // BF16_ROWS) * BF16_ROWS, cap - win)
    n_win = jnp.maximum((end - start + win - 1) // win, 1).reshape(-1)
    wstart = (start + jnp.arange(n_exp, dtype=I32)[:, None] * cap).reshape(-1)
    yspec = lambda e: pl.BlockSpec((pl.Element(win), pl.Element(d)),
                                   lambda a, ws, nw: (pl.multiple_of(ws[e * ng + a], BF16_ROWS), 0))
    gs = pltpu.PrefetchScalarGridSpec(
        num_scalar_prefetch=2,
        grid=(ng,),
        in_specs=[yspec(e) for e in range(n_exp)]
        + [pl.BlockSpec(posmap.shape, lambda a, ws, nw: (0, 0, 0)),
           pl.BlockSpec(memory_space=pl.ANY),
           pl.BlockSpec((tok, d), lambda a, ws, nw: (a, 0)),
           pl.BlockSpec((1, d), lambda a, ws, nw: (0, 0)), pl.BlockSpec((1, d), lambda a, ws, nw: (0, 0))],
        out_specs=pl.BlockSpec((tok, d), lambda a, ws, nw: (a, 0)),
        scratch_shapes=[pltpu.VMEM((tok, d), F32), pltpu.VMEM((win, d), BF16), pltpu.SemaphoreType.DMA(())],
    )
    return pl.pallas_call(
        functools.partial(_combine_kernel, n_exp=n_exp, cap=cap, win=win, gpt=gpt, final_norm=final_norm),
        grid_spec=gs,
        out_shape=jax.ShapeDtypeStruct((n, d), F32),
        compiler_params=_cparams(("arbitrary",)),
        name="combine_final" if final_norm else "combine",
    )(wstart, n_win, *([ye] * n_exp), posmap, ye, x, m5, g_final)


def _rope_tables(n):
    n_rows = n // GRID_W
    row = jnp.arange(n_rows).astype(F32)
    col = jnp.arange(GRID_W).astype(F32)

    def tables(d, signs):
        nf = d // 4
        inv = ROPE_BASE ** (-jnp.arange(nf, dtype=F32) / nf)
        ang_r, ang_c = row[:, None] * inv, col[:, None] * inv
        reps = LANES // (2 * nf)

        def lanes(f_r, f_c, factors):
            z_r, z_c = jnp.zeros_like(f_r), jnp.zeros_like(f_c)
            pat_r = jnp.concatenate([part for k in range(reps) for part in (f_r * factors[2 * k], z_r)], axis=1)
            pat_c = jnp.concatenate([part for k in range(reps) for part in (z_c, f_c * factors[2 * k + 1])], axis=1)
            return (pat_r[:, None, :] + pat_c[None, :, :]).reshape(n, LANES)

        ones = (1.0,) * (2 * reps)
        cos = lanes(jnp.cos(ang_r), jnp.cos(ang_c), ones)
        return [cos] + [lanes(jnp.sin(ang_r), jnp.sin(ang_c), sg) for sg in signs]

    c128, s128 = tables(RET_D, [(-1.0, -1.0, 1.0, 1.0)])
    c64, sa, sb = tables(HEAD_D, [(-1.0, -1.0, 0.0, 0.0) * 2, (0.0, 0.0, 1.0, 1.0) * 2])
    return c128, s128, c64, sa, sb


WIN_HEAD_ORDER = (0, 4, 1, 5, 2, 6, 3, 7)


def _permute_in_weight(w):
    d = w.shape[0]
    sizes = (("q_r", 512), ("k_r", 512), ("v_r", 512), ("g_r", 512), ("q_w", 512), ("k_w", 128), ("v_w", 128),
             ("q_n", 512), ("k_n", 512), ("v_n", 512), ("gates", 3 * d))
    off, lay = 0, {}
    for name, size in sizes:
        lay[name] = (off, off + size)
        off += size
    seg = lambda name: w[:, lay[name][0]:lay[name][1]]
    q0 = lay["q_w"][0]
    q_w = [w[:, q0 + h * HEAD_D:q0 + (h + 1) * HEAD_D] for h in WIN_HEAD_ORDER]
    pad = jnp.zeros((d, SEG - 2 * LANES), w.dtype)
    rope = jnp.concatenate([seg("q_r"), seg("k_r")] + q_w + [seg("k_w"), seg("v_w"), pad], axis=1).astype(BF16)
    plain = jnp.concatenate([seg("gates"), seg("v_r"), seg("g_r"), seg("q_n"), seg("k_n"), seg("v_n")], axis=1).astype(BF16)
    return rope, plain


def _permute_win_rows(w):
    return jnp.concatenate([w[h * HEAD_D:(h + 1) * HEAD_D] for h in WIN_HEAD_ORDER], axis=0).astype(BF16)


def kernel(x, c, ctx, c_ctx, w_mod, b_mod, g_mix, g_ffn, w_in, ret_decay_logit, ret_gn, w_ret, win_sink, w_win, na_rpb,
           w_na, w_out, w_router, w_exp_gate, w_exp_up, w_exp_down, g_final):
    _, n, d = x.shape
    l = ctx.shape[1]
    depth = w_in.shape[0]
    xs, cs = x[0], ctx[0]
    cc = jnp.zeros((SUBLANES, d), F32).at[0].set(c[0]).at[1].set(c_ctx)
    mods = _modulation(cc, w_mod, b_mod)
    c128, s128, c64, sa64, sb64 = _rope_tables(n)
    k_scale = RET_D ** -0.5
    plain_scales = jnp.ones((PLAIN_TILES,), F32).at[T_QN].set(Q_SCALE)
    ctx_rope_scales = jnp.ones((R_KVW + 1,), F32).at[R_KR].set(k_scale).at[R_QW].set(Q_SCALE)
    cap_x = CAPACITY_FACTOR * n // N_EXPERTS
    cap_c = CAPACITY_FACTOR * l // N_EXPERTS
    grp_c = -(-(l // LANES) // BF16_ROWS) * BF16_ROWS
    vec = lambda v: v.reshape(1, -1)

    for layer in range(depth):
        need_ctx = layer < depth - 1
        last = layer == depth - 1
        mx = [vec(mods[layer, 0, k * d:(k + 1) * d]) for k in range(N_MOD)]
        mc = [vec(mods[layer, 1, k * d:(k + 1) * d]) for k in range(N_MOD)]
        w_rope, w_plain = _permute_in_weight(w_in[layer])
        wr = w_ret[layer].astype(BF16)
        ww = _permute_win_rows(w_win[layer])
        wn = w_na[layer].astype(BF16)
        wo = w_out[layer].astype(BF16)
        wrt = w_router[layer].T
        sink = win_sink[layer].astype(F32) * LOG2E
        lg_rows = jnp.broadcast_to(ret_decay_logit[layer].astype(F32).reshape(2 * RET_HEADS, 1), (2 * RET_HEADS, LANES))
        gn = vec(ret_gn[layer])
        bias_tab = _na_bias_table(na_rpb[layer].astype(F32) * LOG2E)
        gmix, gffn = vec(g_mix[layer]), vec(g_ffn[layer])

        hx = _norm_call(xs, gmix, mx[0], mx[1], tm=min(NORM_ROWS, n))
        hc = _norm_call(cs, gmix, mc[0], mc[1], tm=l)
        p_x = _proj(hx, w_plain, plain_scales, tm=min(PROJ_ROWS, n))
        p_c = _proj(hc, w_plain, plain_scales, tm=l)
        r_c = _proj(hc, w_rope, ctx_rope_scales, tm=l)
        rope = functools.partial(_proj_rope, hx, w_rope, tm=min(PROJ_ROWS, n))
        q_r = rope(R_QR, (c128, s128), head_d=RET_D, scale=1.0, n_rot=SEG // LANES, name="proj_q_ret")
        k_r = rope(R_KR, (c128, s128), head_d=RET_D, scale=k_scale, n_rot=SEG // LANES, name="proj_k_ret")
        q_w = rope(R_QW, (c64, sa64, sb64), head_d=HEAD_D, scale=Q_SCALE, n_rot=SEG // LANES, name="proj_q_win")
        kv_w = rope(R_KVW, (c64, sa64, sb64), head_d=HEAD_D, scale=1.0, n_rot=1, name="proj_kv_win")

        yf, yb = _retention((q_r, 0), (k_r, 0), (p_x, T_VR), (r_c, R_KR), (p_c, T_VR), lg_rows, zero_init=False)
        yw = _window_attention((q_w, 0), (kv_w, 0), (r_c, R_KVW), sink)
        yn = _na_attention(p_x, p_c, bias_tab)
        xs, hx3, lt = _merge(yf, yb, p_x, gn, yw, yn, wr, ww, wn, wo, xs, mx[2], gffn, mx[3], mx[4], wrt, tm=min(MERGE_ROWS, n))

        idx, gate, posmap, goff = _route(lt.reshape(N_EXPERTS, n // LANES, LANES), cap=cap_x, n_groups=n // LANES)
        ye = _expert_ffn(idx, gate, hx3, w_exp_gate, w_exp_up, w_exp_down, layer=layer, cap=cap_x, tm=min(FFN_ROWS, cap_x))
        xs = _combine(ye, posmap, goff, xs, mx[5], vec(g_final), cap=cap_x, final_norm=last)

        if need_ctx:
            yfc, ybc = _retention((r_c, R_QR), (r_c, R_KR), (p_c, T_VR), (r_c, R_KR), (p_c, T_VR), lg_rows, zero_init=True)
            ywc, ync = _ctx_attention(r_c, p_c, sink)
            cs, hc3, ltc = _merge(yfc, ybc, p_c, gn, ywc, ync, wr, ww, wn, wo, cs, mc[2], gffn, mc[3], mc[4], wrt, tm=l)
            ltc = jnp.pad(ltc.reshape(N_EXPERTS, l // LANES, LANES), ((0, 0), (0, grp_c - l // LANES), (0, 0)))
            idc, gtc, posc, goffc = _route(ltc, cap=cap_c, n_groups=l // LANES)
            yec = _expert_ffn(idc, gtc, hc3, w_exp_gate, w_exp_up, w_exp_down, layer=layer, cap=cap_c, tm=cap_c)
            cs = _combine(yec, posc, goffc, cs, mc[5], vec(g_final), cap=cap_c, final_norm=False)

    return xs[None]
```

```python
import functools

import jax
import jax.numpy as jnp
import numpy as np
from jax import lax
from jax.experimental import pallas as pl
from jax.experimental.pallas import tpu as pltpu

F32, BF16, I32 = jnp.float32, jnp.bfloat16, jnp.int32
HIGHEST = lax.Precision.HIGHEST

GRID_W = 64
RET_HEADS, RET_D, RET_CHUNK = 4, 128, 128
WIN_HEADS, WIN_KV_HEADS, HEAD_D, WINDOW, WIN_BLOCK = 8, 2, 64, 128, 128
NA_HEADS, NA_ROWS, NA_COLS = 8, 8, 16
N_EXPERTS, CAPACITY_FACTOR = 16, 2
N_MOD = 6
ROPE_BASE = 10000.0
EPS = 1e-6
NEG_INF = -1e30

LANES = 128
SUBLANES = 8
BF16_ROWS = 16
VMEM_LIMIT = 56 * 1024 * 1024

SEG = 512
T_GATES, T_VR, T_GR, T_QN, T_KN, T_VN = 0, 6, 7, 8, 9, 10
PLAIN_TILES = 11
R_QR, R_KR, R_QW, R_KVW = 0, 1, 2, 3
LOG2E = 1.4426950408889634
Q_SCALE = HEAD_D ** -0.5 * LOG2E
NORM_ROWS = 1024
PROJ_ROWS = 2048
MERGE_ROWS = 512
FFN_ROWS = 512
RET_BLOCK = 4 * RET_CHUNK
WIN_QBLOCKS = 2
NA_QROWS = 4
NA_SLAB_ROWS = 12
NA_BIAS_ZERO = NA_SLAB_ROWS - 2
NA_BIAS_PAIRS = NA_BIAS_ZERO + NA_SLAB_ROWS - 1
COMBINE_GROUPS = 4
COMBINE_WINDOW = 128
MXU_DEPTH = 256
BISECT_STEPS = 48


def _cparams(sem):
    return pltpu.CompilerParams(dimension_semantics=sem, vmem_limit_bytes=VMEM_LIMIT)


def _dot(a, b):
    return jnp.dot(a, b, preferred_element_type=F32)


def _dot_nt(a, b, precision=None):
    return lax.dot_general(a, b, (((1,), (1,)), ((), ())), precision=precision, preferred_element_type=F32)


def _dot_tn(a, b, precision=None):
    return lax.dot_general(a, b, (((0,), (0,)), ((), ())), precision=precision, preferred_element_type=F32)


def _iota(shape, dim, dtype=I32):
    return lax.broadcasted_iota(I32, shape, dim).astype(dtype)


def _mod_kernel(s_ref, w_ref, b_ref, o_ref):
    s = s_ref[...]
    s = s * jax.nn.sigmoid(s)
    o_ref[...] = jnp.dot(s, w_ref[...], precision=HIGHEST, preferred_element_type=F32) + b_ref[...]


def _modulation(cc, w_mod, b_mod):
    depth, d, md = w_mod.shape
    tn = 1536
    return pl.pallas_call(
        _mod_kernel,
        grid=(depth, md // tn),
        in_specs=[
            pl.BlockSpec((SUBLANES, d), lambda l, j: (0, 0)),
            pl.BlockSpec((None, d, tn), lambda l, j: (l, 0, j)),
            pl.BlockSpec((None, 1, tn), lambda l, j: (l, 0, j)),
        ],
        out_specs=pl.BlockSpec((None, SUBLANES, tn), lambda l, j: (l, 0, j)),
        out_shape=jax.ShapeDtypeStruct((depth, SUBLANES, md), F32),
        compiler_params=_cparams(("arbitrary", "arbitrary")),
        name="modulation",
    )(cc, w_mod, b_mod.reshape(depth, 1, md))


def _norm_mod(x, g, shift, scale):
    ms = jnp.mean(x * x, axis=-1, keepdims=True)
    y = x * lax.rsqrt(ms + EPS) * g
    return y * (1.0 + scale) + shift


def _norm_kernel(x_ref, g_ref, sh_ref, sc_ref, o_ref):
    o_ref[...] = _norm_mod(x_ref[...], g_ref[...], sh_ref[...], sc_ref[...]).astype(BF16)


def _norm_call(x, g, shift, scale, *, tm):
    n, d = x.shape
    vec = lambda: pl.BlockSpec((1, d), lambda i: (0, 0))
    return pl.pallas_call(
        _norm_kernel,
        grid=(n // tm,),
        in_specs=[pl.BlockSpec((tm, d), lambda i: (i, 0)), vec(), vec(), vec()],
        out_specs=pl.BlockSpec((tm, d), lambda i: (i, 0)),
        out_shape=jax.ShapeDtypeStruct((n, d), BF16),
        compiler_params=_cparams(("arbitrary",)),
        name="norm_modulate",
    )(x, g, shift, scale)


def _proj_kernel(scale_ref, hx_ref, w_ref, o_ref):
    o_ref[...] = (_dot(hx_ref[...], w_ref[...]) * scale_ref[pl.program_id(1)]).astype(BF16)


def _proj(hx, w, scales, *, tm):
    n, d = hx.shape
    tiles = w.shape[1] // SEG
    return pl.pallas_call(
        _proj_kernel,
        grid=(n // tm, tiles),
        in_specs=[pl.BlockSpec(memory_space=pltpu.SMEM),
                  pl.BlockSpec((tm, d), lambda i, j: (i, 0)), pl.BlockSpec((d, SEG), lambda i, j: (0, j))],
        out_specs=pl.BlockSpec((tm, SEG), lambda i, j: (i, j)),
        out_shape=jax.ShapeDtypeStruct((n, tiles * SEG), BF16),
        compiler_params=_cparams(("arbitrary", "arbitrary")),
        name="projection",
    )(scales, hx, w)


def _proj_rope_kernel(hx_ref, w_ref, *refs, head_d, scale, n_rot):
    tabs, o_ref = refs[:-1], refs[-1]
    acc = _dot(hx_ref[...], w_ref[...])
    if head_d == RET_D:
        c = tabs[0][...] * scale
        s = tabs[1][...] * scale
        rot = lambda a: a * c + pltpu.roll(a, 64, 1) * s
    else:
        c = tabs[0][...] * scale
        sa = tabs[1][...] * scale
        sb = tabs[2][...] * scale
        rot = lambda a: a * c + pltpu.roll(a, 96, 1) * sa + pltpu.roll(a, 32, 1) * sb
    groups = [acc[:, k * LANES:(k + 1) * LANES] for k in range(SEG // LANES)]
    o_ref[...] = jnp.concatenate([rot(a) if k < n_rot else a for k, a in enumerate(groups)], axis=1).astype(BF16)


def _proj_rope(hx, w, tile, tabs, *, head_d, scale, n_rot, tm, name):
    n, d = hx.shape
    return pl.pallas_call(
        functools.partial(_proj_rope_kernel, head_d=head_d, scale=scale, n_rot=n_rot),
        grid=(n // tm,),
        in_specs=[pl.BlockSpec((tm, d), lambda i: (i, 0)), pl.BlockSpec((d, SEG), lambda i: (0, tile))]
        + [pl.BlockSpec((tm, LANES), lambda i: (i, 0)) for _ in tabs],
        out_specs=pl.BlockSpec((tm, SEG), lambda i: (i, 0)),
        out_shape=jax.ShapeDtypeStruct((n, SEG), BF16),
        compiler_params=_cparams(("arbitrary",)),
        name=name,
    )(hx, w, *tabs)


def _ret_kernel(lg_ref, qf_ref, kf_ref, vf_ref, qb_ref, kb_ref, vb_ref, kc_ref, vc_ref, yf_ref, yb_ref,
                rf_ref, rb_ref, dmat_ref, vec_ref, *, zero_init):
    c = pl.program_id(0)
    ch = qf_ref.shape[0]
    dh = RET_D
    states = (rf_ref, rb_ref)

    @pl.when(c == 0)
    def _():
        lg = jax.nn.log_sigmoid(lg_ref[...])
        ii = _iota((ch, ch), 0, F32)
        jj = _iota((ch, ch), 1, F32)
        iv = _iota((ch, dh), 0, F32)
        n_ctx = kc_ref.shape[0]
        mm = _iota((n_ctx, dh), 0, F32)
        for d in range(2):
            for h in range(RET_HEADS):
                row = lg[RET_HEADS * d + h:RET_HEADS * d + h + 1, :]
                l = jnp.concatenate([jnp.broadcast_to(row, (ch, dh))] * (ch // dh), axis=1)
                lv = jnp.broadcast_to(row, (ch, dh))
                if d == 0:
                    diff = ii - jj
                    dmat_ref[d, h] = jnp.where(diff >= 0.0, jnp.exp(jnp.maximum(diff, 0.0) * l), 0.0)
                    vec_ref[d, h, 0] = jnp.exp((ch - 1.0 - iv) * lv)
                    vec_ref[d, h, 1] = jnp.exp((iv + 1.0) * lv)
                else:
                    diff = jj - ii
                    dmat_ref[d, h] = jnp.where(diff >= 1.0, jnp.exp(jnp.maximum(diff, 0.0) * l), 0.0)
                    vec_ref[d, h, 0] = jnp.exp(iv * lv)
                    vec_ref[d, h, 1] = jnp.exp((ch - iv) * lv)
                vec_ref[d, h, 2] = jnp.exp(float(ch) * lv)
                if zero_init:
                    states[d][h] = jnp.zeros((dh, dh), F32)
                else:
                    lc = jnp.broadcast_to(row, (n_ctx, dh))
                    w = jnp.exp((n_ctx - 1.0 - mm) * lc) if d == 0 else jnp.exp(mm * lc)
                    sl = slice(h * dh, (h + 1) * dh)
                    kw = (kc_ref[:, sl].astype(F32) * w).astype(BF16)
                    states[d][h] = _dot_tn(kw, vc_ref[:, sl])

    for d, (q_ref, k_ref, v_ref, y_ref) in enumerate(((qf_ref, kf_ref, vf_ref, yf_ref), (qb_ref, kb_ref, vb_ref, yb_ref))):
        for h in range(RET_HEADS):
            sl = slice(h * dh, (h + 1) * dh)
            q, k, v = q_ref[:, sl], k_ref[:, sl], v_ref[:, sl]
            r = states[d][h]
            a = (_dot_nt(q, k) * dmat_ref[d, h]).astype(BF16)
            inner = _dot(a, v)
            cross = _dot(q, r.astype(BF16)) * vec_ref[d, h, 1]
            y_ref[:, sl] = inner + cross
            kz = (k.astype(F32) * vec_ref[d, h, 0]).astype(BF16)
            states[d][h] = vec_ref[d, h, 2][:dh, :] * r + _dot_tn(kz, v)


def _retention(q, k, v, kc, vc, lg_rows, *, zero_init):
    n = q[0].shape[0]
    blk = min(RET_BLOCK, n)
    nc = n // blk
    w = RET_HEADS * RET_D
    fwd = lambda sg: pl.BlockSpec((blk, w), lambda c: (c, sg[1]))
    bwd = lambda sg: pl.BlockSpec((blk, w), lambda c: (nc - 1 - c, sg[1]))
    ctx = lambda sg: pl.BlockSpec((sg[0].shape[0], w), lambda c: (0, sg[1]))
    return pl.pallas_call(
        functools.partial(_ret_kernel, zero_init=zero_init),
        grid=(nc,),
        in_specs=[pl.BlockSpec((SUBLANES, LANES), lambda c: (0, 0)),
                  fwd(q), fwd(k), fwd(v), bwd(q), bwd(k), bwd(v), ctx(kc), ctx(vc)],
        out_specs=[pl.BlockSpec((blk, w), lambda c: (c, 0)), pl.BlockSpec((blk, w), lambda c: (nc - 1 - c, 0))],
        out_shape=[jax.ShapeDtypeStruct((n, w), F32), jax.ShapeDtypeStruct((n, w), F32)],
        scratch_shapes=[pltpu.VMEM((RET_HEADS, RET_D, RET_D), F32), pltpu.VMEM((RET_HEADS, RET_D, RET_D), F32),
                        pltpu.VMEM((2, RET_HEADS, blk, blk), F32), pltpu.VMEM((2, RET_HEADS, 3, blk, RET_D), F32)],
        compiler_params=_cparams(("arbitrary",)),
        name="retention_ctx" if zero_init else "retention",
    )(lg_rows, q[0], k[0], v[0], q[0], k[0], v[0], kc[0], vc[0])


def _half_mask(x, half):
    lane = _iota(x.shape, 1)
    keep = (lane < HEAD_D) if half == 0 else (lane >= HEAD_D)
    return jnp.where(keep, x, jnp.zeros_like(x))


def _softmax_pv(s, v, extra=None):
    m = jnp.max(s, axis=1, keepdims=True)
    if extra is not None:
        m = jnp.maximum(m, extra)
    p = jnp.exp2(s - m)
    den = jnp.sum(p, axis=1, keepdims=True)
    if extra is not None:
        den = den + jnp.exp2(extra - m)
    return _dot(p.astype(BF16), v) / den


def _softmax_pv_t(st, v, extra=None):
    m = jnp.max(st, axis=0, keepdims=True)
    if extra is not None:
        m = jnp.maximum(m, extra)
    p = jnp.exp2(st - m)
    den = jnp.sum(p, axis=0, keepdims=True)
    if extra is not None:
        den = den + jnp.exp2(extra - m)
    return _dot_tn(v, p.astype(BF16)) / den


def _win_kernel(sink_ref, q_ref, *refs):
    nkb = WIN_QBLOCKS + 2
    k_refs, v_refs = refs[:nkb], refs[nkb:2 * nkb]
    kx_ref, vx_ref, mask_ref, o_ref = refs[2 * nkb:]
    s = pl.program_id(0)
    last = pl.num_programs(0) - 1
    nq = WIN_QBLOCKS * WIN_BLOCK
    nk = nkb * WIN_BLOCK
    k_all = jnp.concatenate([r[...] for r in k_refs] + [kx_ref[...]], axis=0)
    v_all = jnp.concatenate([r[...] for r in v_refs] + [vx_ref[...]], axis=0)
    n_cols = SEG // LANES
    qs = jnp.concatenate([q_ref[:, c * LANES:(c + 1) * LANES] for c in range(n_cols)], axis=0)
    variant = jnp.where(s == 0, 1, 0) + jnp.where(s == last, 2, 0)
    valid = mask_ref[variant] > 0.5
    valid = jnp.concatenate([valid] * n_cols, axis=1)
    out_t = None
    for g in range(WIN_KV_HEADS):
        st = _dot_nt(_half_mask(k_all, g), qs)
        st = jnp.concatenate([jnp.where(valid, st[:nk], NEG_INF), st[nk:]], axis=0)
        sink = jnp.concatenate([jnp.full((1, nq), sink_ref[n_cols * g + c], F32) for c in range(n_cols)], axis=1)
        o = _softmax_pv_t(st, _half_mask(v_all, g), sink)
        out_t = o if out_t is None else out_t + o
    o = out_t.T
    o_ref[...] = jnp.concatenate([o[c * nq:(c + 1) * nq, :] for c in range(n_cols)], axis=1).astype(BF16)


def _win_valid_table():
    nkb = WIN_QBLOCKS + 2
    qpos = np.arange(WIN_QBLOCKS * WIN_BLOCK)[None, :]
    kpos = np.arange(nkb * WIN_BLOCK)[:, None] - WIN_BLOCK
    band = np.abs(kpos - qpos) <= WINDOW
    tabs = []
    for variant in range(4):
        ok = band.copy()
        if variant & 1:
            ok &= kpos >= 0
        if variant & 2:
            ok &= kpos < WIN_QBLOCKS * WIN_BLOCK
        tabs.append(ok)
    return jnp.asarray(np.stack(tabs).astype(np.float32))


def _window_attention(q, kv, kv_ctx, sink):
    n = q[0].shape[0]
    nb = n // WIN_BLOCK
    assert nb % WIN_QBLOCKS == 0
    l = kv_ctx[0].shape[0]
    nkb = WIN_QBLOCKS + 2
    per = SEG // LANES
    blk = lambda off, j: pl.BlockSpec((WIN_BLOCK, LANES),
                                      lambda i: (jnp.clip(WIN_QBLOCKS * i - 1 + j, 0, nb - 1), per * kv[1] + off))
    mask_tab = _win_valid_table()
    return pl.pallas_call(
        _win_kernel,
        grid=(nb // WIN_QBLOCKS,),
        in_specs=[pl.BlockSpec(memory_space=pltpu.SMEM),
                  pl.BlockSpec((WIN_QBLOCKS * WIN_BLOCK, SEG), lambda i: (i, q[1]))]
        + [blk(0, j) for j in range(nkb)] + [blk(1, j) for j in range(nkb)]
        + [pl.BlockSpec((l, LANES), lambda i: (0, per * kv_ctx[1])), pl.BlockSpec((l, LANES), lambda i: (0, per * kv_ctx[1] + 1)),
           pl.BlockSpec(mask_tab.shape, lambda i: (0, 0, 0))],
        out_specs=pl.BlockSpec((WIN_QBLOCKS * WIN_BLOCK, SEG), lambda i: (i, 0)),
        out_shape=jax.ShapeDtypeStruct((n, SEG), BF16),
        compiler_params=_cparams(("arbitrary",)),
        name="window_attention",
    )(sink, q[0], *([kv[0]] * (2 * nkb)), kv_ctx[0], kv_ctx[0], mask_tab)


def _na_slab_start(s, half_rows):
    return jnp.clip((NA_QROWS // 2) * s - NA_ROWS // 4, 0, half_rows - NA_SLAB_ROWS // 2)


def _na_kernel(q_ref, *refs, rows):
    nslab = NA_SLAB_ROWS // 2
    k_refs, v_refs = refs[:nslab], refs[nslab:2 * nslab]
    kx_ref, vx_ref, bias_ref, mask_ref, o_ref = refs[2 * nslab:]
    s = pl.program_id(0)
    last = pl.num_programs(0) - 1
    w = GRID_W
    nq = NA_QROWS * w
    nk = NA_SLAB_ROWS * w
    delta = 2 * _na_slab_start(s, rows // 2) - NA_QROWS * s
    variant = jnp.where(s == 0, 0, jnp.where(s == last, 2, 1))
    valid = mask_ref[variant] > 0.5
    valid = jnp.concatenate([valid, valid], axis=1)
    row = _iota((LANES, nq), 0)
    for pair in range(NA_HEADS // 2):
        sl = slice(pair * LANES, (pair + 1) * LANES)
        q = q_ref[:, sl]
        qs = jnp.concatenate([_half_mask(q, 0), _half_mask(q, 1)], axis=0)
        k_all = jnp.concatenate([r[:, sl] for r in k_refs] + [kx_ref[:, sl]], axis=0)
        v_all = jnp.concatenate([r[:, sl] for r in v_refs] + [vx_ref[:, sl]], axis=0)
        st = _dot_nt(k_all, qs)
        bias = jnp.concatenate(
            [jnp.concatenate([bias_ref[2 * pair + u, delta + 2 * i - a + NA_BIAS_ZERO]
                              for u in range(2) for a in range(0, NA_QROWS, 2)], axis=1) for i in range(nslab)], axis=0)
        s_loc = jnp.where(valid, st[:nk] + bias, NEG_INF)
        o = _softmax_pv_t(jnp.concatenate([s_loc, st[nk:]], axis=0), v_all)
        o_ref[:, sl] = jnp.where(row < HEAD_D, o[:, :nq], o[:, nq:]).T.astype(BF16)


def _na_valid_table():
    kk = np.arange(NA_SLAB_ROWS)[:, None, None, None]
    ck = np.arange(GRID_W)[None, :, None, None]
    a = np.arange(NA_QROWS)[None, None, :, None]
    cq = np.arange(GRID_W)[None, None, None, :]
    c_start = np.clip(cq - NA_COLS // 2, 0, GRID_W - NA_COLS)
    col_ok = (ck >= c_start) & (ck < c_start + NA_COLS)
    first_row = (0 * a, a, 0 * a + NA_SLAB_ROWS - NA_ROWS)
    tabs = [(col_ok & (kk >= f) & (kk < f + NA_ROWS)).reshape(NA_SLAB_ROWS * GRID_W, NA_QROWS * GRID_W) for f in first_row]
    return jnp.asarray(np.stack(tabs).astype(np.float32))


def _na_attention(p, p_ctx, bias_tab):
    n = p.shape[0]
    rows = n // GRID_W
    assert rows % NA_QROWS == 0 and rows >= NA_SLAB_ROWS
    steps = rows // NA_QROWS
    l = p_ctx.shape[0]
    slab = lambda t, i: pl.BlockSpec((2 * GRID_W, SEG), lambda s: (_na_slab_start(s, rows // 2) + i, t))
    nslab = NA_SLAB_ROWS // 2
    mask_tab = _na_valid_table()
    return pl.pallas_call(
        functools.partial(_na_kernel, rows=rows),
        grid=(steps,),
        in_specs=[pl.BlockSpec((NA_QROWS * GRID_W, SEG), lambda s: (s, T_QN))]
        + [slab(T_KN, i) for i in range(nslab)] + [slab(T_VN, i) for i in range(nslab)]
        + [pl.BlockSpec((l, SEG), lambda s: (0, T_KN)), pl.BlockSpec((l, SEG), lambda s: (0, T_VN)),
           pl.BlockSpec(bias_tab.shape, lambda s: (0, 0, 0, 0)), pl.BlockSpec(mask_tab.shape, lambda s: (0, 0, 0))],
        out_specs=pl.BlockSpec((NA_QROWS * GRID_W, SEG), lambda s: (s, 0)),
        out_shape=jax.ShapeDtypeStruct((n, SEG), BF16),
        compiler_params=_cparams(("arbitrary",)),
        name="neighbourhood_attention",
    )(p, *([p] * (2 * nslab)), p_ctx, p_ctx, bias_tab, mask_tab)


def _na_bias_table(rpb):
    n_r, n_c = 2 * NA_ROWS - 1, 2 * NA_COLS - 1
    rpb = rpb.astype(F32)
    e_i, uk_i, wq_i = np.meshgrid(np.arange(NA_BIAS_PAIRS), np.arange(2), np.arange(2), indexing="ij")
    src = np.clip(e_i - NA_BIAS_ZERO + uk_i - wq_i + NA_ROWS - 1, 0, n_r - 1)
    pick = jnp.asarray((src[..., None] == np.arange(n_r)).astype(np.float32))
    rows = jnp.einsum("euwr,hrj->heuwj", pick, rpb, precision=HIGHEST)
    ck = np.arange(GRID_W)[:, None]
    cq = np.arange(GRID_W)[None, :]
    ci = np.clip(ck - cq, -(NA_COLS - 1), NA_COLS - 1) + NA_COLS - 1
    sel = (ci[None] == np.arange(n_c)[:, None, None]).astype(np.float32)
    sel2 = np.zeros((2, n_c, GRID_W, 2, GRID_W), np.float32)
    for wq in range(2):
        sel2[wq, :, :, wq, :] = sel
    sel2 = jnp.asarray(sel2.reshape(2 * n_c, GRID_W, 2 * GRID_W))
    rows = rows.reshape(rpb.shape[0], NA_BIAS_PAIRS, 2, 2 * n_c)
    tab = jnp.einsum("heuj,jkl->heukl", rows, sel2, precision=HIGHEST)
    return tab.reshape(rpb.shape[0], NA_BIAS_PAIRS, 2 * GRID_W, 2 * GRID_W)


def _ctx_attn_kernel(sink_ref, r_ref, p_ref, ow_ref, on_ref):
    l = p_ref.shape[0]
    n_cols = SEG // LANES
    k_all = r_ref[:, R_KVW * SEG:R_KVW * SEG + LANES]
    v_all = r_ref[:, R_KVW * SEG + LANES:R_KVW * SEG + 2 * LANES]
    qs = jnp.concatenate([r_ref[:, R_QW * SEG + c * LANES:R_QW * SEG + (c + 1) * LANES] for c in range(n_cols)], axis=0)
    outs = []
    for g in range(WIN_KV_HEADS):
        s = _dot_nt(qs, _half_mask(k_all, g))
        sink = jnp.concatenate([jnp.full((l, 1), sink_ref[n_cols * g + c], F32) for c in range(n_cols)], axis=0)
        outs.append(_softmax_pv(s, _half_mask(v_all, g), sink))
    o = outs[0] + outs[1]
    ow_ref[...] = jnp.concatenate([o[c * l:(c + 1) * l, :] for c in range(n_cols)], axis=1).astype(BF16)
    for pair in range(NA_HEADS // 2):
        sl = lambda t: slice(t * SEG + pair * LANES, t * SEG + (pair + 1) * LANES)
        q, k, v = p_ref[:, sl(T_QN)], p_ref[:, sl(T_KN)], p_ref[:, sl(T_VN)]
        out = None
        for u in range(2):
            o = _softmax_pv(_dot_nt(q, _half_mask(k, u)), _half_mask(v, u))
            out = o if out is None else out + o
        on_ref[:, pair * LANES:(pair + 1) * LANES] = out.astype(BF16)


def _ctx_attention(r_ctx, p_ctx, sink):
    l = p_ctx.shape[0]
    return pl.pallas_call(
        _ctx_attn_kernel,
        in_specs=[pl.BlockSpec(memory_space=pltpu.SMEM), pl.BlockSpec(r_ctx.shape, lambda: (0, 0)),
                  pl.BlockSpec(p_ctx.shape, lambda: (0, 0))],
        out_specs=[pl.BlockSpec((l, SEG), lambda: (0, 0)), pl.BlockSpec((l, SEG), lambda: (0, 0))],
        out_shape=[jax.ShapeDtypeStruct((l, SEG), BF16), jax.ShapeDtypeStruct((l, SEG), BF16)],
        compiler_params=pltpu.CompilerParams(vmem_limit_bytes=VMEM_LIMIT),
        name="context_attention",
    )(sink, r_ctx, p_ctx)


def _merge_kernel(yf_ref, yb_ref, gr_ref, gn_ref, yw_ref, yn_ref, ga_ref, gb_ref, gc_ref,
                  wr_ref, ww_ref, wn_ref, wo_ref, x_ref, m2_ref, gf_ref, m3_ref, m4_ref, wrt_ref,
                  xo_ref, hx_ref, lt_ref):
    y = yf_ref[...] + yb_ref[...]
    parts = []
    for h in range(RET_HEADS):
        yh = y[:, h * RET_D:(h + 1) * RET_D]
        mu = jnp.mean(yh, axis=-1, keepdims=True)
        var = jnp.mean(jnp.square(yh - mu), axis=-1, keepdims=True)
        parts.append((yh - mu) * lax.rsqrt(var + EPS))
    g = gr_ref[...].astype(F32)
    ya = jnp.concatenate(parts, axis=1) * gn_ref[...] * (g * jax.nn.sigmoid(g))
    za = _dot(ya.astype(BF16), wr_ref[...])
    zb = _dot(yw_ref[...], ww_ref[...])
    zc = _dot(yn_ref[...], wn_ref[...])
    sig = lambda r: jax.nn.sigmoid(r[...].astype(F32))
    mix = sig(ga_ref) * za + sig(gb_ref) * zb + sig(gc_ref) * zc
    x_new = x_ref[...] + m2_ref[...] * _dot(mix.astype(BF16), wo_ref[...])
    xo_ref[...] = x_new
    h2 = _norm_mod(x_new, gf_ref[...], m3_ref[...], m4_ref[...])
    n_sub = h2.shape[1] // LANES
    for s in range(n_sub):
        hx_ref[pl.ds(s, h2.shape[0], stride=n_sub), :] = h2[:, s * LANES:(s + 1) * LANES]
    split = lambda a: (a.astype(BF16), (a - a.astype(BF16).astype(F32)).astype(BF16))
    (w_hi, w_lo), (h_hi, h_lo) = split(wrt_ref[...]), split(h2)
    lt_ref[...] = _dot_nt(w_hi, h_hi) + _dot_nt(w_lo, h_hi) + _dot_nt(w_hi, h_lo)


def _merge(yf, yb, p, gn, yw, yn, wr, ww, wn, wo, x, m2, gf, m3, m4, wrt, *, tm):
    n, d = x.shape
    row = lambda wdt, t: pl.BlockSpec((tm, wdt), lambda i: (i, t))
    full = lambda a: pl.BlockSpec(a.shape, lambda i: (0,) * a.ndim)
    gate0 = T_GATES * SEG // d
    return pl.pallas_call(
        _merge_kernel,
        grid=(n // tm,),
        in_specs=[row(SEG, 0), row(SEG, 0), row(SEG, T_GR), full(gn), row(SEG, 0), row(SEG, 0),
                  row(d, gate0), row(d, gate0 + 1), row(d, gate0 + 2),
                  full(wr), full(ww), full(wn), full(wo), row(d, 0), full(m2), full(gf), full(m3), full(m4), full(wrt)],
        out_specs=[pl.BlockSpec((tm, d), lambda i: (i, 0)),
                   pl.BlockSpec((tm * (d // LANES), LANES), lambda i: (i, 0)),
                   pl.BlockSpec((N_EXPERTS, tm), lambda i: (0, i))],
        out_shape=[jax.ShapeDtypeStruct((n, d), F32), jax.ShapeDtypeStruct((n * (d // LANES), LANES), F32),
                   jax.ShapeDtypeStruct((N_EXPERTS, n), F32)],
        compiler_params=_cparams(("arbitrary",)),
        name="merge",
    )(yf, yb, p, gn, yw, yn, p, p, p, wr, ww, wn, wo, x, m2, gf, m3, m4, wrt)


def _route_kernel(lt_ref, idx_ref, gate_ref, pos_ref, goff_ref, aff_ref, thr_ref, *, cap, n_groups):
    e = pl.program_id(0)
    n_exp, ag, _ = lt_ref.shape
    capp = idx_ref.shape[-1]

    @pl.when(e == 0)
    def _():
        lt = lt_ref[...]
        ex = jnp.exp(lt - jnp.max(lt, axis=0, keepdims=True))
        aff = ex / jnp.sum(ex, axis=0, keepdims=True)
        real = _iota(aff.shape, 1) < n_groups
        aff = jnp.where(real, aff, 0.0)
        aff_ref[...] = aff

        def count_ge(v):
            return jnp.sum(jnp.sum((aff >= v).astype(F32), axis=2, keepdims=True), axis=1, keepdims=True)

        def body(k, carry):
            lo, hi = carry
            mid = jnp.where(lo > 0.0, jnp.sqrt(lo) * jnp.sqrt(hi), hi * 2.0 ** -16)
            ge = count_ge(mid) >= float(cap)
            return jnp.where(ge, mid, lo), jnp.where(ge, hi, mid)

        lo, _ = lax.fori_loop(0, BISECT_STEPS, body, (jnp.zeros((n_exp, 1, 1), F32), jnp.full((n_exp, 1, 1), 2.0, F32)))
        big = jnp.where(aff >= lo, aff, 4.0)
        t = jnp.min(jnp.min(big, axis=2, keepdims=True), axis=1, keepdims=True)
        thr_ref[...] = jnp.broadcast_to(t, thr_ref.shape)

    a = aff_ref[e]
    t = thr_ref[e][0:1, :]
    gt = a > t
    eq = (a == t) & (_iota(a.shape, 0) < n_groups)
    tri_lane_strict = (_iota((LANES, LANES), 0) < _iota((LANES, LANES), 1)).astype(BF16)
    tri_lane_incl = (_iota((LANES, LANES), 0) <= _iota((LANES, LANES), 1)).astype(BF16)
    tri_grp_strict = (_iota((ag, ag), 1) < _iota((ag, ag), 0)).astype(BF16)
    tri_grp_incl = (_iota((ag, ag), 1) <= _iota((ag, ag), 0)).astype(BF16)

    def total(mask_f):
        return jnp.sum(jnp.sum(mask_f, axis=1, keepdims=True), axis=0, keepdims=True)

    def group_sum(mask_f):
        return jnp.broadcast_to(jnp.sum(mask_f, axis=1, keepdims=True), (ag, LANES)).astype(BF16)

    eq_f = eq.astype(F32)
    need = float(cap) - total(gt.astype(F32))
    rank_eq = _dot(tri_grp_strict, group_sum(eq_f)) + _dot(eq_f.astype(BF16), tri_lane_strict)
    sel = gt | (eq & (rank_eq < need))
    sel_f = sel.astype(F32)
    cl = _dot(sel_f.astype(BF16), tri_lane_incl)
    cg = _dot(tri_grp_incl, group_sum(sel_f))
    goff = cg - jnp.broadcast_to(jnp.sum(sel_f, axis=1, keepdims=True), (ag, LANES))
    pos_ref[...] = jnp.where(sel, goff + cl - 1.0, -1.0)
    diag = _iota((ag, ag), 0) == _iota((ag, ag), 1)
    goff_sq = goff if ag == LANES else goff[:, :ag]
    goff_ref[...] = jnp.sum(jnp.where(diag, goff_sq, 0.0), axis=0, keepdims=True).astype(I32)

    pp = _iota((ag, capp), 1, F32)
    cg_b = jnp.broadcast_to(cg[:, 0:1], (ag, capp))
    below = cg_b <= pp
    grp = jnp.sum(below.astype(F32), axis=0, keepdims=True)
    off = jnp.max(jnp.where(below, cg_b, 0.0), axis=0, keepdims=True)
    onehot = _iota((ag, capp), 0, F32) == grp
    in_grp = _dot_tn(cl.astype(BF16), onehot.astype(BF16))
    local = pp[0:1, :] - off
    lane_of = jnp.sum((in_grp <= local).astype(F32), axis=0, keepdims=True)
    live = pp[0:1, :] < float(cap)
    idx = jnp.where(live, grp * float(LANES) + lane_of, 0.0)
    idx_ref[...] = idx.astype(I32)
    aff_grp = _dot_tn(a, onehot.astype(F32), precision=HIGHEST)
    pick = _iota((LANES, capp), 0, F32) == lane_of
    gate = jnp.sum(jnp.where(pick, aff_grp, 0.0), axis=0, keepdims=True)
    gate_ref[...] = jnp.where(live, gate, 0.0)


def _route(logits_t, *, cap, n_groups):
    n_exp, ag, _ = logits_t.shape
    capp = -(-cap // LANES) * LANES
    return pl.pallas_call(
        functools.partial(_route_kernel, cap=cap, n_groups=n_groups),
        grid=(n_exp,),
        in_specs=[pl.BlockSpec(logits_t.shape, lambda e: (0, 0, 0))],
        out_specs=[pl.BlockSpec((None, 1, capp), lambda e: (e, 0, 0)),
                   pl.BlockSpec((None, 1, capp), lambda e: (e, 0, 0)),
                   pl.BlockSpec((None, ag, LANES), lambda e: (e, 0, 0)),
                   pl.BlockSpec((None, 1, ag), lambda e: (e, 0, 0))],
        out_shape=[jax.ShapeDtypeStruct((n_exp, 1, capp), I32), jax.ShapeDtypeStruct((n_exp, 1, capp), F32),
                   jax.ShapeDtypeStruct((n_exp, ag, LANES), F32), jax.ShapeDtypeStruct((n_exp, 1, ag), I32)],
        scratch_shapes=[pltpu.VMEM((n_exp, ag, LANES), F32), pltpu.VMEM((n_exp, SUBLANES, LANES), F32)],
        compiler_params=_cparams(("arbitrary",)),
        name="route",
    )(logits_t)


def _ffn_kernel(idx_ref, idxn_ref, gate_ref, hx_ref, wg_ref, wu_ref, wd_ref, o_ref, xbuf, sem, wgb, wub, wdb, *, tm):
    e = pl.program_id(0)
    j = pl.program_id(1)
    nt = pl.num_programs(1)
    step = e * nt + j
    last = pl.num_programs(0) * nt - 1

    sub = SUBLANES
    rows = tm * sub

    def row_copy(rows_ref, r, slot):
        src = hx_ref.at[pl.ds(pl.multiple_of(rows_ref[0, 0, r] * sub, sub), sub)]
        dst = xbuf.at[pl.ds(pl.multiple_of(slot * rows + r * sub, sub), sub)]
        return pltpu.make_async_copy(src, dst, sem.at[slot])

    def slot_copy(slot):
        return pltpu.make_async_copy(hx_ref.at[pl.ds(0, rows)], xbuf.at[pl.ds(pl.multiple_of(slot * rows, sub), rows)],
                                     sem.at[slot])

    @pl.when(step == 0)
    def _():
        def body(r, carry):
            row_copy(idx_ref, r, 0).start()
            return carry
        lax.fori_loop(0, tm, body, 0, unroll=8)

    @pl.when(j == 0)
    def _():
        wgb[...] = wg_ref[...].astype(BF16)
        wub[...] = wu_ref[...].astype(BF16)
        wdb[...] = wd_ref[...].astype(BF16)

    slot = step % 2
    nslot = 1 - slot
    for r in range(tm):
        row_copy(idxn_ref, r, nslot).start()
    slot_copy(slot).wait()
    base = slot * rows
    x = jnp.concatenate([xbuf[pl.ds(base + s, tm, stride=sub), :] for s in range(sub)], axis=1).astype(BF16)
    g = _dot(x, wgb[...])
    u = _dot(x, wub[...])
    hid = (g * jax.nn.sigmoid(g)) * u
    y = _dot(hid.astype(BF16), wdb[...])
    eye = _iota((tm, tm), 0) == _iota((tm, tm), 1)
    gcol = jnp.sum(jnp.where(eye, jnp.broadcast_to(gate_ref[0], (tm, tm)), 0.0), axis=1, keepdims=True)
    o_ref[...] = (y * gcol).astype(BF16)

    @pl.when(step == last)
    def _():
        slot_copy(nslot).wait()


def _expert_ffn(idx, gate, hx3, w_gate, w_up, w_down, *, layer, cap, tm):
    n_exp = idx.shape[0]
    d = w_gate.shape[2]
    f = w_gate.shape[3]
    assert d == SUBLANES * LANES, "a token row must be exactly one (8, 128) f32 tile"
    nt = cap // tm
    idx_t = idx[:, :, :cap].reshape(n_exp * nt, 1, tm)
    gate_t = gate[:, :, :cap].reshape(n_exp * nt, 1, tm)
    n_steps = n_exp * nt
    wspec = lambda a, b: pl.BlockSpec((None, None, a, b), lambda e, j: (layer, e, 0, 0))
    return pl.pallas_call(
        functools.partial(_ffn_kernel, tm=tm),
        grid=(n_exp, nt),
        in_specs=[pl.BlockSpec((1, 1, tm), lambda e, j: (e * nt + j, 0, 0), memory_space=pltpu.SMEM),
                  pl.BlockSpec((1, 1, tm), lambda e, j: (jnp.minimum(e * nt + j + 1, n_steps - 1), 0, 0), memory_space=pltpu.SMEM),
                  pl.BlockSpec((1, 1, tm), lambda e, j: (e * nt + j, 0, 0)),
                  pl.BlockSpec(memory_space=pl.ANY),
                  wspec(d, f), wspec(d, f), wspec(f, d)],
        out_specs=pl.BlockSpec((tm, d), lambda e, j: (e * nt + j, 0)),
        out_shape=jax.ShapeDtypeStruct((n_exp * cap, d), BF16),
        scratch_shapes=[pltpu.VMEM((2 * tm * SUBLANES, LANES), F32), pltpu.SemaphoreType.DMA((2,)),
                        pltpu.VMEM((d, f), BF16), pltpu.VMEM((d, f), BF16), pltpu.VMEM((f, d), BF16)],
        compiler_params=_cparams(("arbitrary", "arbitrary")),
        name="expert_ffn",
    )(idx_t, idx_t, gate_t, hx3, w_gate, w_up, w_down)


def _combine_kernel(ws_ref, nw_ref, *refs, n_exp, cap, win, gpt, final_norm):
    y_refs = refs[:n_exp]
    pos_ref, ye_ref, x_ref, m5_ref, gf_ref, o_ref, acc_ref, xwin, sem = refs[n_exp:]
    a = pl.program_id(0)
    ng = pl.num_programs(0)
    per = max(1, MXU_DEPTH // win)

    def slots(e):
        pos = pos_ref[e, pl.ds(a * gpt, gpt), :]
        return jnp.concatenate([jnp.broadcast_to(pos[j:j + 1, :], (win, LANES)) for j in range(gpt)], axis=1)

    lp = _iota((win, gpt * LANES), 0, F32)
    acc = None
    for e0 in range(0, n_exp, per):
        es = range(e0, min(e0 + per, n_exp))
        hot = jnp.concatenate([((slots(e) - (ws_ref[e * ng + a] - e * cap).astype(F32)) == lp).astype(BF16) for e in es],
                              axis=0)
        rows = jnp.concatenate([y_refs[e][...] for e in es], axis=0)
        part = _dot_tn(hot, rows)
        acc = part if acc is None else acc + part
    acc_ref[...] = acc

    for e in range(n_exp):
        first = ws_ref[e * ng + a] - e * cap

        def extra(k, carry, e=e, first=first):
            lo = first + k * win
            row = jnp.minimum(lo, cap - win)
            cp = pltpu.make_async_copy(ye_ref.at[pl.ds(pl.multiple_of(e * cap + row, BF16_ROWS), win)], xwin, sem)
            cp.start()
            cp.wait()
            pos = slots(e)
            hot = ((pos - row.astype(F32)) == lp) & (pos >= lo.astype(F32))
            acc_ref[...] += _dot_tn(hot.astype(BF16), xwin[...])
            return carry

        lax.fori_loop(1, nw_ref[e * ng + a], extra, 0)

    x_new = x_ref[...] + m5_ref[...] * acc_ref[...]
    if final_norm:
        ms = jnp.mean(x_new * x_new, axis=-1, keepdims=True)
        x_new = x_new * lax.rsqrt(ms + EPS) * gf_ref[...]
    o_ref[...] = x_new


def _combine(ye, posmap, goff, x, m5, g_final, *, cap, final_norm):
    n, d = x.shape
    n_exp = posmap.shape[0]
    gpt = min(COMBINE_GROUPS, n // LANES)
    ng = n // (gpt * LANES)
    tok = gpt * LANES
    win = min(COMBINE_WINDOW, cap)
    goff = goff[:, 0, :ng * gpt:gpt]
    end = jnp.concatenate([goff[:, 1:], jnp.full((n_exp, 1), cap, I32)], axis=1)
    start = jnp.minimum((goff // BF16_ROWS) * BF16_ROWS, cap - win)
    n_win = jnp.maximum((end - start + win - 1) // win, 1).reshape(-1)
    wstart = (start + jnp.arange(n_exp, dtype=I32)[:, None] * cap).reshape(-1)
    yspec = lambda e: pl.BlockSpec((pl.Element(win), pl.Element(d)),
                                   lambda a, ws, nw: (pl.multiple_of(ws[e * ng + a], BF16_ROWS), 0))
    gs = pltpu.PrefetchScalarGridSpec(
        num_scalar_prefetch=2,
        grid=(ng,),
        in_specs=[yspec(e) for e in range(n_exp)]
        + [pl.BlockSpec(posmap.shape, lambda a, ws, nw: (0, 0, 0)),
           pl.BlockSpec(memory_space=pl.ANY),
           pl.BlockSpec((tok, d), lambda a, ws, nw: (a, 0)),
           pl.BlockSpec((1, d), lambda a, ws, nw: (0, 0)), pl.BlockSpec((1, d), lambda a, ws, nw: (0, 0))],
        out_specs=pl.BlockSpec((tok, d), lambda a, ws, nw: (a, 0)),
        scratch_shapes=[pltpu.VMEM((tok, d), F32), pltpu.VMEM((win, d), BF16), pltpu.SemaphoreType.DMA(())],
    )
    return pl.pallas_call(
        functools.partial(_combine_kernel, n_exp=n_exp, cap=cap, win=win, gpt=gpt, final_norm=final_norm),
        grid_spec=gs,
        out_shape=jax.ShapeDtypeStruct((n, d), F32),
        compiler_params=_cparams(("arbitrary",)),
        name="combine_final" if final_norm else "combine",
    )(wstart, n_win, *([ye] * n_exp), posmap, ye, x, m5, g_final)


def _rope_tables(n):
    n_rows = n // GRID_W
    row = jnp.arange(n_rows).astype(F32)
    col = jnp.arange(GRID_W).astype(F32)

    def tables(d, signs):
        nf = d // 4
        inv = ROPE_BASE ** (-jnp.arange(nf, dtype=F32) / nf)
        ang_r, ang_c = row[:, None] * inv, col[:, None] * inv
        reps = LANES // (2 * nf)

        def lanes(f_r, f_c, factors):
            z_r, z_c = jnp.zeros_like(f_r), jnp.zeros_like(f_c)
            pat_r = jnp.concatenate([part for k in range(reps) for part in (f_r * factors[2 * k], z_r)], axis=1)
            pat_c = jnp.concatenate([part for k in range(reps) for part in (z_c, f_c * factors[2 * k + 1])], axis=1)
            return (pat_r[:, None, :] + pat_c[None, :, :]).reshape(n, LANES)

        ones = (1.0,) * (2 * reps)
        cos = lanes(jnp.cos(ang_r), jnp.cos(ang_c), ones)
        return [cos] + [lanes(jnp.sin(ang_r), jnp.sin(ang_c), sg) for sg in signs]

    c128, s128 = tables(RET_D, [(-1.0, -1.0, 1.0, 1.0)])
    c64, sa, sb = tables(HEAD_D, [(-1.0, -1.0, 0.0, 0.0) * 2, (0.0, 0.0, 1.0, 1.0) * 2])
    return c128, s128, c64, sa, sb


WIN_HEAD_ORDER = (0, 4, 1, 5, 2, 6, 3, 7)


def _permute_in_weight(w):
    d = w.shape[0]
    sizes = (("q_r", 512), ("k_r", 512), ("v_r", 512), ("g_r", 512), ("q_w", 512), ("k_w", 128), ("v_w", 128),
             ("q_n", 512), ("k_n", 512), ("v_n", 512), ("gates", 3 * d))
    off, lay = 0, {}
    for name, size in sizes:
        lay[name] = (off, off + size)
        off += size
    seg = lambda name: w[:, lay[name][0]:lay[name][1]]
    q0 = lay["q_w"][0]
    q_w = [w[:, q0 + h * HEAD_D:q0 + (h + 1) * HEAD_D] for h in WIN_HEAD_ORDER]
    pad = jnp.zeros((d, SEG - 2 * LANES), w.dtype)
    rope = jnp.concatenate([seg("q_r"), seg("k_r")] + q_w + [seg("k_w"), seg("v_w"), pad], axis=1).astype(BF16)
    plain = jnp.concatenate([seg("gates"), seg("v_r"), seg("g_r"), seg("q_n"), seg("k_n"), seg("v_n")], axis=1).astype(BF16)
    return rope, plain


def _permute_win_rows(w):
    return jnp.concatenate([w[h * HEAD_D:(h + 1) * HEAD_D] for h in WIN_HEAD_ORDER], axis=0).astype(BF16)


def kernel(x, c, ctx, c_ctx, w_mod, b_mod, g_mix, g_ffn, w_in, ret_decay_logit, ret_gn, w_ret, win_sink, w_win, na_rpb,
           w_na, w_out, w_router, w_exp_gate, w_exp_up, w_exp_down, g_final):
    _, n, d = x.shape
    l = ctx.shape[1]
    depth = w_in.shape[0]
    xs, cs = x[0], ctx[0]
    cc = jnp.zeros((SUBLANES, d), F32).at[0].set(c[0]).at[1].set(c_ctx)
    mods = _modulation(cc, w_mod, b_mod)
    c128, s128, c64, sa64, sb64 = _rope_tables(n)
    k_scale = RET_D ** -0.5
    plain_scales = jnp.ones((PLAIN_TILES,), F32).at[T_QN].set(Q_SCALE)
    ctx_rope_scales = jnp.ones((R_KVW + 1,), F32).at[R_KR].set(k_scale).at[R_QW].set(Q_SCALE)
    cap_x = CAPACITY_FACTOR * n // N_EXPERTS
    cap_c = CAPACITY_FACTOR * l // N_EXPERTS
    grp_c = -(-(l // LANES) // BF16_ROWS) * BF16_ROWS
    vec = lambda v: v.reshape(1, -1)

    for layer in range(depth):
        need_ctx = layer < depth - 1
        last = layer == depth - 1
        mx = [vec(mods[layer, 0, k * d:(k + 1) * d]) for k in range(N_MOD)]
        mc = [vec(mods[layer, 1, k * d:(k + 1) * d]) for k in range(N_MOD)]
        w_rope, w_plain = _permute_in_weight(w_in[layer])
        wr = w_ret[layer].astype(BF16)
        ww = _permute_win_rows(w_win[layer])
        wn = w_na[layer].astype(BF16)
        wo = w_out[layer].astype(BF16)
        wrt = w_router[layer].T
        sink = win_sink[layer].astype(F32) * LOG2E
        lg_rows = jnp.broadcast_to(ret_decay_logit[layer].astype(F32).reshape(2 * RET_HEADS, 1), (2 * RET_HEADS, LANES))
        gn = vec(ret_gn[layer])
        bias_tab = _na_bias_table(na_rpb[layer].astype(F32) * LOG2E)
        gmix, gffn = vec(g_mix[layer]), vec(g_ffn[layer])

        hx = _norm_call(xs, gmix, mx[0], mx[1], tm=min(NORM_ROWS, n))
        hc = _norm_call(cs, gmix, mc[0], mc[1], tm=l)
        p_x = _proj(hx, w_plain, plain_scales, tm=min(PROJ_ROWS, n))
        p_c = _proj(hc, w_plain, plain_scales, tm=l)
        r_c = _proj(hc, w_rope, ctx_rope_scales, tm=l)
        rope = functools.partial(_proj_rope, hx, w_rope, tm=min(PROJ_ROWS, n))
        q_r = rope(R_QR, (c128, s128), head_d=RET_D, scale=1.0, n_rot=SEG // LANES, name="proj_q_ret")
        k_r = rope(R_KR, (c128, s128), head_d=RET_D, scale=k_scale, n_rot=SEG // LANES, name="proj_k_ret")
        q_w = rope(R_QW, (c64, sa64, sb64), head_d=HEAD_D, scale=Q_SCALE, n_rot=SEG // LANES, name="proj_q_win")
        kv_w = rope(R_KVW, (c64, sa64, sb64), head_d=HEAD_D, scale=1.0, n_rot=1, name="proj_kv_win")

        yf, yb = _retention((q_r, 0), (k_r, 0), (p_x, T_VR), (r_c, R_KR), (p_c, T_VR), lg_rows, zero_init=False)
        yw = _window_attention((q_w, 0), (kv_w, 0), (r_c, R_KVW), sink)
        yn = _na_attention(p_x, p_c, bias_tab)
        xs, hx3, lt = _merge(yf, yb, p_x, gn, yw, yn, wr, ww, wn, wo, xs, mx[2], gffn, mx[3], mx[4], wrt, tm=min(MERGE_ROWS, n))

        idx, gate, posmap, goff = _route(lt.reshape(N_EXPERTS, n // LANES, LANES), cap=cap_x, n_groups=n // LANES)
        ye = _expert_ffn(idx, gate, hx3, w_exp_gate, w_exp_up, w_exp_down, layer=layer, cap=cap_x, tm=min(FFN_ROWS, cap_x))
        xs = _combine(ye, posmap, goff, xs, mx[5], vec(g_final), cap=cap_x, final_norm=last)

        if need_ctx:
            yfc, ybc = _retention((r_c, R_QR), (r_c, R_KR), (p_c, T_VR), (r_c, R_KR), (p_c, T_VR), lg_rows, zero_init=True)
            ywc, ync = _ctx_attention(r_c, p_c, sink)
            cs, hc3, ltc = _merge(yfc, ybc, p_c, gn, ywc, ync, wr, ww, wn, wo, cs, mc[2], gffn, mc[3], mc[4], wrt, tm=l)
            ltc = jnp.pad(ltc.reshape(N_EXPERTS, l // LANES, LANES), ((0, 0), (0, grp_c - l // LANES), (0, 0)))
            idc, gtc, posc, goffc = _route(ltc, cap=cap_c, n_groups=l // LANES)
            yec = _expert_ffn(idc, gtc, hc3, w_exp_gate, w_exp_up, w_exp_down, layer=layer, cap=cap_c, tm=cap_c)
            cs = _combine(yec, posc, goffc, cs, mc[5], vec(g_final), cap=cap_c, final_norm=False)

    return xs[None]
```

```python
import functools

import jax
import jax.numpy as jnp
import numpy as np
from jax import lax
from jax.experimental import pallas as pl
from jax.experimental.pallas import tpu as pltpu

F32, BF16, I32 = jnp.float32, jnp.bfloat16, jnp.int32
HIGHEST = lax.Precision.HIGHEST

GRID_W = 64
RET_HEADS, RET_D, RET_CHUNK = 4, 128, 128
WIN_HEADS, WIN_KV_HEADS, HEAD_D, WINDOW, WIN_BLOCK = 8, 2, 64, 128, 128
NA_HEADS, NA_ROWS, NA_COLS = 8, 8, 16
N_EXPERTS, CAPACITY_FACTOR = 16, 2
N_MOD = 6
ROPE_BASE = 10000.0
EPS = 1e-6
NEG_INF = -1e30

LANES = 128
SUBLANES = 8
BF16_ROWS = 16
VMEM_LIMIT = 56 * 1024 * 1024

SEG = 512
T_GATES, T_VR, T_GR, T_QN, T_KN, T_VN = 0, 6, 7, 8, 9, 10
PLAIN_TILES = 11
R_QR, R_KR, R_QW, R_KVW = 0, 1, 2, 3
LOG2E = 1.4426950408889634
Q_SCALE = HEAD_D ** -0.5 * LOG2E
NORM_ROWS = 1024
PROJ_ROWS = 2048
MERGE_ROWS = 512
FFN_ROWS = 512
RET_BLOCK = 2 * RET_CHUNK
WIN_QBLOCKS = 2
NA_QROWS = 4
NA_SLAB_ROWS = 12
NA_BIAS_ZERO = NA_SLAB_ROWS - 2
NA_BIAS_PAIRS = NA_BIAS_ZERO + NA_SLAB_ROWS - 1
COMBINE_GROUPS = 4
COMBINE_WINDOW = 128
MXU_DEPTH = 256
BISECT_STEPS = 48


def _cparams(sem):
    return pltpu.CompilerParams(dimension_semantics=sem, vmem_limit_bytes=VMEM_LIMIT)


def _dot(a, b):
    return jnp.dot(a, b, preferred_element_type=F32)


def _dot_nt(a, b, precision=None):
    return lax.dot_general(a, b, (((1,), (1,)), ((), ())), precision=precision, preferred_element_type=F32)


def _dot_tn(a, b, precision=None):
    return lax.dot_general(a, b, (((0,), (0,)), ((), ())), precision=precision, preferred_element_type=F32)


def _iota(shape, dim, dtype=I32):
    return lax.broadcasted_iota(I32, shape, dim).astype(dtype)


def _mod_kernel(s_ref, w_ref, b_ref, o_ref):
    s = s_ref[...]
    s = s * jax.nn.sigmoid(s)
    o_ref[...] = jnp.dot(s, w_ref[...], precision=HIGHEST, preferred_element_type=F32) + b_ref[...]


def _modulation(cc, w_mod, b_mod):
    depth, d, md = w_mod.shape
    tn = 1536
    return pl.pallas_call(
        _mod_kernel,
        grid=(depth, md // tn),
        in_specs=[
            pl.BlockSpec((SUBLANES, d), lambda l, j: (0, 0)),
            pl.BlockSpec((None, d, tn), lambda l, j: (l, 0, j)),
            pl.BlockSpec((None, 1, tn), lambda l, j: (l, 0, j)),
        ],
        out_specs=pl.BlockSpec((None, SUBLANES, tn), lambda l, j: (l, 0, j)),
        out_shape=jax.ShapeDtypeStruct((depth, SUBLANES, md), F32),
        compiler_params=_cparams(("arbitrary", "arbitrary")),
        name="modulation",
    )(cc, w_mod, b_mod.reshape(depth, 1, md))


def _norm_mod(x, g, shift, scale):
    ms = jnp.mean(x * x, axis=-1, keepdims=True)
    y = x * lax.rsqrt(ms + EPS) * g
    return y * (1.0 + scale) + shift


def _norm_kernel(x_ref, g_ref, sh_ref, sc_ref, o_ref):
    o_ref[...] = _norm_mod(x_ref[...], g_ref[...], sh_ref[...], sc_ref[...]).astype(BF16)


def _norm_call(x, g, shift, scale, *, tm):
    n, d = x.shape
    vec = lambda: pl.BlockSpec((1, d), lambda i: (0, 0))
    return pl.pallas_call(
        _norm_kernel,
        grid=(n // tm,),
        in_specs=[pl.BlockSpec((tm, d), lambda i: (i, 0)), vec(), vec(), vec()],
        out_specs=pl.BlockSpec((tm, d), lambda i: (i, 0)),
        out_shape=jax.ShapeDtypeStruct((n, d), BF16),
        compiler_params=_cparams(("arbitrary",)),
        name="norm_modulate",
    )(x, g, shift, scale)


def _proj_kernel(scale_ref, hx_ref, w_ref, o_ref):
    o_ref[...] = (_dot(hx_ref[...], w_ref[...]) * scale_ref[pl.program_id(1)]).astype(BF16)


def _proj(hx, w, scales, *, tm):
    n, d = hx.shape
    tiles = w.shape[1] // SEG
    return pl.pallas_call(
        _proj_kernel,
        grid=(n // tm, tiles),
        in_specs=[pl.BlockSpec(memory_space=pltpu.SMEM),
                  pl.BlockSpec((tm, d), lambda i, j: (i, 0)), pl.BlockSpec((d, SEG), lambda i, j: (0, j))],
        out_specs=pl.BlockSpec((tm, SEG), lambda i, j: (i, j)),
        out_shape=jax.ShapeDtypeStruct((n, tiles * SEG), BF16),
        compiler_params=_cparams(("arbitrary", "arbitrary")),
        name="projection",
    )(scales, hx, w)


def _proj_rope_kernel(hx_ref, w_ref, *refs, head_d, scale, n_rot):
    tabs, o_ref = refs[:-1], refs[-1]
    acc = _dot(hx_ref[...], w_ref[...])
    if head_d == RET_D:
        c = tabs[0][...] * scale
        s = tabs[1][...] * scale
        rot = lambda a: a * c + pltpu.roll(a, 64, 1) * s
    else:
        c = tabs[0][...] * scale
        sa = tabs[1][...] * scale
        sb = tabs[2][...] * scale
        rot = lambda a: a * c + pltpu.roll(a, 96, 1) * sa + pltpu.roll(a, 32, 1) * sb
    groups = [acc[:, k * LANES:(k + 1) * LANES] for k in range(SEG // LANES)]
    o_ref[...] = jnp.concatenate([rot(a) if k < n_rot else a for k, a in enumerate(groups)], axis=1).astype(BF16)


def _proj_rope(hx, w, tile, tabs, *, head_d, scale, n_rot, tm, name):
    n, d = hx.shape
    return pl.pallas_call(
        functools.partial(_proj_rope_kernel, head_d=head_d, scale=scale, n_rot=n_rot),
        grid=(n // tm,),
        in_specs=[pl.BlockSpec((tm, d), lambda i: (i, 0)), pl.BlockSpec((d, SEG), lambda i: (0, tile))]
        + [pl.BlockSpec((tm, LANES), lambda i: (i, 0)) for _ in tabs],
        out_specs=pl.BlockSpec((tm, SEG), lambda i: (i, 0)),
        out_shape=jax.ShapeDtypeStruct((n, SEG), BF16),
        compiler_params=_cparams(("arbitrary",)),
        name=name,
    )(hx, w, *tabs)


def _ret_kernel(lg_ref, qf_ref, kf_ref, vf_ref, qb_ref, kb_ref, vb_ref, kc_ref, vc_ref, yf_ref, yb_ref,
                rf_ref, rb_ref, dmat_ref, vec_ref, *, zero_init):
    c = pl.program_id(0)
    ch = qf_ref.shape[0]
    dh = RET_D
    states = (rf_ref, rb_ref)

    @pl.when(c == 0)
    def _():
        lg = jax.nn.log_sigmoid(lg_ref[...])
        ii = _iota((ch, ch), 0, F32)
        jj = _iota((ch, ch), 1, F32)
        iv = _iota((ch, dh), 0, F32)
        n_ctx = kc_ref.shape[0]
        mm = _iota((n_ctx, dh), 0, F32)
        for d in range(2):
            for h in range(RET_HEADS):
                row = lg[RET_HEADS * d + h:RET_HEADS * d + h + 1, :]
                l = jnp.concatenate([jnp.broadcast_to(row, (ch, dh))] * (ch // dh), axis=1)
                lv = jnp.broadcast_to(row, (ch, dh))
                if d == 0:
                    diff = ii - jj
                    dmat_ref[d, h] = jnp.where(diff >= 0.0, jnp.exp(jnp.maximum(diff, 0.0) * l), 0.0)
                    vec_ref[d, h, 0] = jnp.exp((ch - 1.0 - iv) * lv)
                    vec_ref[d, h, 1] = jnp.exp((iv + 1.0) * lv)
                else:
                    diff = jj - ii
                    dmat_ref[d, h] = jnp.where(diff >= 1.0, jnp.exp(jnp.maximum(diff, 0.0) * l), 0.0)
                    vec_ref[d, h, 0] = jnp.exp(iv * lv)
                    vec_ref[d, h, 1] = jnp.exp((ch - iv) * lv)
                vec_ref[d, h, 2] = jnp.exp(float(ch) * lv)
                if zero_init:
                    states[d][h] = jnp.zeros((dh, dh), F32)
                else:
                    lc = jnp.broadcast_to(row, (n_ctx, dh))
                    w = jnp.exp((n_ctx - 1.0 - mm) * lc) if d == 0 else jnp.exp(mm * lc)
                    sl = slice(h * dh, (h + 1) * dh)
                    kw = (kc_ref[:, sl].astype(F32) * w).astype(BF16)
                    states[d][h] = _dot_tn(kw, vc_ref[:, sl])

    for d, (q_ref, k_ref, v_ref, y_ref) in enumerate(((qf_ref, kf_ref, vf_ref, yf_ref), (qb_ref, kb_ref, vb_ref, yb_ref))):
        for h in range(RET_HEADS):
            sl = slice(h * dh, (h + 1) * dh)
            q, k, v = q_ref[:, sl], k_ref[:, sl], v_ref[:, sl]
            r = states[d][h]
            a = (_dot_nt(q, k) * dmat_ref[d, h]).astype(BF16)
            inner = _dot(a, v)
            cross = _dot(q, r.astype(BF16)) * vec_ref[d, h, 1]
            y_ref[:, sl] = inner + cross
            kz = (k.astype(F32) * vec_ref[d, h, 0]).astype(BF16)
            states[d][h] = vec_ref[d, h, 2][:dh, :] * r + _dot_tn(kz, v)


def _retention(q, k, v, kc, vc, lg_rows, *, zero_init):
    n = q[0].shape[0]
    blk = min(RET_BLOCK, n)
    nc = n // blk
    w = RET_HEADS * RET_D
    fwd = lambda sg: pl.BlockSpec((blk, w), lambda c: (c, sg[1]))
    bwd = lambda sg: pl.BlockSpec((blk, w), lambda c: (nc - 1 - c, sg[1]))
    ctx = lambda sg: pl.BlockSpec((sg[0].shape[0], w), lambda c: (0, sg[1]))
    return pl.pallas_call(
        functools.partial(_ret_kernel, zero_init=zero_init),
        grid=(nc,),
        in_specs=[pl.BlockSpec((SUBLANES, LANES), lambda c: (0, 0)),
                  fwd(q), fwd(k), fwd(v), bwd(q), bwd(k), bwd(v), ctx(kc), ctx(vc)],
        out_specs=[pl.BlockSpec((blk, w), lambda c: (c, 0)), pl.BlockSpec((blk, w), lambda c: (nc - 1 - c, 0))],
        out_shape=[jax.ShapeDtypeStruct((n, w), F32), jax.ShapeDtypeStruct((n, w), F32)],
        scratch_shapes=[pltpu.VMEM((RET_HEADS, RET_D, RET_D), F32), pltpu.VMEM((RET_HEADS, RET_D, RET_D), F32),
                        pltpu.VMEM((2, RET_HEADS, blk, blk), F32), pltpu.VMEM((2, RET_HEADS, 3, blk, RET_D), F32)],
        compiler_params=_cparams(("arbitrary",)),
        name="retention_ctx" if zero_init else "retention",
    )(lg_rows, q[0], k[0], v[0], q[0], k[0], v[0], kc[0], vc[0])


def _half_mask(x, half):
    lane = _iota(x.shape, 1)
    keep = (lane < HEAD_D) if half == 0 else (lane >= HEAD_D)
    return jnp.where(keep, x, jnp.zeros_like(x))


def _softmax_pv(s, v, extra=None):
    m = jnp.max(s, axis=1, keepdims=True)
    if extra is not None:
        m = jnp.maximum(m, extra)
    p = jnp.exp2(s - m)
    den = jnp.sum(p, axis=1, keepdims=True)
    if extra is not None:
        den = den + jnp.exp2(extra - m)
    return _dot(p.astype(BF16), v) / den


def _softmax_pv_t(st, v, extra=None):
    m = jnp.max(st, axis=0, keepdims=True)
    if extra is not None:
        m = jnp.maximum(m, extra)
    p = jnp.exp2(st - m)
    den = jnp.sum(p, axis=0, keepdims=True)
    if extra is not None:
        den = den + jnp.exp2(extra - m)
    return _dot_tn(v, p.astype(BF16)) / den


def _win_kernel(sink_ref, q_ref, *refs):
    nkb = WIN_QBLOCKS + 2
    k_refs, v_refs = refs[:nkb], refs[nkb:2 * nkb]
    kx_ref, vx_ref, mask_ref, o_ref = refs[2 * nkb:]
    s = pl.program_id(0)
    last = pl.num_programs(0) - 1
    nq = WIN_QBLOCKS * WIN_BLOCK
    nk = nkb * WIN_BLOCK
    k_all = jnp.concatenate([r[...] for r in k_refs] + [kx_ref[...]], axis=0)
    v_all = jnp.concatenate([r[...] for r in v_refs] + [vx_ref[...]], axis=0)
    n_cols = SEG // LANES
    qs = jnp.concatenate([q_ref[:, c * LANES:(c + 1) * LANES] for c in range(n_cols)], axis=0)
    variant = jnp.where(s == 0, 1, 0) + jnp.where(s == last, 2, 0)
    valid = mask_ref[variant] > 0.5
    valid = jnp.concatenate([valid] * n_cols, axis=1)
    out_t = None
    for g in range(WIN_KV_HEADS):
        st = _dot_nt(_half_mask(k_all, g), qs)
        st = jnp.concatenate([jnp.where(valid, st[:nk], NEG_INF), st[nk:]], axis=0)
        sink = jnp.concatenate([jnp.full((1, nq), sink_ref[n_cols * g + c], F32) for c in range(n_cols)], axis=1)
        o = _softmax_pv_t(st, _half_mask(v_all, g), sink)
        out_t = o if out_t is None else out_t + o
    o = out_t.T
    o_ref[...] = jnp.concatenate([o[c * nq:(c + 1) * nq, :] for c in range(n_cols)], axis=1).astype(BF16)


def _win_valid_table():
    nkb = WIN_QBLOCKS + 2
    qpos = np.arange(WIN_QBLOCKS * WIN_BLOCK)[None, :]
    kpos = np.arange(nkb * WIN_BLOCK)[:, None] - WIN_BLOCK
    band = np.abs(kpos - qpos) <= WINDOW
    tabs = []
    for variant in range(4):
        ok = band.copy()
        if variant & 1:
            ok &= kpos >= 0
        if variant & 2:
            ok &= kpos < WIN_QBLOCKS * WIN_BLOCK
        tabs.append(ok)
    return jnp.asarray(np.stack(tabs).astype(np.float32))


def _window_attention(q, kv, kv_ctx, sink):
    n = q[0].shape[0]
    nb = n // WIN_BLOCK
    assert nb % WIN_QBLOCKS == 0
    l = kv_ctx[0].shape[0]
    nkb = WIN_QBLOCKS + 2
    per = SEG // LANES
    blk = lambda off, j: pl.BlockSpec((WIN_BLOCK, LANES),
                                      lambda i: (jnp.clip(WIN_QBLOCKS * i - 1 + j, 0, nb - 1), per * kv[1] + off))
    mask_tab = _win_valid_table()
    return pl.pallas_call(
        _win_kernel,
        grid=(nb // WIN_QBLOCKS,),
        in_specs=[pl.BlockSpec(memory_space=pltpu.SMEM),
                  pl.BlockSpec((WIN_QBLOCKS * WIN_BLOCK, SEG), lambda i: (i, q[1]))]
        + [blk(0, j) for j in range(nkb)] + [blk(1, j) for j in range(nkb)]
        + [pl.BlockSpec((l, LANES), lambda i: (0, per * kv_ctx[1])), pl.BlockSpec((l, LANES), lambda i: (0, per * kv_ctx[1] + 1)),
           pl.BlockSpec(mask_tab.shape, lambda i: (0, 0, 0))],
        out_specs=pl.BlockSpec((WIN_QBLOCKS * WIN_BLOCK, SEG), lambda i: (i, 0)),
        out_shape=jax.ShapeDtypeStruct((n, SEG), BF16),
        compiler_params=_cparams(("arbitrary",)),
        name="window_attention",
    )(sink, q[0], *([kv[0]] * (2 * nkb)), kv_ctx[0], kv_ctx[0], mask_tab)


def _na_slab_start(s, half_rows):
    return jnp.clip((NA_QROWS // 2) * s - NA_ROWS // 4, 0, half_rows - NA_SLAB_ROWS // 2)


def _na_kernel(q_ref, *refs, rows):
    nslab = NA_SLAB_ROWS // 2
    k_refs, v_refs = refs[:nslab], refs[nslab:2 * nslab]
    kx_ref, vx_ref, bias_ref, mask_ref, o_ref = refs[2 * nslab:]
    s = pl.program_id(0)
    last = pl.num_programs(0) - 1
    w = GRID_W
    nq = NA_QROWS * w
    nk = NA_SLAB_ROWS * w
    delta = 2 * _na_slab_start(s, rows // 2) - NA_QROWS * s
    variant = jnp.where(s == 0, 0, jnp.where(s == last, 2, 1))
    valid = mask_ref[variant] > 0.5
    valid = jnp.concatenate([valid, valid], axis=1)
    row = _iota((LANES, nq), 0)
    for pair in range(NA_HEADS // 2):
        sl = slice(pair * LANES, (pair + 1) * LANES)
        q = q_ref[:, sl]
        qs = jnp.concatenate([_half_mask(q, 0), _half_mask(q, 1)], axis=0)
        k_all = jnp.concatenate([r[:, sl] for r in k_refs] + [kx_ref[:, sl]], axis=0)
        v_all = jnp.concatenate([r[:, sl] for r in v_refs] + [vx_ref[:, sl]], axis=0)
        st = _dot_nt(k_all, qs)
        bias = jnp.concatenate(
            [jnp.concatenate([bias_ref[2 * pair + u, delta + 2 * i - a + NA_BIAS_ZERO]
                              for u in range(2) for a in range(0, NA_QROWS, 2)], axis=1) for i in range(nslab)], axis=0)
        s_loc = jnp.where(valid, st[:nk] + bias, NEG_INF)
        o = _softmax_pv_t(jnp.concatenate([s_loc, st[nk:]], axis=0), v_all)
        o_ref[:, sl] = jnp.where(row < HEAD_D, o[:, :nq], o[:, nq:]).T.astype(BF16)


def _na_valid_table():
    kk = np.arange(NA_SLAB_ROWS)[:, None, None, None]
    ck = np.arange(GRID_W)[None, :, None, None]
    a = np.arange(NA_QROWS)[None, None, :, None]
    cq = np.arange(GRID_W)[None, None, None, :]
    c_start = np.clip(cq - NA_COLS // 2, 0, GRID_W - NA_COLS)
    col_ok = (ck >= c_start) & (ck < c_start + NA_COLS)
    first_row = (0 * a, a, 0 * a + NA_SLAB_ROWS - NA_ROWS)
    tabs = [(col_ok & (kk >= f) & (kk < f + NA_ROWS)).reshape(NA_SLAB_ROWS * GRID_W, NA_QROWS * GRID_W) for f in first_row]
    return jnp.asarray(np.stack(tabs).astype(np.float32))


def _na_attention(p, p_ctx, bias_tab):
    n = p.shape[0]
    rows = n // GRID_W
    assert rows % NA_QROWS == 0 and rows >= NA_SLAB_ROWS
    steps = rows // NA_QROWS
    l = p_ctx.shape[0]
    slab = lambda t, i: pl.BlockSpec((2 * GRID_W, SEG), lambda s: (_na_slab_start(s, rows // 2) + i, t))
    nslab = NA_SLAB_ROWS // 2
    mask_tab = _na_valid_table()
    return pl.pallas_call(
        functools.partial(_na_kernel, rows=rows),
        grid=(steps,),
        in_specs=[pl.BlockSpec((NA_QROWS * GRID_W, SEG), lambda s: (s, T_QN))]
        + [slab(T_KN, i) for i in range(nslab)] + [slab(T_VN, i) for i in range(nslab)]
        + [pl.BlockSpec((l, SEG), lambda s: (0, T_KN)), pl.BlockSpec((l, SEG), lambda s: (0, T_VN)),
           pl.BlockSpec(bias_tab.shape, lambda s: (0, 0, 0, 0)), pl.BlockSpec(mask_tab.shape, lambda s: (0, 0, 0))],
        out_specs=pl.BlockSpec((NA_QROWS * GRID_W, SEG), lambda s: (s, 0)),
        out_shape=jax.ShapeDtypeStruct((n, SEG), BF16),
        compiler_params=_cparams(("arbitrary",)),
        name="neighbourhood_attention",
    )(p, *([p] * (2 * nslab)), p_ctx, p_ctx, bias_tab, mask_tab)


def _na_bias_table(rpb):
    n_r, n_c = 2 * NA_ROWS - 1, 2 * NA_COLS - 1
    rpb = rpb.astype(F32)
    e_i, uk_i, wq_i = np.meshgrid(np.arange(NA_BIAS_PAIRS), np.arange(2), np.arange(2), indexing="ij")
    src = np.clip(e_i - NA_BIAS_ZERO + uk_i - wq_i + NA_ROWS - 1, 0, n_r - 1)
    pick = jnp.asarray((src[..., None] == np.arange(n_r)).astype(np.float32))
    rows = jnp.einsum("euwr,hrj->heuwj", pick, rpb, precision=HIGHEST)
    ck = np.arange(GRID_W)[:, None]
    cq = np.arange(GRID_W)[None, :]
    ci = np.clip(ck - cq, -(NA_COLS - 1), NA_COLS - 1) + NA_COLS - 1
    sel = (ci[None] == np.arange(n_c)[:, None, None]).astype(np.float32)
    sel2 = np.zeros((2, n_c, GRID_W, 2, GRID_W), np.float32)
    for wq in range(2):
        sel2[wq, :, :, wq, :] = sel
    sel2 = jnp.asarray(sel2.reshape(2 * n_c, GRID_W, 2 * GRID_W))
    rows = rows.reshape(rpb.shape[0], NA_BIAS_PAIRS, 2, 2 * n_c)
    tab = jnp.einsum("heuj,jkl->heukl", rows, sel2, precision=HIGHEST)
    return tab.reshape(rpb.shape[0], NA_BIAS_PAIRS, 2 * GRID_W, 2 * GRID_W)


def _ctx_attn_kernel(sink_ref, r_ref, p_ref, ow_ref, on_ref):
    l = p_ref.shape[0]
    n_cols = SEG // LANES
    k_all = r_ref[:, R_KVW * SEG:R_KVW * SEG + LANES]
    v_all = r_ref[:, R_KVW * SEG + LANES:R_KVW * SEG + 2 * LANES]
    qs = jnp.concatenate([r_ref[:, R_QW * SEG + c * LANES:R_QW * SEG + (c + 1) * LANES] for c in range(n_cols)], axis=0)
    outs = []
    for g in range(WIN_KV_HEADS):
        s = _dot_nt(qs, _half_mask(k_all, g))
        sink = jnp.concatenate([jnp.full((l, 1), sink_ref[n_cols * g + c], F32) for c in range(n_cols)], axis=0)
        outs.append(_softmax_pv(s, _half_mask(v_all, g), sink))
    o = outs[0] + outs[1]
    ow_ref[...] = jnp.concatenate([o[c * l:(c + 1) * l, :] for c in range(n_cols)], axis=1).astype(BF16)
    for pair in range(NA_HEADS // 2):
        sl = lambda t: slice(t * SEG + pair * LANES, t * SEG + (pair + 1) * LANES)
        q, k, v = p_ref[:, sl(T_QN)], p_ref[:, sl(T_KN)], p_ref[:, sl(T_VN)]
        out = None
        for u in range(2):
            o = _softmax_pv(_dot_nt(q, _half_mask(k, u)), _half_mask(v, u))
            out = o if out is None else out + o
        on_ref[:, pair * LANES:(pair + 1) * LANES] = out.astype(BF16)


def _ctx_attention(r_ctx, p_ctx, sink):
    l = p_ctx.shape[0]
    return pl.pallas_call(
        _ctx_attn_kernel,
        in_specs=[pl.BlockSpec(memory_space=pltpu.SMEM), pl.BlockSpec(r_ctx.shape, lambda: (0, 0)),
                  pl.BlockSpec(p_ctx.shape, lambda: (0, 0))],
        out_specs=[pl.BlockSpec((l, SEG), lambda: (0, 0)), pl.BlockSpec((l, SEG), lambda: (0, 0))],
        out_shape=[jax.ShapeDtypeStruct((l, SEG), BF16), jax.ShapeDtypeStruct((l, SEG), BF16)],
        compiler_params=pltpu.CompilerParams(vmem_limit_bytes=VMEM_LIMIT),
        name="context_attention",
    )(sink, r_ctx, p_ctx)


def _merge_kernel(yf_ref, yb_ref, gr_ref, gn_ref, yw_ref, yn_ref, ga_ref, gb_ref, gc_ref,
                  wr_ref, ww_ref, wn_ref, wo_ref, x_ref, m2_ref, gf_ref, m3_ref, m4_ref, wrt_ref,
                  xo_ref, hx_ref, lt_ref):
    y = yf_ref[...] + yb_ref[...]
    parts = []
    for h in range(RET_HEADS):
        yh = y[:, h * RET_D:(h + 1) * RET_D]
        mu = jnp.mean(yh, axis=-1, keepdims=True)
        var = jnp.mean(jnp.square(yh - mu), axis=-1, keepdims=True)
        parts.append((yh - mu) * lax.rsqrt(var + EPS))
    g = gr_ref[...].astype(F32)
    ya = jnp.concatenate(parts, axis=1) * gn_ref[...] * (g * jax.nn.sigmoid(g))
    za = _dot(ya.astype(BF16), wr_ref[...])
    zb = _dot(yw_ref[...], ww_ref[...])
    zc = _dot(yn_ref[...], wn_ref[...])
    sig = lambda r: jax.nn.sigmoid(r[...].astype(F32))
    mix = sig(ga_ref) * za + sig(gb_ref) * zb + sig(gc_ref) * zc
    x_new = x_ref[...] + m2_ref[...] * _dot(mix.astype(BF16), wo_ref[...])
    xo_ref[...] = x_new
    h2 = _norm_mod(x_new, gf_ref[...], m3_ref[...], m4_ref[...])
    n_sub = h2.shape[1] // LANES
    for s in range(n_sub):
        hx_ref[pl.ds(s, h2.shape[0], stride=n_sub), :] = h2[:, s * LANES:(s + 1) * LANES]
    split = lambda a: (a.astype(BF16), (a - a.astype(BF16).astype(F32)).astype(BF16))
    (w_hi, w_lo), (h_hi, h_lo) = split(wrt_ref[...]), split(h2)
    lt_ref[...] = _dot_nt(w_hi, h_hi) + _dot_nt(w_lo, h_hi) + _dot_nt(w_hi, h_lo)


def _merge(yf, yb, p, gn, yw, yn, wr, ww, wn, wo, x, m2, gf, m3, m4, wrt, *, tm):
    n, d = x.shape
    row = lambda wdt, t: pl.BlockSpec((tm, wdt), lambda i: (i, t))
    full = lambda a: pl.BlockSpec(a.shape, lambda i: (0,) * a.ndim)
    gate0 = T_GATES * SEG // d
    return pl.pallas_call(
        _merge_kernel,
        grid=(n // tm,),
        in_specs=[row(SEG, 0), row(SEG, 0), row(SEG, T_GR), full(gn), row(SEG, 0), row(SEG, 0),
                  row(d, gate0), row(d, gate0 + 1), row(d, gate0 + 2),
                  full(wr), full(ww), full(wn), full(wo), row(d, 0), full(m2), full(gf), full(m3), full(m4), full(wrt)],
        out_specs=[pl.BlockSpec((tm, d), lambda i: (i, 0)),
                   pl.BlockSpec((tm * (d // LANES), LANES), lambda i: (i, 0)),
                   pl.BlockSpec((N_EXPERTS, tm), lambda i: (0, i))],
        out_shape=[jax.ShapeDtypeStruct((n, d), F32), jax.ShapeDtypeStruct((n * (d // LANES), LANES), F32),
                   jax.ShapeDtypeStruct((N_EXPERTS, n), F32)],
        compiler_params=_cparams(("arbitrary",)),
        name="merge",
    )(yf, yb, p, gn, yw, yn, p, p, p, wr, ww, wn, wo, x, m2, gf, m3, m4, wrt)


def _route_kernel(lt_ref, idx_ref, gate_ref, pos_ref, goff_ref, aff_ref, thr_ref, *, cap, n_groups):
    e = pl.program_id(0)
    n_exp, ag, _ = lt_ref.shape
    capp = idx_ref.shape[-1]

    @pl.when(e == 0)
    def _():
        lt = lt_ref[...]
        ex = jnp.exp(lt - jnp.max(lt, axis=0, keepdims=True))
        aff = ex / jnp.sum(ex, axis=0, keepdims=True)
        real = _iota(aff.shape, 1) < n_groups
        aff = jnp.where(real, aff, 0.0)
        aff_ref[...] = aff

        def count_ge(v):
            return jnp.sum(jnp.sum((aff >= v).astype(F32), axis=1, keepdims=True), axis=2, keepdims=True)

        def body(k, carry):
            lo, hi = carry
            mid = jnp.where(lo > 0.0, jnp.sqrt(lo) * jnp.sqrt(hi), hi * 2.0 ** -16)
            ge = count_ge(mid) >= float(cap)
            return jnp.where(ge, mid, lo), jnp.where(ge, hi, mid)

        lo, _ = lax.fori_loop(0, BISECT_STEPS, body, (jnp.zeros((n_exp, 1, 1), F32), jnp.full((n_exp, 1, 1), 2.0, F32)))
        big = jnp.where(aff >= lo, aff, 4.0)
        t = jnp.min(jnp.min(big, axis=1, keepdims=True), axis=2, keepdims=True)
        thr_ref[...] = jnp.broadcast_to(t, thr_ref.shape)

    a = aff_ref[e]
    t = thr_ref[e][0:1, :]
    gt = a > t
    eq = (a == t) & (_iota(a.shape, 0) < n_groups)
    tri_lane_strict = (_iota((LANES, LANES), 0) < _iota((LANES, LANES), 1)).astype(BF16)
    tri_lane_incl = (_iota((LANES, LANES), 0) <= _iota((LANES, LANES), 1)).astype(BF16)
    tri_grp_strict = (_iota((ag, ag), 1) < _iota((ag, ag), 0)).astype(BF16)
    tri_grp_incl = (_iota((ag, ag), 1) <= _iota((ag, ag), 0)).astype(BF16)

    def total(mask_f):
        return jnp.sum(jnp.sum(mask_f, axis=1, keepdims=True), axis=0, keepdims=True)

    def group_sum(mask_f):
        return jnp.broadcast_to(jnp.sum(mask_f, axis=1, keepdims=True), (ag, LANES)).astype(BF16)

    eq_f = eq.astype(F32)
    need = float(cap) - total(gt.astype(F32))
    rank_eq = _dot(tri_grp_strict, group_sum(eq_f)) + _dot(eq_f.astype(BF16), tri_lane_strict)
    sel = gt | (eq & (rank_eq < need))
    sel_f = sel.astype(F32)
    cl = _dot(sel_f.astype(BF16), tri_lane_incl)
    cg = _dot(tri_grp_incl, group_sum(sel_f))
    goff = cg - jnp.broadcast_to(jnp.sum(sel_f, axis=1, keepdims=True), (ag, LANES))
    pos_ref[...] = jnp.where(sel, goff + cl - 1.0, -1.0)
    diag = _iota((ag, ag), 0) == _iota((ag, ag), 1)
    goff_sq = goff if ag == LANES else goff[:, :ag]
    goff_ref[...] = jnp.sum(jnp.where(diag, goff_sq, 0.0), axis=0, keepdims=True).astype(I32)

    pp = _iota((ag, capp), 1, F32)
    cg_b = jnp.broadcast_to(cg[:, 0:1], (ag, capp))
    below = cg_b <= pp
    grp = jnp.sum(below.astype(F32), axis=0, keepdims=True)
    off = jnp.max(jnp.where(below, cg_b, 0.0), axis=0, keepdims=True)
    onehot = _iota((ag, capp), 0, F32) == grp
    in_grp = _dot_tn(cl.astype(BF16), onehot.astype(BF16))
    local = pp[0:1, :] - off
    lane_of = jnp.sum((in_grp <= local).astype(F32), axis=0, keepdims=True)
    live = pp[0:1, :] < float(cap)
    idx = jnp.where(live, grp * float(LANES) + lane_of, 0.0)
    idx_ref[...] = idx.astype(I32)
    a_hi = a.astype(BF16)
    r1 = a - a_hi.astype(F32)
    a_mid = r1.astype(BF16)
    a_lo = (r1 - a_mid.astype(F32)).astype(BF16)
    hot = onehot.astype(BF16)
    aff_grp = (_dot_tn(a_hi, hot) + _dot_tn(a_mid, hot)) + _dot_tn(a_lo, hot)
    pick = _iota((LANES, capp), 0, F32) == lane_of
    gate = jnp.sum(jnp.where(pick, aff_grp, 0.0), axis=0, keepdims=True)
    gate_ref[...] = jnp.where(live, gate, 0.0)


def _route(logits_t, *, cap, n_groups):
    n_exp, ag, _ = logits_t.shape
    capp = -(-cap // LANES) * LANES
    return pl.pallas_call(
        functools.partial(_route_kernel, cap=cap, n_groups=n_groups),
        grid=(n_exp,),
        in_specs=[pl.BlockSpec(logits_t.shape, lambda e: (0, 0, 0))],
        out_specs=[pl.BlockSpec((None, 1, capp), lambda e: (e, 0, 0)),
                   pl.BlockSpec((None, 1, capp), lambda e: (e, 0, 0)),
                   pl.BlockSpec((None, ag, LANES), lambda e: (e, 0, 0)),
                   pl.BlockSpec((None, 1, ag), lambda e: (e, 0, 0))],
        out_shape=[jax.ShapeDtypeStruct((n_exp, 1, capp), I32), jax.ShapeDtypeStruct((n_exp, 1, capp), F32),
                   jax.ShapeDtypeStruct((n_exp, ag, LANES), F32), jax.ShapeDtypeStruct((n_exp, 1, ag), I32)],
        scratch_shapes=[pltpu.VMEM((n_exp, ag, LANES), F32), pltpu.VMEM((n_exp, SUBLANES, LANES), F32)],
        compiler_params=_cparams(("arbitrary",)),
        name="route",
    )(logits_t)


def _ffn_kernel(idx_ref, idxn_ref, gate_ref, hx_ref, wg_ref, wu_ref, wd_ref, o_ref, xbuf, sem, wgb, wub, wdb, *, tm):
    e = pl.program_id(0)
    j = pl.program_id(1)
    nt = pl.num_programs(1)
    step = e * nt + j
    last = pl.num_programs(0) * nt - 1

    sub = SUBLANES
    rows = tm * sub

    def row_copy(rows_ref, r, slot):
        src = hx_ref.at[pl.ds(pl.multiple_of(rows_ref[0, 0, r] * sub, sub), sub)]
        dst = xbuf.at[pl.ds(pl.multiple_of(slot * rows + r * sub, sub), sub)]
        return pltpu.make_async_copy(src, dst, sem.at[slot])

    def slot_copy(slot):
        return pltpu.make_async_copy(hx_ref.at[pl.ds(0, rows)], xbuf.at[pl.ds(pl.multiple_of(slot * rows, sub), rows)],
                                     sem.at[slot])

    @pl.when(step == 0)
    def _():
        def body(r, carry):
            row_copy(idx_ref, r, 0).start()
            return carry
        lax.fori_loop(0, tm, body, 0, unroll=8)

    @pl.when(j == 0)
    def _():
        wgb[...] = wg_ref[...].astype(BF16)
        wub[...] = wu_ref[...].astype(BF16)
        wdb[...] = wd_ref[...].astype(BF16)

    slot = step % 2
    nslot = 1 - slot
    for r in range(tm):
        row_copy(idxn_ref, r, nslot).start()
    slot_copy(slot).wait()
    base = slot * rows
    x = jnp.concatenate([xbuf[pl.ds(base + s, tm, stride=sub), :] for s in range(sub)], axis=1).astype(BF16)
    g = _dot(x, wgb[...])
    u = _dot(x, wub[...])
    hid = (g * jax.nn.sigmoid(g)) * u
    y = _dot(hid.astype(BF16), wdb[...])
    eye = _iota((tm, tm), 0) == _iota((tm, tm), 1)
    gcol = jnp.sum(jnp.where(eye, jnp.broadcast_to(gate_ref[0], (tm, tm)), 0.0), axis=1, keepdims=True)
    o_ref[...] = (y * gcol).astype(BF16)

    @pl.when(step == last)
    def _():
        slot_copy(nslot).wait()


def _expert_ffn(idx, gate, hx3, w_gate, w_up, w_down, *, layer, cap, tm):
    n_exp = idx.shape[0]
    d = w_gate.shape[2]
    f = w_gate.shape[3]
    assert d == SUBLANES * LANES, "a token row must be exactly one (8, 128) f32 tile"
    nt = cap // tm
    idx_t = idx[:, :, :cap].reshape(n_exp * nt, 1, tm)
    gate_t = gate[:, :, :cap].reshape(n_exp * nt, 1, tm)
    n_steps = n_exp * nt
    wspec = lambda a, b: pl.BlockSpec((None, None, a, b), lambda e, j: (layer, e, 0, 0))
    return pl.pallas_call(
        functools.partial(_ffn_kernel, tm=tm),
        grid=(n_exp, nt),
        in_specs=[pl.BlockSpec((1, 1, tm), lambda e, j: (e * nt + j, 0, 0), memory_space=pltpu.SMEM),
                  pl.BlockSpec((1, 1, tm), lambda e, j: (jnp.minimum(e * nt + j + 1, n_steps - 1), 0, 0), memory_space=pltpu.SMEM),
                  pl.BlockSpec((1, 1, tm), lambda e, j: (e * nt + j, 0, 0)),
                  pl.BlockSpec(memory_space=pl.ANY),
                  wspec(d, f), wspec(d, f), wspec(f, d)],
        out_specs=pl.BlockSpec((tm, d), lambda e, j: (e * nt + j, 0)),
        out_shape=jax.ShapeDtypeStruct((n_exp * cap, d), BF16),
        scratch_shapes=[pltpu.VMEM((2 * tm * SUBLANES, LANES), F32), pltpu.SemaphoreType.DMA((2,)),
                        pltpu.VMEM((d, f), BF16), pltpu.VMEM((d, f), BF16), pltpu.VMEM((f, d), BF16)],
        compiler_params=_cparams(("arbitrary", "arbitrary")),
        name="expert_ffn",
    )(idx_t, idx_t, gate_t, hx3, w_gate, w_up, w_down)


def _combine_kernel(ws_ref, nw_ref, *refs, n_exp, cap, win, gpt, final_norm):
    y_refs = refs[:n_exp]
    pos_ref, ye_ref, x_ref, m5_ref, gf_ref, o_ref, acc_ref, xwin, sem = refs[n_exp:]
    a = pl.program_id(0)
    ng = pl.num_programs(0)
    per = max(1, MXU_DEPTH // win)

    def slots(e):
        pos = pos_ref[e, pl.ds(a * gpt, gpt), :]
        return jnp.concatenate([jnp.broadcast_to(pos[j:j + 1, :], (win, LANES)) for j in range(gpt)], axis=1)

    lp = _iota((win, gpt * LANES), 0, F32)
    acc = None
    for e0 in range(0, n_exp, per):
        es = range(e0, min(e0 + per, n_exp))
        hot = jnp.concatenate([((slots(e) - (ws_ref[e * ng + a] - e * cap).astype(F32)) == lp).astype(BF16) for e in es],
                              axis=0)
        rows = jnp.concatenate([y_refs[e][...] for e in es], axis=0)
        part = _dot_tn(hot, rows)
        acc = part if acc is None else acc + part
    acc_ref[...] = acc

    for e in range(n_exp):
        first = ws_ref[e * ng + a] - e * cap

        def extra(k, carry, e=e, first=first):
            lo = first + k * win
            row = jnp.minimum(lo, cap - win)
            cp = pltpu.make_async_copy(ye_ref.at[pl.ds(pl.multiple_of(e * cap + row, BF16_ROWS), win)], xwin, sem)
            cp.start()
            cp.wait()
            pos = slots(e)
            hot = ((pos - row.astype(F32)) == lp) & (pos >= lo.astype(F32))
            acc_ref[...] += _dot_tn(hot.astype(BF16), xwin[...])
            return carry

        lax.fori_loop(1, nw_ref[e * ng + a], extra, 0)

    x_new = x_ref[...] + m5_ref[...] * acc_ref[...]
    if final_norm:
        ms = jnp.mean(x_new * x_new, axis=-1, keepdims=True)
        x_new = x_new * lax.rsqrt(ms + EPS) * gf_ref[...]
    o_ref[...] = x_new


def _combine(ye, posmap, goff, x, m5, g_final, *, cap, final_norm):
    n, d = x.shape
    n_exp = posmap.shape[0]
    gpt = min(COMBINE_GROUPS, n // LANES)
    ng = n // (gpt * LANES)
    tok = gpt * LANES
    win = min(COMBINE_WINDOW, cap)
    goff = goff[:, 0, :ng * gpt:gpt]
    end = jnp.concatenate([goff[:, 1:], jnp.full((n_exp, 1), cap, I32)], axis=1)
    start = jnp.minimum((goff // BF16_ROWS) * BF16_ROWS, cap - win)
    n_win = jnp.maximum((end - start + win - 1) // win, 1).reshape(-1)
    wstart = (start + jnp.arange(n_exp, dtype=I32)[:, None] * cap).reshape(-1)
    yspec = lambda e: pl.BlockSpec((pl.Element(win), pl.Element(d)),
                                   lambda a, ws, nw: (pl.multiple_of(ws[e * ng + a], BF16_ROWS), 0))
    gs = pltpu.PrefetchScalarGridSpec(
        num_scalar_prefetch=2,
        grid=(ng,),
        in_specs=[yspec(e) for e in range(n_exp)]
        + [pl.BlockSpec(posmap.shape, lambda a, ws, nw: (0, 0, 0)),
           pl.BlockSpec(memory_space=pl.ANY),
           pl.BlockSpec((tok, d), lambda a, ws, nw: (a, 0)),
           pl.BlockSpec((1, d), lambda a, ws, nw: (0, 0)), pl.BlockSpec((1, d), lambda a, ws, nw: (0, 0))],
        out_specs=pl.BlockSpec((tok, d), lambda a, ws, nw: (a, 0)),
        scratch_shapes=[pltpu.VMEM((tok, d), F32), pltpu.VMEM((win, d), BF16), pltpu.SemaphoreType.DMA(())],
    )
    return pl.pallas_call(
        functools.partial(_combine_kernel, n_exp=n_exp, cap=cap, win=win, gpt=gpt, final_norm=final_norm),
        grid_spec=gs,
        out_shape=jax.ShapeDtypeStruct((n, d), F32),
        compiler_params=_cparams(("arbitrary",)),
        name="combine_final" if final_norm else "combine",
    )(wstart, n_win, *([ye] * n_exp), posmap, ye, x, m5, g_final)


def _rope_tables(n):
    n_rows = n // GRID_W
    row = jnp.arange(n_rows).astype(F32)
    col = jnp.arange(GRID_W).astype(F32)

    def tables(d, signs):
        nf = d // 4
        inv = ROPE_BASE ** (-jnp.arange(nf, dtype=F32) / nf)
        ang_r, ang_c = row[:, None] * inv, col[:, None] * inv
        reps = LANES // (2 * nf)

        def lanes(f_r, f_c, factors):
            z_r, z_c = jnp.zeros_like(f_r), jnp.zeros_like(f_c)
            pat_r = jnp.concatenate([part for k in range(reps) for part in (f_r * factors[2 * k], z_r)], axis=1)
            pat_c = jnp.concatenate([part for k in range(reps) for part in (z_c, f_c * factors[2 * k + 1])], axis=1)
            return (pat_r[:, None, :] + pat_c[None, :, :]).reshape(n, LANES)

        ones = (1.0,) * (2 * reps)
        cos = lanes(jnp.cos(ang_r), jnp.cos(ang_c), ones)
        return [cos] + [lanes(jnp.sin(ang_r), jnp.sin(ang_c), sg) for sg in signs]

    c128, s128 = tables(RET_D, [(-1.0, -1.0, 1.0, 1.0)])
    c64, sa, sb = tables(HEAD_D, [(-1.0, -1.0, 0.0, 0.0) * 2, (0.0, 0.0, 1.0, 1.0) * 2])
    return c128, s128, c64, sa, sb


WIN_HEAD_ORDER = (0, 4, 1, 5, 2, 6, 3, 7)


def _permute_in_weight(w):
    d = w.shape[0]
    sizes = (("q_r", 512), ("k_r", 512), ("v_r", 512), ("g_r", 512), ("q_w", 512), ("k_w", 128), ("v_w", 128),
             ("q_n", 512), ("k_n", 512), ("v_n", 512), ("gates", 3 * d))
    off, lay = 0, {}
    for name, size in sizes:
        lay[name] = (off, off + size)
        off += size
    seg = lambda name: w[:, lay[name][0]:lay[name][1]]
    q0 = lay["q_w"][0]
    q_w = [w[:, q0 + h * HEAD_D:q0 + (h + 1) * HEAD_D] for h in WIN_HEAD_ORDER]
    pad = jnp.zeros((d, SEG - 2 * LANES), w.dtype)
    rope = jnp.concatenate([seg("q_r"), seg("k_r")] + q_w + [seg("k_w"), seg("v_w"), pad], axis=1).astype(BF16)
    plain = jnp.concatenate([seg("gates"), seg("v_r"), seg("g_r"), seg("q_n"), seg("k_n"), seg("v_n")], axis=1).astype(BF16)
    return rope, plain


def _permute_win_rows(w):
    return jnp.concatenate([w[h * HEAD_D:(h + 1) * HEAD_D] for h in WIN_HEAD_ORDER], axis=0).astype(BF16)


def kernel(x, c, ctx, c_ctx, w_mod, b_mod, g_mix, g_ffn, w_in, ret_decay_logit, ret_gn, w_ret, win_sink, w_win, na_rpb,
           w_na, w_out, w_router, w_exp_gate, w_exp_up, w_exp_down, g_final):
    _, n, d = x.shape
    l = ctx.shape[1]
    depth = w_in.shape[0]
    xs, cs = x[0], ctx[0]
    cc = jnp.zeros((SUBLANES, d), F32).at[0].set(c[0]).at[1].set(c_ctx)
    mods = _modulation(cc, w_mod, b_mod)
    c128, s128, c64, sa64, sb64 = _rope_tables(n)
    k_scale = RET_D ** -0.5
    plain_scales = jnp.ones((PLAIN_TILES,), F32).at[T_QN].set(Q_SCALE)
    ctx_rope_scales = jnp.ones((R_KVW + 1,), F32).at[R_KR].set(k_scale).at[R_QW].set(Q_SCALE)
    cap_x = CAPACITY_FACTOR * n // N_EXPERTS
    cap_c = CAPACITY_FACTOR * l // N_EXPERTS
    grp_c = -(-(l // LANES) // BF16_ROWS) * BF16_ROWS
    vec = lambda v: v.reshape(1, -1)

    for layer in range(depth):
        need_ctx = layer < depth - 1
        last = layer == depth - 1
        mx = [vec(mods[layer, 0, k * d:(k + 1) * d]) for k in range(N_MOD)]
        mc = [vec(mods[layer, 1, k * d:(k + 1) * d]) for k in range(N_MOD)]
        w_rope, w_plain = _permute_in_weight(w_in[layer])
        wr = w_ret[layer].astype(BF16)
        ww = _permute_win_rows(w_win[layer])
        wn = w_na[layer].astype(BF16)
        wo = w_out[layer].astype(BF16)
        wrt = w_router[layer].T
        sink = win_sink[layer].astype(F32) * LOG2E
        lg_rows = jnp.broadcast_to(ret_decay_logit[layer].astype(F32).reshape(2 * RET_HEADS, 1), (2 * RET_HEADS, LANES))
        gn = vec(ret_gn[layer])
        bias_tab = _na_bias_table(na_rpb[layer].astype(F32) * LOG2E)
        gmix, gffn = vec(g_mix[layer]), vec(g_ffn[layer])

        hx = _norm_call(xs, gmix, mx[0], mx[1], tm=min(NORM_ROWS, n))
        hc = _norm_call(cs, gmix, mc[0], mc[1], tm=l)
        p_x = _proj(hx, w_plain, plain_scales, tm=min(PROJ_ROWS, n))
        p_c = _proj(hc, w_plain, plain_scales, tm=l)
        r_c = _proj(hc, w_rope, ctx_rope_scales, tm=l)
        rope = functools.partial(_proj_rope, hx, w_rope, tm=min(PROJ_ROWS, n))
        q_r = rope(R_QR, (c128, s128), head_d=RET_D, scale=1.0, n_rot=SEG // LANES, name="proj_q_ret")
        k_r = rope(R_KR, (c128, s128), head_d=RET_D, scale=k_scale, n_rot=SEG // LANES, name="proj_k_ret")
        q_w = rope(R_QW, (c64, sa64, sb64), head_d=HEAD_D, scale=Q_SCALE, n_rot=SEG // LANES, name="proj_q_win")
        kv_w = rope(R_KVW, (c64, sa64, sb64), head_d=HEAD_D, scale=1.0, n_rot=1, name="proj_kv_win")

        yf, yb = _retention((q_r, 0), (k_r, 0), (p_x, T_VR), (r_c, R_KR), (p_c, T_VR), lg_rows, zero_init=False)
        yw = _window_attention((q_w, 0), (kv_w, 0), (r_c, R_KVW), sink)
        yn = _na_attention(p_x, p_c, bias_tab)
        xs, hx3, lt = _merge(yf, yb, p_x, gn, yw, yn, wr, ww, wn, wo, xs, mx[2], gffn, mx[3], mx[4], wrt, tm=min(MERGE_ROWS, n))

        idx, gate, posmap, goff = _route(lt.reshape(N_EXPERTS, n // LANES, LANES), cap=cap_x, n_groups=n // LANES)
        ye = _expert_ffn(idx, gate, hx3, w_exp_gate, w_exp_up, w_exp_down, layer=layer, cap=cap_x, tm=min(FFN_ROWS, cap_x))
        xs = _combine(ye, posmap, goff, xs, mx[5], vec(g_final), cap=cap_x, final_norm=last)

        if need_ctx:
            yfc, ybc = _retention((r_c, R_QR), (r_c, R_KR), (p_c, T_VR), (r_c, R_KR), (p_c, T_VR), lg_rows, zero_init=True)
            ywc, ync = _ctx_attention(r_c, p_c, sink)
            cs, hc3, ltc = _merge(yfc, ybc, p_c, gn, ywc, ync, wr, ww, wn, wo, cs, mc[2], gffn, mc[3], mc[4], wrt, tm=l)
            ltc = jnp.pad(ltc.reshape(N_EXPERTS, l // LANES, LANES), ((0, 0), (0, grp_c - l // LANES), (0, 0)))
            idc, gtc, posc, goffc = _route(ltc, cap=cap_c, n_groups=l // LANES)
            yec = _expert_ffn(idc, gtc, hc3, w_exp_gate, w_exp_up, w_exp_down, layer=layer, cap=cap_c, tm=cap_c)
            cs = _combine(yec, posc, goffc, cs, mc[5], vec(g_final), cap=cap_c, final_norm=False)

    return xs[None]
```

```python
import functools

import jax
import jax.numpy as jnp
import numpy as np
from jax import lax
from jax.experimental import pallas as pl
from jax.experimental.pallas import tpu as pltpu

F32, BF16, I32 = jnp.float32, jnp.bfloat16, jnp.int32
HIGHEST = lax.Precision.HIGHEST

GRID_W = 64
RET_HEADS, RET_D, RET_CHUNK = 4, 128, 128
WIN_HEADS, WIN_KV_HEADS, HEAD_D, WINDOW, WIN_BLOCK = 8, 2, 64, 128, 128
NA_HEADS, NA_ROWS, NA_COLS = 8, 8, 16
N_EXPERTS, CAPACITY_FACTOR = 16, 2
N_MOD = 6
ROPE_BASE = 10000.0
EPS = 1e-6
NEG_INF = -1e30

LANES = 128
SUBLANES = 8
BF16_ROWS = 16
VMEM_LIMIT = 56 * 1024 * 1024

SEG = 512
T_GATES, T_VR, T_GR, T_QN, T_KN, T_VN = 0, 6, 7, 8, 9, 10
PLAIN_TILES = 11
R_QR, R_KR, R_QW, R_KVW = 0, 1, 2, 3
LOG2E = 1.4426950408889634
Q_SCALE = HEAD_D ** -0.5 * LOG2E
NORM_ROWS = 1024
PROJ_ROWS = 2048
MERGE_ROWS = 512
FFN_ROWS = 512
RET_BLOCK = 2 * RET_CHUNK
WIN_QBLOCKS = 2
NA_QROWS = 4
NA_SLAB_ROWS = 12
NA_BIAS_ZERO = NA_SLAB_ROWS - 2
NA_BIAS_PAIRS = NA_BIAS_ZERO + NA_SLAB_ROWS - 1
COMBINE_GROUPS = 4
COMBINE_WINDOW = 128
MXU_DEPTH = 256
BISECT_STEPS = 48


def _cparams(sem):
    return pltpu.CompilerParams(dimension_semantics=sem, vmem_limit_bytes=VMEM_LIMIT)


def _dot(a, b):
    return jnp.dot(a, b, preferred_element_type=F32)


def _dot_nt(a, b, precision=None):
    return lax.dot_general(a, b, (((1,), (1,)), ((), ())), precision=precision, preferred_element_type=F32)


def _dot_tn(a, b, precision=None):
    return lax.dot_general(a, b, (((0,), (0,)), ((), ())), precision=precision, preferred_element_type=F32)


def _iota(shape, dim, dtype=I32):
    return lax.broadcasted_iota(I32, shape, dim).astype(dtype)


def _mod_kernel(s_ref, w_ref, b_ref, o_ref):
    s = s_ref[...]
    s = s * jax.nn.sigmoid(s)
    o_ref[...] = jnp.dot(s, w_ref[...], precision=HIGHEST, preferred_element_type=F32) + b_ref[...]


def _modulation(cc, w_mod, b_mod):
    depth, d, md = w_mod.shape
    tn = 1536
    return pl.pallas_call(
        _mod_kernel,
        grid=(depth, md // tn),
        in_specs=[
            pl.BlockSpec((SUBLANES, d), lambda l, j: (0, 0)),
            pl.BlockSpec((None, d, tn), lambda l, j: (l, 0, j)),
            pl.BlockSpec((None, 1, tn), lambda l, j: (l, 0, j)),
        ],
        out_specs=pl.BlockSpec((None, SUBLANES, tn), lambda l, j: (l, 0, j)),
        out_shape=jax.ShapeDtypeStruct((depth, SUBLANES, md), F32),
        compiler_params=_cparams(("arbitrary", "arbitrary")),
        name="modulation",
    )(cc, w_mod, b_mod.reshape(depth, 1, md))


def _norm_mod(x, g, shift, scale):
    ms = jnp.mean(x * x, axis=-1, keepdims=True)
    y = x * lax.rsqrt(ms + EPS) * g
    return y * (1.0 + scale) + shift


def _norm_kernel(x_ref, g_ref, sh_ref, sc_ref, o_ref):
    o_ref[...] = _norm_mod(x_ref[...], g_ref[...], sh_ref[...], sc_ref[...]).astype(BF16)


def _norm_call(x, g, shift, scale, *, tm):
    n, d = x.shape
    vec = lambda: pl.BlockSpec((1, d), lambda i: (0, 0))
    return pl.pallas_call(
        _norm_kernel,
        grid=(n // tm,),
        in_specs=[pl.BlockSpec((tm, d), lambda i: (i, 0)), vec(), vec(), vec()],
        out_specs=pl.BlockSpec((tm, d), lambda i: (i, 0)),
        out_shape=jax.ShapeDtypeStruct((n, d), BF16),
        compiler_params=_cparams(("arbitrary",)),
        name="norm_modulate",
    )(x, g, shift, scale)


def _proj_kernel(scale_ref, hx_ref, w_ref, o_ref):
    o_ref[...] = (_dot(hx_ref[...], w_ref[...]) * scale_ref[pl.program_id(1)]).astype(BF16)


def _proj(hx, w, scales, *, tm):
    n, d = hx.shape
    tiles = w.shape[1] // SEG
    return pl.pallas_call(
        _proj_kernel,
        grid=(n // tm, tiles),
        in_specs=[pl.BlockSpec(memory_space=pltpu.SMEM),
                  pl.BlockSpec((tm, d), lambda i, j: (i, 0)), pl.BlockSpec((d, SEG), lambda i, j: (0, j))],
        out_specs=pl.BlockSpec((tm, SEG), lambda i, j: (i, j)),
        out_shape=jax.ShapeDtypeStruct((n, tiles * SEG), BF16),
        compiler_params=_cparams(("arbitrary", "arbitrary")),
        name="projection",
    )(scales, hx, w)


def _proj_rope_kernel(hx_ref, w_ref, *refs, head_d, scale, n_rot):
    tabs, o_ref = refs[:-1], refs[-1]
    acc = _dot(hx_ref[...], w_ref[...])
    if head_d == RET_D:
        c = tabs[0][...] * scale
        s = tabs[1][...] * scale
        rot = lambda a: a * c + pltpu.roll(a, 64, 1) * s
    else:
        c = tabs[0][...] * scale
        sa = tabs[1][...] * scale
        sb = tabs[2][...] * scale
        rot = lambda a: a * c + pltpu.roll(a, 96, 1) * sa + pltpu.roll(a, 32, 1) * sb
    groups = [acc[:, k * LANES:(k + 1) * LANES] for k in range(SEG // LANES)]
    o_ref[...] = jnp.concatenate([rot(a) if k < n_rot else a for k, a in enumerate(groups)], axis=1).astype(BF16)


def _proj_rope(hx, w, tile, tabs, *, head_d, scale, n_rot, tm, name):
    n, d = hx.shape
    return pl.pallas_call(
        functools.partial(_proj_rope_kernel, head_d=head_d, scale=scale, n_rot=n_rot),
        grid=(n // tm,),
        in_specs=[pl.BlockSpec((tm, d), lambda i: (i, 0)), pl.BlockSpec((d, SEG), lambda i: (0, tile))]
        + [pl.BlockSpec((tm, LANES), lambda i: (i, 0)) for _ in tabs],
        out_specs=pl.BlockSpec((tm, SEG), lambda i: (i, 0)),
        out_shape=jax.ShapeDtypeStruct((n, SEG), BF16),
        compiler_params=_cparams(("arbitrary",)),
        name=name,
    )(hx, w, *tabs)


def _ret_kernel(lg_ref, qf_ref, kf_ref, vf_ref, qb_ref, kb_ref, vb_ref, kc_ref, vc_ref, yf_ref, yb_ref,
                rf_ref, rb_ref, dmat_ref, vec_ref, *, zero_init):
    c = pl.program_id(0)
    ch = qf_ref.shape[0]
    dh = RET_D
    states = (rf_ref, rb_ref)

    @pl.when(c == 0)
    def _():
        lg = jax.nn.log_sigmoid(lg_ref[...])
        ii = _iota((ch, ch), 0, F32)
        jj = _iota((ch, ch), 1, F32)
        iv = _iota((ch, dh), 0, F32)
        n_ctx = kc_ref.shape[0]
        mm = _iota((n_ctx, dh), 0, F32)
        for d in range(2):
            for h in range(RET_HEADS):
                row = lg[RET_HEADS * d + h:RET_HEADS * d + h + 1, :]
                l = jnp.concatenate([jnp.broadcast_to(row, (ch, dh))] * (ch // dh), axis=1)
                lv = jnp.broadcast_to(row, (ch, dh))
                if d == 0:
                    diff = ii - jj
                    dmat_ref[d, h] = jnp.where(diff >= 0.0, jnp.exp(jnp.maximum(diff, 0.0) * l), 0.0)
                    vec_ref[d, h, 0] = jnp.exp((ch - 1.0 - iv) * lv)
                    vec_ref[d, h, 1] = jnp.exp((iv + 1.0) * lv)
                else:
                    diff = jj - ii
                    dmat_ref[d, h] = jnp.where(diff >= 1.0, jnp.exp(jnp.maximum(diff, 0.0) * l), 0.0)
                    vec_ref[d, h, 0] = jnp.exp(iv * lv)
                    vec_ref[d, h, 1] = jnp.exp((ch - iv) * lv)
                vec_ref[d, h, 2] = jnp.exp(float(ch) * lv)
                if zero_init:
                    states[d][h] = jnp.zeros((dh, dh), F32)
                else:
                    lc = jnp.broadcast_to(row, (n_ctx, dh))
                    w = jnp.exp((n_ctx - 1.0 - mm) * lc) if d == 0 else jnp.exp(mm * lc)
                    sl = slice(h * dh, (h + 1) * dh)
                    kw = (kc_ref[:, sl].astype(F32) * w).astype(BF16)
                    states[d][h] = _dot_tn(kw, vc_ref[:, sl])

    for d, (q_ref, k_ref, v_ref, y_ref) in enumerate(((qf_ref, kf_ref, vf_ref, yf_ref), (qb_ref, kb_ref, vb_ref, yb_ref))):
        for h in range(RET_HEADS):
            sl = slice(h * dh, (h + 1) * dh)
            q, k, v = q_ref[:, sl], k_ref[:, sl], v_ref[:, sl]
            r = states[d][h]
            a = (_dot_nt(q, k) * dmat_ref[d, h]).astype(BF16)
            inner = _dot(a, v)
            cross = _dot(q, r.astype(BF16)) * vec_ref[d, h, 1]
            y_ref[:, sl] = inner + cross
            kz = (k.astype(F32) * vec_ref[d, h, 0]).astype(BF16)
            states[d][h] = vec_ref[d, h, 2][:dh, :] * r + _dot_tn(kz, v)


def _retention(q, k, v, kc, vc, lg_rows, *, zero_init):
    n = q[0].shape[0]
    blk = min(RET_BLOCK, n)
    nc = n // blk
    w = RET_HEADS * RET_D
    fwd = lambda sg: pl.BlockSpec((blk, w), lambda c: (c, sg[1]))
    bwd = lambda sg: pl.BlockSpec((blk, w), lambda c: (nc - 1 - c, sg[1]))
    ctx = lambda sg: pl.BlockSpec((sg[0].shape[0], w), lambda c: (0, sg[1]))
    return pl.pallas_call(
        functools.partial(_ret_kernel, zero_init=zero_init),
        grid=(nc,),
        in_specs=[pl.BlockSpec((SUBLANES, LANES), lambda c: (0, 0)),
                  fwd(q), fwd(k), fwd(v), bwd(q), bwd(k), bwd(v), ctx(kc), ctx(vc)],
        out_specs=[pl.BlockSpec((blk, w), lambda c: (c, 0)), pl.BlockSpec((blk, w), lambda c: (nc - 1 - c, 0))],
        out_shape=[jax.ShapeDtypeStruct((n, w), F32), jax.ShapeDtypeStruct((n, w), F32)],
        scratch_shapes=[pltpu.VMEM((RET_HEADS, RET_D, RET_D), F32), pltpu.VMEM((RET_HEADS, RET_D, RET_D), F32),
                        pltpu.VMEM((2, RET_HEADS, blk, blk), F32), pltpu.VMEM((2, RET_HEADS, 3, blk, RET_D), F32)],
        compiler_params=_cparams(("arbitrary",)),
        name="retention_ctx" if zero_init else "retention",
    )(lg_rows, q[0], k[0], v[0], q[0], k[0], v[0], kc[0], vc[0])


def _half_mask(x, half):
    lane = _iota(x.shape, 1)
    keep = (lane < HEAD_D) if half == 0 else (lane >= HEAD_D)
    return jnp.where(keep, x, jnp.zeros_like(x))


def _softmax_pv(s, v, extra=None):
    m = jnp.max(s, axis=1, keepdims=True)
    if extra is not None:
        m = jnp.maximum(m, extra)
    p = jnp.exp2(s - m)
    den = jnp.sum(p, axis=1, keepdims=True)
    if extra is not None:
        den = den + jnp.exp2(extra - m)
    return _dot(p.astype(BF16), v) / den


def _softmax_pv_t(st, v, extra=None):
    m = jnp.max(st, axis=0, keepdims=True)
    if extra is not None:
        m = jnp.maximum(m, extra)
    p = jnp.exp2(st - m)
    den = jnp.sum(p, axis=0, keepdims=True)
    if extra is not None:
        den = den + jnp.exp2(extra - m)
    return _dot_tn(v, p.astype(BF16)) / den


def _win_kernel(sink_ref, q_ref, *refs):
    nkb = WIN_QBLOCKS + 2
    k_refs, v_refs = refs[:nkb], refs[nkb:2 * nkb]
    kx_ref, vx_ref, mask_ref, o_ref = refs[2 * nkb:]
    s = pl.program_id(0)
    last = pl.num_programs(0) - 1
    nq = WIN_QBLOCKS * WIN_BLOCK
    nk = nkb * WIN_BLOCK
    k_all = jnp.concatenate([r[...] for r in k_refs] + [kx_ref[...]], axis=0)
    v_all = jnp.concatenate([r[...] for r in v_refs] + [vx_ref[...]], axis=0)
    n_cols = SEG // LANES
    qs = jnp.concatenate([q_ref[:, c * LANES:(c + 1) * LANES] for c in range(n_cols)], axis=0)
    variant = jnp.where(s == 0, 1, 0) + jnp.where(s == last, 2, 0)
    valid = mask_ref[variant] > 0.5
    valid = jnp.concatenate([valid] * n_cols, axis=1)
    out_t = None
    for g in range(WIN_KV_HEADS):
        st = _dot_nt(_half_mask(k_all, g), qs)
        st = jnp.concatenate([jnp.where(valid, st[:nk], NEG_INF), st[nk:]], axis=0)
        sink = jnp.concatenate([jnp.full((1, nq), sink_ref[n_cols * g + c], F32) for c in range(n_cols)], axis=1)
        o = _softmax_pv_t(st, _half_mask(v_all, g), sink)
        out_t = o if out_t is None else out_t + o
    o = out_t.T
    o_ref[...] = jnp.concatenate([o[c * nq:(c + 1) * nq, :] for c in range(n_cols)], axis=1).astype(BF16)


def _win_valid_table():
    nkb = WIN_QBLOCKS + 2
    qpos = np.arange(WIN_QBLOCKS * WIN_BLOCK)[None, :]
    kpos = np.arange(nkb * WIN_BLOCK)[:, None] - WIN_BLOCK
    band = np.abs(kpos - qpos) <= WINDOW
    tabs = []
    for variant in range(4):
        ok = band.copy()
        if variant & 1:
            ok &= kpos >= 0
        if variant & 2:
            ok &= kpos < WIN_QBLOCKS * WIN_BLOCK
        tabs.append(ok)
    return jnp.asarray(np.stack(tabs).astype(np.float32))


def _window_attention(q, kv, kv_ctx, sink):
    n = q[0].shape[0]
    nb = n // WIN_BLOCK
    assert nb % WIN_QBLOCKS == 0
    l = kv_ctx[0].shape[0]
    nkb = WIN_QBLOCKS + 2
    per = SEG // LANES
    blk = lambda off, j: pl.BlockSpec((WIN_BLOCK, LANES),
                                      lambda i: (jnp.clip(WIN_QBLOCKS * i - 1 + j, 0, nb - 1), per * kv[1] + off))
    mask_tab = _win_valid_table()
    return pl.pallas_call(
        _win_kernel,
        grid=(nb // WIN_QBLOCKS,),
        in_specs=[pl.BlockSpec(memory_space=pltpu.SMEM),
                  pl.BlockSpec((WIN_QBLOCKS * WIN_BLOCK, SEG), lambda i: (i, q[1]))]
        + [blk(0, j) for j in range(nkb)] + [blk(1, j) for j in range(nkb)]
        + [pl.BlockSpec((l, LANES), lambda i: (0, per * kv_ctx[1])), pl.BlockSpec((l, LANES), lambda i: (0, per * kv_ctx[1] + 1)),
           pl.BlockSpec(mask_tab.shape, lambda i: (0, 0, 0))],
        out_specs=pl.BlockSpec((WIN_QBLOCKS * WIN_BLOCK, SEG), lambda i: (i, 0)),
        out_shape=jax.ShapeDtypeStruct((n, SEG), BF16),
        compiler_params=_cparams(("arbitrary",)),
        name="window_attention",
    )(sink, q[0], *([kv[0]] * (2 * nkb)), kv_ctx[0], kv_ctx[0], mask_tab)


def _na_slab_start(s, half_rows):
    return jnp.clip((NA_QROWS // 2) * s - NA_ROWS // 4, 0, half_rows - NA_SLAB_ROWS // 2)


def _na_kernel(q_ref, *refs, rows):
    nslab = NA_SLAB_ROWS // 2
    k_refs, v_refs = refs[:nslab], refs[nslab:2 * nslab]
    kx_ref, vx_ref, bias_ref, mask_ref, o_ref = refs[2 * nslab:]
    s = pl.program_id(0)
    last = pl.num_programs(0) - 1
    w = GRID_W
    nq = NA_QROWS * w
    nk = NA_SLAB_ROWS * w
    delta = 2 * _na_slab_start(s, rows // 2) - NA_QROWS * s
    variant = jnp.where(s == 0, 0, jnp.where(s == last, 2, 1))
    valid = mask_ref[variant] > 0.5
    valid = jnp.concatenate([valid, valid], axis=1)
    row = _iota((LANES, nq), 0)
    for pair in range(NA_HEADS // 2):
        sl = slice(pair * LANES, (pair + 1) * LANES)
        q = q_ref[:, sl]
        qs = jnp.concatenate([_half_mask(q, 0), _half_mask(q, 1)], axis=0)
        k_all = jnp.concatenate([r[:, sl] for r in k_refs] + [kx_ref[:, sl]], axis=0)
        v_all = jnp.concatenate([r[:, sl] for r in v_refs] + [vx_ref[:, sl]], axis=0)
        st = _dot_nt(k_all, qs)
        bias = jnp.concatenate(
            [jnp.concatenate([bias_ref[2 * pair + u, delta + 2 * i - a + NA_BIAS_ZERO]
                              for u in range(2) for a in range(0, NA_QROWS, 2)], axis=1) for i in range(nslab)], axis=0)
        s_loc = jnp.where(valid, st[:nk] + bias, NEG_INF)
        o = _softmax_pv_t(jnp.concatenate([s_loc, st[nk:]], axis=0), v_all)
        o_ref[:, sl] = jnp.where(row < HEAD_D, o[:, :nq], o[:, nq:]).T.astype(BF16)


def _na_valid_table():
    kk = np.arange(NA_SLAB_ROWS)[:, None, None, None]
    ck = np.arange(GRID_W)[None, :, None, None]
    a = np.arange(NA_QROWS)[None, None, :, None]
    cq = np.arange(GRID_W)[None, None, None, :]
    c_start = np.clip(cq - NA_COLS // 2, 0, GRID_W - NA_COLS)
    col_ok = (ck >= c_start) & (ck < c_start + NA_COLS)
    first_row = (0 * a, a, 0 * a + NA_SLAB_ROWS - NA_ROWS)
    tabs = [(col_ok & (kk >= f) & (kk < f + NA_ROWS)).reshape(NA_SLAB_ROWS * GRID_W, NA_QROWS * GRID_W) for f in first_row]
    return jnp.asarray(np.stack(tabs).astype(np.float32))


def _na_attention(p, p_ctx, bias_tab):
    n = p.shape[0]
    rows = n // GRID_W
    assert rows % NA_QROWS == 0 and rows >= NA_SLAB_ROWS
    steps = rows // NA_QROWS
    l = p_ctx.shape[0]
    slab = lambda t, i: pl.BlockSpec((2 * GRID_W, SEG), lambda s: (_na_slab_start(s, rows // 2) + i, t))
    nslab = NA_SLAB_ROWS // 2
    mask_tab = _na_valid_table()
    return pl.pallas_call(
        functools.partial(_na_kernel, rows=rows),
        grid=(steps,),
        in_specs=[pl.BlockSpec((NA_QROWS * GRID_W, SEG), lambda s: (s, T_QN))]
        + [slab(T_KN, i) for i in range(nslab)] + [slab(T_VN, i) for i in range(nslab)]
        + [pl.BlockSpec((l, SEG), lambda s: (0, T_KN)), pl.BlockSpec((l, SEG), lambda s: (0, T_VN)),
           pl.BlockSpec(bias_tab.shape, lambda s: (0, 0, 0, 0)), pl.BlockSpec(mask_tab.shape, lambda s: (0, 0, 0))],
        out_specs=pl.BlockSpec((NA_QROWS * GRID_W, SEG), lambda s: (s, 0)),
        out_shape=jax.ShapeDtypeStruct((n, SEG), BF16),
        compiler_params=_cparams(("arbitrary",)),
        name="neighbourhood_attention",
    )(p, *([p] * (2 * nslab)), p_ctx, p_ctx, bias_tab, mask_tab)


def _na_bias_table(rpb):
    n_r, n_c = 2 * NA_ROWS - 1, 2 * NA_COLS - 1
    rpb = rpb.astype(F32)
    e_i, uk_i, wq_i = np.meshgrid(np.arange(NA_BIAS_PAIRS), np.arange(2), np.arange(2), indexing="ij")
    src = np.clip(e_i - NA_BIAS_ZERO + uk_i - wq_i + NA_ROWS - 1, 0, n_r - 1)
    pick = jnp.asarray((src[..., None] == np.arange(n_r)).astype(np.float32))
    rows = jnp.einsum("euwr,hrj->heuwj", pick, rpb, precision=HIGHEST)
    ck = np.arange(GRID_W)[:, None]
    cq = np.arange(GRID_W)[None, :]
    ci = np.clip(ck - cq, -(NA_COLS - 1), NA_COLS - 1) + NA_COLS - 1
    sel = (ci[None] == np.arange(n_c)[:, None, None]).astype(np.float32)
    sel2 = np.zeros((2, n_c, GRID_W, 2, GRID_W), np.float32)
    for wq in range(2):
        sel2[wq, :, :, wq, :] = sel
    sel2 = jnp.asarray(sel2.reshape(2 * n_c, GRID_W, 2 * GRID_W))
    rows = rows.reshape(rpb.shape[0], NA_BIAS_PAIRS, 2, 2 * n_c)
    tab = jnp.einsum("heuj,jkl->heukl", rows, sel2, precision=HIGHEST)
    return tab.reshape(rpb.shape[0], NA_BIAS_PAIRS, 2 * GRID_W, 2 * GRID_W)


def _ctx_attn_kernel(sink_ref, r_ref, p_ref, ow_ref, on_ref):
    l = p_ref.shape[0]
    n_cols = SEG // LANES
    k_all = r_ref[:, R_KVW * SEG:R_KVW * SEG + LANES]
    v_all = r_ref[:, R_KVW * SEG + LANES:R_KVW * SEG + 2 * LANES]
    qs = jnp.concatenate([r_ref[:, R_QW * SEG + c * LANES:R_QW * SEG + (c + 1) * LANES] for c in range(n_cols)], axis=0)
    outs = []
    for g in range(WIN_KV_HEADS):
        s = _dot_nt(qs, _half_mask(k_all, g))
        sink = jnp.concatenate([jnp.full((l, 1), sink_ref[n_cols * g + c], F32) for c in range(n_cols)], axis=0)
        outs.append(_softmax_pv(s, _half_mask(v_all, g), sink))
    o = outs[0] + outs[1]
    ow_ref[...] = jnp.concatenate([o[c * l:(c + 1) * l, :] for c in range(n_cols)], axis=1).astype(BF16)
    for pair in range(NA_HEADS // 2):
        sl = lambda t: slice(t * SEG + pair * LANES, t * SEG + (pair + 1) * LANES)
        q, k, v = p_ref[:, sl(T_QN)], p_ref[:, sl(T_KN)], p_ref[:, sl(T_VN)]
        out = None
        for u in range(2):
            o = _softmax_pv(_dot_nt(q, _half_mask(k, u)), _half_mask(v, u))
            out = o if out is None else out + o
        on_ref[:, pair * LANES:(pair + 1) * LANES] = out.astype(BF16)


def _ctx_attention(r_ctx, p_ctx, sink):
    l = p_ctx.shape[0]
    return pl.pallas_call(
        _ctx_attn_kernel,
        in_specs=[pl.BlockSpec(memory_space=pltpu.SMEM), pl.BlockSpec(r_ctx.shape, lambda: (0, 0)),
                  pl.BlockSpec(p_ctx.shape, lambda: (0, 0))],
        out_specs=[pl.BlockSpec((l, SEG), lambda: (0, 0)), pl.BlockSpec((l, SEG), lambda: (0, 0))],
        out_shape=[jax.ShapeDtypeStruct((l, SEG), BF16), jax.ShapeDtypeStruct((l, SEG), BF16)],
        compiler_params=pltpu.CompilerParams(vmem_limit_bytes=VMEM_LIMIT),
        name="context_attention",
    )(sink, r_ctx, p_ctx)


def _merge_kernel(yf_ref, yb_ref, gr_ref, gn_ref, yw_ref, yn_ref, ga_ref, gb_ref, gc_ref,
                  wr_ref, ww_ref, wn_ref, wo_ref, x_ref, m2_ref, gf_ref, m3_ref, m4_ref, wrt_ref,
                  xo_ref, hx_ref, lt_ref):
    y = yf_ref[...] + yb_ref[...]
    parts = []
    for h in range(RET_HEADS):
        yh = y[:, h * RET_D:(h + 1) * RET_D]
        mu = jnp.mean(yh, axis=-1, keepdims=True)
        var = jnp.mean(jnp.square(yh - mu), axis=-1, keepdims=True)
        parts.append((yh - mu) * lax.rsqrt(var + EPS))
    g = gr_ref[...].astype(F32)
    ya = jnp.concatenate(parts, axis=1) * gn_ref[...] * (g * jax.nn.sigmoid(g))
    za = _dot(ya.astype(BF16), wr_ref[...])
    zb = _dot(yw_ref[...], ww_ref[...])
    zc = _dot(yn_ref[...], wn_ref[...])
    sig = lambda r: jax.nn.sigmoid(r[...].astype(F32))
    mix = sig(ga_ref) * za + sig(gb_ref) * zb + sig(gc_ref) * zc
    x_new = x_ref[...] + m2_ref[...] * _dot(mix.astype(BF16), wo_ref[...])
    xo_ref[...] = x_new
    h2 = _norm_mod(x_new, gf_ref[...], m3_ref[...], m4_ref[...])
    n_sub = h2.shape[1] // LANES
    for s in range(n_sub):
        hx_ref[pl.ds(s, h2.shape[0], stride=n_sub), :] = h2[:, s * LANES:(s + 1) * LANES]
    split = lambda a: (a.astype(BF16), (a - a.astype(BF16).astype(F32)).astype(BF16))
    (w_hi, w_lo), (h_hi, h_lo) = split(wrt_ref[...]), split(h2)
    lt_ref[...] = _dot_nt(w_hi, h_hi) + _dot_nt(w_lo, h_hi) + _dot_nt(w_hi, h_lo)


def _merge(yf, yb, p, gn, yw, yn, wr, ww, wn, wo, x, m2, gf, m3, m4, wrt, *, tm):
    n, d = x.shape
    row = lambda wdt, t: pl.BlockSpec((tm, wdt), lambda i: (i, t))
    full = lambda a: pl.BlockSpec(a.shape, lambda i: (0,) * a.ndim)
    gate0 = T_GATES * SEG // d
    return pl.pallas_call(
        _merge_kernel,
        grid=(n // tm,),
        in_specs=[row(SEG, 0), row(SEG, 0), row(SEG, T_GR), full(gn), row(SEG, 0), row(SEG, 0),
                  row(d, gate0), row(d, gate0 + 1), row(d, gate0 + 2),
                  full(wr), full(ww), full(wn), full(wo), row(d, 0), full(m2), full(gf), full(m3), full(m4), full(wrt)],
        out_specs=[pl.BlockSpec((tm, d), lambda i: (i, 0)),
                   pl.BlockSpec((tm * (d // LANES), LANES), lambda i: (i, 0)),
                   pl.BlockSpec((N_EXPERTS, tm), lambda i: (0, i))],
        out_shape=[jax.ShapeDtypeStruct((n, d), F32), jax.ShapeDtypeStruct((n * (d // LANES), LANES), F32),
                   jax.ShapeDtypeStruct((N_EXPERTS, n), F32)],
        compiler_params=_cparams(("arbitrary",)),
        name="merge",
    )(yf, yb, p, gn, yw, yn, p, p, p, wr, ww, wn, wo, x, m2, gf, m3, m4, wrt)


def _route_kernel(lt_ref, idx_ref, gate_ref, pos_ref, goff_ref, aff_ref, thr_ref, *, cap, n_groups):
    e = pl.program_id(0)
    n_exp, ag, _ = lt_ref.shape
    capp = idx_ref.shape[-1]

    @pl.when(e == 0)
    def _():
        lt = lt_ref[...]
        ex = jnp.exp(lt - jnp.max(lt, axis=0, keepdims=True))
        aff = ex / jnp.sum(ex, axis=0, keepdims=True)
        real = _iota(aff.shape, 1) < n_groups
        aff = jnp.where(real, aff, 0.0)
        aff_ref[...] = aff

        def count_ge(v):
            return jnp.sum(jnp.sum((aff >= v).astype(F32), axis=1, keepdims=True), axis=2, keepdims=True)

        def body(k, carry):
            lo, hi = carry
            mid = jnp.where(lo > 0.0, jnp.sqrt(lo) * jnp.sqrt(hi), hi * 2.0 ** -16)
            ge = count_ge(mid) >= float(cap)
            return jnp.where(ge, mid, lo), jnp.where(ge, hi, mid)

        lo, _ = lax.fori_loop(0, BISECT_STEPS, body, (jnp.zeros((n_exp, 1, 1), F32), jnp.full((n_exp, 1, 1), 2.0, F32)))
        big = jnp.where(aff >= lo, aff, 4.0)
        t = jnp.min(jnp.min(big, axis=1, keepdims=True), axis=2, keepdims=True)
        thr_ref[...] = jnp.broadcast_to(t, thr_ref.shape)

    a = aff_ref[e]
    t = thr_ref[e][0:1, :]
    gt = a > t
    eq = (a == t) & (_iota(a.shape, 0) < n_groups)
    tri_lane_strict = (_iota((LANES, LANES), 0) < _iota((LANES, LANES), 1)).astype(BF16)
    tri_lane_incl = (_iota((LANES, LANES), 0) <= _iota((LANES, LANES), 1)).astype(BF16)
    tri_grp_strict = (_iota((ag, ag), 1) < _iota((ag, ag), 0)).astype(BF16)
    tri_grp_incl = (_iota((ag, ag), 1) <= _iota((ag, ag), 0)).astype(BF16)

    def total(mask_f):
        return jnp.sum(jnp.sum(mask_f, axis=1, keepdims=True), axis=0, keepdims=True)

    def group_sum(mask_f):
        return jnp.broadcast_to(jnp.sum(mask_f, axis=1, keepdims=True), (ag, LANES)).astype(BF16)

    eq_f = eq.astype(F32)
    need = float(cap) - total(gt.astype(F32))
    rank_eq = _dot(tri_grp_strict, group_sum(eq_f)) + _dot(eq_f.astype(BF16), tri_lane_strict)
    sel = gt | (eq & (rank_eq < need))
    sel_f = sel.astype(F32)
    cl = _dot(sel_f.astype(BF16), tri_lane_incl)
    cg = _dot(tri_grp_incl, group_sum(sel_f))
    goff = cg - jnp.broadcast_to(jnp.sum(sel_f, axis=1, keepdims=True), (ag, LANES))
    pos_ref[...] = jnp.where(sel, goff + cl - 1.0, -1.0)
    diag = _iota((ag, ag), 0) == _iota((ag, ag), 1)
    goff_sq = goff if ag == LANES else goff[:, :ag]
    goff_ref[...] = jnp.sum(jnp.where(diag, goff_sq, 0.0), axis=0, keepdims=True).astype(I32)

    pp = _iota((ag, capp), 1, F32)
    cg_b = jnp.broadcast_to(cg[:, 0:1], (ag, capp))
    below = cg_b <= pp
    grp = jnp.sum(below.astype(F32), axis=0, keepdims=True)
    off = jnp.max(jnp.where(below, cg_b, 0.0), axis=0, keepdims=True)
    onehot = _iota((ag, capp), 0, F32) == grp
    in_grp = _dot_tn(cl.astype(BF16), onehot.astype(BF16))
    local = pp[0:1, :] - off
    lane_of = jnp.sum((in_grp <= local).astype(F32), axis=0, keepdims=True)
    live = pp[0:1, :] < float(cap)
    idx = jnp.where(live, grp * float(LANES) + lane_of, 0.0)
    idx_ref[...] = idx.astype(I32)
    a_hi = a.astype(BF16)
    r1 = a - a_hi.astype(F32)
    a_mid = r1.astype(BF16)
    a_lo = (r1 - a_mid.astype(F32)).astype(BF16)
    hot = onehot.astype(BF16)
    aff_grp = (_dot_tn(a_hi, hot) + _dot_tn(a_mid, hot)) + _dot_tn(a_lo, hot)
    pick = _iota((LANES, capp), 0, F32) == lane_of
    gate = jnp.sum(jnp.where(pick, aff_grp, 0.0), axis=0, keepdims=True)
    gate_ref[...] = jnp.where(live, gate, 0.0)


def _route(logits_t, *, cap, n_groups):
    n_exp, ag, _ = logits_t.shape
    capp = -(-cap // LANES) * LANES
    return pl.pallas_call(
        functools.partial(_route_kernel, cap=cap, n_groups=n_groups),
        grid=(n_exp,),
        in_specs=[pl.BlockSpec(logits_t.shape, lambda e: (0, 0, 0))],
        out_specs=[pl.BlockSpec((None, 1, capp), lambda e: (e, 0, 0)),
                   pl.BlockSpec((None, 1, capp), lambda e: (e, 0, 0)),
                   pl.BlockSpec((None, ag, LANES), lambda e: (e, 0, 0)),
                   pl.BlockSpec((None, 1, ag), lambda e: (e, 0, 0))],
        out_shape=[jax.ShapeDtypeStruct((n_exp, 1, capp), I32), jax.ShapeDtypeStruct((n_exp, 1, capp), F32),
                   jax.ShapeDtypeStruct((n_exp, ag, LANES), F32), jax.ShapeDtypeStruct((n_exp, 1, ag), I32)],
        scratch_shapes=[pltpu.VMEM((n_exp, ag, LANES), F32), pltpu.VMEM((n_exp, SUBLANES, LANES), F32)],
        compiler_params=_cparams(("arbitrary",)),
        name="route",
    )(logits_t)


def _ffn_kernel(idx_ref, idxn_ref, gate_ref, hx_ref, wg_ref, wu_ref, wd_ref, o_ref, xbuf, sem, wgb, wub, wdb, *, tm):
    e = pl.program_id(0)
    j = pl.program_id(1)
    nt = pl.num_programs(1)
    step = e * nt + j
    last = pl.num_programs(0) * nt - 1

    sub = SUBLANES
    rows = tm * sub

    def row_copy(rows_ref, r, slot):
        src = hx_ref.at[pl.ds(pl.multiple_of(rows_ref[0, 0, r] * sub, sub), sub)]
        dst = xbuf.at[pl.ds(pl.multiple_of(slot * rows + r * sub, sub), sub)]
        return pltpu.make_async_copy(src, dst, sem.at[slot])

    def slot_copy(slot):
        return pltpu.make_async_copy(hx_ref.at[pl.ds(0, rows)], xbuf.at[pl.ds(pl.multiple_of(slot * rows, sub), rows)],
                                     sem.at[slot])

    @pl.when(step == 0)
    def _():
        def body(r, carry):
            row_copy(idx_ref, r, 0).start()
            return carry
        lax.fori_loop(0, tm, body, 0, unroll=8)

    @pl.when(j == 0)
    def _():
        wgb[...] = wg_ref[...].astype(BF16)
        wub[...] = wu_ref[...].astype(BF16)
        wdb[...] = wd_ref[...].astype(BF16)

    slot = step % 2
    nslot = 1 - slot
    for r in range(tm):
        row_copy(idxn_ref, r, nslot).start(priority=r % 2)
    slot_copy(slot).wait()
    base = slot * rows
    x = jnp.concatenate([xbuf[pl.ds(base + s, tm, stride=sub), :] for s in range(sub)], axis=1).astype(BF16)
    g = _dot(x, wgb[...])
    u = _dot(x, wub[...])
    hid = (g * jax.nn.sigmoid(g)) * u
    y = _dot(hid.astype(BF16), wdb[...])
    eye = _iota((tm, tm), 0) == _iota((tm, tm), 1)
    gcol = jnp.sum(jnp.where(eye, jnp.broadcast_to(gate_ref[0], (tm, tm)), 0.0), axis=1, keepdims=True)
    o_ref[...] = (y * gcol).astype(BF16)

    @pl.when(step == last)
    def _():
        slot_copy(nslot).wait()


def _expert_ffn(idx, gate, hx3, w_gate, w_up, w_down, *, layer, cap, tm):
    n_exp = idx.shape[0]
    d = w_gate.shape[2]
    f = w_gate.shape[3]
    assert d == SUBLANES * LANES, "a token row must be exactly one (8, 128) f32 tile"
    nt = cap // tm
    idx_t = idx[:, :, :cap].reshape(n_exp * nt, 1, tm)
    gate_t = gate[:, :, :cap].reshape(n_exp * nt, 1, tm)
    n_steps = n_exp * nt
    wspec = lambda a, b: pl.BlockSpec((None, None, a, b), lambda e, j: (layer, e, 0, 0))
    return pl.pallas_call(
        functools.partial(_ffn_kernel, tm=tm),
        grid=(n_exp, nt),
        in_specs=[pl.BlockSpec((1, 1, tm), lambda e, j: (e * nt + j, 0, 0), memory_space=pltpu.SMEM),
                  pl.BlockSpec((1, 1, tm), lambda e, j: (jnp.minimum(e * nt + j + 1, n_steps - 1), 0, 0), memory_space=pltpu.SMEM),
                  pl.BlockSpec((1, 1, tm), lambda e, j: (e * nt + j, 0, 0)),
                  pl.BlockSpec(memory_space=pl.ANY),
                  wspec(d, f), wspec(d, f), wspec(f, d)],
        out_specs=pl.BlockSpec((tm, d), lambda e, j: (e * nt + j, 0)),
        out_shape=jax.ShapeDtypeStruct((n_exp * cap, d), BF16),
        scratch_shapes=[pltpu.VMEM((2 * tm * SUBLANES, LANES), F32), pltpu.SemaphoreType.DMA((2,)),
                        pltpu.VMEM((d, f), BF16), pltpu.VMEM((d, f), BF16), pltpu.VMEM((f, d), BF16)],
        compiler_params=_cparams(("arbitrary", "arbitrary")),
        name="expert_ffn",
    )(idx_t, idx_t, gate_t, hx3, w_gate, w_up, w_down)


def _combine_kernel(ws_ref, nw_ref, *refs, n_exp, cap, win, gpt, final_norm, next_norm):
    y_refs = refs[:n_exp]
    if next_norm:
        pos_ref, ye_ref, x_ref, m5_ref, gf_ref, ng_ref, nsh_ref, nsc_ref, o_ref, h_ref, acc_ref, xwin, sem = refs[n_exp:]
    else:
        pos_ref, ye_ref, x_ref, m5_ref, gf_ref, o_ref, acc_ref, xwin, sem = refs[n_exp:]
    a = pl.program_id(0)
    ng = pl.num_programs(0)
    per = max(1, MXU_DEPTH // win)

    def slots(e):
        pos = pos_ref[e, pl.ds(a * gpt, gpt), :]
        return jnp.concatenate([jnp.broadcast_to(pos[j:j + 1, :], (win, LANES)) for j in range(gpt)], axis=1)

    lp = _iota((win, gpt * LANES), 0, F32)
    acc = None
    for e0 in range(0, n_exp, per):
        es = range(e0, min(e0 + per, n_exp))
        hot = jnp.concatenate([((slots(e) - (ws_ref[e * ng + a] - e * cap).astype(F32)) == lp).astype(BF16) for e in es],
                              axis=0)
        rows = jnp.concatenate([y_refs[e][...] for e in es], axis=0)
        part = _dot_tn(hot, rows)
        acc = part if acc is None else acc + part
    acc_ref[...] = acc

    for e in range(n_exp):
        first = ws_ref[e * ng + a] - e * cap

        def extra(k, carry, e=e, first=first):
            lo = first + k * win
            row = jnp.minimum(lo, cap - win)
            cp = pltpu.make_async_copy(ye_ref.at[pl.ds(pl.multiple_of(e * cap + row, BF16_ROWS), win)], xwin, sem)
            cp.start()
            cp.wait()
            pos = slots(e)
            hot = ((pos - row.astype(F32)) == lp) & (pos >= lo.astype(F32))
            acc_ref[...] += _dot_tn(hot.astype(BF16), xwin[...])
            return carry

        lax.fori_loop(1, nw_ref[e * ng + a], extra, 0)

    x_new = x_ref[...] + m5_ref[...] * acc_ref[...]
    if final_norm:
        ms = jnp.mean(x_new * x_new, axis=-1, keepdims=True)
        x_new = x_new * lax.rsqrt(ms + EPS) * gf_ref[...]
    o_ref[...] = x_new
    if next_norm:
        h_ref[...] = _norm_mod(x_new, ng_ref[...], nsh_ref[...], nsc_ref[...]).astype(BF16)


def _combine(ye, posmap, goff, x, m5, g_final, *, cap, final_norm, next_mod=None):
    n, d = x.shape
    n_exp = posmap.shape[0]
    gpt = min(COMBINE_GROUPS, n // LANES)
    ng = n // (gpt * LANES)
    tok = gpt * LANES
    win = min(COMBINE_WINDOW, cap)
    goff = goff[:, 0, :ng * gpt:gpt]
    end = jnp.concatenate([goff[:, 1:], jnp.full((n_exp, 1), cap, I32)], axis=1)
    start = jnp.minimum((goff // BF16_ROWS) * BF16_ROWS, cap - win)
    n_win = jnp.maximum((end - start + win - 1) // win, 1).reshape(-1)
    wstart = (start + jnp.arange(n_exp, dtype=I32)[:, None] * cap).reshape(-1)
    yspec = lambda e: pl.BlockSpec((pl.Element(win), pl.Element(d)),
                                   lambda a, ws, nw: (pl.multiple_of(ws[e * ng + a], BF16_ROWS), 0))
    gs = pltpu.PrefetchScalarGridSpec(
        num_scalar_prefetch=2,
        grid=(ng,),
        in_specs=[yspec(e) for e in range(n_exp)]
        + [pl.BlockSpec(posmap.shape, lambda a, ws, nw: (0, 0, 0)),
           pl.BlockSpec(memory_space=pl.ANY),
           pl.BlockSpec((tok, d), lambda a, ws, nw: (a, 0)),
           pl.BlockSpec((1, d), lambda a, ws, nw: (0, 0)), pl.BlockSpec((1, d), lambda a, ws, nw: (0, 0))]
        + [pl.BlockSpec((1, d), lambda a, ws, nw: (0, 0)) for _ in (next_mod or ())],
        out_specs=[pl.BlockSpec((tok, d), lambda a, ws, nw: (a, 0))] * (2 if next_mod else 1),
        scratch_shapes=[pltpu.VMEM((tok, d), F32), pltpu.VMEM((win, d), BF16), pltpu.SemaphoreType.DMA(())],
    )
    out = pl.pallas_call(
        functools.partial(_combine_kernel, n_exp=n_exp, cap=cap, win=win, gpt=gpt, final_norm=final_norm,
                          next_norm=next_mod is not None),
        grid_spec=gs,
        out_shape=[jax.ShapeDtypeStruct((n, d), F32)] + ([jax.ShapeDtypeStruct((n, d), BF16)] if next_mod else []),
        compiler_params=_cparams(("arbitrary",)),
        name="combine_final" if final_norm else ("combine_norm" if next_mod else "combine"),
    )(wstart, n_win, *([ye] * n_exp), posmap, ye, x, m5, g_final, *(next_mod or ()))
    return out if next_mod else out[0]


def _rope_tables(n):
    n_rows = n // GRID_W
    row = jnp.arange(n_rows).astype(F32)
    col = jnp.arange(GRID_W).astype(F32)

    def tables(d, signs):
        nf = d // 4
        inv = ROPE_BASE ** (-jnp.arange(nf, dtype=F32) / nf)
        ang_r, ang_c = row[:, None] * inv, col[:, None] * inv
        reps = LANES // (2 * nf)

        def lanes(f_r, f_c, factors):
            z_r, z_c = jnp.zeros_like(f_r), jnp.zeros_like(f_c)
            pat_r = jnp.concatenate([part for k in range(reps) for part in (f_r * factors[2 * k], z_r)], axis=1)
            pat_c = jnp.concatenate([part for k in range(reps) for part in (z_c, f_c * factors[2 * k + 1])], axis=1)
            return (pat_r[:, None, :] + pat_c[None, :, :]).reshape(n, LANES)

        ones = (1.0,) * (2 * reps)
        cos = lanes(jnp.cos(ang_r), jnp.cos(ang_c), ones)
        return [cos] + [lanes(jnp.sin(ang_r), jnp.sin(ang_c), sg) for sg in signs]

    c128, s128 = tables(RET_D, [(-1.0, -1.0, 1.0, 1.0)])
    c64, sa, sb = tables(HEAD_D, [(-1.0, -1.0, 0.0, 0.0) * 2, (0.0, 0.0, 1.0, 1.0) * 2])
    return c128, s128, c64, sa, sb


WIN_HEAD_ORDER = (0, 4, 1, 5, 2, 6, 3, 7)


def _permute_in_weight(w):
    d = w.shape[0]
    sizes = (("q_r", 512), ("k_r", 512), ("v_r", 512), ("g_r", 512), ("q_w", 512), ("k_w", 128), ("v_w", 128),
             ("q_n", 512), ("k_n", 512), ("v_n", 512), ("gates", 3 * d))
    off, lay = 0, {}
    for name, size in sizes:
        lay[name] = (off, off + size)
        off += size
    seg = lambda name: w[:, lay[name][0]:lay[name][1]]
    q0 = lay["q_w"][0]
    q_w = [w[:, q0 + h * HEAD_D:q0 + (h + 1) * HEAD_D] for h in WIN_HEAD_ORDER]
    pad = jnp.zeros((d, SEG - 2 * LANES), w.dtype)
    rope = jnp.concatenate([seg("q_r"), seg("k_r")] + q_w + [seg("k_w"), seg("v_w"), pad], axis=1).astype(BF16)
    plain = jnp.concatenate([seg("gates"), seg("v_r"), seg("g_r"), seg("q_n"), seg("k_n"), seg("v_n")], axis=1).astype(BF16)
    return rope, plain


def _permute_win_rows(w):
    return jnp.concatenate([w[h * HEAD_D:(h + 1) * HEAD_D] for h in WIN_HEAD_ORDER], axis=0).astype(BF16)


def kernel(x, c, ctx, c_ctx, w_mod, b_mod, g_mix, g_ffn, w_in, ret_decay_logit, ret_gn, w_ret, win_sink, w_win, na_rpb,
           w_na, w_out, w_router, w_exp_gate, w_exp_up, w_exp_down, g_final):
    _, n, d = x.shape
    l = ctx.shape[1]
    depth = w_in.shape[0]
    xs, cs = x[0], ctx[0]
    cc = jnp.zeros((SUBLANES, d), F32).at[0].set(c[0]).at[1].set(c_ctx)
    mods = _modulation(cc, w_mod, b_mod)
    c128, s128, c64, sa64, sb64 = _rope_tables(n)
    k_scale = RET_D ** -0.5
    plain_scales = jnp.ones((PLAIN_TILES,), F32).at[T_QN].set(Q_SCALE)
    ctx_rope_scales = jnp.ones((R_KVW + 1,), F32).at[R_KR].set(k_scale).at[R_QW].set(Q_SCALE)
    cap_x = CAPACITY_FACTOR * n // N_EXPERTS
    cap_c = CAPACITY_FACTOR * l // N_EXPERTS
    grp_c = -(-(l // LANES) // BF16_ROWS) * BF16_ROWS
    vec = lambda v: v.reshape(1, -1)

    for layer in range(depth):
        need_ctx = layer < depth - 1
        last = layer == depth - 1
        mx = [vec(mods[layer, 0, k * d:(k + 1) * d]) for k in range(N_MOD)]
        mc = [vec(mods[layer, 1, k * d:(k + 1) * d]) for k in range(N_MOD)]
        w_rope, w_plain = _permute_in_weight(w_in[layer])
        wr = w_ret[layer].astype(BF16)
        ww = _permute_win_rows(w_win[layer])
        wn = w_na[layer].astype(BF16)
        wo = w_out[layer].astype(BF16)
        wrt = w_router[layer].T
        sink = win_sink[layer].astype(F32) * LOG2E
        lg_rows = jnp.broadcast_to(ret_decay_logit[layer].astype(F32).reshape(2 * RET_HEADS, 1), (2 * RET_HEADS, LANES))
        gn = vec(ret_gn[layer])
        bias_tab = _na_bias_table(na_rpb[layer].astype(F32) * LOG2E)
        gmix, gffn = vec(g_mix[layer]), vec(g_ffn[layer])

        if layer == 0:
            hx = _norm_call(xs, gmix, mx[0], mx[1], tm=min(NORM_ROWS, n))
        hc = _norm_call(cs, gmix, mc[0], mc[1], tm=l)
        p_x = _proj(hx, w_plain, plain_scales, tm=min(PROJ_ROWS, n))
        p_c = _proj(hc, w_plain, plain_scales, tm=l)
        r_c = _proj(hc, w_rope, ctx_rope_scales, tm=l)
        rope = functools.partial(_proj_rope, hx, w_rope, tm=min(PROJ_ROWS, n))
        q_r = rope(R_QR, (c128, s128), head_d=RET_D, scale=1.0, n_rot=SEG // LANES, name="proj_q_ret")
        k_r = rope(R_KR, (c128, s128), head_d=RET_D, scale=k_scale, n_rot=SEG // LANES, name="proj_k_ret")
        q_w = rope(R_QW, (c64, sa64, sb64), head_d=HEAD_D, scale=Q_SCALE, n_rot=SEG // LANES, name="proj_q_win")
        kv_w = rope(R_KVW, (c64, sa64, sb64), head_d=HEAD_D, scale=1.0, n_rot=1, name="proj_kv_win")

        yf, yb = _retention((q_r, 0), (k_r, 0), (p_x, T_VR), (r_c, R_KR), (p_c, T_VR), lg_rows, zero_init=False)
        yw = _window_attention((q_w, 0), (kv_w, 0), (r_c, R_KVW), sink)
        yn = _na_attention(p_x, p_c, bias_tab)
        xs, hx3, lt = _merge(yf, yb, p_x, gn, yw, yn, wr, ww, wn, wo, xs, mx[2], gffn, mx[3], mx[4], wrt, tm=min(MERGE_ROWS, n))

        idx, gate, posmap, goff = _route(lt.reshape(N_EXPERTS, n // LANES, LANES), cap=cap_x, n_groups=n // LANES)
        ye = _expert_ffn(idx, gate, hx3, w_exp_gate, w_exp_up, w_exp_down, layer=layer, cap=cap_x, tm=min(FFN_ROWS, cap_x))
        if last:
            xs = _combine(ye, posmap, goff, xs, mx[5], vec(g_final), cap=cap_x, final_norm=True)
        else:
            nxt = (vec(g_mix[layer + 1]), vec(mods[layer + 1, 0, :d]), vec(mods[layer + 1, 0, d:2 * d]))
            xs, hx = _combine(ye, posmap, goff, xs, mx[5], vec(g_final), cap=cap_x, final_norm=False, next_mod=nxt)

        if need_ctx:
            yfc, ybc = _retention((r_c, R_QR), (r_c, R_KR), (p_c, T_VR), (r_c, R_KR), (p_c, T_VR), lg_rows, zero_init=True)
            ywc, ync = _ctx_attention(r_c, p_c, sink)
            cs, hc3, ltc = _merge(yfc, ybc, p_c, gn, ywc, ync, wr, ww, wn, wo, cs, mc[2], gffn, mc[3], mc[4], wrt, tm=l)
            ltc = jnp.pad(ltc.reshape(N_EXPERTS, l // LANES, LANES), ((0, 0), (0, grp_c - l // LANES), (0, 0)))
            idc, gtc, posc, goffc = _route(ltc, cap=cap_c, n_groups=l // LANES)
            yec = _expert_ffn(idc, gtc, hc3, w_exp_gate, w_exp_up, w_exp_down, layer=layer, cap=cap_c, tm=cap_c)
            cs = _combine(yec, posc, goffc, cs, mc[5], vec(g_final), cap=cap_c, final_norm=False)

    return xs[None]
```

```python
import functools

import jax
import jax.numpy as jnp
import numpy as np
from jax import lax
from jax.experimental import pallas as pl
from jax.experimental.pallas import tpu as pltpu

F32, BF16, I32 = jnp.float32, jnp.bfloat16, jnp.int32
HIGHEST = lax.Precision.HIGHEST

GRID_W = 64
RET_HEADS, RET_D, RET_CHUNK = 4, 128, 128
WIN_HEADS, WIN_KV_HEADS, HEAD_D, WINDOW, WIN_BLOCK = 8, 2, 64, 128, 128
NA_HEADS, NA_ROWS, NA_COLS = 8, 8, 16
N_EXPERTS, CAPACITY_FACTOR = 16, 2
N_MOD = 6
ROPE_BASE = 10000.0
EPS = 1e-6
NEG_INF = -1e30

LANES = 128
SUBLANES = 8
BF16_ROWS = 16
VMEM_LIMIT = 56 * 1024 * 1024

SEG = 512
T_GATES, T_VR, T_GR, T_QN, T_KN, T_VN = 0, 6, 7, 8, 9, 10
PLAIN_TILES = 11
R_QR, R_KR, R_QW, R_KVW = 0, 1, 2, 3
LOG2E = 1.4426950408889634
Q_SCALE = HEAD_D ** -0.5 * LOG2E
NORM_ROWS = 1024
PROJ_ROWS = 2048
MERGE_ROWS = 512
FFN_ROWS = 512
RET_BLOCK = 2 * RET_CHUNK
WIN_QBLOCKS = 2
NA_QROWS = 4
NA_SLAB_ROWS = 12
NA_BIAS_ZERO = NA_SLAB_ROWS - 2
NA_BIAS_PAIRS = NA_BIAS_ZERO + NA_SLAB_ROWS - 1
COMBINE_GROUPS = 4
COMBINE_WINDOW = 128
MXU_DEPTH = 256
BISECT_STEPS = 48


def _cparams(sem):
    return pltpu.CompilerParams(dimension_semantics=sem, vmem_limit_bytes=VMEM_LIMIT)


def _dot(a, b):
    return jnp.dot(a, b, preferred_element_type=F32)


def _dot_nt(a, b, precision=None):
    return lax.dot_general(a, b, (((1,), (1,)), ((), ())), precision=precision, preferred_element_type=F32)


def _dot_tn(a, b, precision=None):
    return lax.dot_general(a, b, (((0,), (0,)), ((), ())), precision=precision, preferred_element_type=F32)


def _iota(shape, dim, dtype=I32):
    return lax.broadcasted_iota(I32, shape, dim).astype(dtype)


def _mod_kernel(s_ref, w_ref, b_ref, o_ref):
    s = s_ref[...]
    s = s * jax.nn.sigmoid(s)
    o_ref[...] = jnp.dot(s, w_ref[...], precision=HIGHEST, preferred_element_type=F32) + b_ref[...]


def _modulation(cc, w_mod, b_mod):
    depth, d, md = w_mod.shape
    tn = 1536
    return pl.pallas_call(
        _mod_kernel,
        grid=(depth, md // tn),
        in_specs=[
            pl.BlockSpec((SUBLANES, d), lambda l, j: (0, 0)),
            pl.BlockSpec((None, d, tn), lambda l, j: (l, 0, j)),
            pl.BlockSpec((None, 1, tn), lambda l, j: (l, 0, j)),
        ],
        out_specs=pl.BlockSpec((None, SUBLANES, tn), lambda l, j: (l, 0, j)),
        out_shape=jax.ShapeDtypeStruct((depth, SUBLANES, md), F32),
        compiler_params=_cparams(("arbitrary", "arbitrary")),
        name="modulation",
    )(cc, w_mod, b_mod.reshape(depth, 1, md))


def _norm_mod(x, g, shift, scale):
    ms = jnp.mean(x * x, axis=-1, keepdims=True)
    y = x * lax.rsqrt(ms + EPS) * g
    return y * (1.0 + scale) + shift


def _norm_kernel(x_ref, g_ref, sh_ref, sc_ref, o_ref):
    o_ref[...] = _norm_mod(x_ref[...], g_ref[...], sh_ref[...], sc_ref[...]).astype(BF16)


def _norm_call(x, g, shift, scale, *, tm):
    n, d = x.shape
    vec = lambda: pl.BlockSpec((1, d), lambda i: (0, 0))
    return pl.pallas_call(
        _norm_kernel,
        grid=(n // tm,),
        in_specs=[pl.BlockSpec((tm, d), lambda i: (i, 0)), vec(), vec(), vec()],
        out_specs=pl.BlockSpec((tm, d), lambda i: (i, 0)),
        out_shape=jax.ShapeDtypeStruct((n, d), BF16),
        compiler_params=_cparams(("arbitrary",)),
        name="norm_modulate",
    )(x, g, shift, scale)


def _proj_kernel(scale_ref, hx_ref, w_ref, o_ref):
    o_ref[...] = (_dot(hx_ref[...], w_ref[...]) * scale_ref[pl.program_id(1)]).astype(BF16)


def _proj(hx, w, scales, *, tm):
    n, d = hx.shape
    tiles = w.shape[1] // SEG
    return pl.pallas_call(
        _proj_kernel,
        grid=(n // tm, tiles),
        in_specs=[pl.BlockSpec(memory_space=pltpu.SMEM),
                  pl.BlockSpec((tm, d), lambda i, j: (i, 0)), pl.BlockSpec((d, SEG), lambda i, j: (0, j))],
        out_specs=pl.BlockSpec((tm, SEG), lambda i, j: (i, j)),
        out_shape=jax.ShapeDtypeStruct((n, tiles * SEG), BF16),
        compiler_params=_cparams(("arbitrary", "arbitrary")),
        name="projection",
    )(scales, hx, w)


def _proj_rope_kernel(hx_ref, w_ref, *refs, head_d, scale, n_rot):
    tabs, o_ref = refs[:-1], refs[-1]
    acc = _dot(hx_ref[...], w_ref[...])
    if head_d == RET_D:
        c = tabs[0][...] * scale
        s = tabs[1][...] * scale
        rot = lambda a: a * c + pltpu.roll(a, 64, 1) * s
    else:
        c = tabs[0][...] * scale
        sa = tabs[1][...] * scale
        sb = tabs[2][...] * scale
        rot = lambda a: a * c + pltpu.roll(a, 96, 1) * sa + pltpu.roll(a, 32, 1) * sb
    groups = [acc[:, k * LANES:(k + 1) * LANES] for k in range(SEG // LANES)]
    o_ref[...] = jnp.concatenate([rot(a) if k < n_rot else a for k, a in enumerate(groups)], axis=1).astype(BF16)


def _proj_rope(hx, w, tile, tabs, *, head_d, scale, n_rot, tm, name):
    n, d = hx.shape
    return pl.pallas_call(
        functools.partial(_proj_rope_kernel, head_d=head_d, scale=scale, n_rot=n_rot),
        grid=(n // tm,),
        in_specs=[pl.BlockSpec((tm, d), lambda i: (i, 0)), pl.BlockSpec((d, SEG), lambda i: (0, tile))]
        + [pl.BlockSpec((tm, LANES), lambda i: (i, 0)) for _ in tabs],
        out_specs=pl.BlockSpec((tm, SEG), lambda i: (i, 0)),
        out_shape=jax.ShapeDtypeStruct((n, SEG), BF16),
        compiler_params=_cparams(("arbitrary",)),
        name=name,
    )(hx, w, *tabs)


def _ret_kernel(lg_ref, qf_ref, kf_ref, vf_ref, qb_ref, kb_ref, vb_ref, kc_ref, vc_ref, yf_ref, yb_ref,
                rf_ref, rb_ref, dmat_ref, vec_ref, *, zero_init):
    c = pl.program_id(0)
    ch = qf_ref.shape[0]
    dh = RET_D
    states = (rf_ref, rb_ref)

    @pl.when(c == 0)
    def _():
        lg = jax.nn.log_sigmoid(lg_ref[...])
        ii = _iota((ch, ch), 0, F32)
        jj = _iota((ch, ch), 1, F32)
        iv = _iota((ch, dh), 0, F32)
        n_ctx = kc_ref.shape[0]
        mm = _iota((n_ctx, dh), 0, F32)
        for d in range(2):
            for h in range(RET_HEADS):
                row = lg[RET_HEADS * d + h:RET_HEADS * d + h + 1, :]
                l = jnp.concatenate([jnp.broadcast_to(row, (ch, dh))] * (ch // dh), axis=1)
                lv = jnp.broadcast_to(row, (ch, dh))
                if d == 0:
                    diff = ii - jj
                    dmat_ref[d, h] = jnp.where(diff >= 0.0, jnp.exp(jnp.maximum(diff, 0.0) * l), 0.0)
                    vec_ref[d, h, 0] = jnp.exp((ch - 1.0 - iv) * lv)
                    vec_ref[d, h, 1] = jnp.exp((iv + 1.0) * lv)
                else:
                    diff = jj - ii
                    dmat_ref[d, h] = jnp.where(diff >= 1.0, jnp.exp(jnp.maximum(diff, 0.0) * l), 0.0)
                    vec_ref[d, h, 0] = jnp.exp(iv * lv)
                    vec_ref[d, h, 1] = jnp.exp((ch - iv) * lv)
                vec_ref[d, h, 2] = jnp.exp(float(ch) * lv)
                if zero_init:
                    states[d][h] = jnp.zeros((dh, dh), F32)
                else:
                    lc = jnp.broadcast_to(row, (n_ctx, dh))
                    w = jnp.exp((n_ctx - 1.0 - mm) * lc) if d == 0 else jnp.exp(mm * lc)
                    sl = slice(h * dh, (h + 1) * dh)
                    kw = (kc_ref[:, sl].astype(F32) * w).astype(BF16)
                    states[d][h] = _dot_tn(kw, vc_ref[:, sl])

    for d, (q_ref, k_ref, v_ref, y_ref) in enumerate(((qf_ref, kf_ref, vf_ref, yf_ref), (qb_ref, kb_ref, vb_ref, yb_ref))):
        for h in range(RET_HEADS):
            sl = slice(h * dh, (h + 1) * dh)
            q, k, v = q_ref[:, sl], k_ref[:, sl], v_ref[:, sl]
            r = states[d][h]
            a = (_dot_nt(q, k) * dmat_ref[d, h]).astype(BF16)
            inner = _dot(a, v)
            cross = _dot(q, r.astype(BF16)) * vec_ref[d, h, 1]
            y_ref[:, sl] = inner + cross
            kz = (k.astype(F32) * vec_ref[d, h, 0]).astype(BF16)
            states[d][h] = vec_ref[d, h, 2][:dh, :] * r + _dot_tn(kz, v)


def _retention(q, k, v, kc, vc, lg_rows, *, zero_init):
    n = q[0].shape[0]
    blk = min(RET_BLOCK, n)
    nc = n // blk
    w = RET_HEADS * RET_D
    fwd = lambda sg: pl.BlockSpec((blk, w), lambda c: (c, sg[1]))
    bwd = lambda sg: pl.BlockSpec((blk, w), lambda c: (nc - 1 - c, sg[1]))
    ctx = lambda sg: pl.BlockSpec((sg[0].shape[0], w), lambda c: (0, sg[1]))
    return pl.pallas_call(
        functools.partial(_ret_kernel, zero_init=zero_init),
        grid=(nc,),
        in_specs=[pl.BlockSpec((SUBLANES, LANES), lambda c: (0, 0)),
                  fwd(q), fwd(k), fwd(v), bwd(q), bwd(k), bwd(v), ctx(kc), ctx(vc)],
        out_specs=[pl.BlockSpec((blk, w), lambda c: (c, 0)), pl.BlockSpec((blk, w), lambda c: (nc - 1 - c, 0))],
        out_shape=[jax.ShapeDtypeStruct((n, w), F32), jax.ShapeDtypeStruct((n, w), F32)],
        scratch_shapes=[pltpu.VMEM((RET_HEADS, RET_D, RET_D), F32), pltpu.VMEM((RET_HEADS, RET_D, RET_D), F32),
                        pltpu.VMEM((2, RET_HEADS, blk, blk), F32), pltpu.VMEM((2, RET_HEADS, 3, blk, RET_D), F32)],
        compiler_params=_cparams(("arbitrary",)),
        name="retention_ctx" if zero_init else "retention",
    )(lg_rows, q[0], k[0], v[0], q[0], k[0], v[0], kc[0], vc[0])


def _half_mask(x, half):
    lane = _iota(x.shape, 1)
    keep = (lane < HEAD_D) if half == 0 else (lane >= HEAD_D)
    return jnp.where(keep, x, jnp.zeros_like(x))


def _softmax_pv(s, v, extra=None):
    m = jnp.max(s, axis=1, keepdims=True)
    if extra is not None:
        m = jnp.maximum(m, extra)
    p = jnp.exp2(s - m)
    den = jnp.sum(p, axis=1, keepdims=True)
    if extra is not None:
        den = den + jnp.exp2(extra - m)
    return _dot(p.astype(BF16), v) / den


def _softmax_pv_t(st, v, extra=None):
    m = jnp.max(st, axis=0, keepdims=True)
    if extra is not None:
        m = jnp.maximum(m, extra)
    p = jnp.exp2(st - m)
    den = jnp.sum(p, axis=0, keepdims=True)
    if extra is not None:
        den = den + jnp.exp2(extra - m)
    return _dot_tn(v, p.astype(BF16)) / den


def _win_kernel(sink_ref, q_ref, *refs):
    nkb = WIN_QBLOCKS + 2
    k_refs, v_refs = refs[:nkb], refs[nkb:2 * nkb]
    kx_ref, vx_ref, mask_ref, o_ref = refs[2 * nkb:]
    s = pl.program_id(0)
    last = pl.num_programs(0) - 1
    nq = WIN_QBLOCKS * WIN_BLOCK
    nk = nkb * WIN_BLOCK
    k_all = jnp.concatenate([r[...] for r in k_refs] + [kx_ref[...]], axis=0)
    v_all = jnp.concatenate([r[...] for r in v_refs] + [vx_ref[...]], axis=0)
    n_cols = SEG // LANES
    qs = jnp.concatenate([q_ref[:, c * LANES:(c + 1) * LANES] for c in range(n_cols)], axis=0)
    variant = jnp.where(s == 0, 1, 0) + jnp.where(s == last, 2, 0)
    valid = mask_ref[variant] > 0.5
    valid = jnp.concatenate([valid] * n_cols, axis=1)
    out_t = None
    for g in range(WIN_KV_HEADS):
        st = _dot_nt(_half_mask(k_all, g), qs)
        st = jnp.concatenate([jnp.where(valid, st[:nk], NEG_INF), st[nk:]], axis=0)
        sink = jnp.concatenate([jnp.full((1, nq), sink_ref[n_cols * g + c], F32) for c in range(n_cols)], axis=1)
        o = _softmax_pv_t(st, _half_mask(v_all, g), sink)
        out_t = o if out_t is None else out_t + o
    o = out_t.T
    o_ref[...] = jnp.concatenate([o[c * nq:(c + 1) * nq, :] for c in range(n_cols)], axis=1).astype(BF16)


def _win_valid_table():
    nkb = WIN_QBLOCKS + 2
    qpos = np.arange(WIN_QBLOCKS * WIN_BLOCK)[None, :]
    kpos = np.arange(nkb * WIN_BLOCK)[:, None] - WIN_BLOCK
    band = np.abs(kpos - qpos) <= WINDOW
    tabs = []
    for variant in range(4):
        ok = band.copy()
        if variant & 1:
            ok &= kpos >= 0
        if variant & 2:
            ok &= kpos < WIN_QBLOCKS * WIN_BLOCK
        tabs.append(ok)
    return jnp.asarray(np.stack(tabs).astype(np.float32))


def _window_attention(q, kv, kv_ctx, sink):
    n = q[0].shape[0]
    nb = n // WIN_BLOCK
    assert nb % WIN_QBLOCKS == 0
    l = kv_ctx[0].shape[0]
    nkb = WIN_QBLOCKS + 2
    per = SEG // LANES
    blk = lambda off, j: pl.BlockSpec((WIN_BLOCK, LANES),
                                      lambda i: (jnp.clip(WIN_QBLOCKS * i - 1 + j, 0, nb - 1), per * kv[1] + off))
    mask_tab = _win_valid_table()
    return pl.pallas_call(
        _win_kernel,
        grid=(nb // WIN_QBLOCKS,),
        in_specs=[pl.BlockSpec(memory_space=pltpu.SMEM),
                  pl.BlockSpec((WIN_QBLOCKS * WIN_BLOCK, SEG), lambda i: (i, q[1]))]
        + [blk(0, j) for j in range(nkb)] + [blk(1, j) for j in range(nkb)]
        + [pl.BlockSpec((l, LANES), lambda i: (0, per * kv_ctx[1])), pl.BlockSpec((l, LANES), lambda i: (0, per * kv_ctx[1] + 1)),
           pl.BlockSpec(mask_tab.shape, lambda i: (0, 0, 0))],
        out_specs=pl.BlockSpec((WIN_QBLOCKS * WIN_BLOCK, SEG), lambda i: (i, 0)),
        out_shape=jax.ShapeDtypeStruct((n, SEG), BF16),
        compiler_params=_cparams(("arbitrary",)),
        name="window_attention",
    )(sink, q[0], *([kv[0]] * (2 * nkb)), kv_ctx[0], kv_ctx[0], mask_tab)


def _na_slab_start(s, half_rows):
    return jnp.clip((NA_QROWS // 2) * s - NA_ROWS // 4, 0, half_rows - NA_SLAB_ROWS // 2)


def _na_kernel(q_ref, *refs, rows):
    nslab = NA_SLAB_ROWS // 2
    k_refs, v_refs = refs[:nslab], refs[nslab:2 * nslab]
    kx_ref, vx_ref, bias_ref, mask_ref, o_ref = refs[2 * nslab:]
    s = pl.program_id(0)
    last = pl.num_programs(0) - 1
    w = GRID_W
    nq = NA_QROWS * w
    nk = NA_SLAB_ROWS * w
    delta = 2 * _na_slab_start(s, rows // 2) - NA_QROWS * s
    variant = jnp.where(s == 0, 0, jnp.where(s == last, 2, 1))
    valid = mask_ref[variant] > 0.5
    valid = jnp.concatenate([valid, valid], axis=1)
    row = _iota((LANES, nq), 0)
    for pair in range(NA_HEADS // 2):
        sl = slice(pair * LANES, (pair + 1) * LANES)
        q = q_ref[:, sl]
        qs = jnp.concatenate([_half_mask(q, 0), _half_mask(q, 1)], axis=0)
        k_all = jnp.concatenate([r[:, sl] for r in k_refs] + [kx_ref[:, sl]], axis=0)
        v_all = jnp.concatenate([r[:, sl] for r in v_refs] + [vx_ref[:, sl]], axis=0)
        st = _dot_nt(k_all, qs)
        bias = jnp.concatenate(
            [jnp.concatenate([bias_ref[2 * pair + u, delta + 2 * i - a + NA_BIAS_ZERO]
                              for u in range(2) for a in range(0, NA_QROWS, 2)], axis=1) for i in range(nslab)], axis=0)
        s_loc = jnp.where(valid, st[:nk] + bias, NEG_INF)
        o = _softmax_pv_t(jnp.concatenate([s_loc, st[nk:]], axis=0), v_all)
        o_ref[:, sl] = jnp.where(row < HEAD_D, o[:, :nq], o[:, nq:]).T.astype(BF16)


def _na_valid_table():
    kk = np.arange(NA_SLAB_ROWS)[:, None, None, None]
    ck = np.arange(GRID_W)[None, :, None, None]
    a = np.arange(NA_QROWS)[None, None, :, None]
    cq = np.arange(GRID_W)[None, None, None, :]
    c_start = np.clip(cq - NA_COLS // 2, 0, GRID_W - NA_COLS)
    col_ok = (ck >= c_start) & (ck < c_start + NA_COLS)
    first_row = (0 * a, a, 0 * a + NA_SLAB_ROWS - NA_ROWS)
    tabs = [(col_ok & (kk >= f) & (kk < f + NA_ROWS)).reshape(NA_SLAB_ROWS * GRID_W, NA_QROWS * GRID_W) for f in first_row]
    return jnp.asarray(np.stack(tabs).astype(np.float32))


def _na_attention(p, p_ctx, bias_tab):
    n = p.shape[0]
    rows = n // GRID_W
    assert rows % NA_QROWS == 0 and rows >= NA_SLAB_ROWS
    steps = rows // NA_QROWS
    l = p_ctx.shape[0]
    slab = lambda t, i: pl.BlockSpec((2 * GRID_W, SEG), lambda s: (_na_slab_start(s, rows // 2) + i, t))
    nslab = NA_SLAB_ROWS // 2
    mask_tab = _na_valid_table()
    return pl.pallas_call(
        functools.partial(_na_kernel, rows=rows),
        grid=(steps,),
        in_specs=[pl.BlockSpec((NA_QROWS * GRID_W, SEG), lambda s: (s, T_QN))]
        + [slab(T_KN, i) for i in range(nslab)] + [slab(T_VN, i) for i in range(nslab)]
        + [pl.BlockSpec((l, SEG), lambda s: (0, T_KN)), pl.BlockSpec((l, SEG), lambda s: (0, T_VN)),
           pl.BlockSpec(bias_tab.shape, lambda s: (0, 0, 0, 0)), pl.BlockSpec(mask_tab.shape, lambda s: (0, 0, 0))],
        out_specs=pl.BlockSpec((NA_QROWS * GRID_W, SEG), lambda s: (s, 0)),
        out_shape=jax.ShapeDtypeStruct((n, SEG), BF16),
        compiler_params=_cparams(("arbitrary",)),
        name="neighbourhood_attention",
    )(p, *([p] * (2 * nslab)), p_ctx, p_ctx, bias_tab, mask_tab)


def _na_bias_table(rpb):
    n_r, n_c = 2 * NA_ROWS - 1, 2 * NA_COLS - 1
    rpb = rpb.astype(F32)
    e_i, uk_i, wq_i = np.meshgrid(np.arange(NA_BIAS_PAIRS), np.arange(2), np.arange(2), indexing="ij")
    src = np.clip(e_i - NA_BIAS_ZERO + uk_i - wq_i + NA_ROWS - 1, 0, n_r - 1)
    pick = jnp.asarray((src[..., None] == np.arange(n_r)).astype(np.float32))
    rows = jnp.einsum("euwr,hrj->heuwj", pick, rpb, precision=HIGHEST)
    ck = np.arange(GRID_W)[:, None]
    cq = np.arange(GRID_W)[None, :]
    ci = np.clip(ck - cq, -(NA_COLS - 1), NA_COLS - 1) + NA_COLS - 1
    sel = (ci[None] == np.arange(n_c)[:, None, None]).astype(np.float32)
    sel2 = np.zeros((2, n_c, GRID_W, 2, GRID_W), np.float32)
    for wq in range(2):
        sel2[wq, :, :, wq, :] = sel
    sel2 = jnp.asarray(sel2.reshape(2 * n_c, GRID_W, 2 * GRID_W))
    rows = rows.reshape(rpb.shape[0], NA_BIAS_PAIRS, 2, 2 * n_c)
    tab = jnp.einsum("heuj,jkl->heukl", rows, sel2, precision=HIGHEST)
    return tab.reshape(rpb.shape[0], NA_BIAS_PAIRS, 2 * GRID_W, 2 * GRID_W)


def _ctx_attn_kernel(sink_ref, r_ref, p_ref, ow_ref, on_ref):
    l = p_ref.shape[0]
    n_cols = SEG // LANES
    k_all = r_ref[:, R_KVW * SEG:R_KVW * SEG + LANES]
    v_all = r_ref[:, R_KVW * SEG + LANES:R_KVW * SEG + 2 * LANES]
    qs = jnp.concatenate([r_ref[:, R_QW * SEG + c * LANES:R_QW * SEG + (c + 1) * LANES] for c in range(n_cols)], axis=0)
    outs = []
    for g in range(WIN_KV_HEADS):
        s = _dot_nt(qs, _half_mask(k_all, g))
        sink = jnp.concatenate([jnp.full((l, 1), sink_ref[n_cols * g + c], F32) for c in range(n_cols)], axis=0)
        outs.append(_softmax_pv(s, _half_mask(v_all, g), sink))
    o = outs[0] + outs[1]
    ow_ref[...] = jnp.concatenate([o[c * l:(c + 1) * l, :] for c in range(n_cols)], axis=1).astype(BF16)
    for pair in range(NA_HEADS // 2):
        sl = lambda t: slice(t * SEG + pair * LANES, t * SEG + (pair + 1) * LANES)
        q, k, v = p_ref[:, sl(T_QN)], p_ref[:, sl(T_KN)], p_ref[:, sl(T_VN)]
        out = None
        for u in range(2):
            o = _softmax_pv(_dot_nt(q, _half_mask(k, u)), _half_mask(v, u))
            out = o if out is None else out + o
        on_ref[:, pair * LANES:(pair + 1) * LANES] = out.astype(BF16)


def _ctx_attention(r_ctx, p_ctx, sink):
    l = p_ctx.shape[0]
    return pl.pallas_call(
        _ctx_attn_kernel,
        in_specs=[pl.BlockSpec(memory_space=pltpu.SMEM), pl.BlockSpec(r_ctx.shape, lambda: (0, 0)),
                  pl.BlockSpec(p_ctx.shape, lambda: (0, 0))],
        out_specs=[pl.BlockSpec((l, SEG), lambda: (0, 0)), pl.BlockSpec((l, SEG), lambda: (0, 0))],
        out_shape=[jax.ShapeDtypeStruct((l, SEG), BF16), jax.ShapeDtypeStruct((l, SEG), BF16)],
        compiler_params=pltpu.CompilerParams(vmem_limit_bytes=VMEM_LIMIT),
        name="context_attention",
    )(sink, r_ctx, p_ctx)


def _merge_kernel(yf_ref, yb_ref, gr_ref, gn_ref, yw_ref, yn_ref, ga_ref, gb_ref, gc_ref,
                  wr_ref, ww_ref, wn_ref, wo_ref, x_ref, m2_ref, gf_ref, m3_ref, m4_ref, wrt_ref,
                  xo_ref, hx_ref, lt_ref):
    y = yf_ref[...] + yb_ref[...]
    parts = []
    for h in range(RET_HEADS):
        yh = y[:, h * RET_D:(h + 1) * RET_D]
        mu = jnp.mean(yh, axis=-1, keepdims=True)
        var = jnp.mean(jnp.square(yh - mu), axis=-1, keepdims=True)
        parts.append((yh - mu) * lax.rsqrt(var + EPS))
    g = gr_ref[...].astype(F32)
    ya = jnp.concatenate(parts, axis=1) * gn_ref[...] * (g * jax.nn.sigmoid(g))
    za = _dot(ya.astype(BF16), wr_ref[...])
    zb = _dot(yw_ref[...], ww_ref[...])
    zc = _dot(yn_ref[...], wn_ref[...])
    sig = lambda r: jax.nn.sigmoid(r[...].astype(F32))
    mix = sig(ga_ref) * za + sig(gb_ref) * zb + sig(gc_ref) * zc
    x_new = x_ref[...] + m2_ref[...] * _dot(mix.astype(BF16), wo_ref[...])
    xo_ref[...] = x_new
    h2 = _norm_mod(x_new, gf_ref[...], m3_ref[...], m4_ref[...])
    n_sub = h2.shape[1] // LANES
    for s in range(n_sub):
        hx_ref[pl.ds(s, h2.shape[0], stride=n_sub), :] = h2[:, s * LANES:(s + 1) * LANES]
    split = lambda a: (a.astype(BF16), (a - a.astype(BF16).astype(F32)).astype(BF16))
    (w_hi, w_lo), (h_hi, h_lo) = split(wrt_ref[...]), split(h2)
    lt_ref[...] = _dot_nt(w_hi, h_hi) + _dot_nt(w_lo, h_hi) + _dot_nt(w_hi, h_lo)


def _merge(yf, yb, p, gn, yw, yn, wr, ww, wn, wo, x, m2, gf, m3, m4, wrt, *, tm):
    n, d = x.shape
    row = lambda wdt, t: pl.BlockSpec((tm, wdt), lambda i: (i, t))
    full = lambda a: pl.BlockSpec(a.shape, lambda i: (0,) * a.ndim)
    gate0 = T_GATES * SEG // d
    return pl.pallas_call(
        _merge_kernel,
        grid=(n // tm,),
        in_specs=[row(SEG, 0), row(SEG, 0), row(SEG, T_GR), full(gn), row(SEG, 0), row(SEG, 0),
                  row(d, gate0), row(d, gate0 + 1), row(d, gate0 + 2),
                  full(wr), full(ww), full(wn), full(wo), row(d, 0), full(m2), full(gf), full(m3), full(m4), full(wrt)],
        out_specs=[pl.BlockSpec((tm, d), lambda i: (i, 0)),
                   pl.BlockSpec((tm * (d // LANES), LANES), lambda i: (i, 0)),
                   pl.BlockSpec((N_EXPERTS, tm), lambda i: (0, i))],
        out_shape=[jax.ShapeDtypeStruct((n, d), F32), jax.ShapeDtypeStruct((n * (d // LANES), LANES), F32),
                   jax.ShapeDtypeStruct((N_EXPERTS, n), F32)],
        compiler_params=_cparams(("arbitrary",)),
        name="merge",
    )(yf, yb, p, gn, yw, yn, p, p, p, wr, ww, wn, wo, x, m2, gf, m3, m4, wrt)


def _route_kernel(lt_ref, idx_ref, gate_ref, pos_ref, goff_ref, aff_ref, thr_ref, *, cap, n_groups):
    e = pl.program_id(0)
    n_exp, ag, _ = lt_ref.shape
    capp = idx_ref.shape[-1]

    @pl.when(e == 0)
    def _():
        lt = lt_ref[...]
        ex = jnp.exp(lt - jnp.max(lt, axis=0, keepdims=True))
        aff = ex / jnp.sum(ex, axis=0, keepdims=True)
        real = _iota(aff.shape, 1) < n_groups
        aff = jnp.where(real, aff, 0.0)
        aff_ref[...] = aff

        def count_ge(v):
            return jnp.sum(jnp.sum((aff >= v).astype(F32), axis=1, keepdims=True), axis=2, keepdims=True)

        def body(k, carry):
            lo, hi = carry
            mid = jnp.where(lo > 0.0, jnp.sqrt(lo) * jnp.sqrt(hi), hi * 2.0 ** -16)
            ge = count_ge(mid) >= float(cap)
            return jnp.where(ge, mid, lo), jnp.where(ge, hi, mid)

        lo, _ = lax.fori_loop(0, BISECT_STEPS, body, (jnp.zeros((n_exp, 1, 1), F32), jnp.full((n_exp, 1, 1), 2.0, F32)))
        big = jnp.where(aff >= lo, aff, 4.0)
        t = jnp.min(jnp.min(big, axis=1, keepdims=True), axis=2, keepdims=True)
        thr_ref[...] = jnp.broadcast_to(t, thr_ref.shape)

    a = aff_ref[e]
    t = thr_ref[e][0:1, :]
    gt = a > t
    eq = (a == t) & (_iota(a.shape, 0) < n_groups)
    tri_lane_strict = (_iota((LANES, LANES), 0) < _iota((LANES, LANES), 1)).astype(BF16)
    tri_lane_incl = (_iota((LANES, LANES), 0) <= _iota((LANES, LANES), 1)).astype(BF16)
    tri_grp_strict = (_iota((ag, ag), 1) < _iota((ag, ag), 0)).astype(BF16)
    tri_grp_incl = (_iota((ag, ag), 1) <= _iota((ag, ag), 0)).astype(BF16)

    def total(mask_f):
        return jnp.sum(jnp.sum(mask_f, axis=1, keepdims=True), axis=0, keepdims=True)

    def group_sum(mask_f):
        return jnp.broadcast_to(jnp.sum(mask_f, axis=1, keepdims=True), (ag, LANES)).astype(BF16)

    eq_f = eq.astype(F32)
    need = float(cap) - total(gt.astype(F32))
    rank_eq = _dot(tri_grp_strict, group_sum(eq_f)) + _dot(eq_f.astype(BF16), tri_lane_strict)
    sel = gt | (eq & (rank_eq < need))
    sel_f = sel.astype(F32)
    cl = _dot(sel_f.astype(BF16), tri_lane_incl)
    cg = _dot(tri_grp_incl, group_sum(sel_f))
    goff = cg - jnp.broadcast_to(jnp.sum(sel_f, axis=1, keepdims=True), (ag, LANES))
    pos_ref[...] = jnp.where(sel, goff + cl - 1.0, -1.0)
    diag = _iota((ag, ag), 0) == _iota((ag, ag), 1)
    goff_sq = goff if ag == LANES else goff[:, :ag]
    goff_ref[...] = jnp.sum(jnp.where(diag, goff_sq, 0.0), axis=0, keepdims=True).astype(I32)

    pp = _iota((ag, capp), 1, F32)
    cg_b = jnp.broadcast_to(cg[:, 0:1], (ag, capp))
    below = cg_b <= pp
    grp = jnp.sum(below.astype(F32), axis=0, keepdims=True)
    off = jnp.max(jnp.where(below, cg_b, 0.0), axis=0, keepdims=True)
    onehot = _iota((ag, capp), 0, F32) == grp
    in_grp = _dot_tn(cl.astype(BF16), onehot.astype(BF16))
    local = pp[0:1, :] - off
    lane_of = jnp.sum((in_grp <= local).astype(F32), axis=0, keepdims=True)
    live = pp[0:1, :] < float(cap)
    idx = jnp.where(live, grp * float(LANES) + lane_of, 0.0)
    idx_ref[...] = idx.astype(I32)
    a_hi = a.astype(BF16)
    r1 = a - a_hi.astype(F32)
    a_mid = r1.astype(BF16)
    a_lo = (r1 - a_mid.astype(F32)).astype(BF16)
    hot = onehot.astype(BF16)
    aff_grp = (_dot_tn(a_hi, hot) + _dot_tn(a_mid, hot)) + _dot_tn(a_lo, hot)
    pick = _iota((LANES, capp), 0, F32) == lane_of
    gate = jnp.sum(jnp.where(pick, aff_grp, 0.0), axis=0, keepdims=True)
    gate_ref[...] = jnp.where(live, gate, 0.0)


def _route(logits_t, *, cap, n_groups):
    n_exp, ag, _ = logits_t.shape
    capp = -(-cap // LANES) * LANES
    return pl.pallas_call(
        functools.partial(_route_kernel, cap=cap, n_groups=n_groups),
        grid=(n_exp,),
        in_specs=[pl.BlockSpec(logits_t.shape, lambda e: (0, 0, 0))],
        out_specs=[pl.BlockSpec((None, 1, capp), lambda e: (e, 0, 0)),
                   pl.BlockSpec((None, 1, capp), lambda e: (e, 0, 0)),
                   pl.BlockSpec((None, ag, LANES), lambda e: (e, 0, 0)),
                   pl.BlockSpec((None, 1, ag), lambda e: (e, 0, 0))],
        out_shape=[jax.ShapeDtypeStruct((n_exp, 1, capp), I32), jax.ShapeDtypeStruct((n_exp, 1, capp), F32),
                   jax.ShapeDtypeStruct((n_exp, ag, LANES), F32), jax.ShapeDtypeStruct((n_exp, 1, ag), I32)],
        scratch_shapes=[pltpu.VMEM((n_exp, ag, LANES), F32), pltpu.VMEM((n_exp, SUBLANES, LANES), F32)],
        compiler_params=_cparams(("arbitrary",)),
        name="route",
    )(logits_t)


def _ffn_kernel(idx_ref, idxn_ref, gate_ref, hx_ref, wg_ref, wu_ref, wd_ref, o_ref, xbuf, sem, wgb, wub, wdb, *, tm):
    e = pl.program_id(0)
    j = pl.program_id(1)
    nt = pl.num_programs(1)
    step = e * nt + j
    last = pl.num_programs(0) * nt - 1

    sub = SUBLANES
    rows = tm * sub

    def row_copy(rows_ref, r, slot):
        src = hx_ref.at[pl.ds(pl.multiple_of(rows_ref[0, 0, r] * sub, sub), sub)]
        dst = xbuf.at[pl.ds(pl.multiple_of(slot * rows + r * sub, sub), sub)]
        return pltpu.make_async_copy(src, dst, sem.at[slot])

    def slot_copy(slot):
        return pltpu.make_async_copy(hx_ref.at[pl.ds(0, rows)], xbuf.at[pl.ds(pl.multiple_of(slot * rows, sub), rows)],
                                     sem.at[slot])

    @pl.when(step == 0)
    def _():
        def body(r, carry):
            row_copy(idx_ref, r, 0).start()
            return carry
        lax.fori_loop(0, tm, body, 0, unroll=8)

    @pl.when(j == 0)
    def _():
        wgb[...] = wg_ref[...].astype(BF16)
        wub[...] = wu_ref[...].astype(BF16)
        wdb[...] = wd_ref[...].astype(BF16)

    slot = step % 2
    nslot = 1 - slot
    for r in range(tm):
        row_copy(idxn_ref, r, nslot).start()
    slot_copy(slot).wait()
    base = slot * rows
    x = jnp.concatenate([xbuf[pl.ds(base + s, tm, stride=sub), :] for s in range(sub)], axis=1).astype(BF16)
    g = _dot(x, wgb[...])
    u = _dot(x, wub[...])
    hid = (g * jax.nn.sigmoid(g)) * u
    y = _dot(hid.astype(BF16), wdb[...])
    eye = _iota((tm, tm), 0) == _iota((tm, tm), 1)
    gcol = jnp.sum(jnp.where(eye, jnp.broadcast_to(gate_ref[0], (tm, tm)), 0.0), axis=1, keepdims=True)
    o_ref[...] = (y * gcol).astype(BF16)

    @pl.when(step == last)
    def _():
        slot_copy(nslot).wait()


def _expert_ffn(idx, gate, hx3, w_gate, w_up, w_down, *, layer, cap, tm):
    n_exp = idx.shape[0]
    d = w_gate.shape[2]
    f = w_gate.shape[3]
    assert d == SUBLANES * LANES, "a token row must be exactly one (8, 128) f32 tile"
    nt = cap // tm
    idx_t = idx[:, :, :cap].reshape(n_exp * nt, 1, tm)
    gate_t = gate[:, :, :cap].reshape(n_exp * nt, 1, tm)
    n_steps = n_exp * nt
    wspec = lambda a, b: pl.BlockSpec((None, None, a, b), lambda e, j: (layer, e, 0, 0))
    return pl.pallas_call(
        functools.partial(_ffn_kernel, tm=tm),
        grid=(n_exp, nt),
        in_specs=[pl.BlockSpec((1, 1, tm), lambda e, j: (e * nt + j, 0, 0), memory_space=pltpu.SMEM),
                  pl.BlockSpec((1, 1, tm), lambda e, j: (jnp.minimum(e * nt + j + 1, n_steps - 1), 0, 0), memory_space=pltpu.SMEM),
                  pl.BlockSpec((1, 1, tm), lambda e, j: (e * nt + j, 0, 0)),
                  pl.BlockSpec(memory_space=pl.ANY),
                  wspec(d, f), wspec(d, f), wspec(f, d)],
        out_specs=pl.BlockSpec((tm, d), lambda e, j: (e * nt + j, 0)),
        out_shape=jax.ShapeDtypeStruct((n_exp * cap, d), BF16),
        scratch_shapes=[pltpu.VMEM((2 * tm * SUBLANES, LANES), F32), pltpu.SemaphoreType.DMA((2,)),
                        pltpu.VMEM((d, f), BF16), pltpu.VMEM((d, f), BF16), pltpu.VMEM((f, d), BF16)],
        compiler_params=_cparams(("arbitrary", "arbitrary")),
        name="expert_ffn",
    )(idx_t, idx_t, gate_t, hx3, w_gate, w_up, w_down)


def _combine_kernel(ws_ref, nw_ref, *refs, n_exp, cap, win, gpt, final_norm, next_norm):
    y_refs = refs[:n_exp]
    if next_norm:
        pos_ref, ye_ref, x_ref, m5_ref, gf_ref, ng_ref, nsh_ref, nsc_ref, o_ref, h_ref, acc_ref, xwin, sem = refs[n_exp:]
    else:
        pos_ref, ye_ref, x_ref, m5_ref, gf_ref, o_ref, acc_ref, xwin, sem = refs[n_exp:]
    a = pl.program_id(0)
    ng = pl.num_programs(0)
    per = max(1, MXU_DEPTH // win)

    def slots(e):
        pos = pos_ref[e, pl.ds(a * gpt, gpt), :]
        return jnp.concatenate([jnp.broadcast_to(pos[j:j + 1, :], (win, LANES)) for j in range(gpt)], axis=1)

    lp = _iota((win, gpt * LANES), 0, F32)
    acc = None
    for e0 in range(0, n_exp, per):
        es = range(e0, min(e0 + per, n_exp))
        hot = jnp.concatenate([((slots(e) - (ws_ref[e * ng + a] - e * cap).astype(F32)) == lp).astype(BF16) for e in es],
                              axis=0)
        rows = jnp.concatenate([y_refs[e][...] for e in es], axis=0)
        part = _dot_tn(hot, rows)
        acc = part if acc is None else acc + part
    acc_ref[...] = acc

    for e in range(n_exp):
        first = ws_ref[e * ng + a] - e * cap

        def extra(k, carry, e=e, first=first):
            lo = first + k * win
            row = jnp.minimum(lo, cap - win)
            cp = pltpu.make_async_copy(ye_ref.at[pl.ds(pl.multiple_of(e * cap + row, BF16_ROWS), win)], xwin, sem)
            cp.start()
            cp.wait()
            pos = slots(e)
            hot = ((pos - row.astype(F32)) == lp) & (pos >= lo.astype(F32))
            acc_ref[...] += _dot_tn(hot.astype(BF16), xwin[...])
            return carry

        lax.fori_loop(1, nw_ref[e * ng + a], extra, 0)

    x_new = x_ref[...] + m5_ref[...] * acc_ref[...]
    if final_norm:
        ms = jnp.mean(x_new * x_new, axis=-1, keepdims=True)
        x_new = x_new * lax.rsqrt(ms + EPS) * gf_ref[...]
    o_ref[...] = x_new
    if next_norm:
        h_ref[...] = _norm_mod(x_new, ng_ref[...], nsh_ref[...], nsc_ref[...]).astype(BF16)


def _combine(ye, posmap, goff, x, m5, g_final, *, cap, final_norm, next_mod=None):
    n, d = x.shape
    n_exp = posmap.shape[0]
    gpt = min(COMBINE_GROUPS, n // LANES)
    ng = n // (gpt * LANES)
    tok = gpt * LANES
    win = min(COMBINE_WINDOW, cap)
    goff = goff[:, 0, :ng * gpt:gpt]
    end = jnp.concatenate([goff[:, 1:], jnp.full((n_exp, 1), cap, I32)], axis=1)
    start = jnp.minimum((goff // BF16_ROWS) * BF16_ROWS, cap - win)
    n_win = jnp.maximum((end - start + win - 1) // win, 1).reshape(-1)
    wstart = (start + jnp.arange(n_exp, dtype=I32)[:, None] * cap).reshape(-1)
    yspec = lambda e: pl.BlockSpec((pl.Element(win), pl.Element(d)),
                                   lambda a, ws, nw: (pl.multiple_of(ws[e * ng + a], BF16_ROWS), 0))
    gs = pltpu.PrefetchScalarGridSpec(
        num_scalar_prefetch=2,
        grid=(ng,),
        in_specs=[yspec(e) for e in range(n_exp)]
        + [pl.BlockSpec(posmap.shape, lambda a, ws, nw: (0, 0, 0)),
           pl.BlockSpec(memory_space=pl.ANY),
           pl.BlockSpec((tok, d), lambda a, ws, nw: (a, 0)),
           pl.BlockSpec((1, d), lambda a, ws, nw: (0, 0)), pl.BlockSpec((1, d), lambda a, ws, nw: (0, 0))]
        + [pl.BlockSpec((1, d), lambda a, ws, nw: (0, 0)) for _ in (next_mod or ())],
        out_specs=[pl.BlockSpec((tok, d), lambda a, ws, nw: (a, 0))] * (2 if next_mod else 1),
        scratch_shapes=[pltpu.VMEM((tok, d), F32), pltpu.VMEM((win, d), BF16), pltpu.SemaphoreType.DMA(())],
    )
    out = pl.pallas_call(
        functools.partial(_combine_kernel, n_exp=n_exp, cap=cap, win=win, gpt=gpt, final_norm=final_norm,
                          next_norm=next_mod is not None),
        grid_spec=gs,
        out_shape=[jax.ShapeDtypeStruct((n, d), F32)] + ([jax.ShapeDtypeStruct((n, d), BF16)] if next_mod else []),
        compiler_params=_cparams(("arbitrary",)),
        name="combine_final" if final_norm else ("combine_norm" if next_mod else "combine"),
    )(wstart, n_win, *([ye] * n_exp), posmap, ye, x, m5, g_final, *(next_mod or ()))
    return out if next_mod else out[0]


def _rope_tables(n):
    n_rows = n // GRID_W
    row = jnp.arange(n_rows).astype(F32)
    col = jnp.arange(GRID_W).astype(F32)

    def tables(d, signs):
        nf = d // 4
        inv = ROPE_BASE ** (-jnp.arange(nf, dtype=F32) / nf)
        ang_r, ang_c = row[:, None] * inv, col[:, None] * inv
        reps = LANES // (2 * nf)

        def lanes(f_r, f_c, factors):
            z_r, z_c = jnp.zeros_like(f_r), jnp.zeros_like(f_c)
            pat_r = jnp.concatenate([part for k in range(reps) for part in (f_r * factors[2 * k], z_r)], axis=1)
            pat_c = jnp.concatenate([part for k in range(reps) for part in (z_c, f_c * factors[2 * k + 1])], axis=1)
            return (pat_r[:, None, :] + pat_c[None, :, :]).reshape(n, LANES)

        ones = (1.0,) * (2 * reps)
        cos = lanes(jnp.cos(ang_r), jnp.cos(ang_c), ones)
        return [cos] + [lanes(jnp.sin(ang_r), jnp.sin(ang_c), sg) for sg in signs]

    c128, s128 = tables(RET_D, [(-1.0, -1.0, 1.0, 1.0)])
    c64, sa, sb = tables(HEAD_D, [(-1.0, -1.0, 0.0, 0.0) * 2, (0.0, 0.0, 1.0, 1.0) * 2])
    return c128, s128, c64, sa, sb


WIN_HEAD_ORDER = (0, 4, 1, 5, 2, 6, 3, 7)


def _permute_in_weight(w):
    d = w.shape[0]
    sizes = (("q_r", 512), ("k_r", 512), ("v_r", 512), ("g_r", 512), ("q_w", 512), ("k_w", 128), ("v_w", 128),
             ("q_n", 512), ("k_n", 512), ("v_n", 512), ("gates", 3 * d))
    off, lay = 0, {}
    for name, size in sizes:
        lay[name] = (off, off + size)
        off += size
    seg = lambda name: w[:, lay[name][0]:lay[name][1]]
    q0 = lay["q_w"][0]
    q_w = [w[:, q0 + h * HEAD_D:q0 + (h + 1) * HEAD_D] for h in WIN_HEAD_ORDER]
    pad = jnp.zeros((d, SEG - 2 * LANES), w.dtype)
    rope = jnp.concatenate([seg("q_r"), seg("k_r")] + q_w + [seg("k_w"), seg("v_w"), pad], axis=1).astype(BF16)
    plain = jnp.concatenate([seg("gates"), seg("v_r"), seg("g_r"), seg("q_n"), seg("k_n"), seg("v_n")], axis=1).astype(BF16)
    return rope, plain


def _permute_win_rows(w):
    return jnp.concatenate([w[h * HEAD_D:(h + 1) * HEAD_D] for h in WIN_HEAD_ORDER], axis=0).astype(BF16)


def kernel(x, c, ctx, c_ctx, w_mod, b_mod, g_mix, g_ffn, w_in, ret_decay_logit, ret_gn, w_ret, win_sink, w_win, na_rpb,
           w_na, w_out, w_router, w_exp_gate, w_exp_up, w_exp_down, g_final):
    _, n, d = x.shape
    l = ctx.shape[1]
    depth = w_in.shape[0]
    xs, cs = x[0], ctx[0]
    cc = jnp.zeros((SUBLANES, d), F32).at[0].set(c[0]).at[1].set(c_ctx)
    mods = _modulation(cc, w_mod, b_mod)
    c128, s128, c64, sa64, sb64 = _rope_tables(n)
    k_scale = RET_D ** -0.5
    plain_scales = jnp.ones((PLAIN_TILES,), F32).at[T_QN].set(Q_SCALE)
    ctx_rope_scales = jnp.ones((R_KVW + 1,), F32).at[R_KR].set(k_scale).at[R_QW].set(Q_SCALE)
    cap_x = CAPACITY_FACTOR * n // N_EXPERTS
    cap_c = CAPACITY_FACTOR * l // N_EXPERTS
    grp_c = -(-(l // LANES) // BF16_ROWS) * BF16_ROWS
    vec = lambda v: v.reshape(1, -1)

    for layer in range(depth):
        need_ctx = layer < depth - 1
        last = layer == depth - 1
        mx = [vec(mods[layer, 0, k * d:(k + 1) * d]) for k in range(N_MOD)]
        mc = [vec(mods[layer, 1, k * d:(k + 1) * d]) for k in range(N_MOD)]
        w_rope, w_plain = _permute_in_weight(w_in[layer])
        wr = w_ret[layer].astype(BF16)
        ww = _permute_win_rows(w_win[layer])
        wn = w_na[layer].astype(BF16)
        wo = w_out[layer].astype(BF16)
        wrt = w_router[layer].T
        sink = win_sink[layer].astype(F32) * LOG2E
        lg_rows = jnp.broadcast_to(ret_decay_logit[layer].astype(F32).reshape(2 * RET_HEADS, 1), (2 * RET_HEADS, LANES))
        gn = vec(ret_gn[layer])
        bias_tab = _na_bias_table(na_rpb[layer].astype(F32) * LOG2E)
        gmix, gffn = vec(g_mix[layer]), vec(g_ffn[layer])

        if layer == 0:
            hx = _norm_call(xs, gmix, mx[0], mx[1], tm=min(NORM_ROWS, n))
        hc = _norm_call(cs, gmix, mc[0], mc[1], tm=l)
        p_x = _proj(hx, w_plain, plain_scales, tm=min(PROJ_ROWS, n))
        p_c = _proj(hc, w_plain, plain_scales, tm=l)
        r_c = _proj(hc, w_rope, ctx_rope_scales, tm=l)
        rope = functools.partial(_proj_rope, hx, w_rope, tm=min(PROJ_ROWS, n))
        q_r = rope(R_QR, (c128, s128), head_d=RET_D, scale=1.0, n_rot=SEG // LANES, name="proj_q_ret")
        k_r = rope(R_KR, (c128, s128), head_d=RET_D, scale=k_scale, n_rot=SEG // LANES, name="proj_k_ret")
        q_w = rope(R_QW, (c64, sa64, sb64), head_d=HEAD_D, scale=Q_SCALE, n_rot=SEG // LANES, name="proj_q_win")
        kv_w = rope(R_KVW, (c64, sa64, sb64), head_d=HEAD_D, scale=1.0, n_rot=1, name="proj_kv_win")

        yf, yb = _retention((q_r, 0), (k_r, 0), (p_x, T_VR), (r_c, R_KR), (p_c, T_VR), lg_rows, zero_init=False)
        yw = _window_attention((q_w, 0), (kv_w, 0), (r_c, R_KVW), sink)
        yn = _na_attention(p_x, p_c, bias_tab)
        xs, hx3, lt = _merge(yf, yb, p_x, gn, yw, yn, wr, ww, wn, wo, xs, mx[2], gffn, mx[3], mx[4], wrt, tm=min(MERGE_ROWS, n))

        idx, gate, posmap, goff = _route(lt.reshape(N_EXPERTS, n // LANES, LANES), cap=cap_x, n_groups=n // LANES)
        ye = _expert_ffn(idx, gate, hx3, w_exp_gate, w_exp_up, w_exp_down, layer=layer, cap=cap_x, tm=min(FFN_ROWS, cap_x))
        if last:
            xs = _combine(ye, posmap, goff, xs, mx[5], vec(g_final), cap=cap_x, final_norm=True)
        else:
            nxt = (vec(g_mix[layer + 1]), vec(mods[layer + 1, 0, :d]), vec(mods[layer + 1, 0, d:2 * d]))
            xs, hx = _combine(ye, posmap, goff, xs, mx[5], vec(g_final), cap=cap_x, final_norm=False, next_mod=nxt)

        if need_ctx:
            yfc, ybc = _retention((r_c, R_QR), (r_c, R_KR), (p_c, T_VR), (r_c, R_KR), (p_c, T_VR), lg_rows, zero_init=True)
            ywc, ync = _ctx_attention(r_c, p_c, sink)
            cs, hc3, ltc = _merge(yfc, ybc, p_c, gn, ywc, ync, wr, ww, wn, wo, cs, mc[2], gffn, mc[3], mc[4], wrt, tm=l)
            ltc = jnp.pad(ltc.reshape(N_EXPERTS, l // LANES, LANES), ((0, 0), (0, grp_c - l // LANES), (0, 0)))
            idc, gtc, posc, goffc = _route(ltc, cap=cap_c, n_groups=l // LANES)
            yec = _expert_ffn(idc, gtc, hc3, w_exp_gate, w_exp_up, w_exp_down, layer=layer, cap=cap_c, tm=cap_c)
            cs = _combine(yec, posc, goffc, cs, mc[5], vec(g_final), cap=cap_c, final_norm=False)

    return xs[None]
```
